```python
import math
import jax
import jax.numpy as jnp
from jax import lax
import numpy as np

D_MODEL = 1024
BATCH = 2
SEQ = 8192
DEPTH = 2

GRID_W = 64
CTX_LEN = 256
N_BRANCH = 4
BRANCH_W = D_MODEL // N_BRANCH

HG_HEADS = 4
HG_DK = BRANCH_W // HG_HEADS
HG_DV = BRANCH_W // HG_HEADS
HG_CHUNK = 16

HY_W = BRANCH_W
HY_EMB = 33
HY_BANDS = (HY_EMB - 1) // 2
HY_ORDER = 64
HY_FAST_DECAY = 0.3
HY_SLOW_DECAY = 1.5
HY_TARGET = 1e-2

RET_HEADS = 4
RET_DK = BRANCH_W // (2 * RET_HEADS)
RET_DV = BRANCH_W // RET_HEADS
RET_CHUNK = 128
ROPE_BASE = 10000.0

S5_GROUP_CH = 16
S5_GROUPS = BRANCH_W // S5_GROUP_CH
S5_STATE = 64
S5_DT_MIN = 0.001
S5_DT_MAX = 0.1

MOE_GROUPS = 4
MOE_PER_GROUP = 8
MOE_EXPERTS = MOE_GROUPS * MOE_PER_GROUP
MOE_TOPK = 2
MOE_HIDDEN = D_MODEL // 2
MOE_BLOCK = 256

EPS = 1e-6

IN_SIZES = (BRANCH_W, BRANCH_W, BRANCH_W, BRANCH_W, BRANCH_W,
            3 * HY_W,
            RET_HEADS * RET_DK, RET_HEADS * RET_DK, BRANCH_W, BRANCH_W,
            BRANCH_W,
            N_BRANCH * D_MODEL)
N_IN = 5 * BRANCH_W + 3 * HY_W + 2 * RET_HEADS * RET_DK + 3 * BRANCH_W + N_BRANCH * D_MODEL

S5_DIR_PARAMS = ('s5_lam_re', 's5_lam_im', 's5_log_dt', 's5_b_re', 's5_b_im', 's5_c_re', 's5_c_im')

kernel_name = 'hybrid_gated_branch_hmoe_dit'


def rmsnorm(x, gain):
    xf = x.astype(jnp.float32)
    y = xf * lax.rsqrt(jnp.mean(xf * xf, axis=-1, keepdims=True) + EPS)
    return (y * gain.astype(jnp.float32)).astype(x.dtype)


def modulate(x, gain, shift, scale):
    return rmsnorm(x, gain) * (1.0 + scale) + shift


def split_heads(a, n):
    return a.reshape(a.shape[0], a.shape[1], n, a.shape[-1] // n)


def head_rmsnorm(o, gain):
    of = o.astype(jnp.float32)
    of = of * lax.rsqrt(jnp.mean(of * of, axis=-1, keepdims=True) + EPS)
    return of.reshape(o.shape[0], o.shape[1], -1) * gain


def head_groupnorm(o, gain):
    of = o.astype(jnp.float32)
    of = of - jnp.mean(of, axis=-1, keepdims=True)
    of = of * lax.rsqrt(jnp.mean(of * of, axis=-1, keepdims=True) + EPS)
    return of.reshape(o.shape[0], o.shape[1], -1) * gain


def bidirectional(scan_f, scan_b, ctx_f, ctx_b, lat_f, lat_b, s0):
    flip = lambda a: jnp.flip(a, axis=1)
    yc_f, sc_f = scan_f(ctx_f, s0)
    yl_f, _ = scan_f(lat_f, sc_f)
    yc_b, sc_b = scan_b(tuple(flip(a) for a in ctx_b), s0)
    yl_b, _ = scan_b(tuple(flip(a) for a in lat_b), sc_b)
    return yc_f + flip(yc_b), yl_f + flip(yl_b)


def hgrn2_gates(f_pre, lb):
    z = f_pre.astype(jnp.float32)
    f = lb + (1.0 - lb) * jax.nn.sigmoid(z)
    log_f = jnp.log(f)
    k = (1.0 - lb) * jax.nn.sigmoid(-z)
    return log_f, k


def gated_chunk_recurrence(q, k, v, log_f, s0):
    bsz, L, H, dk = q.shape
    dv = v.shape[-1]
    C = HG_CHUNK
    n = L // C
    q, k, log_f = (a.astype(jnp.float32).reshape(bsz, n, C, H, dk) for a in (q, k, log_f))
    v = v.astype(jnp.float32).reshape(bsz, n, C, H, dv)
    b = jnp.cumsum(log_f, axis=2)
    b_last = b[:, :, -1:]
    causal = jnp.tril(jnp.ones((C, C), dtype=bool))[None, None, :, :, None, None]
    rel = jnp.where(causal, b[:, :, :, None] - b[:, :, None], 0.0)
    dec = jnp.where(causal, jnp.exp(rel), 0.0)
    scores = jnp.einsum('bnthd,bnshd,bntshd->bnhts', q, k, dec)
    o = jnp.einsum('bnhts,bnshe->bnthe', scores, v)
    chunk_kv = jnp.einsum('bnshd,bnshe->nbhde', k * jnp.exp(b_last - b), v)
    chunk_decay = jnp.moveaxis(jnp.exp(b_last[:, :, 0]), 1, 0)

    def step(s, inp):
        dc, kv = inp
        return dc[..., None] * s + kv, s

    s_final, s_start = lax.scan(step, s0.astype(jnp.float32), (chunk_decay, chunk_kv))
    o = o + jnp.einsum('bnthd,nbhde->bnthe', q * jnp.exp(b), s_start)
    return o.reshape(bsz, L, H, dv), s_final


def rope(x, pos):
    d = x.shape[-1]
    freqs = ROPE_BASE ** (-jnp.arange(0, d, 2, dtype=jnp.float32) / d)
    ang = pos.astype(jnp.float32)[:, None] * freqs
    cos = jnp.cos(ang)[None, :, None, :]
    sin = jnp.sin(ang)[None, :, None, :]
    x1, x2 = jnp.split(x.astype(jnp.float32), 2, axis=-1)
    return jnp.concatenate([x1 * cos - x2 * sin, x1 * sin + x2 * cos], axis=-1)


def axial_rope(x, row_pos, col_pos):
    half = x.shape[-1] // 2
    return jnp.concatenate([rope(x[..., :half], row_pos), rope(x[..., half:], col_pos)], axis=-1)


def retention_chunked(q, k, v, log_gamma, s0):
    bsz, L, H, dk = q.shape
    dv = v.shape[-1]
    C = RET_CHUNK
    n = L // C
    q = q.astype(jnp.float32).reshape(bsz, n, C, H, dk)
    k = k.astype(jnp.float32).reshape(bsz, n, C, H, dk)
    v = v.astype(jnp.float32).reshape(bsz, n, C, H, dv)
    pos = jnp.arange(C, dtype=jnp.float32)
    rel = pos[:, None] - pos[None, :]
    lower = rel >= 0
    decay = jnp.where(lower, jnp.exp(jnp.where(lower, rel, 0.0) * log_gamma[:, None, None]), 0.0)
    scores = jnp.einsum('bnthd,bnshd->bnhts', q, k) * decay
    o = jnp.einsum('bnhts,bnshe->bnthe', scores, v)
    k_dec = jnp.exp((C - 1.0 - pos)[:, None] * log_gamma)
    q_dec = jnp.exp((pos + 1.0)[:, None] * log_gamma)
    chunk_kv = jnp.einsum('bnshd,bnshe,sh->nbhde', k, v, k_dec)
    g = jnp.exp(C * log_gamma)[:, None, None]

    def step(s, kv):
        return g * s + kv, s

    s_final, s_start = lax.scan(step, s0.astype(jnp.float32), chunk_kv)
    o = o + jnp.einsum('bnthd,nbhde,th->bnthe', q, s_start, q_dec)
    return o.reshape(bsz, L, H, dv), s_final


def short_conv_centred(u, w, b):
    up = jnp.pad(u, ((0, 0), (1, 1), (0, 0)))
    return up[:, :-2] * w[0] + up[:, 1:-1] * w[1] + up[:, 2:] * w[2] + b


def hyena_filters(L, w1, b1, fr1, w2, b2, fr2, w3):
    t = jnp.linspace(0.0, 1.0, L, dtype=jnp.float32)[:, None]
    w = (2.0 * math.pi / L) * jnp.arange(L, dtype=jnp.float32)[:, None]
    f = jnp.linspace(1e-4, HY_BANDS - 1.0, HY_BANDS, dtype=jnp.float32)[None, :]
    z = jnp.concatenate([t, jnp.cos(f * w), -jnp.sin(f * w)], axis=-1)
    hid = jnp.sin(fr1 * (z @ w1 + b1))
    hid = jnp.sin(fr2 * (hid @ w2 + b2))
    filt = (hid @ w3).astype(jnp.float32)
    deltas = jnp.abs(jnp.linspace(math.log(HY_TARGET) / HY_SLOW_DECAY,
                                  math.log(HY_TARGET) / HY_FAST_DECAY, HY_W, dtype=jnp.float32))
    window = jnp.exp(-t * deltas)
    h_fwd = filt[:, :HY_W] * window
    h_bwd = filt[:, HY_W:] * window
    filt2 = jnp.concatenate([h_fwd, jnp.zeros((1, HY_W), jnp.float32), h_bwd[1:][::-1]], axis=0)
    return filt2 * lax.rsqrt(jnp.sum(filt2 * filt2, axis=0, keepdims=True) + EPS)


def long_conv_fft(u, filt):
    L = u.shape[1]
    uf = jnp.fft.rfft(u.astype(jnp.float32), n=2 * L, axis=1)
    kf = jnp.fft.rfft(filt, n=2 * L, axis=0)
    return jnp.fft.irfft(uf * kf[None], n=2 * L, axis=1)[:, :L]


def hyena(u3, p):
    L = u3.shape[1]
    u3 = short_conv_centred(u3, p['hy_conv_w'], p['hy_conv_b'])
    x0, x1, v = jnp.split(u3, 3, axis=-1)
    v = (v * x1).astype(jnp.float32)
    filt = hyena_filters(L, p['hy_filt_w1'], p['hy_filt_b1'], p['hy_filt_freq1'],
                         p['hy_filt_w2'], p['hy_filt_b2'], p['hy_filt_freq2'], p['hy_filt_w3'])
    y = long_conv_fft(v, filt) + v * p['hy_skip']
    return y * x0


def _linear_combine(e1, e2):
    a1, b1 = e1
    a2, b2 = e2
    return a1 * a2, a2 * b1 + b2


def s5_scan(u, s0, lam_re, lam_im, log_dt, b_re, b_im, c_re, c_im):
    bsz, L, _ = u.shape
    uc = u.astype(jnp.float32).reshape(bsz, L, S5_GROUPS, S5_GROUP_CH).astype(jnp.complex64)
    lam = lax.complex(jnp.minimum(lam_re.astype(jnp.float32), -1e-4), lam_im.astype(jnp.float32))
    a_bar = jnp.exp(lam * jnp.exp(log_dt.astype(jnp.float32))[:, None])
    b_bar = ((a_bar - 1.0) / lam)[..., None] * lax.complex(b_re.astype(jnp.float32), b_im.astype(jnp.float32))
    bu = jnp.einsum('gpc,blgc->blgp', b_bar, uc)
    bu = bu.at[:, 0].add(a_bar * s0)
    _, xs = lax.associative_scan(_linear_combine, (jnp.broadcast_to(a_bar, bu.shape), bu), axis=1)
    c_mat = lax.complex(c_re.astype(jnp.float32), c_im.astype(jnp.float32))
    y = jnp.einsum('gcp,blgp->blgc', c_mat, xs).real
    return y.reshape(bsz, L, S5_GROUPS * S5_GROUP_CH), xs[:, -1]


def token_mixer(h, hc, row_pos, col_pos, p, with_ctx_out):
    splits = list(np.cumsum(IN_SIZES)[:-1])
    zl = jnp.split(h @ p['w_in'], splits, axis=-1)
    zc = jnp.split(hc @ p['w_in'], splits, axis=-1)
    bsz = h.shape[0]

    def hg_args(z, d):
        log_f, k = hgrn2_gates(split_heads(z[1 + d], HG_HEADS), p['hg_lb'][d])
        return (split_heads(z[0], HG_HEADS), k, split_heads(z[3], HG_HEADS), log_f)

    hg_scan = lambda args, s0: gated_chunk_recurrence(*args, s0)
    hg_s0 = jnp.zeros((bsz, HG_HEADS, HG_DK, HG_DV), jnp.float32)
    hg_c, hg_l = bidirectional(hg_scan, hg_scan, hg_args(zc, 0), hg_args(zc, 1),
                               hg_args(zl, 0), hg_args(zl, 1), hg_s0)
    hg_read = lambda o, z: head_rmsnorm(o, p['hg_norm']) * jax.nn.silu(z[4])

    log_gamma = jax.nn.log_sigmoid(p['ret_decay_logit'].astype(jnp.float32))

    def rt_args(z, rotate):
        q = split_heads(z[6], RET_HEADS)
        k = split_heads(z[7], RET_HEADS) * RET_DK ** -0.5
        if rotate:
            q, k = axial_rope(q, row_pos, col_pos), axial_rope(k, row_pos, col_pos)
        return (q, k, split_heads(z[8], RET_HEADS))

    rt_f = lambda args, s0: retention_chunked(*args, log_gamma[0], s0)
    rt_b = lambda args, s0: retention_chunked(*args, log_gamma[1], s0)
    rt_s0 = jnp.zeros((bsz, RET_HEADS, RET_DK, RET_DV), jnp.float32)
    rc_args, rl_args = rt_args(zc, False), rt_args(zl, True)
    rt_c, rt_l = bidirectional(rt_f, rt_b, rc_args, rc_args, rl_args, rl_args, rt_s0)
    rt_read = lambda o, z: head_groupnorm(o, p['ret_norm']) * jax.nn.silu(z[9])

    def s5_dir(d):
        prm = tuple(p[name][d] for name in S5_DIR_PARAMS)
        return lambda args, s0: s5_scan(args[0], s0, *prm)

    s5_s0 = jnp.zeros((bsz, S5_GROUPS, S5_STATE), jnp.complex64)
    s5_c, s5_l = bidirectional(s5_dir(0), s5_dir(1), (zc[10],), (zc[10],), (zl[10],), (zl[10],), s5_s0)

    def s5_read(y, z):
        y = jax.nn.gelu(y + z[10].astype(jnp.float32) * p['s5_d'])
        return (y @ p['s5_glu_w1']) * jax.nn.sigmoid(y @ p['s5_glu_w2'])

    def merge(z, branches):
        st = jnp.stack(branches, axis=2)
        proj = jnp.einsum('blnk,nkd->blnd', st, p['w_branch'])
        gates = jax.nn.sigmoid(z[11].reshape(st.shape[0], st.shape[1], N_BRANCH, D_MODEL))
        return jnp.sum(gates * proj, axis=2) @ p['w_out']

    out_l = merge(zl, [hg_read(hg_l, zl), hyena(zl[5], p), rt_read(rt_l, zl), s5_read(s5_l, zl)])
    if not with_ctx_out:
        return out_l, None
    out_c = merge(zc, [hg_read(hg_c, zc), hyena(zc[5], p), rt_read(rt_c, zc), s5_read(s5_c, zc)])
    return out_l, out_c


def swiglu(xb, w1, w3, w2):
    return (jax.nn.silu(xb @ w1) * (xb @ w3)) @ w2


def hierarchical_moe(h, p):
    n_tok, d = h.shape
    hf = h.astype(jnp.float32)
    grp_prob = jax.nn.softmax(hf @ p['moe_w_grp'].astype(jnp.float32) + p['moe_b_grp'], axis=-1)
    grp_p, grp_i = lax.top_k(grp_prob, 1)
    exp_logits = (hf @ p['moe_w_exp'].astype(jnp.float32) + p['moe_b_exp']).reshape(
        n_tok, MOE_GROUPS, MOE_PER_GROUP)
    exp_logits = jnp.take_along_axis(exp_logits, grp_i[:, :, None], axis=1)[:, 0]
    top_p, top_i = lax.top_k(jax.nn.softmax(exp_logits, axis=-1), MOE_TOPK)
    weight = (grp_p * top_p / jnp.sum(top_p, axis=-1, keepdims=True)).reshape(-1)
    expert = (grp_i * MOE_PER_GROUP + top_i).reshape(-1)
    token = jnp.repeat(jnp.arange(n_tok), MOE_TOPK)
    n_assign = n_tok * MOE_TOPK
    order = jnp.argsort(expert)
    e_s, tok_s, w_s = expert[order], token[order], weight[order]
    counts = jnp.zeros((MOE_EXPERTS,), jnp.int32).at[expert].add(1)
    padded = (counts + MOE_BLOCK - 1) // MOE_BLOCK * MOE_BLOCK
    pad_end = jnp.cumsum(padded)
    pad_start = pad_end - padded
    start = jnp.cumsum(counts) - counts
    dest = pad_start[e_s] + jnp.arange(n_assign) - start[e_s]
    n_blocks = -(-n_assign // MOE_BLOCK) + MOE_EXPERTS
    buf = jnp.zeros((n_blocks * MOE_BLOCK, d), h.dtype).at[dest].set(h[tok_s])
    block_expert = jnp.minimum(
        jnp.searchsorted(pad_end, jnp.arange(n_blocks) * MOE_BLOCK, side='right'), MOE_EXPERTS - 1)

    def expert_block(args):
        xb, e = args
        return swiglu(xb, p['moe_w1'][e], p['moe_w3'][e], p['moe_w2'][e])

    y = lax.map(expert_block, (buf.reshape(n_blocks, MOE_BLOCK, d), block_expert)).reshape(-1, d)
    out = jnp.zeros((n_tok, d), jnp.float32).at[tok_s].add(y[dest].astype(jnp.float32) * w_s[:, None])
    return out.astype(h.dtype)


def trunk_layer(x, ctx, c_act, c_ctx_act, row_pos, col_pos, p, last):
    sh1, sc1, g1, sh2, sc2, g2 = (m[:, None] for m in jnp.split(c_act @ p['w_mod'] + p['b_mod'], 6, axis=-1))
    csh1, csc1, cg1, csh2, csc2, cg2 = jnp.split(c_ctx_act @ p['w_mod'] + p['b_mod'], 6, axis=-1)
    h = modulate(x, p['norm1'], sh1, sc1)
    hc = modulate(ctx, p['norm1'], csh1, csc1)
    mix, mix_c = token_mixer(h, hc, row_pos, col_pos, p, not last)
    x = x + g1 * mix
    bsz, L, d = x.shape
    h2 = modulate(x, p['norm2'], sh2, sc2).reshape(-1, d)
    if last:
        return x + g2 * hierarchical_moe(h2, p).reshape(bsz, L, d), ctx
    ctx = ctx + cg1 * mix_c
    hc2 = modulate(ctx, p['norm2'], csh2, csc2).reshape(-1, d)
    n_ctx = hc2.shape[0]
    y = hierarchical_moe(jnp.concatenate([hc2, h2], axis=0), p)
    ctx = ctx + cg2 * y[:n_ctx].reshape(ctx.shape)
    x = x + g2 * y[n_ctx:].reshape(bsz, L, d)
    return x, ctx


def setup_inputs(seed: int = 0) -> dict:
    key = jax.random.key(seed)
    keys = jax.random.split(key, 64)
    counter = iter(range(64))

    def nrm(shape, scale):
        return scale * jax.random.normal(keys[next(counter)], shape, jnp.float32)

    def gain(shape):
        return 1.0 + nrm(shape, 0.01)

    L = DEPTH
    D = D_MODEL
    ret_logit0 = jnp.log(2.0 ** (5.0 + jnp.arange(RET_HEADS, dtype=jnp.float32)) - 1.0)
    lam_im0 = math.pi * jnp.arange(S5_STATE, dtype=jnp.float32)
    return {
        'x': nrm((BATCH, SEQ, D), 1.0),
        'c': nrm((BATCH, D), 1.0),
        'ctx': nrm((BATCH, CTX_LEN, D), 1.0),
        'c_ctx': nrm((D,), 1.0),
        'w_mod': nrm((L, D, 6 * D), 0.5 * D ** -0.5),
        'b_mod': nrm((L, 6 * D), 0.01),
        'norm1': gain((L, D)),
        'norm2': gain((L, D)),
        'w_in': nrm((L, D, N_IN), D ** -0.5),
        'hg_lb_logits': nrm((L, 2, HG_HEADS * HG_DK), 0.1),
        'hg_norm': gain((L, HG_HEADS * HG_DV)),
        'hy_conv_w': nrm((L, 3, 3 * HY_W), 3 ** -0.5),
        'hy_conv_b': nrm((L, 3 * HY_W), 0.01),
        'hy_filt_w1': nrm((L, HY_EMB, HY_ORDER), HY_EMB ** -0.5),
        'hy_filt_b1': nrm((L, HY_ORDER), 0.1),
        'hy_filt_freq1': gain((L, HY_ORDER)),
        'hy_filt_w2': nrm((L, HY_ORDER, HY_ORDER), HY_ORDER ** -0.5),
        'hy_filt_b2': nrm((L, HY_ORDER), 0.1),
        'hy_filt_freq2': gain((L, HY_ORDER)),
        'hy_filt_w3': nrm((L, HY_ORDER, 2 * HY_W), HY_ORDER ** -0.5),
        'hy_skip': nrm((L, HY_W), 1.0),
        'ret_decay_logit': ret_logit0 + nrm((L, 2, RET_HEADS), 0.01),
        'ret_norm': gain((L, RET_HEADS * RET_DV)),
        's5_lam_re': -0.5 + nrm((L, 2, S5_GROUPS, S5_STATE), 0.01),
        's5_lam_im': lam_im0 + nrm((L, 2, S5_GROUPS, S5_STATE), 0.01),
        's5_log_dt': jax.random.uniform(keys[next(counter)], (L, 2, S5_GROUPS), jnp.float32,
                                        math.log(S5_DT_MIN), math.log(S5_DT_MAX)),
        's5_b_re': nrm((L, 2, S5_GROUPS, S5_STATE, S5_GROUP_CH), S5_GROUP_CH ** -0.5),
        's5_b_im': nrm((L, 2, S5_GROUPS, S5_STATE, S5_GROUP_CH), S5_GROUP_CH ** -0.5),
        's5_c_re': nrm((L, 2, S5_GROUPS, S5_GROUP_CH, S5_STATE), S5_STATE ** -0.5),
        's5_c_im': nrm((L, 2, S5_GROUPS, S5_GROUP_CH, S5_STATE), S5_STATE ** -0.5),
        's5_d': nrm((L, BRANCH_W), 1.0),
        's5_glu_w1': nrm((L, BRANCH_W, BRANCH_W), BRANCH_W ** -0.5),
        's5_glu_w2': nrm((L, BRANCH_W, BRANCH_W), BRANCH_W ** -0.5),
        'w_branch': nrm((L, N_BRANCH, BRANCH_W, D), BRANCH_W ** -0.5),
        'w_out': nrm((L, D, D), D ** -0.5),
        'moe_w_grp': nrm((L, D, MOE_GROUPS), D ** -0.5),
        'moe_b_grp': nrm((L, MOE_GROUPS), 0.01),
        'moe_w_exp': nrm((L, D, MOE_EXPERTS), D ** -0.5),
        'moe_b_exp': nrm((L, MOE_EXPERTS), 0.01),
        'moe_w1': nrm((L, MOE_EXPERTS, D, MOE_HIDDEN), D ** -0.5),
        'moe_w3': nrm((L, MOE_EXPERTS, D, MOE_HIDDEN), D ** -0.5),
        'moe_w2': nrm((L, MOE_EXPERTS, MOE_HIDDEN, D), MOE_HIDDEN ** -0.5),
        'norm_final': gain((D,)),
    }


def reference(x, c, ctx, c_ctx, w_mod, b_mod, norm1, norm2, w_in, hg_lb_logits, hg_norm,
              hy_conv_w, hy_conv_b, hy_filt_w1, hy_filt_b1, hy_filt_freq1, hy_filt_w2, hy_filt_b2,
              hy_filt_freq2, hy_filt_w3, hy_skip, ret_decay_logit, ret_norm,
              s5_lam_re, s5_lam_im, s5_log_dt, s5_b_re, s5_b_im, s5_c_re, s5_c_im, s5_d,
              s5_glu_w1, s5_glu_w2, w_branch, w_out, moe_w_grp, moe_b_grp, moe_w_exp, moe_b_exp,
              moe_w1, moe_w3, moe_w2, norm_final):
    rows = x.shape[1] // GRID_W
    row_pos = jnp.repeat(jnp.arange(rows), GRID_W)
    col_pos = jnp.tile(jnp.arange(GRID_W), rows)
    lb_sm = jax.nn.softmax(hg_lb_logits.astype(jnp.float32), axis=0)
    lb_all = jnp.cumsum(lb_sm, axis=0) - lb_sm[0]
    c_act = jax.nn.silu(c)
    c_ctx_act = jax.nn.silu(c_ctx)
    stacked = dict(w_mod=w_mod, b_mod=b_mod, norm1=norm1, norm2=norm2, w_in=w_in, hg_norm=hg_norm,
                   hy_conv_w=hy_conv_w, hy_conv_b=hy_conv_b, hy_filt_w1=hy_filt_w1, hy_filt_b1=hy_filt_b1,
                   hy_filt_freq1=hy_filt_freq1, hy_filt_w2=hy_filt_w2, hy_filt_b2=hy_filt_b2,
                   hy_filt_freq2=hy_filt_freq2, hy_filt_w3=hy_filt_w3, hy_skip=hy_skip,
                   ret_decay_logit=ret_decay_logit, ret_norm=ret_norm,
                   s5_lam_re=s5_lam_re, s5_lam_im=s5_lam_im, s5_log_dt=s5_log_dt, s5_b_re=s5_b_re,
                   s5_b_im=s5_b_im, s5_c_re=s5_c_re, s5_c_im=s5_c_im, s5_d=s5_d,
                   s5_glu_w1=s5_glu_w1, s5_glu_w2=s5_glu_w2, w_branch=w_branch, w_out=w_out,
                   moe_w_grp=moe_w_grp, moe_b_grp=moe_b_grp, moe_w_exp=moe_w_exp, moe_b_exp=moe_b_exp,
                   moe_w1=moe_w1, moe_w3=moe_w3, moe_w2=moe_w2)
    for l in range(DEPTH):
        p = {name: arr[l] for name, arr in stacked.items()}
        p['hg_lb'] = lb_all[l].reshape(2, HG_HEADS, HG_DK)
        x, ctx = trunk_layer(x, ctx, c_act, c_ctx_act, row_pos, col_pos, p, l == DEPTH - 1)
    return rmsnorm(x, norm_final)
```

```python
import functools
import math

import numpy as np
import jax
import jax.numpy as jnp
from jax import lax
from jax.experimental import pallas as pl
from jax.experimental.pallas import tpu as pltpu

F32 = jnp.float32
BF16 = jnp.bfloat16
HIGHEST = lax.Precision.HIGHEST

GRID_W = 64
N_BRANCH = 4
BRANCH_W = 256
HG_HEADS = 4
HG_CHUNK = 16
HY_W = 256
HY_EMB = 33
HY_BANDS = 16
HY_ORDER = 64
HY_FAST_DECAY = 0.3
HY_SLOW_DECAY = 1.5
HY_TARGET = 1e-2
RET_HEADS = 4
RET_DK = 32
RET_DV = 64
ROPE_BASE = 10000.0
S5_GROUP_CH = 16
S5_GROUPS = 16
S5_STATE = 64
MOE_GROUPS = 4
MOE_PER_GROUP = 8
MOE_EXPERTS = 32
MOE_HIDDEN = 512
MOE_BLOCK = 256
EPS = 1e-6

LANES = 128
SUBLANES = 8
TM = 256
FFT_FAST = 128
VMEM_LIMIT = 56 * 1024 * 1024

N_MIX = 3072
N_ZCOLS = N_MIX + 2 * RET_HEADS * RET_DK
CB_HG_Q, CB_HG_FF, CB_HG_FB, CB_HG_I, CB_HG_G = 3, 4, 5, 6, 7
CB_RT_V, CB_RT_G, CB_S5 = 9, 10, 11
CB128_RT_Q, CB128_RT_K, CB128_RT_QR, CB128_RT_KR = 16, 17, 24, 25


def _cparams(sem):
    return pltpu.CompilerParams(dimension_semantics=sem, vmem_limit_bytes=VMEM_LIMIT)


def _sigmoid(x):
    return 1.0 / (1.0 + jnp.exp(-x))


def _silu(x):
    return x * _sigmoid(x)


def _dot(a, b):
    return jnp.dot(a.astype(BF16), b.astype(BF16), preferred_element_type=F32)


def _dot_nt(a, b):
    return lax.dot_general(a.astype(BF16), b.astype(BF16), (((1,), (1,)), ((), ())),
                           preferred_element_type=F32)


def _dot_hi(a, b):
    return jnp.dot(a, b, precision=HIGHEST, preferred_element_type=F32)


def _split_dot(x, m_bf16):
    hi = x.astype(BF16)
    lo = (x - hi.astype(F32)).astype(BF16)
    return (jnp.dot(hi, m_bf16, preferred_element_type=F32)
            + jnp.dot(lo, m_bf16, preferred_element_type=F32))


def _split_dot_left(m_bf16, x):
    hi = x.astype(BF16)
    lo = (x - hi.astype(F32)).astype(BF16)
    return (jnp.dot(m_bf16, hi, preferred_element_type=F32)
            + jnp.dot(m_bf16, lo, preferred_element_type=F32))


def _seg_of_tile(i, tpb):
    return jnp.where(i % tpb == tpb - 1, 0, 1 + i // tpb)


def _scan_tiles(b, s, tpb):
    last = tpb - 1
    fwd = jnp.where(s == 0, last, s - 1)
    bwd = jnp.where(s == 0, last, last - s)
    return b * tpb + fwd, b * tpb + bwd


def _modvec_kernel(c_ref, w_ref, b_ref, o_ref):
    c = c_ref[...]
    o_ref[0] = _dot_hi(_silu(c), w_ref[0]) + b_ref[0]


def modvec(cmat, w_mod, b_mod):
    depth, d, n6 = w_mod.shape
    nb = n6 // d
    return pl.pallas_call(
        _modvec_kernel,
        grid=(depth, nb),
        in_specs=[pl.BlockSpec((SUBLANES, d), lambda l, j: (0, 0)),
                  pl.BlockSpec((1, d, d), lambda l, j: (l, 0, j)),
                  pl.BlockSpec((1, 1, d), lambda l, j: (l, 0, j))],
        out_specs=pl.BlockSpec((1, SUBLANES, d), lambda l, j: (l, 0, j)),
        out_shape=jax.ShapeDtypeStruct((depth, SUBLANES, n6), F32),
        compiler_params=_cparams(("arbitrary", "arbitrary")),
        name="modvec",
    )(cmat, w_mod, b_mod.reshape(depth, 1, n6))


def _modulated_norm(x, gain, shift, scale):
    ms = jnp.mean(x * x, axis=-1, keepdims=True)
    return (x * lax.rsqrt(ms + EPS) * gain) * (1.0 + scale) + shift


def _inproj_kernel(x_ref, mod_ref, g_ref, w_ref, z_ref, *, d):
    h = _modulated_norm(x_ref[...], g_ref[...], mod_ref[0, :, 0:d], mod_ref[0, :, d:2 * d])
    z_ref[...] = jnp.dot(h.astype(BF16), w_ref[...], preferred_element_type=F32)


def inproj(x_all, mod3, gain, w_mix, tpb):
    m, d = x_all.shape
    nz = w_mix.shape[1]
    return pl.pallas_call(
        functools.partial(_inproj_kernel, d=d),
        grid=(m // TM,),
        in_specs=[pl.BlockSpec((TM, d), lambda i: (i, 0)),
                  pl.BlockSpec((1, 1, 6 * d), lambda i: (_seg_of_tile(i, tpb), 0, 0)),
                  pl.BlockSpec((1, d), lambda i: (0, 0)),
                  pl.BlockSpec((d, nz), lambda i: (0, 0))],
        out_specs=pl.BlockSpec((TM, nz), lambda i: (i, 0)),
        out_shape=jax.ShapeDtypeStruct((m, nz), F32),
        compiler_params=_cparams(("parallel",)),
        name="inproj",
    )(x_all, mod3, gain, w_mix)


def _hgrn_direction(q_ref, f_ref, v_ref, lb, o_ref, st_ref, b_s, k_s, kt_s, vt_s, reverse):
    c = HG_CHUNK
    w = BRANCH_W
    nchunk = TM // c
    hd = w // HG_HEADS
    z = f_ref[...]
    f = lb + (1.0 - lb) * _sigmoid(z)
    log_f = jnp.log(f)
    k_s[...] = (1.0 - lb) * _sigmoid(-z)

    ri = lax.broadcasted_iota(jnp.int32, (TM, TM), 0)
    ci = lax.broadcasted_iota(jnp.int32, (TM, TM), 1)
    same = (ri // c) == (ci // c)
    tri = jnp.where(same & ((ci >= ri) if reverse else (ci <= ri)), 1.0, 0.0).astype(BF16)
    b = _split_dot_left(tri, log_f)
    b_s[...] = b
    edge = jnp.where(same & (((ci % c) == 0) if reverse else ((ci % c) == c - 1)), 1.0, 0.0).astype(BF16)
    b_last = _split_dot_left(edge, b)
    kt_s[...] = k_s[...] * jnp.exp(b_last - b)
    vt_s[...] = v_ref[...].T

    head_r = lax.broadcasted_iota(jnp.int32, (w, w), 0) // hd
    head_c = lax.broadcasted_iota(jnp.int32, (w, w), 1) // hd
    hmask = head_r == head_c
    hm = jnp.where(hmask, 1.0, 0.0).astype(BF16)
    rows = lax.broadcasted_iota(jnp.int32, (c, 1), 0)
    tok = lax.broadcasted_iota(jnp.int32, (TM, 1), 0) // c

    def chunk_body(n, carry):
        cidx = (nchunk - 1 - n) if reverse else n
        off = pl.multiple_of(cidx * c, c)
        qc = q_ref[pl.ds(off, c), :]
        bc = b_s[pl.ds(off, c), :]
        kc = k_s[pl.ds(off, c), :]
        vc = v_ref[pl.ds(off, c), :]
        parts = []
        for s in range(c):
            live = (rows <= s) if reverse else (rows >= s)
            e = jnp.where(live, jnp.exp(jnp.where(live, bc - bc[s:s + 1, :], 0.0)), 0.0)
            parts.append(qc * kc[s:s + 1, :] * e)
        p = jnp.concatenate(parts, axis=0)
        r = _split_dot(p, hm)
        o = jnp.zeros((c, w), F32)
        for s in range(c):
            o = o + r[s * c:(s + 1) * c, :] * vc[s:s + 1, :]
        st = st_ref[...]
        o = o + _dot_nt(qc * jnp.exp(bc), st)
        o_ref[pl.ds(off, c), :] = o
        blast = bc[0:1, :] if reverse else bc[c - 1:c, :]
        ktm = jnp.where(tok == cidx, kt_s[...], 0.0)
        kv = _dot(vt_s[...], ktm)
        st_ref[...] = jnp.exp(blast) * st + jnp.where(hmask, kv, 0.0)
        return carry

    lax.fori_loop(0, nchunk, chunk_body, 0)


def _hgrn_kernel(qf_ref, ff_ref, vf_ref, qb_ref, fb_ref, vb_ref, lb_ref, of_ref, ob_ref,
                 stf_ref, stb_ref, b_s, k_s, kt_s, vt_s):
    @pl.when(pl.program_id(1) == 0)
    def _():
        stf_ref[...] = jnp.zeros_like(stf_ref)
        stb_ref[...] = jnp.zeros_like(stb_ref)

    _hgrn_direction(qf_ref, ff_ref, vf_ref, lb_ref[0:1, :], of_ref, stf_ref, b_s, k_s, kt_s, vt_s, False)
    _hgrn_direction(qb_ref, fb_ref, vb_ref, lb_ref[1:2, :], ob_ref, stb_ref, b_s, k_s, kt_s, vt_s, True)


def hgrn(z, lb, bsz, tpb):
    m = z.shape[0]
    w = BRANCH_W

    def spec(cb, d):
        return pl.BlockSpec((TM, w), lambda b, s: (_scan_tiles(b, s, tpb)[d], cb))

    out_spec = [pl.BlockSpec((TM, w), lambda b, s: (_scan_tiles(b, s, tpb)[0], 0)),
                pl.BlockSpec((TM, w), lambda b, s: (_scan_tiles(b, s, tpb)[1], 0))]
    return pl.pallas_call(
        _hgrn_kernel,
        grid=(bsz, tpb),
        in_specs=[spec(CB_HG_Q, 0), spec(CB_HG_FF, 0), spec(CB_HG_I, 0),
                  spec(CB_HG_Q, 1), spec(CB_HG_FB, 1), spec(CB_HG_I, 1),
                  pl.BlockSpec((2, w), lambda b, s: (0, 0))],
        out_specs=out_spec,
        out_shape=[jax.ShapeDtypeStruct((m, w), F32)] * 2,
        scratch_shapes=[pltpu.VMEM((w, w), F32), pltpu.VMEM((w, w), F32),
                        pltpu.VMEM((TM, w), F32), pltpu.VMEM((TM, w), F32),
                        pltpu.VMEM((TM, w), F32), pltpu.VMEM((w, TM), F32)],
        compiler_params=_cparams(("arbitrary", "arbitrary")),
        name="hgrn2",
    )(z, z, z, z, z, z, lb)


def _ret_direction(q_ref, k_ref, qr_ref, kr_ref, v_ref, cos_ref, sin_ref, lgq, lg_ref, d, o_ref, st_ref,
                   reverse):
    nq = RET_HEADS * RET_DK
    nv = RET_HEADS * RET_DV
    cos = cos_ref[...]
    sin = sin_ref[...]
    qx = q_ref[...] * cos + qr_ref[...] * sin
    kx = (k_ref[...] * cos + kr_ref[...] * sin) * (RET_DK ** -0.5)
    v = v_ref[...]
    t = lax.broadcasted_iota(jnp.int32, (TM, 1), 0).astype(F32)
    pos = (TM - 1.0 - t) if reverse else t
    qd = qx * jnp.exp((pos + 1.0) * lgq)
    kd = kx * jnp.exp((TM - 1.0 - pos) * lgq)
    ti = lax.broadcasted_iota(jnp.int32, (TM, TM), 0)
    si = lax.broadcasted_iota(jnp.int32, (TM, TM), 1)
    rel = ((si - ti) if reverse else (ti - si)).astype(F32)
    live = rel >= 0.0
    relc = jnp.where(live, rel, 0.0)
    lane_q = lax.broadcasted_iota(jnp.int32, (1, nq), 1) // RET_DK
    lane_v = lax.broadcasted_iota(jnp.int32, (1, nv), 1) // RET_DV
    st = st_ref[...]
    o = _dot_nt(qd, st)
    vb = v.astype(BF16)
    for h in range(RET_HEADS):
        dmat = jnp.where(live, jnp.exp(relc * lg_ref[d, h]), 0.0)
        sc = _dot_nt(jnp.where(lane_q == h, qx, 0.0), kx) * dmat
        oh = jnp.dot(sc.astype(BF16), vb, preferred_element_type=F32)
        o = o + jnp.where(lane_v == h, oh, 0.0)
    o_ref[...] = o
    kv = _dot(v.T, kd)
    hr = lax.broadcasted_iota(jnp.int32, (nv, nq), 0) // RET_DV
    hc = lax.broadcasted_iota(jnp.int32, (nv, nq), 1) // RET_DK
    st_ref[...] = st * jnp.exp(float(TM) * lgq) + jnp.where(hr == hc, kv, 0.0)


def _ret_kernel(lg_ref, qf, kf, qrf, krf, vf, cosf, sinf, qb, kb, qrb, krb, vb, cosb, sinb, lgq_ref,
                of_ref, ob_ref, stf_ref, stb_ref):
    @pl.when(pl.program_id(1) == 0)
    def _():
        stf_ref[...] = jnp.zeros_like(stf_ref)
        stb_ref[...] = jnp.zeros_like(stb_ref)

    _ret_direction(qf, kf, qrf, krf, vf, cosf, sinf, lgq_ref[0:1, :], lg_ref, 0, of_ref, stf_ref, False)
    _ret_direction(qb, kb, qrb, krb, vb, cosb, sinb, lgq_ref[1:2, :], lg_ref, 1, ob_ref, stb_ref, True)


def retention(z, cos_tab, sin_tab, log_gamma, bsz, tpb):
    m = z.shape[0]
    nq = RET_HEADS * RET_DK
    nv = RET_HEADS * RET_DV
    lgq = jnp.repeat(log_gamma, RET_DK, axis=1)

    def zspec(cb, width, d):
        return pl.BlockSpec((TM, width), lambda b, s: (_scan_tiles(b, s, tpb)[d], cb))

    def tspec(d):
        return pl.BlockSpec((TM, nq), lambda b, s: (_scan_tiles(0, s, tpb)[d], 0))

    def side(d):
        return [zspec(CB128_RT_Q, nq, d), zspec(CB128_RT_K, nq, d), zspec(CB128_RT_QR, nq, d),
                zspec(CB128_RT_KR, nq, d), zspec(CB_RT_V, nv, d), tspec(d), tspec(d)]

    return pl.pallas_call(
        _ret_kernel,
        grid=(bsz, tpb),
        in_specs=[pl.BlockSpec(memory_space=pltpu.SMEM)] + side(0) + side(1)
                 + [pl.BlockSpec((2, nq), lambda b, s: (0, 0))],
        out_specs=[pl.BlockSpec((TM, nv), lambda b, s: (_scan_tiles(b, s, tpb)[0], 0)),
                   pl.BlockSpec((TM, nv), lambda b, s: (_scan_tiles(b, s, tpb)[1], 0))],
        out_shape=[jax.ShapeDtypeStruct((m, nv), F32)] * 2,
        scratch_shapes=[pltpu.VMEM((nv, nq), F32), pltpu.VMEM((nv, nq), F32)],
        compiler_params=_cparams(("arbitrary", "arbitrary")),
        name="retention",
    )(log_gamma, z, z, z, z, z, cos_tab, sin_tab, z, z, z, z, z, cos_tab, sin_tab, lgq)


S5_NP = S5_GROUPS * S5_STATE


def _s5_direction(u_ref, bre_ref, bim_ref, cre_ref, cim_ref, tab_ref, o_ref, car_ref, xr_s, xi_s, reverse):
    nb = TM // SUBLANES
    ub = u_ref[...].astype(BF16)
    xr = jnp.dot(ub, bre_ref[0], preferred_element_type=F32).reshape(nb, SUBLANES, S5_NP)
    xi = jnp.dot(ub, bim_ref[0], preferred_element_type=F32).reshape(nb, SUBLANES, S5_NP)
    sub = lax.broadcasted_iota(jnp.int32, (1, SUBLANES, 1), 1)
    for k, row in ((1, 8), (2, 9), (4, 10)):
        ar = tab_ref[0, 0, row:row + 1, :]
        ai = tab_ref[0, 1, row:row + 1, :]
        shift = (SUBLANES - k) if reverse else k
        keep = (sub < SUBLANES - k) if reverse else (sub >= k)
        sr = jnp.where(keep, pltpu.roll(xr, shift, 1), 0.0)
        si = jnp.where(keep, pltpu.roll(xi, shift, 1), 0.0)
        xr, xi = xr + ar * sr - ai * si, xi + ar * si + ai * sr
    xr_s[...] = xr.reshape(TM, S5_NP)
    xi_s[...] = xi.reshape(TM, S5_NP)
    pr = tab_ref[0, 0, 0:SUBLANES, :]
    pi = tab_ref[0, 1, 0:SUBLANES, :]

    def body(n, carry):
        cr, ci = carry
        blk = (nb - 1 - n) if reverse else n
        off = pl.multiple_of(blk * SUBLANES, SUBLANES)
        br = xr_s[pl.ds(off, SUBLANES), :] + pr * cr - pi * ci
        bi = xi_s[pl.ds(off, SUBLANES), :] + pr * ci + pi * cr
        xr_s[pl.ds(off, SUBLANES), :] = br
        xi_s[pl.ds(off, SUBLANES), :] = bi
        if reverse:
            return br[0:1, :], bi[0:1, :]
        return br[SUBLANES - 1:SUBLANES, :], bi[SUBLANES - 1:SUBLANES, :]

    cr, ci = lax.fori_loop(0, nb, body, (car_ref[0:1, :], car_ref[1:2, :]))
    car_ref[0:1, :] = cr
    car_ref[1:2, :] = ci
    o_ref[...] = (jnp.dot(xr_s[...].astype(BF16), cre_ref[0], preferred_element_type=F32)
                  - jnp.dot(xi_s[...].astype(BF16), cim_ref[0], preferred_element_type=F32))


def _s5_kernel(uf, ub, bre, bim, cre, cim, tabf, tabb, bre2, bim2, cre2, cim2, of_ref, ob_ref,
               carf, carb, xr_s, xi_s):
    @pl.when(pl.program_id(1) == 0)
    def _():
        carf[...] = jnp.zeros_like(carf)
        carb[...] = jnp.zeros_like(carb)

    _s5_direction(uf, bre, bim, cre, cim, tabf, of_ref, carf, xr_s, xi_s, False)
    _s5_direction(ub, bre2, bim2, cre2, cim2, tabb, ob_ref, carb, xr_s, xi_s, True)


def s5(z, bblk_re, bblk_im, cblk_re, cblk_im, tab, bsz, tpb):
    m = z.shape[0]
    w = BRANCH_W

    def pspec(shape, d):
        return pl.BlockSpec((1,) + shape, lambda b, s: (d,) + (0,) * len(shape))

    def uspec(d):
        return pl.BlockSpec((TM, w), lambda b, s: (_scan_tiles(b, s, tpb)[d], CB_S5))

    return pl.pallas_call(
        _s5_kernel,
        grid=(bsz, tpb),
        in_specs=[uspec(0), uspec(1),
                  pspec((w, S5_NP), 0), pspec((w, S5_NP), 0), pspec((S5_NP, w), 0), pspec((S5_NP, w), 0),
                  pspec((2, 16, S5_NP), 0), pspec((2, 16, S5_NP), 1),
                  pspec((w, S5_NP), 1), pspec((w, S5_NP), 1), pspec((S5_NP, w), 1), pspec((S5_NP, w), 1)],
        out_specs=[pl.BlockSpec((TM, w), lambda b, s: (_scan_tiles(b, s, tpb)[0], 0)),
                   pl.BlockSpec((TM, w), lambda b, s: (_scan_tiles(b, s, tpb)[1], 0))],
        out_shape=[jax.ShapeDtypeStruct((m, w), F32)] * 2,
        scratch_shapes=[pltpu.VMEM((2, S5_NP), F32), pltpu.VMEM((2, S5_NP), F32),
                        pltpu.VMEM((TM, S5_NP), F32), pltpu.VMEM((TM, S5_NP), F32)],
        compiler_params=_cparams(("arbitrary", "arbitrary")),
        name="s5",
    )(z, z, bblk_re, bblk_im, cblk_re, cblk_im, tab, tab, bblk_re, bblk_im, cblk_re, cblk_im)


def s5_params(lam_re, lam_im, log_dt, b_re, b_im, c_re, c_im):
    lr = jnp.minimum(lam_re.astype(F32), -1e-4)
    li = lam_im.astype(F32)
    dt = jnp.exp(log_dt.astype(F32))[..., None]

    def apow(k):
        mag = jnp.exp(k * lr * dt)
        return mag * jnp.cos(k * li * dt), mag * jnp.sin(k * li * dt)

    ar, ai = apow(1.0)
    den = lr * lr + li * li
    gr = ((ar - 1.0) * lr + ai * li) / den
    gi = (ai * lr - (ar - 1.0) * li) / den
    bbr = gr[..., None] * b_re - gi[..., None] * b_im
    bbi = gr[..., None] * b_im + gi[..., None] * b_re
    eye = jnp.eye(S5_GROUPS, dtype=F32)

    def blk_in(bb):
        t = jnp.einsum('dgpc,gh->dgchp', bb, eye)
        return t.reshape(2, S5_GROUPS * S5_GROUP_CH, S5_NP).astype(BF16)

    def blk_out(cc):
        t = jnp.einsum('dgcp,gh->dgphc', cc.astype(F32), eye)
        return t.reshape(2, S5_NP, S5_GROUPS * S5_GROUP_CH).astype(BF16)

    rows_r, rows_i = [], []
    for j in range(SUBLANES):
        pr_f, pi_f = apow(float(j + 1))
        pr_b, pi_b = apow(float(SUBLANES - j))
        rows_r.append(jnp.stack([pr_f[0], pr_b[1]]))
        rows_i.append(jnp.stack([pi_f[0], pi_b[1]]))
    for k in (1.0, 2.0, 4.0):
        pr_k, pi_k = apow(k)
        rows_r.append(pr_k)
        rows_i.append(pi_k)
    zero = jnp.zeros_like(rows_r[0])
    while len(rows_r) < 16:
        rows_r.append(zero)
        rows_i.append(zero)
    tab_r = jnp.stack(rows_r, axis=1).reshape(2, 16, S5_NP)
    tab_i = jnp.stack(rows_i, axis=1).reshape(2, 16, S5_NP)
    tab = jnp.stack([tab_r, tab_i], axis=1)
    return blk_in(bbr), blk_in(bbi), blk_out(c_re), blk_out(c_im), tab


def _hy_prep_kernel(u_ref, prev_ref, next_ref, w_ref, b_ref, x0_ref, vx_ref, *, tpb):
    i = pl.program_id(0)
    r = i % tpb
    first = (r == 0) | (r == tpb - 1)
    last = r >= tpb - 2
    u = u_ref[...]
    w = w_ref[...]
    up_row = jnp.where(first, 0.0, prev_ref[SUBLANES - 1:SUBLANES, :])
    dn_row = jnp.where(last, 0.0, next_ref[0:1, :])
    row = lax.broadcasted_iota(jnp.int32, (TM, 1), 0)
    um = jnp.where(row == 0, up_row, pltpu.roll(u, 1, 0))
    un = jnp.where(row == TM - 1, dn_row, pltpu.roll(u, TM - 1, 0))
    y = um * w[0:1, :] + u * w[1:2, :] + un * w[2:3, :] + b_ref[...]
    x0_ref[...] = y[:, 0:HY_W]
    vx_ref[...] = y[:, 2 * HY_W:3 * HY_W] * y[:, HY_W:2 * HY_W]


def hy_prep(z, conv_w, conv_b, tpb):
    m = z.shape[0]
    w3 = 3 * HY_W
    rb = TM // SUBLANES
    nrb = m // SUBLANES
    return pl.pallas_call(
        functools.partial(_hy_prep_kernel, tpb=tpb),
        grid=(m // TM,),
        in_specs=[pl.BlockSpec((TM, w3), lambda i: (i, 0)),
                  pl.BlockSpec((SUBLANES, w3), lambda i: (jnp.maximum(i * rb - 1, 0), 0)),
                  pl.BlockSpec((SUBLANES, w3), lambda i: (jnp.minimum((i + 1) * rb, nrb - 1), 0)),
                  pl.BlockSpec((3, w3), lambda i: (0, 0)),
                  pl.BlockSpec((1, w3), lambda i: (0, 0))],
        out_specs=[pl.BlockSpec((TM, HY_W), lambda i: (i, 0))] * 2,
        out_shape=[jax.ShapeDtypeStruct((m, HY_W), F32)] * 2,
        compiler_params=_cparams(("parallel",)),
        name="hy_prep",
    )(z, z, z, conv_w, conv_b.reshape(1, w3))


def _hy_filter_kernel(zf_ref, w1_ref, b1_ref, f1_ref, w2_ref, b2_ref, f2_ref, w3_ref, dl_ref,
                      hf_ref, hb_ref, ss_ref):
    i = pl.program_id(0)
    zf = zf_ref[...]
    hid = jnp.sin(f1_ref[...] * (_dot_hi(zf, w1_ref[...]) + b1_ref[...]))
    hid = jnp.sin(f2_ref[...] * (_dot_hi(hid, w2_ref[...]) + b2_ref[...]))
    filt = _dot_hi(hid, w3_ref[...])
    window = jnp.exp(-zf[:, 0:1] * dl_ref[...])
    hf = filt[:, 0:HY_W] * window
    hb = filt[:, HY_W:2 * HY_W] * window
    hf_ref[...] = hf
    hb_ref[...] = hb
    row = lax.broadcasted_iota(jnp.int32, (TM, 1), 0) + i * TM
    part = jnp.sum(hf * hf + jnp.where(row == 0, 0.0, hb * hb), axis=0, keepdims=True)

    @pl.when(i == 0)
    def _():
        ss_ref[...] = jnp.zeros_like(ss_ref)

    ss_ref[...] = ss_ref[...] + part


def hy_filter(length, w1, b1, fr1, w2, b2, fr2, w3):
    t = jnp.linspace(0.0, 1.0, length, dtype=F32)[:, None]
    wv = (2.0 * math.pi / length) * jnp.arange(length, dtype=F32)[:, None]
    f = jnp.linspace(1e-4, HY_BANDS - 1.0, HY_BANDS, dtype=F32)[None, :]
    zfeat = jnp.concatenate([t, jnp.cos(f * wv), -jnp.sin(f * wv)], axis=-1)
    zfeat = jnp.pad(zfeat, ((0, 0), (0, LANES - HY_EMB)))
    w1p = jnp.pad(w1, ((0, LANES - HY_EMB), (0, 0)))
    deltas = jnp.abs(jnp.linspace(math.log(HY_TARGET) / HY_SLOW_DECAY,
                                  math.log(HY_TARGET) / HY_FAST_DECAY, HY_W, dtype=F32))[None, :]
    full = lambda shape: pl.BlockSpec(shape, lambda i: (0,) * len(shape))
    hf, hb, ss = pl.pallas_call(
        _hy_filter_kernel,
        grid=(length // TM,),
        in_specs=[pl.BlockSpec((TM, LANES), lambda i: (i, 0)),
                  full((LANES, HY_ORDER)), full((1, HY_ORDER)), full((1, HY_ORDER)),
                  full((HY_ORDER, HY_ORDER)), full((1, HY_ORDER)), full((1, HY_ORDER)),
                  full((HY_ORDER, 2 * HY_W)), full((1, HY_W))],
        out_specs=[pl.BlockSpec((TM, HY_W), lambda i: (i, 0)),
                   pl.BlockSpec((TM, HY_W), lambda i: (i, 0)),
                   pl.BlockSpec((1, HY_W), lambda i: (0, 0))],
        out_shape=[jax.ShapeDtypeStruct((length, HY_W), F32)] * 2 + [jax.ShapeDtypeStruct((1, HY_W), F32)],
        compiler_params=_cparams(("arbitrary",)),
        name="hy_filter",
    )(zfeat, w1p, b1.reshape(1, -1), fr1.reshape(1, -1), w2, b2.reshape(1, -1), fr2.reshape(1, -1), w3, deltas)
    return hf, hb, ss


def _dft_consts(length):
    n = 2 * length
    ns = n // FFT_FAST
    a = jnp.arange(ns, dtype=jnp.int32)
    ang = ((a[:, None] * a[None, :]) % ns).astype(F32) * (2.0 * math.pi / ns)
    f_fwd = jnp.concatenate([jnp.cos(ang), -jnp.sin(ang)], axis=0)
    f_inv = jnp.concatenate([jnp.cos(ang), -jnp.sin(ang)], axis=1)
    b = jnp.arange(FFT_FAST, dtype=jnp.int32)
    kk = a[:, None, None] + ns * b[None, :, None]
    idx = (kk * b[None, None, :]) % n
    th = idx.astype(F32) * (2.0 * math.pi / n)
    mr, mi = jnp.cos(th), -jnp.sin(th)
    m_fwd = jnp.concatenate([jnp.concatenate([mr, -mi], axis=2),
                             jnp.concatenate([mi, mr], axis=2)], axis=1)
    ir = jnp.transpose(mr, (0, 2, 1)) / n
    ii = jnp.transpose(-mi, (0, 2, 1)) / n
    m_inv = jnp.concatenate([jnp.concatenate([ir, -ii], axis=2),
                             jnp.concatenate([ii, ir], axis=2)], axis=1)
    return tuple(t.astype(BF16) for t in (f_fwd, f_inv, m_fwd, m_inv))


def _dft_slow_kernel(f_ref, x_ref, o_ref):
    o_ref[0] = jnp.dot(f_ref[...], x_ref[0].astype(BF16), preferred_element_type=F32)


def dft_slow(fmat, x):
    bx, k, lw = x.shape
    mrows = fmat.shape[0]
    tl = min(lw, 4096)
    return pl.pallas_call(
        _dft_slow_kernel,
        grid=(bx, lw // tl),
        in_specs=[pl.BlockSpec((mrows, k), lambda b, j: (0, 0)),
                  pl.BlockSpec((1, k, tl), lambda b, j: (b, 0, j))],
        out_specs=pl.BlockSpec((1, mrows, tl), lambda b, j: (b, 0, j)),
        out_shape=jax.ShapeDtypeStruct((bx, mrows, lw), F32),
        compiler_params=_cparams(("parallel", "parallel")),
        name="dft_slow",
    )(fmat, x)


FFT_GROUP = 8


def _dft_fast_filter_kernel(m_ref, a_ref, o_ref):
    for j in range(FFT_GROUP):
        a = jnp.concatenate([a_ref[0, 0, j], a_ref[0, 1, j]], axis=0).astype(BF16)
        x = jnp.dot(m_ref[j], a, preferred_element_type=F32)
        o_ref[0, j] = x[0:FFT_FAST]
        o_ref[1, j] = x[FFT_FAST:]


def dft_fast_filter(m_fwd, a5):
    _, _, ns, nf, cw = a5.shape
    g = FFT_GROUP
    return pl.pallas_call(
        _dft_fast_filter_kernel,
        grid=(ns // g,),
        in_specs=[pl.BlockSpec((g, 2 * nf, 2 * nf), lambda i: (i, 0, 0)),
                  pl.BlockSpec((1, 2, g, nf, cw), lambda i: (0, 0, i, 0, 0))],
        out_specs=pl.BlockSpec((2, g, nf, cw), lambda i: (0, i, 0, 0)),
        out_shape=jax.ShapeDtypeStruct((2, ns, nf, cw), F32),
        compiler_params=_cparams(("parallel",)),
        name="dft_fast_filter",
    )(m_fwd, a5)


def _dft_fast_conv_kernel(mf_ref, mi_ref, a_ref, k_ref, o_ref):
    for j in range(FFT_GROUP):
        a = jnp.concatenate([a_ref[0, 0, j], a_ref[0, 1, j]], axis=0).astype(BF16)
        x = jnp.dot(mf_ref[j], a, preferred_element_type=F32)
        xr, xi = x[0:FFT_FAST], x[FFT_FAST:]
        kr, ki = k_ref[0, j], k_ref[1, j]
        y = jnp.concatenate([xr * kr - xi * ki, xr * ki + xi * kr], axis=0).astype(BF16)
        bv = jnp.dot(mi_ref[j], y, preferred_element_type=F32)
        o_ref[0, 0, j] = bv[0:FFT_FAST]
        o_ref[0, 1, j] = bv[FFT_FAST:]


def dft_fast_conv(m_fwd, m_inv, a5, kf):
    bx, _, ns, nf, cw = a5.shape
    g = FFT_GROUP
    return pl.pallas_call(
        _dft_fast_conv_kernel,
        grid=(bx, ns // g),
        in_specs=[pl.BlockSpec((g, 2 * nf, 2 * nf), lambda b, i: (i, 0, 0)),
                  pl.BlockSpec((g, 2 * nf, 2 * nf), lambda b, i: (i, 0, 0)),
                  pl.BlockSpec((1, 2, g, nf, cw), lambda b, i: (b, 0, i, 0, 0)),
                  pl.BlockSpec((2, g, nf, cw), lambda b, i: (0, i, 0, 0))],
        out_specs=pl.BlockSpec((1, 2, g, nf, cw), lambda b, i: (b, 0, i, 0, 0)),
        out_shape=jax.ShapeDtypeStruct((bx, 2, ns, nf, cw), F32),
        compiler_params=_cparams(("parallel", "parallel")),
        name="dft_fast_conv",
    )(m_fwd, m_inv, a5, kf)


def hyena_long_conv(vx_lat, filt2):
    bsz, length, cw = vx_lat.shape
    ns = 2 * length // FFT_FAST
    f_fwd, f_inv, m_fwd, m_inv = _dft_consts(length)
    fa = dft_slow(f_fwd, filt2.reshape(1, ns, FFT_FAST * cw))
    kf = dft_fast_filter(m_fwd, fa.reshape(1, 2, ns, FFT_FAST, cw))
    xa = dft_slow(f_fwd[:, :ns // 2], vx_lat.reshape(bsz, ns // 2, FFT_FAST * cw))
    bv = dft_fast_conv(m_fwd, m_inv, xa.reshape(bsz, 2, ns, FFT_FAST, cw), kf)
    y = dft_slow(f_inv[:ns // 2], bv.reshape(bsz, 2 * ns, FFT_FAST * cw))
    return y.reshape(bsz, length, cw)


def _hy_ctx_conv_kernel(v_ref, g_ref, o_ref, *, lc):
    nb = lc // SUBLANES

    def body(gi, acc):
        base = pl.multiple_of((nb - 1 - gi) * SUBLANES, SUBLANES)
        win = g_ref[pl.ds(base, lc + SUBLANES), :]
        for j in range(SUBLANES):
            vrow = v_ref[pl.ds(gi * SUBLANES + j, 1), :]
            acc = acc + win[SUBLANES - 1 - j:SUBLANES - 1 - j + lc, :] * vrow
        return acc

    o_ref[...] = lax.fori_loop(0, nb, body, jnp.zeros((lc, HY_W), F32))


def hy_ctx_conv(vx, gwin, bsz, tpb, lc):
    return pl.pallas_call(
        functools.partial(_hy_ctx_conv_kernel, lc=lc),
        grid=(bsz,),
        in_specs=[pl.BlockSpec((lc, HY_W), lambda b: (b * tpb + tpb - 1, 0)),
                  pl.BlockSpec((2 * lc, HY_W), lambda b: (0, 0))],
        out_specs=pl.BlockSpec((lc, HY_W), lambda b: (b, 0)),
        out_shape=jax.ShapeDtypeStruct((bsz * lc, HY_W), F32),
        compiler_params=_cparams(("parallel",)),
        name="hy_ctx_conv",
    )(vx, gwin)


ROUTE_E1, ROUTE_E2, ROUTE_W1, ROUTE_W2, ROUTE_R1, ROUTE_R2 = 0, 1, 2, 3, 4, 5
NEG_BIG = -1e30


def _gelu_tanh(x):
    return 0.5 * x * (1.0 + jnp.tanh(math.sqrt(2.0 / math.pi) * (x + 0.044715 * (x * x * x))))


def _lane_pick(slab, lane, idx):
    return jnp.sum(jnp.where(lane == idx, slab, 0.0), axis=-1, keepdims=True)


def _merge_kernel(x_ref, mod_ref, n1_ref, n2_ref, wg_ref, zhg_ref, zrt_ref, zs5_ref,
                  hgf_ref, hgb_ref, rtf_ref, rtb_ref, s5f_ref, s5b_ref, yc_ref, vx_ref, x0_ref,
                  vec_ref, glu1_ref, glu2_ref, wb_ref, wo_ref, wr_ref, br_ref,
                  xo_ref, h2_ref, route_ref, cnt_ref, base_ref, *, d):
    i = pl.program_id(0)
    w = BRANCH_W
    x = x_ref[...]
    mod = mod_ref[0]
    h = _modulated_norm(x, n1_ref[...], mod[:, 0:d], mod[:, d:2 * d]).astype(BF16)

    hd = w // HG_HEADS
    hr = lax.broadcasted_iota(jnp.int32, (w, w), 0) // hd
    hc = lax.broadcasted_iota(jnp.int32, (w, w), 1) // hd
    hm = jnp.where(hr == hc, 1.0, 0.0).astype(BF16)
    inv = 1.0 / hd

    hg = hgf_ref[...] + hgb_ref[...]
    ms = _split_dot(hg * hg, hm) * inv
    br_hg = hg * lax.rsqrt(ms + EPS) * vec_ref[0:1, :] * _silu(zhg_ref[...])

    rt = rtf_ref[...] + rtb_ref[...]
    cen = rt - _split_dot(rt, hm) * inv
    var = _split_dot(cen * cen, hm) * inv
    br_rt = cen * lax.rsqrt(var + EPS) * vec_ref[1:2, :] * _silu(zrt_ref[...])

    vx = vx_ref[...]
    br_hy = (yc_ref[...] + vx * vec_ref[3:4, :]) * x0_ref[...]

    y5 = _gelu_tanh(s5f_ref[...] + s5b_ref[...] + zs5_ref[...] * vec_ref[2:3, :]).astype(BF16)
    br_s5 = (jnp.dot(y5, glu1_ref[...], preferred_element_type=F32)
             * _sigmoid(jnp.dot(y5, glu2_ref[...], preferred_element_type=F32)))

    merged = jnp.zeros((TM, d), F32)
    for n, br in enumerate((br_hg, br_hy, br_rt, br_s5)):
        gate = _sigmoid(jnp.dot(h, wg_ref[:, n * d:(n + 1) * d], preferred_element_type=F32))
        merged = merged + gate * jnp.dot(br.astype(BF16), wb_ref[n], preferred_element_type=F32)
    mix = jnp.dot(merged.astype(BF16), wo_ref[...], preferred_element_type=F32)
    x_mid = x + mod[:, 2 * d:3 * d] * mix
    xo_ref[...] = x_mid
    h2 = _modulated_norm(x_mid, n2_ref[...], mod[:, 3 * d:4 * d], mod[:, 4 * d:5 * d])
    h2_ref[...] = h2

    logits = _dot_hi(h2, wr_ref[...]) + br_ref[...]
    lane = lax.broadcasted_iota(jnp.int32, (TM, LANES), 1).astype(F32)
    gl = jnp.where(lane < MOE_GROUPS, logits, NEG_BIG)
    gmax = jnp.max(gl, axis=-1, keepdims=True)
    grp_p = 1.0 / jnp.sum(jnp.exp(gl - gmax), axis=-1, keepdims=True)
    gi = jnp.min(jnp.where(gl == gmax, lane, float(LANES)), axis=-1, keepdims=True)
    lo = MOE_GROUPS + MOE_PER_GROUP * gi
    el = jnp.where((lane >= lo) & (lane < lo + MOE_PER_GROUP), logits, NEG_BIG)
    m1 = jnp.max(el, axis=-1, keepdims=True)
    i1 = jnp.min(jnp.where(el == m1, lane, float(LANES)), axis=-1, keepdims=True)
    el2 = jnp.where(lane == i1, NEG_BIG, el)
    m2 = jnp.max(el2, axis=-1, keepdims=True)
    i2 = jnp.min(jnp.where(el2 == m2, lane, float(LANES)), axis=-1, keepdims=True)
    ratio = jnp.exp(m2 - m1)
    wt1 = grp_p / (1.0 + ratio)
    wt2 = grp_p * ratio / (1.0 + ratio)
    e1 = i1 - MOE_GROUPS
    e2 = i2 - MOE_GROUPS

    @pl.when(i == 0)
    def _():
        base_ref[...] = jnp.zeros_like(base_ref)

    oh1 = jnp.where(lane == e1, 1.0, 0.0)
    oh2 = jnp.where(lane == e2, 1.0, 0.0)
    ri = lax.broadcasted_iota(jnp.int32, (TM, TM), 0)
    ci = lax.broadcasted_iota(jnp.int32, (TM, TM), 1)
    below = jnp.where(ci < ri, 1.0, 0.0).astype(BF16)
    c1 = jnp.dot(below, oh1.astype(BF16), preferred_element_type=F32)
    c2 = jnp.dot(below, oh2.astype(BF16), preferred_element_type=F32)
    tot1 = jnp.sum(oh1, axis=0, keepdims=True)
    tot2 = jnp.sum(oh2, axis=0, keepdims=True)
    base = base_ref[...]
    r1 = jnp.sum(oh1 * (base + c1), axis=-1, keepdims=True)
    r2 = jnp.sum(oh2 * (base + tot1 + c2), axis=-1, keepdims=True)
    base = base + tot1 + tot2
    base_ref[...] = base
    cnt_ref[...] = jnp.broadcast_to(base, cnt_ref.shape)
    route = jnp.zeros((TM, LANES), F32)
    for idx, val in ((ROUTE_E1, e1), (ROUTE_E2, e2), (ROUTE_W1, wt1), (ROUTE_W2, wt2),
                     (ROUTE_R1, r1), (ROUTE_R2, r2)):
        route = jnp.where(lane == idx, val, route)
    route_ref[...] = route


def merge(x_all, mod3, n1, n2, w_gate, z, hgf, hgb, rtf, rtb, s5f, s5b, yconv, vx, x0c,
          vecs, glu1, glu2, w_branch, w_out, w_router, b_router, tpb):
    m, d = x_all.shape
    w = BRANCH_W
    row = lambda width: pl.BlockSpec((TM, width), lambda i: (i, 0))
    zcol = lambda cb: pl.BlockSpec((TM, w), lambda i: (i, cb))
    full = lambda shape: pl.BlockSpec(shape, lambda i: (0,) * len(shape))
    return pl.pallas_call(
        functools.partial(_merge_kernel, d=d),
        grid=(m // TM,),
        in_specs=[row(d),
                  pl.BlockSpec((1, 1, 6 * d), lambda i: (_seg_of_tile(i, tpb), 0, 0)),
                  full((1, d)), full((1, d)), full((d, N_BRANCH * d)),
                  zcol(CB_HG_G), zcol(CB_RT_G), zcol(CB_S5)]
                 + [row(w)] * 9
                 + [full((SUBLANES, w)), full((w, w)), full((w, w)), full((N_BRANCH, w, d)), full((d, d)),
                    full((d, LANES)), full((1, LANES))],
        out_specs=[row(d), row(d), row(LANES), full((SUBLANES, LANES))],
        out_shape=[jax.ShapeDtypeStruct((m, d), F32), jax.ShapeDtypeStruct((m, d), F32),
                   jax.ShapeDtypeStruct((m, LANES), F32), jax.ShapeDtypeStruct((SUBLANES, LANES), F32)],
        scratch_shapes=[pltpu.VMEM((1, LANES), F32)],
        compiler_params=_cparams(("arbitrary",)),
        name="merge",
    )(x_all, mod3, n1, n2, w_gate, z, z, z, hgf, hgb, rtf, rtb, s5f, s5b, yconv, vx, x0c,
      vecs, glu1, glu2, w_branch, w_out, w_router, b_router)


def _row_copy(src_ref, src_row, dst_ref, dst_row, sem):
    return pltpu.make_async_copy(src_ref.at[pl.ds(src_row, 1), :], dst_ref.at[pl.ds(dst_row, 1), :], sem)


def _dispatch_kernel(dest_ref, h_ref, zero_ref, buf_ref, sem):
    del zero_ref
    base = pl.program_id(0) * TM

    def issue(r, carry):
        for k in range(2):
            _row_copy(h_ref, r, buf_ref, dest_ref[(base + r) * 2 + k], sem).start()
        return carry

    def drain(r, carry):
        for k in range(2):
            _row_copy(h_ref, r, buf_ref, dest_ref[(base + r) * 2 + k], sem).wait()
        return carry

    lax.fori_loop(0, TM, issue, 0)
    lax.fori_loop(0, TM, drain, 0)


def moe_dispatch(dest, h2, n_rows):
    m, d = h2.shape
    grid_spec = pltpu.PrefetchScalarGridSpec(
        num_scalar_prefetch=1,
        grid=(m // TM,),
        in_specs=[pl.BlockSpec((TM, d), lambda i, dest: (i, 0)),
                  pl.BlockSpec(memory_space=pl.ANY)],
        out_specs=pl.BlockSpec(memory_space=pl.ANY),
        scratch_shapes=[pltpu.SemaphoreType.DMA(())],
    )
    return pl.pallas_call(
        _dispatch_kernel,
        grid_spec=grid_spec,
        out_shape=jax.ShapeDtypeStruct((n_rows, d), F32),
        input_output_aliases={2: 0},
        compiler_params=_cparams(("arbitrary",)),
        name="moe_dispatch",
    )(dest, h2, jnp.zeros((n_rows, d), F32))


def _expert_kernel(be_ref, x_ref, w1_ref, w3_ref, w2_ref, y_ref):
    del be_ref
    xb = x_ref[...].astype(BF16)
    a = jnp.dot(xb, w1_ref[0].astype(BF16), preferred_element_type=F32)
    g = jnp.dot(xb, w3_ref[0].astype(BF16), preferred_element_type=F32)
    y_ref[...] = jnp.dot((_silu(a) * g).astype(BF16), w2_ref[0].astype(BF16), preferred_element_type=F32)


def moe_experts(block_expert, buf, w1, w3, w2):
    n_rows, d = buf.shape
    hid = w1.shape[-1]
    grid_spec = pltpu.PrefetchScalarGridSpec(
        num_scalar_prefetch=1,
        grid=(n_rows // MOE_BLOCK,),
        in_specs=[pl.BlockSpec((MOE_BLOCK, d), lambda j, be: (j, 0)),
                  pl.BlockSpec((1, d, hid), lambda j, be: (be[j], 0, 0)),
                  pl.BlockSpec((1, d, hid), lambda j, be: (be[j], 0, 0)),
                  pl.BlockSpec((1, hid, d), lambda j, be: (be[j], 0, 0))],
        out_specs=pl.BlockSpec((MOE_BLOCK, d), lambda j, be: (j, 0)),
    )
    return pl.pallas_call(
        _expert_kernel,
        grid_spec=grid_spec,
        out_shape=jax.ShapeDtypeStruct((n_rows, d), F32),
        compiler_params=_cparams(("arbitrary",)),
        name="moe_experts",
    )(block_expert, buf, w1, w3, w2)


def _combine_kernel(dest_ref, x_ref, route_ref, mod_ref, gfin_ref, y_ref, o_ref, gath, sem, *, d, final):
    base = pl.program_id(0) * TM

    def issue(r, carry):
        for k in range(2):
            _row_copy(y_ref, dest_ref[(base + r) * 2 + k], gath.at[k], r, sem).start()
        return carry

    def drain(r, carry):
        for k in range(2):
            _row_copy(y_ref, dest_ref[(base + r) * 2 + k], gath.at[k], r, sem).wait()
        return carry

    lax.fori_loop(0, TM, issue, 0)
    lax.fori_loop(0, TM, drain, 0)
    route = route_ref[...]
    lane = lax.broadcasted_iota(jnp.int32, (TM, LANES), 1)
    moe = _lane_pick(route, lane, ROUTE_W1) * gath[0] + _lane_pick(route, lane, ROUTE_W2) * gath[1]
    x_new = x_ref[...] + mod_ref[0][:, 5 * d:6 * d] * moe
    if final:
        ms = jnp.mean(x_new * x_new, axis=-1, keepdims=True)
        x_new = x_new * lax.rsqrt(ms + EPS) * gfin_ref[...]
    o_ref[...] = x_new


def moe_combine(dest, x_mid, route, mod3, gain_final, ybuf, tpb, final):
    m, d = x_mid.shape
    grid_spec = pltpu.PrefetchScalarGridSpec(
        num_scalar_prefetch=1,
        grid=(m // TM,),
        in_specs=[pl.BlockSpec((TM, d), lambda i, dest: (i, 0)),
                  pl.BlockSpec((TM, LANES), lambda i, dest: (i, 0)),
                  pl.BlockSpec((1, 1, 6 * d), lambda i, dest: (_seg_of_tile(i, tpb), 0, 0)),
                  pl.BlockSpec((1, d), lambda i, dest: (0, 0)),
                  pl.BlockSpec(memory_space=pl.ANY)],
        out_specs=pl.BlockSpec((TM, d), lambda i, dest: (i, 0)),
        scratch_shapes=[pltpu.VMEM((2, TM, d), F32), pltpu.SemaphoreType.DMA(())],
    )
    return pl.pallas_call(
        functools.partial(_combine_kernel, d=d, final=final),
        grid_spec=grid_spec,
        out_shape=jax.ShapeDtypeStruct((m, d), F32),
        compiler_params=_cparams(("arbitrary",)),
        name="moe_combine",
    )(dest, x_mid, route, mod3, gain_final, ybuf)


def moe_plan(route, counts_row, n_blocks):
    counts = counts_row[:MOE_EXPERTS].astype(jnp.int32)
    padded = (counts + MOE_BLOCK - 1) // MOE_BLOCK * MOE_BLOCK
    pad_end = jnp.cumsum(padded)
    pad_start = pad_end - padded
    expert = route[:, ROUTE_E1:ROUTE_E2 + 1].astype(jnp.int32)
    rank = route[:, ROUTE_R1:ROUTE_R2 + 1].astype(jnp.int32)
    dest = (pad_start[expert] + rank).reshape(-1)
    block_row0 = jnp.arange(n_blocks, dtype=jnp.int32) * MOE_BLOCK
    block_expert = jnp.minimum(jnp.sum((pad_end[None, :] <= block_row0[:, None]).astype(jnp.int32), axis=1),
                               MOE_EXPERTS - 1)
    return dest, block_expert


def _rope_tables(length, lc):
    half = RET_DK // 2
    freqs = ROPE_BASE ** (-jnp.arange(0, half, 2, dtype=F32) / half)
    t = jnp.arange(length)
    row_pos = (t // GRID_W).astype(F32)[:, None]
    col_pos = (t % GRID_W).astype(F32)[:, None]
    lane = jnp.arange(RET_HEADS * RET_DK)
    fr = freqs[lane % (half // 2)][None, :]
    ang = jnp.where(((lane % RET_DK) < half)[None, :], row_pos, col_pos) * fr
    cos = jnp.concatenate([jnp.cos(ang), jnp.ones((lc, lane.shape[0]), F32)], axis=0)
    sin = jnp.concatenate([jnp.sin(ang), jnp.zeros((lc, lane.shape[0]), F32)], axis=0)
    return cos, sin


def _rotate_half_cols(wq):
    quarter = RET_DK // 4
    lane = np.arange(wq.shape[1])
    first = (lane % (2 * quarter)) < quarter
    src = np.where(first, lane + quarter, lane - quarter)
    sign = jnp.asarray(np.where(first, -1.0, 1.0), F32)
    return wq[:, src] * sign


def kernel(x, c, ctx, c_ctx, w_mod, b_mod, norm1, norm2, w_in, hg_lb_logits, hg_norm, hy_conv_w, hy_conv_b, hy_filt_w1, hy_filt_b1, hy_filt_freq1, hy_filt_w2, hy_filt_b2, hy_filt_freq2, hy_filt_w3, hy_skip, ret_decay_logit, ret_norm, s5_lam_re, s5_lam_im, s5_log_dt, s5_b_re, s5_b_im, s5_c_re, s5_c_im, s5_d, s5_glu_w1, s5_glu_w2, w_branch, w_out, moe_w_grp, moe_b_grp, moe_w_exp, moe_b_exp, moe_w1, moe_w3, moe_w2, norm_final):
    bsz, length, d = x.shape
    lc = ctx.shape[1]
    depth = w_mod.shape[0]
    assert lc == TM and length % TM == 0 and (2 * length) % (FFT_FAST * FFT_GROUP) == 0
    tt = length + lc
    tpb = tt // TM
    m = bsz * tt
    n_blocks = -(-(m * 2) // MOE_BLOCK) + MOE_EXPERTS
    w = BRANCH_W

    x_all = jnp.concatenate([x, ctx], axis=1).reshape(m, d)
    cmat = jnp.zeros((SUBLANES, d), F32).at[0].set(c_ctx).at[1:1 + bsz].set(c)
    mods = modvec(cmat, w_mod, b_mod)
    lb_sm = jax.nn.softmax(hg_lb_logits.astype(F32), axis=0)
    lb_all = jnp.cumsum(lb_sm, axis=0) - lb_sm[0]
    cos_tab, sin_tab = _rope_tables(length, lc)
    log_gamma = jax.nn.log_sigmoid(ret_decay_logit.astype(F32))
    gfin = norm_final.reshape(1, d)

    for l in range(depth):
        wl = w_in[l]
        hy0 = 5 * w
        rq0 = hy0 + 3 * HY_W
        nq = RET_HEADS * RET_DK
        w_mix = jnp.concatenate(
            [wl[:, hy0:rq0], wl[:, 0:hy0], wl[:, rq0:N_MIX],
             _rotate_half_cols(wl[:, rq0:rq0 + nq]), _rotate_half_cols(wl[:, rq0 + nq:rq0 + 2 * nq])],
            axis=1).astype(BF16)
        w_gate = wl[:, N_MIX:].astype(BF16)
        mod3 = mods[l].reshape(SUBLANES, 1, 6 * d)

        z = inproj(x_all, mod3, norm1[l].reshape(1, d), w_mix, tpb)
        hgf, hgb = hgrn(z, lb_all[l], bsz, tpb)
        rtf, rtb = retention(z, cos_tab, sin_tab, log_gamma[l], bsz, tpb)
        s5f, s5b = s5(z, *s5_params(s5_lam_re[l], s5_lam_im[l], s5_log_dt[l], s5_b_re[l], s5_b_im[l],
                                    s5_c_re[l], s5_c_im[l]), bsz, tpb)

        x0c, vx = hy_prep(z, hy_conv_w[l], hy_conv_b[l], tpb)
        fargs = (hy_filt_w1[l], hy_filt_b1[l], hy_filt_freq1[l], hy_filt_w2[l], hy_filt_b2[l],
                 hy_filt_freq2[l], hy_filt_w3[l])
        hf, hb, ss = hy_filter(length, *fargs)
        filt2 = jnp.concatenate([hf, jnp.zeros((1, HY_W), F32), hb[1:][::-1]], axis=0) * lax.rsqrt(ss + EPS)
        vx3 = vx.reshape(bsz, tt, HY_W)
        y_lat = hyena_long_conv(vx3[:, :length], filt2)
        hfc, hbc, ssc = hy_filter(lc, *fargs)
        gwin = jnp.concatenate([hbc[1:][::-1], hfc, jnp.zeros((1, HY_W), F32)], axis=0) * lax.rsqrt(ssc + EPS)
        y_ctx = hy_ctx_conv(vx, gwin, bsz, tpb, lc).reshape(bsz, lc, HY_W)
        yconv = jnp.concatenate([y_lat, y_ctx], axis=1).reshape(m, HY_W)

        vecs = jnp.zeros((SUBLANES, w), F32).at[0].set(hg_norm[l]).at[1].set(ret_norm[l]) \
            .at[2].set(s5_d[l]).at[3].set(hy_skip[l])
        w_router = jnp.zeros((d, LANES), F32).at[:, :MOE_GROUPS].set(moe_w_grp[l]) \
            .at[:, MOE_GROUPS:MOE_GROUPS + MOE_EXPERTS].set(moe_w_exp[l])
        b_router = jnp.zeros((1, LANES), F32).at[0, :MOE_GROUPS].set(moe_b_grp[l]) \
            .at[0, MOE_GROUPS:MOE_GROUPS + MOE_EXPERTS].set(moe_b_exp[l])
        x_mid, h2, route, counts = merge(
            x_all, mod3, norm1[l].reshape(1, d), norm2[l].reshape(1, d), w_gate, z,
            hgf, hgb, rtf, rtb, s5f, s5b, yconv, vx, x0c, vecs,
            s5_glu_w1[l].astype(BF16), s5_glu_w2[l].astype(BF16), w_branch[l].astype(BF16),
            w_out[l].astype(BF16), w_router, b_router, tpb)

        dest, block_expert = moe_plan(route, counts[0], n_blocks)
        buf = moe_dispatch(dest, h2, n_blocks * MOE_BLOCK)
        ybuf = moe_experts(block_expert, buf, moe_w1[l], moe_w3[l], moe_w2[l])
        x_all = moe_combine(dest, x_mid, route, mod3, gfin, ybuf, tpb, l == depth - 1)

    return x_all.reshape(bsz, tt, d)[:, :length]
```

```python
import functools
import math

import numpy as np
import jax
import jax.numpy as jnp
from jax import lax
from jax.experimental import pallas as pl
from jax.experimental.pallas import tpu as pltpu

F32 = jnp.float32
BF16 = jnp.bfloat16
HIGHEST = lax.Precision.HIGHEST

GRID_W = 64
N_BRANCH = 4
BRANCH_W = 256
HG_HEADS = 4
HG_CHUNK = 16
HY_W = 256
HY_EMB = 33
HY_BANDS = 16
HY_ORDER = 64
HY_FAST_DECAY = 0.3
HY_SLOW_DECAY = 1.5
HY_TARGET = 1e-2
RET_HEADS = 4
RET_DK = 32
RET_DV = 64
ROPE_BASE = 10000.0
S5_GROUP_CH = 16
S5_GROUPS = 16
S5_STATE = 64
MOE_GROUPS = 4
MOE_PER_GROUP = 8
MOE_EXPERTS = 32
MOE_HIDDEN = 512
MOE_BLOCK = 256
EPS = 1e-6

LANES = 128
SUBLANES = 8
TM = 256
FFT_FAST = 128
VMEM_LIMIT = 56 * 1024 * 1024

N_MIX = 3072
N_ZCOLS = N_MIX + 2 * RET_HEADS * RET_DK
CB_HG_Q, CB_HG_FF, CB_HG_FB, CB_HG_I, CB_HG_G = 3, 4, 5, 6, 7
CB_RT_V, CB_RT_G, CB_S5 = 9, 10, 11
CB128_RT_Q, CB128_RT_K, CB128_RT_QR, CB128_RT_KR = 16, 17, 24, 25


def _cparams(sem):
    return pltpu.CompilerParams(dimension_semantics=sem, vmem_limit_bytes=VMEM_LIMIT)


def _sigmoid(x):
    return 1.0 / (1.0 + jnp.exp(-x))


def _silu(x):
    return x * _sigmoid(x)


def _dot(a, b):
    return jnp.dot(a.astype(BF16), b.astype(BF16), preferred_element_type=F32)


def _dot_nt(a, b):
    return lax.dot_general(a.astype(BF16), b.astype(BF16), (((1,), (1,)), ((), ())),
                           preferred_element_type=F32)


def _dot_hi(a, b):
    return jnp.dot(a, b, precision=HIGHEST, preferred_element_type=F32)


def _split_dot(x, m_bf16):
    hi = x.astype(BF16)
    lo = (x - hi.astype(F32)).astype(BF16)
    return (jnp.dot(hi, m_bf16, preferred_element_type=F32)
            + jnp.dot(lo, m_bf16, preferred_element_type=F32))


def _split_dot_left(m_bf16, x):
    hi = x.astype(BF16)
    lo = (x - hi.astype(F32)).astype(BF16)
    return (jnp.dot(m_bf16, hi, preferred_element_type=F32)
            + jnp.dot(m_bf16, lo, preferred_element_type=F32))


def _seg_of_tile(i, tpb):
    return jnp.where(i % tpb == tpb - 1, 0, 1 + i // tpb)


def _scan_tiles(b, s, tpb):
    last = tpb - 1
    fwd = jnp.where(s == 0, last, s - 1)
    bwd = jnp.where(s == 0, last, last - s)
    return b * tpb + fwd, b * tpb + bwd


def _modvec_kernel(c_ref, w_ref, b_ref, o_ref):
    c = c_ref[...]
    o_ref[0] = _dot_hi(_silu(c), w_ref[0]) + b_ref[0]


def modvec(cmat, w_mod, b_mod):
    depth, d, n6 = w_mod.shape
    nb = n6 // d
    return pl.pallas_call(
        _modvec_kernel,
        grid=(depth, nb),
        in_specs=[pl.BlockSpec((SUBLANES, d), lambda l, j: (0, 0)),
                  pl.BlockSpec((1, d, d), lambda l, j: (l, 0, j)),
                  pl.BlockSpec((1, 1, d), lambda l, j: (l, 0, j))],
        out_specs=pl.BlockSpec((1, SUBLANES, d), lambda l, j: (l, 0, j)),
        out_shape=jax.ShapeDtypeStruct((depth, SUBLANES, n6), F32),
        compiler_params=_cparams(("arbitrary", "arbitrary")),
        name="modvec",
    )(cmat, w_mod, b_mod.reshape(depth, 1, n6))


def _modulated_norm(x, gain, shift, scale):
    ms = jnp.mean(x * x, axis=-1, keepdims=True)
    return (x * lax.rsqrt(ms + EPS) * gain) * (1.0 + scale) + shift


def _inproj_kernel(x_ref, mod_ref, g_ref, w_ref, wrot_ref, z_ref, *, d):
    h = _modulated_norm(x_ref[...], g_ref[...], mod_ref[0, :, 0:d], mod_ref[0, :, d:2 * d]).astype(BF16)
    hy0 = 5 * BRANCH_W
    hy1 = hy0 + 3 * HY_W
    z_ref[:, 0:hy1 - hy0] = jnp.dot(h, w_ref[:, hy0:hy1], preferred_element_type=F32)
    z_ref[:, hy1 - hy0:hy1] = jnp.dot(h, w_ref[:, 0:hy0], preferred_element_type=F32)
    z_ref[:, hy1:N_MIX] = jnp.dot(h, w_ref[:, hy1:N_MIX], preferred_element_type=F32)
    z_ref[:, N_MIX:N_ZCOLS] = jnp.dot(h, wrot_ref[...], preferred_element_type=F32)


def inproj(x_all, mod3, gain, w_in_bf, layer, w_rot, tpb):
    m, d = x_all.shape
    return pl.pallas_call(
        functools.partial(_inproj_kernel, d=d),
        grid=(m // TM,),
        in_specs=[pl.BlockSpec((TM, d), lambda i: (i, 0)),
                  pl.BlockSpec((1, 1, 6 * d), lambda i: (_seg_of_tile(i, tpb), 0, 0)),
                  pl.BlockSpec((1, d), lambda i: (0, 0)),
                  pl.BlockSpec((None, d, N_MIX), lambda i: (layer, 0, 0)),
                  pl.BlockSpec((d, N_ZCOLS - N_MIX), lambda i: (0, 0))],
        out_specs=pl.BlockSpec((TM, N_ZCOLS), lambda i: (i, 0)),
        out_shape=jax.ShapeDtypeStruct((m, N_ZCOLS), F32),
        compiler_params=_cparams(("parallel",)),
        name="inproj",
    )(x_all, mod3, gain, w_in_bf, w_rot)


def _head_masked_rows(x, nchunk, c, hd):
    w = x.shape[-1]
    x3 = x.reshape(nchunk, c, w)
    lane_head = lax.broadcasted_iota(jnp.int32, (1, 1, w), 2) // hd
    parts = [jnp.where(lane_head == h, x3, 0.0) for h in range(w // hd)]
    return jnp.concatenate(parts, axis=1).reshape(nchunk * (w // hd) * c, w)


def _hgrn_direction(q_ref, f_ref, v_ref, lb, tri_ref, o_ref, st_ref, b_s, k_s, qe_s, ke_s, reverse):
    c = HG_CHUNK
    w = BRANCH_W
    nchunk = TM // c
    nh = HG_HEADS
    hd = w // nh
    half = c // 2
    z = f_ref[...]
    sg = _sigmoid(z)
    log2_f = jnp.log2(lb + (1.0 - lb) * sg)
    k = (1.0 - lb) * (1.0 - sg)
    b2 = _split_dot_left(tri_ref[0], log2_f)
    rem2 = _split_dot_left(tri_ref[1], log2_f)
    b_s[...] = b2
    k_s[...] = k
    qe_s[...] = _head_masked_rows(q_ref[...] * jnp.exp2(b2), nchunk, c, hd).astype(BF16)
    ke_s[...] = _head_masked_rows(k * jnp.exp2(rem2), nchunk, c, hd).astype(BF16)

    head_r = lax.broadcasted_iota(jnp.int32, (w, w), 0) // hd
    head_c = lax.broadcasted_iota(jnp.int32, (w, w), 1) // hd
    hm = jnp.where(head_r == head_c, 1.0, 0.0).astype(BF16)
    rows = lax.broadcasted_iota(jnp.int32, (half, 1), 0)

    pieces = []
    for s in range(c):
        for g in range(2):
            lo, hi = g * half, (g + 1) * half - 1
            if (hi < s) if not reverse else (lo > s):
                continue
            full = (lo >= s) if not reverse else (hi <= s)
            pieces.append((s, g, full))

    def chunk_step(n):
        cidx = (nchunk - 1 - n) if reverse else n
        off = pl.multiple_of(cidx * c, c)
        eoff = pl.multiple_of(cidx * (nh * c), nh * c)
        qc = q_ref[pl.ds(off, c), :]
        bc = b_s[pl.ds(off, c), :]
        kc = k_s[pl.ds(off, c), :]
        vc = v_ref[pl.ds(off, c), :]
        parts = []
        for s, g, full in pieces:
            sl = slice(g * half, (g + 1) * half)
            e = jnp.exp2(bc[sl] - bc[s:s + 1, :])
            if not full:
                t = rows + g * half
                e = jnp.where((t <= s) if reverse else (t >= s), e, 0.0)
            parts.append((qc[sl] * kc[s:s + 1, :]) * e)
        p = jnp.concatenate(parts, axis=0).astype(BF16)
        r = jnp.dot(p, hm, preferred_element_type=F32)
        og = [jnp.zeros((half, w), F32), jnp.zeros((half, w), F32)]
        for i, (s, g, _) in enumerate(pieces):
            og[g] = og[g] + r[i * half:(i + 1) * half, :] * vc[s:s + 1, :]
        st = st_ref[...]
        ri = _dot_nt(qe_s[pl.ds(eoff, nh * c), :], st)
        inter = jnp.concatenate([ri[h * c:(h + 1) * c, :] for h in range(nh)], axis=1)
        o_ref[pl.ds(off, c), :] = jnp.concatenate(og, axis=0) + inter
        vexp = jnp.concatenate([vc[:, h * hd:(h + 1) * hd] for h in range(nh)], axis=0)
        kv = lax.dot_general(vexp.astype(BF16), ke_s[pl.ds(eoff, nh * c), :], (((0,), (0,)), ((), ())),
                             preferred_element_type=F32)
        blast = bc[0:1, :] if reverse else bc[c - 1:c, :]
        st_ref[...] = jnp.exp2(blast) * st + kv

    return chunk_step


def _hgrn_kernel(qf_ref, ff_ref, vf_ref, qb_ref, fb_ref, vb_ref, lb_ref, trif_ref, trib_ref, of_ref, ob_ref,
                 stf_ref, stb_ref, bf_s, kf_s, qef_s, kef_s, bb_s, kb_s, qeb_s, keb_s):
    @pl.when(pl.program_id(1) == 0)
    def _():
        stf_ref[...] = jnp.zeros_like(stf_ref)
        stb_ref[...] = jnp.zeros_like(stb_ref)

    step_f = _hgrn_direction(qf_ref, ff_ref, vf_ref, lb_ref[0:1, :], trif_ref, of_ref, stf_ref,
                             bf_s, kf_s, qef_s, kef_s, False)
    step_b = _hgrn_direction(qb_ref, fb_ref, vb_ref, lb_ref[1:2, :], trib_ref, ob_ref, stb_ref,
                             bb_s, kb_s, qeb_s, keb_s, True)

    def body(n, carry):
        step_f(n)
        step_b(n)
        return carry

    lax.fori_loop(0, TM // HG_CHUNK, body, 0, unroll=2)


def hgrn(z, lb, bsz, tpb):
    m = z.shape[0]
    w = BRANCH_W
    hd = w // HG_HEADS
    ri = jnp.arange(TM)[:, None]
    ci = jnp.arange(TM)[None, :]
    same = (ri // HG_CHUNK) == (ci // HG_CHUNK)
    tri = jnp.stack([same & (ci <= ri), same & (ci > ri), same & (ci >= ri), same & (ci < ri)])
    tri = tri.astype(BF16).reshape(2, 2, TM, TM)

    def spec(cb, d):
        return pl.BlockSpec((TM, w), lambda b, s: (_scan_tiles(b, s, tpb)[d], cb))

    out_spec = [pl.BlockSpec((TM, w), lambda b, s: (_scan_tiles(b, s, tpb)[0], 0)),
                pl.BlockSpec((TM, w), lambda b, s: (_scan_tiles(b, s, tpb)[1], 0))]
    return pl.pallas_call(
        _hgrn_kernel,
        grid=(bsz, tpb),
        in_specs=[spec(CB_HG_Q, 0), spec(CB_HG_FF, 0), spec(CB_HG_I, 0),
                  spec(CB_HG_Q, 1), spec(CB_HG_FB, 1), spec(CB_HG_I, 1),
                  pl.BlockSpec((2, w), lambda b, s: (0, 0)),
                  pl.BlockSpec((None, 2, TM, TM), lambda b, s: (0, 0, 0, 0)),
                  pl.BlockSpec((None, 2, TM, TM), lambda b, s: (1, 0, 0, 0))],
        out_specs=out_spec,
        out_shape=[jax.ShapeDtypeStruct((m, w), F32)] * 2,
        scratch_shapes=[pltpu.VMEM((hd, w), F32), pltpu.VMEM((hd, w), F32)]
                       + [pltpu.VMEM((TM, w), F32), pltpu.VMEM((TM, w), F32),
                          pltpu.VMEM((HG_HEADS * TM, w), BF16), pltpu.VMEM((HG_HEADS * TM, w), BF16)] * 2,
        compiler_params=_cparams(("arbitrary", "arbitrary")),
        name="hgrn2",
    )(z, z, z, z, z, z, lb, tri, tri)


def _ret_direction(q_ref, k_ref, qr_ref, kr_ref, v_ref, cos_ref, sin_ref, lgq, lg_ref, d, o_ref, st_ref,
                   reverse):
    nq = RET_HEADS * RET_DK
    nv = RET_HEADS * RET_DV
    cos = cos_ref[...]
    sin = sin_ref[...]
    qx = q_ref[...] * cos + qr_ref[...] * sin
    kx = (k_ref[...] * cos + kr_ref[...] * sin) * (RET_DK ** -0.5)
    v = v_ref[...]
    t = lax.broadcasted_iota(jnp.int32, (TM, 1), 0).astype(F32)
    pos = (TM - 1.0 - t) if reverse else t
    qd = qx * jnp.exp((pos + 1.0) * lgq)
    kd = kx * jnp.exp((TM - 1.0 - pos) * lgq)
    ti = lax.broadcasted_iota(jnp.int32, (TM, TM), 0)
    si = lax.broadcasted_iota(jnp.int32, (TM, TM), 1)
    rel = ((si - ti) if reverse else (ti - si)).astype(F32)
    live = rel >= 0.0
    relc = jnp.where(live, rel, 0.0)
    lane_q = lax.broadcasted_iota(jnp.int32, (1, nq), 1) // RET_DK
    lane_v = lax.broadcasted_iota(jnp.int32, (1, nv), 1) // RET_DV
    st = st_ref[...]
    o = _dot_nt(qd, st)
    vb = v.astype(BF16)
    for h in range(RET_HEADS):
        dmat = jnp.where(live, jnp.exp(relc * lg_ref[d, h]), 0.0)
        sc = _dot_nt(jnp.where(lane_q == h, qx, 0.0), kx) * dmat
        oh = jnp.dot(sc.astype(BF16), vb, preferred_element_type=F32)
        o = o + jnp.where(lane_v == h, oh, 0.0)
    o_ref[...] = o
    kv = _dot(v.T, kd)
    hr = lax.broadcasted_iota(jnp.int32, (nv, nq), 0) // RET_DV
    hc = lax.broadcasted_iota(jnp.int32, (nv, nq), 1) // RET_DK
    st_ref[...] = st * jnp.exp(float(TM) * lgq) + jnp.where(hr == hc, kv, 0.0)


def _ret_kernel(lg_ref, qf, kf, qrf, krf, vf, cosf, sinf, qb, kb, qrb, krb, vb, cosb, sinb, lgq_ref,
                of_ref, ob_ref, stf_ref, stb_ref):
    @pl.when(pl.program_id(1) == 0)
    def _():
        stf_ref[...] = jnp.zeros_like(stf_ref)
        stb_ref[...] = jnp.zeros_like(stb_ref)

    _ret_direction(qf, kf, qrf, krf, vf, cosf, sinf, lgq_ref[0:1, :], lg_ref, 0, of_ref, stf_ref, False)
    _ret_direction(qb, kb, qrb, krb, vb, cosb, sinb, lgq_ref[1:2, :], lg_ref, 1, ob_ref, stb_ref, True)


def retention(z, cos_tab, sin_tab, log_gamma, bsz, tpb):
    m = z.shape[0]
    nq = RET_HEADS * RET_DK
    nv = RET_HEADS * RET_DV
    lgq = jnp.repeat(log_gamma, RET_DK, axis=1)

    def zspec(cb, width, d):
        return pl.BlockSpec((TM, width), lambda b, s: (_scan_tiles(b, s, tpb)[d], cb))

    def tspec(d):
        return pl.BlockSpec((TM, nq), lambda b, s: (_scan_tiles(0, s, tpb)[d], 0))

    def side(d):
        return [zspec(CB128_RT_Q, nq, d), zspec(CB128_RT_K, nq, d), zspec(CB128_RT_QR, nq, d),
                zspec(CB128_RT_KR, nq, d), zspec(CB_RT_V, nv, d), tspec(d), tspec(d)]

    return pl.pallas_call(
        _ret_kernel,
        grid=(bsz, tpb),
        in_specs=[pl.BlockSpec(memory_space=pltpu.SMEM)] + side(0) + side(1)
                 + [pl.BlockSpec((2, nq), lambda b, s: (0, 0))],
        out_specs=[pl.BlockSpec((TM, nv), lambda b, s: (_scan_tiles(b, s, tpb)[0], 0)),
                   pl.BlockSpec((TM, nv), lambda b, s: (_scan_tiles(b, s, tpb)[1], 0))],
        out_shape=[jax.ShapeDtypeStruct((m, nv), F32)] * 2,
        scratch_shapes=[pltpu.VMEM((nv, nq), F32), pltpu.VMEM((nv, nq), F32)],
        compiler_params=_cparams(("arbitrary", "arbitrary")),
        name="retention",
    )(log_gamma, z, z, z, z, z, cos_tab, sin_tab, z, z, z, z, z, cos_tab, sin_tab, lgq)


S5_NP = S5_GROUPS * S5_STATE


def _s5_direction(u_ref, bre_ref, bim_ref, cre_ref, cim_ref, tab_ref, o_ref, car_ref, xr_s, xi_s, reverse):
    nb = TM // SUBLANES
    ub = u_ref[...].astype(BF16)
    xr = jnp.dot(ub, bre_ref[0], preferred_element_type=F32).reshape(nb, SUBLANES, S5_NP)
    xi = jnp.dot(ub, bim_ref[0], preferred_element_type=F32).reshape(nb, SUBLANES, S5_NP)
    sub = lax.broadcasted_iota(jnp.int32, (1, SUBLANES, 1), 1)
    for k, row in ((1, 8), (2, 9), (4, 10)):
        ar = tab_ref[0, 0, row:row + 1, :]
        ai = tab_ref[0, 1, row:row + 1, :]
        shift = (SUBLANES - k) if reverse else k
        keep = (sub < SUBLANES - k) if reverse else (sub >= k)
        sr = jnp.where(keep, pltpu.roll(xr, shift, 1), 0.0)
        si = jnp.where(keep, pltpu.roll(xi, shift, 1), 0.0)
        xr, xi = xr + ar * sr - ai * si, xi + ar * si + ai * sr
    xr_s[...] = xr.reshape(TM, S5_NP)
    xi_s[...] = xi.reshape(TM, S5_NP)
    pr = tab_ref[0, 0, 0:SUBLANES, :]
    pi = tab_ref[0, 1, 0:SUBLANES, :]

    def body(n, carry):
        cr, ci = carry
        blk = (nb - 1 - n) if reverse else n
        off = pl.multiple_of(blk * SUBLANES, SUBLANES)
        br = xr_s[pl.ds(off, SUBLANES), :] + pr * cr - pi * ci
        bi = xi_s[pl.ds(off, SUBLANES), :] + pr * ci + pi * cr
        xr_s[pl.ds(off, SUBLANES), :] = br
        xi_s[pl.ds(off, SUBLANES), :] = bi
        if reverse:
            return br[0:1, :], bi[0:1, :]
        return br[SUBLANES - 1:SUBLANES, :], bi[SUBLANES - 1:SUBLANES, :]

    cr, ci = lax.fori_loop(0, nb, body, (car_ref[0:1, :], car_ref[1:2, :]))
    car_ref[0:1, :] = cr
    car_ref[1:2, :] = ci
    o_ref[...] = (jnp.dot(xr_s[...].astype(BF16), cre_ref[0], preferred_element_type=F32)
                  - jnp.dot(xi_s[...].astype(BF16), cim_ref[0], preferred_element_type=F32))


def _s5_kernel(uf, ub, bre, bim, cre, cim, tabf, tabb, bre2, bim2, cre2, cim2, of_ref, ob_ref,
               carf, carb, xr_s, xi_s):
    @pl.when(pl.program_id(1) == 0)
    def _():
        carf[...] = jnp.zeros_like(carf)
        carb[...] = jnp.zeros_like(carb)

    _s5_direction(uf, bre, bim, cre, cim, tabf, of_ref, carf, xr_s, xi_s, False)
    _s5_direction(ub, bre2, bim2, cre2, cim2, tabb, ob_ref, carb, xr_s, xi_s, True)


def s5(z, bblk_re, bblk_im, cblk_re, cblk_im, tab, bsz, tpb):
    m = z.shape[0]
    w = BRANCH_W

    def pspec(shape, d):
        return pl.BlockSpec((1,) + shape, lambda b, s: (d,) + (0,) * len(shape))

    def uspec(d):
        return pl.BlockSpec((TM, w), lambda b, s: (_scan_tiles(b, s, tpb)[d], CB_S5))

    return pl.pallas_call(
        _s5_kernel,
        grid=(bsz, tpb),
        in_specs=[uspec(0), uspec(1),
                  pspec((w, S5_NP), 0), pspec((w, S5_NP), 0), pspec((S5_NP, w), 0), pspec((S5_NP, w), 0),
                  pspec((2, 16, S5_NP), 0), pspec((2, 16, S5_NP), 1),
                  pspec((w, S5_NP), 1), pspec((w, S5_NP), 1), pspec((S5_NP, w), 1), pspec((S5_NP, w), 1)],
        out_specs=[pl.BlockSpec((TM, w), lambda b, s: (_scan_tiles(b, s, tpb)[0], 0)),
                   pl.BlockSpec((TM, w), lambda b, s: (_scan_tiles(b, s, tpb)[1], 0))],
        out_shape=[jax.ShapeDtypeStruct((m, w), F32)] * 2,
        scratch_shapes=[pltpu.VMEM((2, S5_NP), F32), pltpu.VMEM((2, S5_NP), F32),
                        pltpu.VMEM((TM, S5_NP), F32), pltpu.VMEM((TM, S5_NP), F32)],
        compiler_params=_cparams(("arbitrary", "arbitrary")),
        name="s5",
    )(z, z, bblk_re, bblk_im, cblk_re, cblk_im, tab, tab, bblk_re, bblk_im, cblk_re, cblk_im)


def s5_params(lam_re, lam_im, log_dt, b_re, b_im, c_re, c_im):
    lr = jnp.minimum(lam_re.astype(F32), -1e-4)
    li = lam_im.astype(F32)
    dt = jnp.exp(log_dt.astype(F32))[..., None]

    def apow(k):
        mag = jnp.exp(k * lr * dt)
        return mag * jnp.cos(k * li * dt), mag * jnp.sin(k * li * dt)

    ar, ai = apow(1.0)
    den = lr * lr + li * li
    gr = ((ar - 1.0) * lr + ai * li) / den
    gi = (ai * lr - (ar - 1.0) * li) / den
    bbr = gr[..., None] * b_re - gi[..., None] * b_im
    bbi = gr[..., None] * b_im + gi[..., None] * b_re
    eye = jnp.eye(S5_GROUPS, dtype=F32)

    def blk_in(bb):
        t = jnp.einsum('dgpc,gh->dgchp', bb, eye)
        return t.reshape(2, S5_GROUPS * S5_GROUP_CH, S5_NP).astype(BF16)

    def blk_out(cc):
        t = jnp.einsum('dgcp,gh->dgphc', cc.astype(F32), eye)
        return t.reshape(2, S5_NP, S5_GROUPS * S5_GROUP_CH).astype(BF16)

    rows_r, rows_i = [], []
    for j in range(SUBLANES):
        pr_f, pi_f = apow(float(j + 1))
        pr_b, pi_b = apow(float(SUBLANES - j))
        rows_r.append(jnp.stack([pr_f[0], pr_b[1]]))
        rows_i.append(jnp.stack([pi_f[0], pi_b[1]]))
    for k in (1.0, 2.0, 4.0):
        pr_k, pi_k = apow(k)
        rows_r.append(pr_k)
        rows_i.append(pi_k)
    zero = jnp.zeros_like(rows_r[0])
    while len(rows_r) < 16:
        rows_r.append(zero)
        rows_i.append(zero)
    tab_r = jnp.stack(rows_r, axis=1).reshape(2, 16, S5_NP)
    tab_i = jnp.stack(rows_i, axis=1).reshape(2, 16, S5_NP)
    tab = jnp.stack([tab_r, tab_i], axis=1)
    return blk_in(bbr), blk_in(bbi), blk_out(c_re), blk_out(c_im), tab


def _hy_prep_kernel(u_ref, prev_ref, next_ref, w_ref, b_ref, x0_ref, vx_ref, *, tpb):
    i = pl.program_id(0)
    r = i % tpb
    first = (r == 0) | (r == tpb - 1)
    last = r >= tpb - 2
    u = u_ref[...]
    w = w_ref[...]
    up_row = jnp.where(first, 0.0, prev_ref[SUBLANES - 1:SUBLANES, :])
    dn_row = jnp.where(last, 0.0, next_ref[0:1, :])
    row = lax.broadcasted_iota(jnp.int32, (TM, 1), 0)
    um = jnp.where(row == 0, up_row, pltpu.roll(u, 1, 0))
    un = jnp.where(row == TM - 1, dn_row, pltpu.roll(u, TM - 1, 0))
    y = um * w[0:1, :] + u * w[1:2, :] + un * w[2:3, :] + b_ref[...]
    x0_ref[...] = y[:, 0:HY_W]
    vx_ref[...] = y[:, 2 * HY_W:3 * HY_W] * y[:, HY_W:2 * HY_W]


def hy_prep(z, conv_w, conv_b, tpb):
    m = z.shape[0]
    w3 = 3 * HY_W
    rb = TM // SUBLANES
    nrb = m // SUBLANES
    return pl.pallas_call(
        functools.partial(_hy_prep_kernel, tpb=tpb),
        grid=(m // TM,),
        in_specs=[pl.BlockSpec((TM, w3), lambda i: (i, 0)),
                  pl.BlockSpec((SUBLANES, w3), lambda i: (jnp.maximum(i * rb - 1, 0), 0)),
                  pl.BlockSpec((SUBLANES, w3), lambda i: (jnp.minimum((i + 1) * rb, nrb - 1), 0)),
                  pl.BlockSpec((3, w3), lambda i: (0, 0)),
                  pl.BlockSpec((1, w3), lambda i: (0, 0))],
        out_specs=[pl.BlockSpec((TM, HY_W), lambda i: (i, 0))] * 2,
        out_shape=[jax.ShapeDtypeStruct((m, HY_W), F32)] * 2,
        compiler_params=_cparams(("parallel",)),
        name="hy_prep",
    )(z, z, z, conv_w, conv_b.reshape(1, w3))


def _hy_filter_kernel(zf_ref, w1_ref, b1_ref, f1_ref, w2_ref, b2_ref, f2_ref, w3_ref, dl_ref,
                      hfb_ref, ss_ref):
    i = pl.program_id(0)
    zf = zf_ref[...]
    hid = jnp.sin(f1_ref[...] * (_dot_hi(zf, w1_ref[...]) + b1_ref[...]))
    hid = jnp.sin(f2_ref[...] * (_dot_hi(hid, w2_ref[...]) + b2_ref[...]))
    filt = _dot_hi(hid, w3_ref[...])
    window = jnp.exp(-zf[:, 0:1] * dl_ref[...])
    hf = filt[:, 0:HY_W] * window
    row = lax.broadcasted_iota(jnp.int32, (TM, 1), 0) + i * TM
    hb = jnp.where(row == 0, 0.0, filt[:, HY_W:2 * HY_W] * window)
    hfb_ref[0] = hf
    hfb_ref[1] = hb
    part = jnp.sum(hf * hf + hb * hb, axis=0, keepdims=True)

    @pl.when(i == 0)
    def _():
        ss_ref[...] = jnp.zeros_like(ss_ref)

    ss_ref[...] = ss_ref[...] + part


def hy_filter(length, w1, b1, fr1, w2, b2, fr2, w3):
    t = jnp.linspace(0.0, 1.0, length, dtype=F32)[:, None]
    wv = (2.0 * math.pi / length) * jnp.arange(length, dtype=F32)[:, None]
    f = jnp.linspace(1e-4, HY_BANDS - 1.0, HY_BANDS, dtype=F32)[None, :]
    zfeat = jnp.concatenate([t, jnp.cos(f * wv), -jnp.sin(f * wv)], axis=-1)
    zfeat = jnp.pad(zfeat, ((0, 0), (0, LANES - HY_EMB)))
    w1p = jnp.pad(w1, ((0, LANES - HY_EMB), (0, 0)))
    deltas = jnp.abs(jnp.linspace(math.log(HY_TARGET) / HY_SLOW_DECAY,
                                  math.log(HY_TARGET) / HY_FAST_DECAY, HY_W, dtype=F32))[None, :]
    full = lambda shape: pl.BlockSpec(shape, lambda i: (0,) * len(shape))
    return pl.pallas_call(
        _hy_filter_kernel,
        grid=(length // TM,),
        in_specs=[pl.BlockSpec((TM, LANES), lambda i: (i, 0)),
                  full((LANES, HY_ORDER)), full((1, HY_ORDER)), full((1, HY_ORDER)),
                  full((HY_ORDER, HY_ORDER)), full((1, HY_ORDER)), full((1, HY_ORDER)),
                  full((HY_ORDER, 2 * HY_W)), full((1, HY_W))],
        out_specs=[pl.BlockSpec((2, TM, HY_W), lambda i: (0, i, 0)),
                   pl.BlockSpec((1, HY_W), lambda i: (0, 0))],
        out_shape=[jax.ShapeDtypeStruct((2, length, HY_W), F32), jax.ShapeDtypeStruct((1, HY_W), F32)],
        compiler_params=_cparams(("arbitrary",)),
        name="hy_filter",
    )(zfeat, w1p, b1.reshape(1, -1), fr1.reshape(1, -1), w2, b2.reshape(1, -1), fr2.reshape(1, -1), w3, deltas)


def _dft_consts(length):
    n = 2 * length
    ns = n // FFT_FAST
    a = jnp.arange(ns, dtype=jnp.int32)
    ang = ((a[:, None] * a[None, :]) % ns).astype(F32) * (2.0 * math.pi / ns)
    f_fwd = jnp.concatenate([jnp.cos(ang), -jnp.sin(ang)], axis=0)
    f_inv = jnp.concatenate([jnp.cos(ang), -jnp.sin(ang)], axis=1)
    b = jnp.arange(FFT_FAST, dtype=jnp.int32)
    kk = a[:, None, None] + ns * b[None, :, None]
    idx = (kk * b[None, None, :]) % n
    th = idx.astype(F32) * (2.0 * math.pi / n)
    mr, mi = jnp.cos(th), -jnp.sin(th)
    m_fwd = jnp.concatenate([jnp.concatenate([mr, -mi], axis=2),
                             jnp.concatenate([mi, mr], axis=2)], axis=1)
    ir = jnp.transpose(mr, (0, 2, 1)) / n
    ii = jnp.transpose(-mi, (0, 2, 1)) / n
    m_inv = jnp.concatenate([jnp.concatenate([ir, -ii], axis=2),
                             jnp.concatenate([ii, ir], axis=2)], axis=1)
    return tuple(t.astype(BF16) for t in (f_fwd, f_inv, m_fwd, m_inv))


def _dft_slow_kernel(f_ref, x_ref, o_ref):
    o_ref[0] = jnp.dot(f_ref[...], x_ref[0].astype(BF16), preferred_element_type=F32)


def dft_slow(fmat, x):
    bx, k, lw = x.shape
    mrows = fmat.shape[0]
    tl = min(lw, 4096)
    return pl.pallas_call(
        _dft_slow_kernel,
        grid=(bx, lw // tl),
        in_specs=[pl.BlockSpec((mrows, k), lambda b, j: (0, 0)),
                  pl.BlockSpec((1, k, tl), lambda b, j: (b, 0, j))],
        out_specs=pl.BlockSpec((1, mrows, tl), lambda b, j: (b, 0, j)),
        out_shape=jax.ShapeDtypeStruct((bx, mrows, lw), F32),
        compiler_params=_cparams(("parallel", "parallel")),
        name="dft_slow",
    )(fmat, x)


FFT_GROUP = 8


def _dft_fast_filter_kernel(m_ref, a_ref, sc_ref, o_ref):
    sc = sc_ref[...]
    for j in range(FFT_GROUP):
        af = jnp.concatenate([a_ref[0, 0, j], a_ref[0, 1, j]], axis=0).astype(BF16)
        ab = jnp.concatenate([a_ref[1, 0, j], a_ref[1, 1, j]], axis=0).astype(BF16)
        xf = jnp.dot(m_ref[j], af, preferred_element_type=F32)
        xb = jnp.dot(m_ref[j], ab, preferred_element_type=F32)
        o_ref[0, j] = (xf[0:FFT_FAST] + xb[0:FFT_FAST]) * sc
        o_ref[1, j] = (xf[FFT_FAST:] - xb[FFT_FAST:]) * sc


def dft_fast_filter(m_fwd, a5, scale):
    _, _, ns, nf, cw = a5.shape
    g = FFT_GROUP
    return pl.pallas_call(
        _dft_fast_filter_kernel,
        grid=(ns // g,),
        in_specs=[pl.BlockSpec((g, 2 * nf, 2 * nf), lambda i: (i, 0, 0)),
                  pl.BlockSpec((2, 2, g, nf, cw), lambda i: (0, 0, i, 0, 0)),
                  pl.BlockSpec((1, cw), lambda i: (0, 0))],
        out_specs=pl.BlockSpec((2, g, nf, cw), lambda i: (0, i, 0, 0)),
        out_shape=jax.ShapeDtypeStruct((2, ns, nf, cw), F32),
        compiler_params=_cparams(("parallel",)),
        name="dft_fast_filter",
    )(m_fwd, a5, scale)


def _dft_fast_conv_kernel(mf_ref, mi_ref, a_ref, k_ref, o_ref):
    for j in range(FFT_GROUP):
        a = jnp.concatenate([a_ref[0, 0, j], a_ref[0, 1, j]], axis=0).astype(BF16)
        x = jnp.dot(mf_ref[j], a, preferred_element_type=F32)
        xr, xi = x[0:FFT_FAST], x[FFT_FAST:]
        kr, ki = k_ref[0, j], k_ref[1, j]
        y = jnp.concatenate([xr * kr - xi * ki, xr * ki + xi * kr], axis=0).astype(BF16)
        bv = jnp.dot(mi_ref[j], y, preferred_element_type=F32)
        o_ref[0, 0, j] = bv[0:FFT_FAST]
        o_ref[0, 1, j] = bv[FFT_FAST:]


def dft_fast_conv(m_fwd, m_inv, a5, kf):
    bx, _, ns, nf, cw = a5.shape
    g = FFT_GROUP
    return pl.pallas_call(
        _dft_fast_conv_kernel,
        grid=(bx, ns // g),
        in_specs=[pl.BlockSpec((g, 2 * nf, 2 * nf), lambda b, i: (i, 0, 0)),
                  pl.BlockSpec((g, 2 * nf, 2 * nf), lambda b, i: (i, 0, 0)),
                  pl.BlockSpec((1, 2, g, nf, cw), lambda b, i: (b, 0, i, 0, 0)),
                  pl.BlockSpec((2, g, nf, cw), lambda b, i: (0, i, 0, 0))],
        out_specs=pl.BlockSpec((1, 2, g, nf, cw), lambda b, i: (b, 0, i, 0, 0)),
        out_shape=jax.ShapeDtypeStruct((bx, 2, ns, nf, cw), F32),
        compiler_params=_cparams(("parallel", "parallel")),
        name="dft_fast_conv",
    )(m_fwd, m_inv, a5, kf)


def hyena_long_conv(vx_lat, hfb, scale):
    bsz, length, cw = vx_lat.shape
    ns = 2 * length // FFT_FAST
    f_fwd, f_inv, m_fwd, m_inv = _dft_consts(length)
    fa = dft_slow(f_fwd[:, :ns // 2], hfb.reshape(2, ns // 2, FFT_FAST * cw))
    kf = dft_fast_filter(m_fwd, fa.reshape(2, 2, ns, FFT_FAST, cw), scale)
    xa = dft_slow(f_fwd[:, :ns // 2], vx_lat.reshape(bsz, ns // 2, FFT_FAST * cw))
    bv = dft_fast_conv(m_fwd, m_inv, xa.reshape(bsz, 2, ns, FFT_FAST, cw), kf)
    y = dft_slow(f_inv[:ns // 2], bv.reshape(bsz, 2 * ns, FFT_FAST * cw))
    return y.reshape(bsz, length, cw)


def _hy_ctx_conv_kernel(v_ref, g_ref, o_ref, *, lc):
    nb = lc // SUBLANES

    def body(gi, acc):
        base = pl.multiple_of((nb - 1 - gi) * SUBLANES, SUBLANES)
        win = g_ref[pl.ds(base, lc + SUBLANES), :]
        for j in range(SUBLANES):
            vrow = v_ref[pl.ds(gi * SUBLANES + j, 1), :]
            acc = acc + win[SUBLANES - 1 - j:SUBLANES - 1 - j + lc, :] * vrow
        return acc

    o_ref[...] = lax.fori_loop(0, nb, body, jnp.zeros((lc, HY_W), F32))


def hy_ctx_conv(vx, gwin, bsz, tpb, lc):
    return pl.pallas_call(
        functools.partial(_hy_ctx_conv_kernel, lc=lc),
        grid=(bsz,),
        in_specs=[pl.BlockSpec((lc, HY_W), lambda b: (b * tpb + tpb - 1, 0)),
                  pl.BlockSpec((2 * lc, HY_W), lambda b: (0, 0))],
        out_specs=pl.BlockSpec((lc, HY_W), lambda b: (b, 0)),
        out_shape=jax.ShapeDtypeStruct((bsz * lc, HY_W), F32),
        compiler_params=_cparams(("parallel",)),
        name="hy_ctx_conv",
    )(vx, gwin)


ROUTE_E1, ROUTE_E2, ROUTE_W1, ROUTE_W2, ROUTE_R1, ROUTE_R2 = 0, 1, 2, 3, 4, 5
NEG_BIG = -1e30


def _gelu_tanh(x):
    return 0.5 * x * (1.0 + jnp.tanh(math.sqrt(2.0 / math.pi) * (x + 0.044715 * (x * x * x))))


def _lane_pick(slab, lane, idx):
    return jnp.sum(jnp.where(lane == idx, slab, 0.0), axis=-1, keepdims=True)


def _merge_kernel(x_ref, mod_ref, n1_ref, n2_ref, wg0_ref, wg1_ref, wg2_ref, wg3_ref, zhg_ref, zrt_ref, zs5_ref,
                  hgf_ref, hgb_ref, rtf_ref, rtb_ref, s5f_ref, s5b_ref, yc_ref, vx_ref, x0_ref,
                  vec_ref, glu1_ref, glu2_ref, wb_ref, wo_ref, wr_ref, br_ref,
                  xo_ref, h2_ref, route_ref, cnt_ref, base_ref, *, d):
    i = pl.program_id(0)
    w = BRANCH_W
    x = x_ref[...]
    mod = mod_ref[0]
    h = _modulated_norm(x, n1_ref[...], mod[:, 0:d], mod[:, d:2 * d]).astype(BF16)

    hd = w // HG_HEADS
    hr = lax.broadcasted_iota(jnp.int32, (w, w), 0) // hd
    hc = lax.broadcasted_iota(jnp.int32, (w, w), 1) // hd
    hm = jnp.where(hr == hc, 1.0, 0.0).astype(BF16)
    inv = 1.0 / hd

    hg = hgf_ref[...] + hgb_ref[...]
    ms = _split_dot(hg * hg, hm) * inv
    br_hg = hg * lax.rsqrt(ms + EPS) * vec_ref[0:1, :] * _silu(zhg_ref[...])

    rt = rtf_ref[...] + rtb_ref[...]
    cen = rt - _split_dot(rt, hm) * inv
    var = _split_dot(cen * cen, hm) * inv
    br_rt = cen * lax.rsqrt(var + EPS) * vec_ref[1:2, :] * _silu(zrt_ref[...])

    vx = vx_ref[...]
    br_hy = (yc_ref[...] + vx * vec_ref[3:4, :]) * x0_ref[...]

    y5 = _gelu_tanh(s5f_ref[...] + s5b_ref[...] + zs5_ref[...] * vec_ref[2:3, :]).astype(BF16)
    br_s5 = (jnp.dot(y5, glu1_ref[...], preferred_element_type=F32)
             * _sigmoid(jnp.dot(y5, glu2_ref[...], preferred_element_type=F32)))

    merged = jnp.zeros((TM, d), F32)
    for n, (br, wg_ref) in enumerate(zip((br_hg, br_hy, br_rt, br_s5), (wg0_ref, wg1_ref, wg2_ref, wg3_ref))):
        gate = _sigmoid(jnp.dot(h, wg_ref[...], preferred_element_type=F32))
        merged = merged + gate * jnp.dot(br.astype(BF16), wb_ref[n], preferred_element_type=F32)
    mix = jnp.dot(merged.astype(BF16), wo_ref[...], preferred_element_type=F32)
    x_mid = x + mod[:, 2 * d:3 * d] * mix
    xo_ref[...] = x_mid
    h2 = _modulated_norm(x_mid, n2_ref[...], mod[:, 3 * d:4 * d], mod[:, 4 * d:5 * d])
    h2_ref[...] = h2

    logits = _dot_hi(h2, wr_ref[...]) + br_ref[...]
    lane = lax.broadcasted_iota(jnp.int32, (TM, LANES), 1).astype(F32)
    gl = jnp.where(lane < MOE_GROUPS, logits, NEG_BIG)
    gmax = jnp.max(gl, axis=-1, keepdims=True)
    grp_p = 1.0 / jnp.sum(jnp.exp(gl - gmax), axis=-1, keepdims=True)
    gi = jnp.min(jnp.where(gl == gmax, lane, float(LANES)), axis=-1, keepdims=True)
    lo = MOE_GROUPS + MOE_PER_GROUP * gi
    el = jnp.where((lane >= lo) & (lane < lo + MOE_PER_GROUP), logits, NEG_BIG)
    m1 = jnp.max(el, axis=-1, keepdims=True)
    i1 = jnp.min(jnp.where(el == m1, lane, float(LANES)), axis=-1, keepdims=True)
    el2 = jnp.where(lane == i1, NEG_BIG, el)
    m2 = jnp.max(el2, axis=-1, keepdims=True)
    i2 = jnp.min(jnp.where(el2 == m2, lane, float(LANES)), axis=-1, keepdims=True)
    ratio = jnp.exp(m2 - m1)
    wt1 = grp_p / (1.0 + ratio)
    wt2 = grp_p * ratio / (1.0 + ratio)
    e1 = i1 - MOE_GROUPS
    e2 = i2 - MOE_GROUPS

    @pl.when(i == 0)
    def _():
        base_ref[...] = jnp.zeros_like(base_ref)

    oh1 = jnp.where(lane == e1, 1.0, 0.0)
    oh2 = jnp.where(lane == e2, 1.0, 0.0)
    ri = lax.broadcasted_iota(jnp.int32, (TM, TM), 0)
    ci = lax.broadcasted_iota(jnp.int32, (TM, TM), 1)
    below = jnp.where(ci < ri, 1.0, 0.0).astype(BF16)
    c1 = jnp.dot(below, oh1.astype(BF16), preferred_element_type=F32)
    c2 = jnp.dot(below, oh2.astype(BF16), preferred_element_type=F32)
    tot1 = jnp.sum(oh1, axis=0, keepdims=True)
    tot2 = jnp.sum(oh2, axis=0, keepdims=True)
    base = base_ref[...]
    r1 = jnp.sum(oh1 * (base + c1), axis=-1, keepdims=True)
    r2 = jnp.sum(oh2 * (base + tot1 + c2), axis=-1, keepdims=True)
    base = base + tot1 + tot2
    base_ref[...] = base
    cnt_ref[...] = jnp.broadcast_to(base, cnt_ref.shape)
    route = jnp.zeros((TM, LANES), F32)
    for idx, val in ((ROUTE_E1, e1), (ROUTE_E2, e2), (ROUTE_W1, wt1), (ROUTE_W2, wt2),
                     (ROUTE_R1, r1), (ROUTE_R2, r2)):
        route = jnp.where(lane == idx, val, route)
    route_ref[...] = route


def merge(x_all, mod3, n1, n2, w_in_bf, layer, z, hgf, hgb, rtf, rtb, s5f, s5b, yconv, vx, x0c,
          vecs, glu1, glu2, w_branch, w_out, w_router, b_router, tpb):
    m, d = x_all.shape
    w = BRANCH_W
    row = lambda width: pl.BlockSpec((TM, width), lambda i: (i, 0))
    zcol = lambda cb: pl.BlockSpec((TM, w), lambda i: (i, cb))
    full = lambda shape: pl.BlockSpec(shape, lambda i: (0,) * len(shape))
    gate_w = lambda n: pl.BlockSpec((None, d, d), lambda i: (layer, 0, N_MIX // d + n))
    return pl.pallas_call(
        functools.partial(_merge_kernel, d=d),
        grid=(m // TM,),
        in_specs=[row(d),
                  pl.BlockSpec((1, 1, 6 * d), lambda i: (_seg_of_tile(i, tpb), 0, 0)),
                  full((1, d)), full((1, d)), gate_w(0), gate_w(1), gate_w(2), gate_w(3),
                  zcol(CB_HG_G), zcol(CB_RT_G), zcol(CB_S5)]
                 + [row(w)] * 9
                 + [full((SUBLANES, w)), full((w, w)), full((w, w)), full((N_BRANCH, w, d)), full((d, d)),
                    full((d, LANES)), full((1, LANES))],
        out_specs=[row(d), row(d), row(LANES), full((SUBLANES, LANES))],
        out_shape=[jax.ShapeDtypeStruct((m, d), F32), jax.ShapeDtypeStruct((m, d), F32),
                   jax.ShapeDtypeStruct((m, LANES), F32), jax.ShapeDtypeStruct((SUBLANES, LANES), F32)],
        scratch_shapes=[pltpu.VMEM((1, LANES), F32)],
        compiler_params=_cparams(("arbitrary",)),
        name="merge",
    )(x_all, mod3, n1, n2, w_in_bf, w_in_bf, w_in_bf, w_in_bf, z, z, z,
      hgf, hgb, rtf, rtb, s5f, s5b, yconv, vx, x0c,
      vecs, glu1, glu2, w_branch, w_out, w_router, b_router)


DMA_UNROLL = 8


def _row_copy(src_ref, src_row, dst_ref, dst_row, sem):
    return pltpu.make_async_copy(src_ref.at[pl.ds(src_row, 1), :], dst_ref.at[pl.ds(dst_row, 1), :], sem)


def _dispatch_kernel(dest_ref, h_ref, zero_ref, buf_ref, sem):
    del zero_ref
    base = pl.program_id(0) * TM

    def issue(r, carry):
        for k in range(2):
            _row_copy(h_ref, r, buf_ref, dest_ref[(base + r) * 2 + k], sem).start(priority=k)
        return carry

    lax.fori_loop(0, TM, issue, 0, unroll=DMA_UNROLL)
    for k in range(2):
        pltpu.make_async_copy(h_ref, buf_ref.at[pl.ds(0, TM), :], sem).wait()


def moe_dispatch(dest, h2, n_rows):
    m, d = h2.shape
    grid_spec = pltpu.PrefetchScalarGridSpec(
        num_scalar_prefetch=1,
        grid=(m // TM,),
        in_specs=[pl.BlockSpec((TM, d), lambda i, dest: (i, 0)),
                  pl.BlockSpec(memory_space=pl.ANY)],
        out_specs=pl.BlockSpec(memory_space=pl.ANY),
        scratch_shapes=[pltpu.SemaphoreType.DMA(())],
    )
    return pl.pallas_call(
        _dispatch_kernel,
        grid_spec=grid_spec,
        out_shape=jax.ShapeDtypeStruct((n_rows, d), F32),
        input_output_aliases={2: 0},
        compiler_params=_cparams(("arbitrary",)),
        name="moe_dispatch",
    )(dest, h2, jnp.zeros((n_rows, d), F32))


def _expert_kernel(be_ref, x_ref, w1_ref, w3_ref, w2_ref, y_ref):
    del be_ref
    xb = x_ref[...].astype(BF16)
    a = jnp.dot(xb, w1_ref[...].astype(BF16), preferred_element_type=F32)
    g = jnp.dot(xb, w3_ref[...].astype(BF16), preferred_element_type=F32)
    y_ref[...] = jnp.dot((_silu(a) * g).astype(BF16), w2_ref[...].astype(BF16), preferred_element_type=F32)


def moe_experts(block_expert, buf, w1, w3, w2, layer):
    n_rows, d = buf.shape
    hid = w1.shape[-1]
    grid_spec = pltpu.PrefetchScalarGridSpec(
        num_scalar_prefetch=1,
        grid=(n_rows // MOE_BLOCK,),
        in_specs=[pl.BlockSpec((MOE_BLOCK, d), lambda j, be: (j, 0)),
                  pl.BlockSpec((None, None, d, hid), lambda j, be: (layer, be[j], 0, 0)),
                  pl.BlockSpec((None, None, d, hid), lambda j, be: (layer, be[j], 0, 0)),
                  pl.BlockSpec((None, None, hid, d), lambda j, be: (layer, be[j], 0, 0))],
        out_specs=pl.BlockSpec((MOE_BLOCK, d), lambda j, be: (j, 0)),
    )
    return pl.pallas_call(
        _expert_kernel,
        grid_spec=grid_spec,
        out_shape=jax.ShapeDtypeStruct((n_rows, d), F32),
        compiler_params=_cparams(("arbitrary",)),
        name="moe_experts",
    )(block_expert, buf, w1, w3, w2)


def _combine_kernel(dest_ref, x_ref, route_ref, mod_ref, gfin_ref, y_ref, o_ref, gath, sem, *, d, tpb, final):
    base = (pl.program_id(0) * tpb + pl.program_id(1)) * TM

    def issue(r, carry):
        for k in range(2):
            _row_copy(y_ref, dest_ref[(base + r) * 2 + k], gath.at[k], r, sem).start(priority=k)
        return carry

    lax.fori_loop(0, TM, issue, 0, unroll=DMA_UNROLL)
    for k in range(2):
        pltpu.make_async_copy(y_ref.at[pl.ds(0, TM), :], gath.at[k], sem).wait()
    route = route_ref[...]
    lane = lax.broadcasted_iota(jnp.int32, (TM, LANES), 1)
    moe = _lane_pick(route, lane, ROUTE_W1) * gath[0] + _lane_pick(route, lane, ROUTE_W2) * gath[1]
    x_new = x_ref[...] + mod_ref[0][:, 5 * d:6 * d] * moe
    if final:
        ms = jnp.mean(x_new * x_new, axis=-1, keepdims=True)
        x_new = x_new * lax.rsqrt(ms + EPS) * gfin_ref[...]
    o_ref[...] = x_new


def moe_combine(dest, x_mid, route, mod3, gain_final, ybuf, bsz, tpb, final):
    m, d = x_mid.shape
    nr = tpb - 1 if final else tpb
    tile = lambda b, r: b * tpb + r
    grid_spec = pltpu.PrefetchScalarGridSpec(
        num_scalar_prefetch=1,
        grid=(bsz, nr),
        in_specs=[pl.BlockSpec((TM, d), lambda b, r, dest: (tile(b, r), 0)),
                  pl.BlockSpec((TM, LANES), lambda b, r, dest: (tile(b, r), 0)),
                  pl.BlockSpec((1, 1, 6 * d), lambda b, r, dest: (_seg_of_tile(tile(b, r), tpb), 0, 0)),
                  pl.BlockSpec((1, d), lambda b, r, dest: (0, 0)),
                  pl.BlockSpec(memory_space=pl.ANY)],
        out_specs=pl.BlockSpec((TM, d), lambda b, r, dest: (b * nr + r, 0)),
        scratch_shapes=[pltpu.VMEM((2, TM, d), F32), pltpu.SemaphoreType.DMA(())],
    )
    return pl.pallas_call(
        functools.partial(_combine_kernel, d=d, tpb=tpb, final=final),
        grid_spec=grid_spec,
        out_shape=jax.ShapeDtypeStruct((bsz * nr * TM, d), F32),
        compiler_params=_cparams(("arbitrary", "arbitrary")),
        name="moe_combine",
    )(dest, x_mid, route, mod3, gain_final, ybuf)


def moe_plan(route, counts_row, n_blocks):
    counts = counts_row[:MOE_EXPERTS].astype(jnp.int32)
    padded = (counts + MOE_BLOCK - 1) // MOE_BLOCK * MOE_BLOCK
    pad_end = jnp.cumsum(padded)
    pad_start = pad_end - padded
    expert = route[:, ROUTE_E1:ROUTE_E2 + 1].astype(jnp.int32)
    rank = route[:, ROUTE_R1:ROUTE_R2 + 1].astype(jnp.int32)
    dest = (pad_start[expert] + rank).reshape(-1)
    block_row0 = jnp.arange(n_blocks, dtype=jnp.int32) * MOE_BLOCK
    block_expert = jnp.minimum(jnp.sum((pad_end[None, :] <= block_row0[:, None]).astype(jnp.int32), axis=1),
                               MOE_EXPERTS - 1)
    return dest, block_expert


def _rope_tables(length, lc):
    half = RET_DK // 2
    freqs = ROPE_BASE ** (-jnp.arange(0, half, 2, dtype=F32) / half)
    t = jnp.arange(length)
    row_pos = (t // GRID_W).astype(F32)[:, None]
    col_pos = (t % GRID_W).astype(F32)[:, None]
    lane = jnp.arange(RET_HEADS * RET_DK)
    fr = freqs[lane % (half // 2)][None, :]
    ang = jnp.where(((lane % RET_DK) < half)[None, :], row_pos, col_pos) * fr
    cos = jnp.concatenate([jnp.cos(ang), jnp.ones((lc, lane.shape[0]), F32)], axis=0)
    sin = jnp.concatenate([jnp.sin(ang), jnp.zeros((lc, lane.shape[0]), F32)], axis=0)
    return cos, sin


def _rotate_half_cols(wq):
    quarter = RET_DK // 4
    lane = np.arange(wq.shape[1])
    first = (lane % (2 * quarter)) < quarter
    src = np.where(first, lane + quarter, lane - quarter)
    sign = jnp.asarray(np.where(first, -1.0, 1.0), F32)
    return wq[:, src] * sign


def kernel(x, c, ctx, c_ctx, w_mod, b_mod, norm1, norm2, w_in, hg_lb_logits, hg_norm, hy_conv_w, hy_conv_b, hy_filt_w1, hy_filt_b1, hy_filt_freq1, hy_filt_w2, hy_filt_b2, hy_filt_freq2, hy_filt_w3, hy_skip, ret_decay_logit, ret_norm, s5_lam_re, s5_lam_im, s5_log_dt, s5_b_re, s5_b_im, s5_c_re, s5_c_im, s5_d, s5_glu_w1, s5_glu_w2, w_branch, w_out, moe_w_grp, moe_b_grp, moe_w_exp, moe_b_exp, moe_w1, moe_w3, moe_w2, norm_final):
    bsz, length, d = x.shape
    lc = ctx.shape[1]
    depth = w_mod.shape[0]
    assert lc == TM and length % TM == 0 and (2 * length) % (FFT_FAST * FFT_GROUP) == 0
    tt = length + lc
    tpb = tt // TM
    m = bsz * tt
    n_blocks = -(-(m * 2) // MOE_BLOCK) + MOE_EXPERTS
    w = BRANCH_W

    x_all = jnp.concatenate([x, ctx], axis=1).reshape(m, d)
    cmat = jnp.zeros((SUBLANES, d), F32).at[0].set(c_ctx).at[1:1 + bsz].set(c)
    mods = modvec(cmat, w_mod, b_mod)
    lb_sm = jax.nn.softmax(hg_lb_logits.astype(F32), axis=0)
    lb_all = jnp.cumsum(lb_sm, axis=0) - lb_sm[0]
    cos_tab, sin_tab = _rope_tables(length, lc)
    log_gamma = jax.nn.log_sigmoid(ret_decay_logit.astype(F32))
    gfin = norm_final.reshape(1, d)

    w_in_bf = w_in.astype(BF16)
    rq0 = 5 * w + 3 * HY_W
    nq = RET_HEADS * RET_DK

    for l in range(depth):
        w_rot = jnp.concatenate([_rotate_half_cols(w_in[l, :, rq0:rq0 + nq]),
                                 _rotate_half_cols(w_in[l, :, rq0 + nq:rq0 + 2 * nq])], axis=1).astype(BF16)
        mod3 = mods[l].reshape(SUBLANES, 1, 6 * d)

        z = inproj(x_all, mod3, norm1[l].reshape(1, d), w_in_bf, l, w_rot, tpb)
        hgf, hgb = hgrn(z, lb_all[l], bsz, tpb)
        rtf, rtb = retention(z, cos_tab, sin_tab, log_gamma[l], bsz, tpb)
        s5f, s5b = s5(z, *s5_params(s5_lam_re[l], s5_lam_im[l], s5_log_dt[l], s5_b_re[l], s5_b_im[l],
                                    s5_c_re[l], s5_c_im[l]), bsz, tpb)

        x0c, vx = hy_prep(z, hy_conv_w[l], hy_conv_b[l], tpb)
        fargs = (hy_filt_w1[l], hy_filt_b1[l], hy_filt_freq1[l], hy_filt_w2[l], hy_filt_b2[l],
                 hy_filt_freq2[l], hy_filt_w3[l])
        hfb, ss = hy_filter(length, *fargs)
        vx3 = vx.reshape(bsz, tt, HY_W)
        y_lat = hyena_long_conv(vx3[:, :length], hfb, lax.rsqrt(ss + EPS))
        hfbc, ssc = hy_filter(lc, *fargs)
        gwin = jnp.concatenate([hfbc[1, 1:][::-1], hfbc[0], jnp.zeros((1, HY_W), F32)], axis=0) \
            * lax.rsqrt(ssc + EPS)
        y_ctx = hy_ctx_conv(vx, gwin, bsz, tpb, lc).reshape(bsz, lc, HY_W)
        yconv = jnp.concatenate([y_lat, y_ctx], axis=1).reshape(m, HY_W)

        vecs = jnp.zeros((SUBLANES, w), F32).at[0].set(hg_norm[l]).at[1].set(ret_norm[l]) \
            .at[2].set(s5_d[l]).at[3].set(hy_skip[l])
        w_router = jnp.zeros((d, LANES), F32).at[:, :MOE_GROUPS].set(moe_w_grp[l]) \
            .at[:, MOE_GROUPS:MOE_GROUPS + MOE_EXPERTS].set(moe_w_exp[l])
        b_router = jnp.zeros((1, LANES), F32).at[0, :MOE_GROUPS].set(moe_b_grp[l]) \
            .at[0, MOE_GROUPS:MOE_GROUPS + MOE_EXPERTS].set(moe_b_exp[l])
        x_mid, h2, route, counts = merge(
            x_all, mod3, norm1[l].reshape(1, d), norm2[l].reshape(1, d), w_in_bf, l, z,
            hgf, hgb, rtf, rtb, s5f, s5b, yconv, vx, x0c, vecs,
            s5_glu_w1[l].astype(BF16), s5_glu_w2[l].astype(BF16), w_branch[l].astype(BF16),
            w_out[l].astype(BF16), w_router, b_router, tpb)

        dest, block_expert = moe_plan(route, counts[0], n_blocks)
        buf = moe_dispatch(dest, h2, n_blocks * MOE_BLOCK)
        ybuf = moe_experts(block_expert, buf, moe_w1, moe_w3, moe_w2, l)
        x_all = moe_combine(dest, x_mid, route, mod3, gfin, ybuf, bsz, tpb, l == depth - 1)

    return x_all.reshape(bsz, length, d)
```

```python
import functools
import math

import numpy as np
import jax
import jax.numpy as jnp
from jax import lax
from jax.experimental import pallas as pl
from jax.experimental.pallas import tpu as pltpu

F32 = jnp.float32
BF16 = jnp.bfloat16
HIGHEST = lax.Precision.HIGHEST

GRID_W = 64
N_BRANCH = 4
BRANCH_W = 256
HG_HEADS = 4
HG_CHUNK = 16
HY_W = 256
HY_EMB = 33
HY_BANDS = 16
HY_ORDER = 64
HY_FAST_DECAY = 0.3
HY_SLOW_DECAY = 1.5
HY_TARGET = 1e-2
RET_HEADS = 4
RET_DK = 32
RET_DV = 64
ROPE_BASE = 10000.0
S5_GROUP_CH = 16
S5_GROUPS = 16
S5_STATE = 64
MOE_GROUPS = 4
MOE_PER_GROUP = 8
MOE_EXPERTS = 32
MOE_HIDDEN = 512
MOE_BLOCK = 256
EPS = 1e-6

LANES = 128
SUBLANES = 8
TM = 256
FFT_FAST = 128
VMEM_LIMIT = 56 * 1024 * 1024

N_MIX = 3072
N_ZCOLS = N_MIX + 2 * RET_HEADS * RET_DK
CB_HG_Q, CB_HG_FF, CB_HG_FB, CB_HG_I, CB_HG_G = 3, 4, 5, 6, 7
CB_RT_V, CB_RT_G, CB_S5 = 9, 10, 11
CB128_RT_Q, CB128_RT_K, CB128_RT_QR, CB128_RT_KR = 16, 17, 24, 25


def _cparams(sem):
    return pltpu.CompilerParams(dimension_semantics=sem, vmem_limit_bytes=VMEM_LIMIT)


def _sigmoid(x):
    return 1.0 / (1.0 + jnp.exp(-x))


def _silu(x):
    return x * _sigmoid(x)


def _dot(a, b):
    return jnp.dot(a.astype(BF16), b.astype(BF16), preferred_element_type=F32)


def _dot_nt(a, b):
    return lax.dot_general(a.astype(BF16), b.astype(BF16), (((1,), (1,)), ((), ())),
                           preferred_element_type=F32)


def _dot_hi(a, b):
    return jnp.dot(a, b, precision=HIGHEST, preferred_element_type=F32)


def _dot3(a, b):
    a_hi = a.astype(BF16)
    a_lo = (a - a_hi.astype(F32)).astype(BF16)
    b_hi = b.astype(BF16)
    b_lo = (b - b_hi.astype(F32)).astype(BF16)
    return (jnp.dot(a_hi, b_hi, preferred_element_type=F32)
            + jnp.dot(a_lo, b_hi, preferred_element_type=F32)
            + jnp.dot(a_hi, b_lo, preferred_element_type=F32))


def _split_dot(x, m_bf16):
    hi = x.astype(BF16)
    lo = (x - hi.astype(F32)).astype(BF16)
    return (jnp.dot(hi, m_bf16, preferred_element_type=F32)
            + jnp.dot(lo, m_bf16, preferred_element_type=F32))


def _split_dot_left(m_bf16, x):
    hi = x.astype(BF16)
    lo = (x - hi.astype(F32)).astype(BF16)
    return (jnp.dot(m_bf16, hi, preferred_element_type=F32)
            + jnp.dot(m_bf16, lo, preferred_element_type=F32))


def _seg_of_tile(i, tpb):
    return jnp.where(i % tpb == tpb - 1, 0, 1 + i // tpb)


def _split_tile(i, bsz, tpb):
    b, r = i // tpb, i % tpb
    return jnp.where(r == tpb - 1, bsz * (tpb - 1) + b, b * (tpb - 1) + r)


def _scan_tiles(b, s, tpb):
    last = tpb - 1
    fwd = jnp.where(s == 0, last, s - 1)
    bwd = jnp.where(s == 0, last, last - s)
    return b * tpb + fwd, b * tpb + bwd


def _modvec_kernel(c_ref, w_ref, b_ref, o_ref):
    c = c_ref[...]
    o_ref[0] = _dot_hi(_silu(c), w_ref[0]) + b_ref[0]


def modvec(cmat, w_mod, b_mod):
    depth, d, n6 = w_mod.shape
    nb = n6 // d
    return pl.pallas_call(
        _modvec_kernel,
        grid=(depth, nb),
        in_specs=[pl.BlockSpec((SUBLANES, d), lambda l, j: (0, 0)),
                  pl.BlockSpec((1, d, d), lambda l, j: (l, 0, j)),
                  pl.BlockSpec((1, 1, d), lambda l, j: (l, 0, j))],
        out_specs=pl.BlockSpec((1, SUBLANES, d), lambda l, j: (l, 0, j)),
        out_shape=jax.ShapeDtypeStruct((depth, SUBLANES, n6), F32),
        compiler_params=_cparams(("arbitrary", "arbitrary")),
        name="modvec",
    )(cmat, w_mod, b_mod.reshape(depth, 1, n6))


def _modulated_norm(x, gain, shift, scale):
    ms = jnp.mean(x * x, axis=-1, keepdims=True)
    return (x * lax.rsqrt(ms + EPS) * gain) * (1.0 + scale) + shift


def _inproj_kernel(x_ref, mod_ref, g_ref, w_ref, wrot_ref, z_ref, *, d):
    h = _modulated_norm(x_ref[...], g_ref[...], mod_ref[0, :, 0:d], mod_ref[0, :, d:2 * d]).astype(BF16)
    hy0 = 5 * BRANCH_W
    hy1 = hy0 + 3 * HY_W
    z_ref[:, 0:hy1 - hy0] = jnp.dot(h, w_ref[:, hy0:hy1], preferred_element_type=F32)
    z_ref[:, hy1 - hy0:hy1] = jnp.dot(h, w_ref[:, 0:hy0], preferred_element_type=F32)
    z_ref[:, hy1:N_MIX] = jnp.dot(h, w_ref[:, hy1:N_MIX], preferred_element_type=F32)
    z_ref[:, N_MIX:N_ZCOLS] = jnp.dot(h, wrot_ref[...], preferred_element_type=F32)


def inproj(x_all, mod3, gain, w_in_bf, layer, w_rot, tpb):
    m, d = x_all.shape
    return pl.pallas_call(
        functools.partial(_inproj_kernel, d=d),
        grid=(m // TM,),
        in_specs=[pl.BlockSpec((TM, d), lambda i: (i, 0)),
                  pl.BlockSpec((1, 1, 6 * d), lambda i: (_seg_of_tile(i, tpb), 0, 0)),
                  pl.BlockSpec((1, d), lambda i: (0, 0)),
                  pl.BlockSpec((None, d, N_MIX), lambda i: (layer, 0, 0)),
                  pl.BlockSpec((d, N_ZCOLS - N_MIX), lambda i: (0, 0))],
        out_specs=pl.BlockSpec((TM, N_ZCOLS), lambda i: (i, 0)),
        out_shape=jax.ShapeDtypeStruct((m, N_ZCOLS), F32),
        compiler_params=_cparams(("parallel",)),
        name="inproj",
    )(x_all, mod3, gain, w_in_bf, w_rot)


def _head_masked_rows(x, nchunk, c, hd):
    w = x.shape[-1]
    x3 = x.reshape(nchunk, c, w)
    lane_head = lax.broadcasted_iota(jnp.int32, (1, 1, w), 2) // hd
    parts = [jnp.where(lane_head == h, x3, 0.0) for h in range(w // hd)]
    return jnp.concatenate(parts, axis=1).reshape(nchunk * (w // hd) * c, w)


def _hgrn_direction(q_ref, f_ref, v_ref, lb, tri_ref, o_ref, st_ref, b_s, k_s, qe_s, ke_s, reverse):
    c = HG_CHUNK
    w = BRANCH_W
    nchunk = TM // c
    nh = HG_HEADS
    hd = w // nh
    half = c // 2
    z = f_ref[...]
    sg = _sigmoid(z)
    log2_f = jnp.log2(lb + (1.0 - lb) * sg)
    k = (1.0 - lb) * (1.0 - sg)
    b2 = _split_dot_left(tri_ref[0], log2_f)
    rem2 = _split_dot_left(tri_ref[1], log2_f)
    b_s[...] = b2
    k_s[...] = k
    qe_s[...] = _head_masked_rows(q_ref[...] * jnp.exp2(b2), nchunk, c, hd).astype(BF16)
    ke_s[...] = _head_masked_rows(k * jnp.exp2(rem2), nchunk, c, hd).astype(BF16)

    head_r = lax.broadcasted_iota(jnp.int32, (w, w), 0) // hd
    head_c = lax.broadcasted_iota(jnp.int32, (w, w), 1) // hd
    hm = jnp.where(head_r == head_c, 1.0, 0.0).astype(BF16)
    rows = lax.broadcasted_iota(jnp.int32, (half, 1), 0)

    pieces = []
    for s in range(c):
        for g in range(2):
            lo, hi = g * half, (g + 1) * half - 1
            if (hi < s) if not reverse else (lo > s):
                continue
            full = (lo >= s) if not reverse else (hi <= s)
            pieces.append((s, g, full))

    def chunk_step(n):
        cidx = (nchunk - 1 - n) if reverse else n
        off = pl.multiple_of(cidx * c, c)
        eoff = pl.multiple_of(cidx * (nh * c), nh * c)
        qc = q_ref[pl.ds(off, c), :]
        bc = b_s[pl.ds(off, c), :]
        kc = k_s[pl.ds(off, c), :]
        vc = v_ref[pl.ds(off, c), :]
        parts = []
        for s, g, full in pieces:
            sl = slice(g * half, (g + 1) * half)
            e = jnp.exp2(bc[sl] - bc[s:s + 1, :])
            if not full:
                t = rows + g * half
                e = jnp.where((t <= s) if reverse else (t >= s), e, 0.0)
            parts.append((qc[sl] * kc[s:s + 1, :]) * e)
        p = jnp.concatenate(parts, axis=0).astype(BF16)
        r = jnp.dot(p, hm, preferred_element_type=F32)
        og = [jnp.zeros((half, w), F32), jnp.zeros((half, w), F32)]
        for i, (s, g, _) in enumerate(pieces):
            og[g] = og[g] + r[i * half:(i + 1) * half, :] * vc[s:s + 1, :]
        st = st_ref[...]
        ri = _dot_nt(qe_s[pl.ds(eoff, nh * c), :], st)
        inter = jnp.concatenate([ri[h * c:(h + 1) * c, :] for h in range(nh)], axis=1)
        o_ref[pl.ds(off, c), :] = jnp.concatenate(og, axis=0) + inter
        vexp = jnp.concatenate([vc[:, h * hd:(h + 1) * hd] for h in range(nh)], axis=0)
        kv = lax.dot_general(vexp.astype(BF16), ke_s[pl.ds(eoff, nh * c), :], (((0,), (0,)), ((), ())),
                             preferred_element_type=F32)
        blast = bc[0:1, :] if reverse else bc[c - 1:c, :]
        st_ref[...] = jnp.exp2(blast) * st + kv

    return chunk_step


def _hgrn_kernel(qf_ref, ff_ref, vf_ref, qb_ref, fb_ref, vb_ref, lb_ref, trif_ref, trib_ref, of_ref, ob_ref,
                 stf_ref, stb_ref, bf_s, kf_s, qef_s, kef_s, bb_s, kb_s, qeb_s, keb_s):
    @pl.when(pl.program_id(1) == 0)
    def _():
        stf_ref[...] = jnp.zeros_like(stf_ref)
        stb_ref[...] = jnp.zeros_like(stb_ref)

    step_f = _hgrn_direction(qf_ref, ff_ref, vf_ref, lb_ref[0:1, :], trif_ref, of_ref, stf_ref,
                             bf_s, kf_s, qef_s, kef_s, False)
    step_b = _hgrn_direction(qb_ref, fb_ref, vb_ref, lb_ref[1:2, :], trib_ref, ob_ref, stb_ref,
                             bb_s, kb_s, qeb_s, keb_s, True)

    def body(n, carry):
        step_f(n)
        step_b(n)
        return carry

    lax.fori_loop(0, TM // HG_CHUNK, body, 0, unroll=2)


def hgrn(z, lb, bsz, tpb):
    m = z.shape[0]
    w = BRANCH_W
    hd = w // HG_HEADS
    ri = jnp.arange(TM)[:, None]
    ci = jnp.arange(TM)[None, :]
    same = (ri // HG_CHUNK) == (ci // HG_CHUNK)
    tri = jnp.stack([same & (ci <= ri), same & (ci > ri), same & (ci >= ri), same & (ci < ri)])
    tri = tri.astype(BF16).reshape(2, 2, TM, TM)

    def spec(cb, d):
        return pl.BlockSpec((TM, w), lambda b, s: (_scan_tiles(b, s, tpb)[d], cb))

    out_spec = [pl.BlockSpec((TM, w), lambda b, s: (_scan_tiles(b, s, tpb)[0], 0)),
                pl.BlockSpec((TM, w), lambda b, s: (_scan_tiles(b, s, tpb)[1], 0))]
    return pl.pallas_call(
        _hgrn_kernel,
        grid=(bsz, tpb),
        in_specs=[spec(CB_HG_Q, 0), spec(CB_HG_FF, 0), spec(CB_HG_I, 0),
                  spec(CB_HG_Q, 1), spec(CB_HG_FB, 1), spec(CB_HG_I, 1),
                  pl.BlockSpec((2, w), lambda b, s: (0, 0)),
                  pl.BlockSpec((None, 2, TM, TM), lambda b, s: (0, 0, 0, 0)),
                  pl.BlockSpec((None, 2, TM, TM), lambda b, s: (1, 0, 0, 0))],
        out_specs=out_spec,
        out_shape=[jax.ShapeDtypeStruct((m, w), F32)] * 2,
        scratch_shapes=[pltpu.VMEM((hd, w), F32), pltpu.VMEM((hd, w), F32)]
                       + [pltpu.VMEM((TM, w), F32), pltpu.VMEM((TM, w), F32),
                          pltpu.VMEM((HG_HEADS * TM, w), BF16), pltpu.VMEM((HG_HEADS * TM, w), BF16)] * 2,
        compiler_params=_cparams(("arbitrary", "arbitrary")),
        name="hgrn2",
    )(z, z, z, z, z, z, lb, tri, tri)


def _ret_direction(q_ref, k_ref, qr_ref, kr_ref, v_ref, cos_ref, sin_ref, lgq, lg_ref, d, o_ref, st_ref,
                   reverse):
    nq = RET_HEADS * RET_DK
    nv = RET_HEADS * RET_DV
    cos = cos_ref[...]
    sin = sin_ref[...]
    qx = q_ref[...] * cos + qr_ref[...] * sin
    kx = (k_ref[...] * cos + kr_ref[...] * sin) * (RET_DK ** -0.5)
    v = v_ref[...]
    t = lax.broadcasted_iota(jnp.int32, (TM, 1), 0).astype(F32)
    pos = (TM - 1.0 - t) if reverse else t
    qd = qx * jnp.exp((pos + 1.0) * lgq)
    kd = kx * jnp.exp((TM - 1.0 - pos) * lgq)
    ti = lax.broadcasted_iota(jnp.int32, (TM, TM), 0)
    si = lax.broadcasted_iota(jnp.int32, (TM, TM), 1)
    rel = ((si - ti) if reverse else (ti - si)).astype(F32)
    live = rel >= 0.0
    relc = jnp.where(live, rel, 0.0)
    lane_q = lax.broadcasted_iota(jnp.int32, (1, nq), 1) // RET_DK
    lane_v = lax.broadcasted_iota(jnp.int32, (1, nv), 1) // RET_DV
    st = st_ref[...]
    o = _dot_nt(qd, st)
    vb = v.astype(BF16)
    for h in range(RET_HEADS):
        dmat = jnp.where(live, jnp.exp(relc * lg_ref[d, h]), 0.0)
        sc = _dot_nt(jnp.where(lane_q == h, qx, 0.0), kx) * dmat
        oh = jnp.dot(sc.astype(BF16), vb, preferred_element_type=F32)
        o = o + jnp.where(lane_v == h, oh, 0.0)
    o_ref[...] = o
    kv = _dot(v.T, kd)
    hr = lax.broadcasted_iota(jnp.int32, (nv, nq), 0) // RET_DV
    hc = lax.broadcasted_iota(jnp.int32, (nv, nq), 1) // RET_DK
    st_ref[...] = st * jnp.exp(float(TM) * lgq) + jnp.where(hr == hc, kv, 0.0)


def _ret_kernel(lg_ref, qf, kf, qrf, krf, vf, cosf, sinf, qb, kb, qrb, krb, vb, cosb, sinb, lgq_ref,
                of_ref, ob_ref, stf_ref, stb_ref):
    @pl.when(pl.program_id(1) == 0)
    def _():
        stf_ref[...] = jnp.zeros_like(stf_ref)
        stb_ref[...] = jnp.zeros_like(stb_ref)

    _ret_direction(qf, kf, qrf, krf, vf, cosf, sinf, lgq_ref[0:1, :], lg_ref, 0, of_ref, stf_ref, False)
    _ret_direction(qb, kb, qrb, krb, vb, cosb, sinb, lgq_ref[1:2, :], lg_ref, 1, ob_ref, stb_ref, True)


def retention(z, cos_tab, sin_tab, log_gamma, bsz, tpb):
    m = z.shape[0]
    nq = RET_HEADS * RET_DK
    nv = RET_HEADS * RET_DV
    lgq = jnp.repeat(log_gamma, RET_DK, axis=1)

    def zspec(cb, width, d):
        return pl.BlockSpec((TM, width), lambda b, s: (_scan_tiles(b, s, tpb)[d], cb))

    def tspec(d):
        return pl.BlockSpec((TM, nq), lambda b, s: (_scan_tiles(0, s, tpb)[d], 0))

    def side(d):
        return [zspec(CB128_RT_Q, nq, d), zspec(CB128_RT_K, nq, d), zspec(CB128_RT_QR, nq, d),
                zspec(CB128_RT_KR, nq, d), zspec(CB_RT_V, nv, d), tspec(d), tspec(d)]

    return pl.pallas_call(
        _ret_kernel,
        grid=(bsz, tpb),
        in_specs=[pl.BlockSpec(memory_space=pltpu.SMEM)] + side(0) + side(1)
                 + [pl.BlockSpec((2, nq), lambda b, s: (0, 0))],
        out_specs=[pl.BlockSpec((TM, nv), lambda b, s: (_scan_tiles(b, s, tpb)[0], 0)),
                   pl.BlockSpec((TM, nv), lambda b, s: (_scan_tiles(b, s, tpb)[1], 0))],
        out_shape=[jax.ShapeDtypeStruct((m, nv), F32)] * 2,
        scratch_shapes=[pltpu.VMEM((nv, nq), F32), pltpu.VMEM((nv, nq), F32)],
        compiler_params=_cparams(("arbitrary", "arbitrary")),
        name="retention",
    )(log_gamma, z, z, z, z, z, cos_tab, sin_tab, z, z, z, z, z, cos_tab, sin_tab, lgq)


S5_NP = S5_GROUPS * S5_STATE


def _s5_direction(u_ref, bre_ref, bim_ref, cre_ref, cim_ref, tab_ref, o_ref, car_ref, xr_s, xi_s, reverse):
    nb = TM // SUBLANES
    ub = u_ref[...].astype(BF16)
    xr = jnp.dot(ub, bre_ref[0], preferred_element_type=F32).reshape(nb, SUBLANES, S5_NP)
    xi = jnp.dot(ub, bim_ref[0], preferred_element_type=F32).reshape(nb, SUBLANES, S5_NP)
    sub = lax.broadcasted_iota(jnp.int32, (1, SUBLANES, 1), 1)
    for k, row in ((1, 8), (2, 9), (4, 10)):
        ar = tab_ref[0, 0, row:row + 1, :]
        ai = tab_ref[0, 1, row:row + 1, :]
        shift = (SUBLANES - k) if reverse else k
        keep = (sub < SUBLANES - k) if reverse else (sub >= k)
        sr = jnp.where(keep, pltpu.roll(xr, shift, 1), 0.0)
        si = jnp.where(keep, pltpu.roll(xi, shift, 1), 0.0)
        xr, xi = xr + ar * sr - ai * si, xi + ar * si + ai * sr
    xr_s[...] = xr.reshape(TM, S5_NP)
    xi_s[...] = xi.reshape(TM, S5_NP)
    pr = tab_ref[0, 0, 0:SUBLANES, :]
    pi = tab_ref[0, 1, 0:SUBLANES, :]

    def body(n, carry):
        cr, ci = carry
        blk = (nb - 1 - n) if reverse else n
        off = pl.multiple_of(blk * SUBLANES, SUBLANES)
        br = xr_s[pl.ds(off, SUBLANES), :] + pr * cr - pi * ci
        bi = xi_s[pl.ds(off, SUBLANES), :] + pr * ci + pi * cr
        xr_s[pl.ds(off, SUBLANES), :] = br
        xi_s[pl.ds(off, SUBLANES), :] = bi
        if reverse:
            return br[0:1, :], bi[0:1, :]
        return br[SUBLANES - 1:SUBLANES, :], bi[SUBLANES - 1:SUBLANES, :]

    cr, ci = lax.fori_loop(0, nb, body, (car_ref[0:1, :], car_ref[1:2, :]))
    car_ref[0:1, :] = cr
    car_ref[1:2, :] = ci
    o_ref[...] = (jnp.dot(xr_s[...].astype(BF16), cre_ref[0], preferred_element_type=F32)
                  - jnp.dot(xi_s[...].astype(BF16), cim_ref[0], preferred_element_type=F32))


def _s5_kernel(uf, ub, bre, bim, cre, cim, tabf, tabb, bre2, bim2, cre2, cim2, of_ref, ob_ref,
               carf, carb, xr_s, xi_s):
    @pl.when(pl.program_id(1) == 0)
    def _():
        carf[...] = jnp.zeros_like(carf)
        carb[...] = jnp.zeros_like(carb)

    _s5_direction(uf, bre, bim, cre, cim, tabf, of_ref, carf, xr_s, xi_s, False)
    _s5_direction(ub, bre2, bim2, cre2, cim2, tabb, ob_ref, carb, xr_s, xi_s, True)


def s5(z, bblk_re, bblk_im, cblk_re, cblk_im, tab, bsz, tpb):
    m = z.shape[0]
    w = BRANCH_W

    def pspec(shape, d):
        return pl.BlockSpec((1,) + shape, lambda b, s: (d,) + (0,) * len(shape))

    def uspec(d):
        return pl.BlockSpec((TM, w), lambda b, s: (_scan_tiles(b, s, tpb)[d], CB_S5))

    return pl.pallas_call(
        _s5_kernel,
        grid=(bsz, tpb),
        in_specs=[uspec(0), uspec(1),
                  pspec((w, S5_NP), 0), pspec((w, S5_NP), 0), pspec((S5_NP, w), 0), pspec((S5_NP, w), 0),
                  pspec((2, 16, S5_NP), 0), pspec((2, 16, S5_NP), 1),
                  pspec((w, S5_NP), 1), pspec((w, S5_NP), 1), pspec((S5_NP, w), 1), pspec((S5_NP, w), 1)],
        out_specs=[pl.BlockSpec((TM, w), lambda b, s: (_scan_tiles(b, s, tpb)[0], 0)),
                   pl.BlockSpec((TM, w), lambda b, s: (_scan_tiles(b, s, tpb)[1], 0))],
        out_shape=[jax.ShapeDtypeStruct((m, w), F32)] * 2,
        scratch_shapes=[pltpu.VMEM((2, S5_NP), F32), pltpu.VMEM((2, S5_NP), F32),
                        pltpu.VMEM((TM, S5_NP), F32), pltpu.VMEM((TM, S5_NP), F32)],
        compiler_params=_cparams(("arbitrary", "arbitrary")),
        name="s5",
    )(z, z, bblk_re, bblk_im, cblk_re, cblk_im, tab, tab, bblk_re, bblk_im, cblk_re, cblk_im)


def s5_params(lam_re, lam_im, log_dt, b_re, b_im, c_re, c_im):
    lr = jnp.minimum(lam_re.astype(F32), -1e-4)
    li = lam_im.astype(F32)
    dt = jnp.exp(log_dt.astype(F32))[..., None]

    def apow(k):
        mag = jnp.exp(k * lr * dt)
        return mag * jnp.cos(k * li * dt), mag * jnp.sin(k * li * dt)

    ar, ai = apow(1.0)
    den = lr * lr + li * li
    gr = ((ar - 1.0) * lr + ai * li) / den
    gi = (ai * lr - (ar - 1.0) * li) / den
    bbr = gr[..., None] * b_re - gi[..., None] * b_im
    bbi = gr[..., None] * b_im + gi[..., None] * b_re
    eye = jnp.eye(S5_GROUPS, dtype=F32)

    def blk_in(bb):
        t = jnp.einsum('dgpc,gh->dgchp', bb, eye)
        return t.reshape(2, S5_GROUPS * S5_GROUP_CH, S5_NP).astype(BF16)

    def blk_out(cc):
        t = jnp.einsum('dgcp,gh->dgphc', cc.astype(F32), eye)
        return t.reshape(2, S5_NP, S5_GROUPS * S5_GROUP_CH).astype(BF16)

    rows_r, rows_i = [], []
    for j in range(SUBLANES):
        pr_f, pi_f = apow(float(j + 1))
        pr_b, pi_b = apow(float(SUBLANES - j))
        rows_r.append(jnp.stack([pr_f[0], pr_b[1]]))
        rows_i.append(jnp.stack([pi_f[0], pi_b[1]]))
    for k in (1.0, 2.0, 4.0):
        pr_k, pi_k = apow(k)
        rows_r.append(pr_k)
        rows_i.append(pi_k)
    zero = jnp.zeros_like(rows_r[0])
    while len(rows_r) < 16:
        rows_r.append(zero)
        rows_i.append(zero)
    tab_r = jnp.stack(rows_r, axis=1).reshape(2, 16, S5_NP)
    tab_i = jnp.stack(rows_i, axis=1).reshape(2, 16, S5_NP)
    tab = jnp.stack([tab_r, tab_i], axis=1)
    return blk_in(bbr), blk_in(bbi), blk_out(c_re), blk_out(c_im), tab


def _hy_prep_kernel(u_ref, prev_ref, next_ref, w_ref, b_ref, x0_ref, vx_ref, *, tpb):
    i = pl.program_id(0)
    r = i % tpb
    first = (r == 0) | (r == tpb - 1)
    last = r >= tpb - 2
    u = u_ref[...]
    w = w_ref[...]
    up_row = jnp.where(first, 0.0, prev_ref[SUBLANES - 1:SUBLANES, :])
    dn_row = jnp.where(last, 0.0, next_ref[0:1, :])
    row = lax.broadcasted_iota(jnp.int32, (TM, 1), 0)
    um = jnp.where(row == 0, up_row, pltpu.roll(u, 1, 0))
    un = jnp.where(row == TM - 1, dn_row, pltpu.roll(u, TM - 1, 0))
    y = um * w[0:1, :] + u * w[1:2, :] + un * w[2:3, :] + b_ref[...]
    x0_ref[...] = y[:, 0:HY_W]
    vx_ref[...] = y[:, 2 * HY_W:3 * HY_W] * y[:, HY_W:2 * HY_W]


def hy_prep(z, conv_w, conv_b, bsz, tpb):
    m = z.shape[0]
    w3 = 3 * HY_W
    rb = TM // SUBLANES
    nrb = m // SUBLANES
    return pl.pallas_call(
        functools.partial(_hy_prep_kernel, tpb=tpb),
        grid=(m // TM,),
        in_specs=[pl.BlockSpec((TM, w3), lambda i: (i, 0)),
                  pl.BlockSpec((SUBLANES, w3), lambda i: (jnp.maximum(i * rb - 1, 0), 0)),
                  pl.BlockSpec((SUBLANES, w3), lambda i: (jnp.minimum((i + 1) * rb, nrb - 1), 0)),
                  pl.BlockSpec((3, w3), lambda i: (0, 0)),
                  pl.BlockSpec((1, w3), lambda i: (0, 0))],
        out_specs=[pl.BlockSpec((TM, HY_W), lambda i: (i, 0)),
                   pl.BlockSpec((TM, HY_W), lambda i: (_split_tile(i, bsz, tpb), 0))],
        out_shape=[jax.ShapeDtypeStruct((m, HY_W), F32)] * 2,
        compiler_params=_cparams(("parallel",)),
        name="hy_prep",
    )(z, z, z, conv_w, conv_b.reshape(1, w3))


def _hy_filter_kernel(zf_ref, w1_ref, b1_ref, f1_ref, w2_ref, b2_ref, f2_ref, w3_ref, dl_ref,
                      hfb_ref, ss_ref):
    i = pl.program_id(0)
    zf = zf_ref[...]
    hid = jnp.sin(f1_ref[...] * (_dot_hi(zf, w1_ref[...]) + b1_ref[...]))
    hid = jnp.sin(f2_ref[...] * (_dot_hi(hid, w2_ref[...]) + b2_ref[...]))
    filt = _dot_hi(hid, w3_ref[...])
    window = jnp.exp(-zf[:, 0:1] * dl_ref[...])
    hf = filt[:, 0:HY_W] * window
    row = lax.broadcasted_iota(jnp.int32, (TM, 1), 0) + i * TM
    hb = jnp.where(row == 0, 0.0, filt[:, HY_W:2 * HY_W] * window)
    hfb_ref[0] = hf
    hfb_ref[1] = hb
    part = jnp.sum(hf * hf + hb * hb, axis=0, keepdims=True)

    @pl.when(i == 0)
    def _():
        ss_ref[...] = jnp.zeros_like(ss_ref)

    ss_ref[...] = ss_ref[...] + part


def hy_filter(length, w1, b1, fr1, w2, b2, fr2, w3):
    t = jnp.linspace(0.0, 1.0, length, dtype=F32)[:, None]
    wv = (2.0 * math.pi / length) * jnp.arange(length, dtype=F32)[:, None]
    f = jnp.linspace(1e-4, HY_BANDS - 1.0, HY_BANDS, dtype=F32)[None, :]
    zfeat = jnp.concatenate([t, jnp.cos(f * wv), -jnp.sin(f * wv)], axis=-1)
    zfeat = jnp.pad(zfeat, ((0, 0), (0, LANES - HY_EMB)))
    w1p = jnp.pad(w1, ((0, LANES - HY_EMB), (0, 0)))
    deltas = jnp.abs(jnp.linspace(math.log(HY_TARGET) / HY_SLOW_DECAY,
                                  math.log(HY_TARGET) / HY_FAST_DECAY, HY_W, dtype=F32))[None, :]
    full = lambda shape: pl.BlockSpec(shape, lambda i: (0,) * len(shape))
    return pl.pallas_call(
        _hy_filter_kernel,
        grid=(length // TM,),
        in_specs=[pl.BlockSpec((TM, LANES), lambda i: (i, 0)),
                  full((LANES, HY_ORDER)), full((1, HY_ORDER)), full((1, HY_ORDER)),
                  full((HY_ORDER, HY_ORDER)), full((1, HY_ORDER)), full((1, HY_ORDER)),
                  full((HY_ORDER, 2 * HY_W)), full((1, HY_W))],
        out_specs=[pl.BlockSpec((2, TM, HY_W), lambda i: (0, i, 0)),
                   pl.BlockSpec((1, HY_W), lambda i: (0, 0))],
        out_shape=[jax.ShapeDtypeStruct((2, length, HY_W), F32), jax.ShapeDtypeStruct((1, HY_W), F32)],
        compiler_params=_cparams(("arbitrary",)),
        name="hy_filter",
    )(zfeat, w1p, b1.reshape(1, -1), fr1.reshape(1, -1), w2, b2.reshape(1, -1), fr2.reshape(1, -1), w3, deltas)


def _dft_consts(length):
    n = 2 * length
    ns = n // FFT_FAST
    a = jnp.arange(ns, dtype=jnp.int32)
    ang = ((a[:, None] * a[None, :]) % ns).astype(F32) * (2.0 * math.pi / ns)
    f_fwd = jnp.concatenate([jnp.cos(ang), -jnp.sin(ang)], axis=0)
    f_inv = jnp.concatenate([jnp.cos(ang), -jnp.sin(ang)], axis=1)
    b = jnp.arange(FFT_FAST, dtype=jnp.int32)
    kk = a[:, None, None] + ns * b[None, :, None]
    idx = (kk * b[None, None, :]) % n
    th = idx.astype(F32) * (2.0 * math.pi / n)
    mr, mi = jnp.cos(th), -jnp.sin(th)
    m_fwd = jnp.concatenate([jnp.concatenate([mr, -mi], axis=2),
                             jnp.concatenate([mi, mr], axis=2)], axis=1)
    ir = jnp.transpose(mr, (0, 2, 1)) / n
    ii = jnp.transpose(-mi, (0, 2, 1)) / n
    m_inv = jnp.concatenate([jnp.concatenate([ir, -ii], axis=2),
                             jnp.concatenate([ii, ir], axis=2)], axis=1)
    return tuple(t.astype(BF16) for t in (f_fwd, f_inv, m_fwd, m_inv))


def _dft_slow_fwd_kernel(f_ref, x_ref, o_ref):
    ns = o_ref.shape[2]
    for bl in range(x_ref.shape[1]):
        r = jnp.dot(f_ref[...], x_ref[:, bl, :].astype(BF16), preferred_element_type=F32)
        o_ref[0, 0, :, bl, :] = r[0:ns]
        o_ref[0, 1, :, bl, :] = r[ns:]


def dft_slow_fwd(fmat, x3, bx):
    two_ns, na = fmat.shape
    ns = two_ns // 2
    cw = x3.shape[-1]
    kb = SUBLANES
    return pl.pallas_call(
        _dft_slow_fwd_kernel,
        grid=(bx, FFT_FAST // kb),
        in_specs=[pl.BlockSpec((two_ns, na), lambda b, j: (0, 0)),
                  pl.BlockSpec((na, kb, cw), lambda b, j: (b, j, 0))],
        out_specs=pl.BlockSpec((1, 2, ns, kb, cw), lambda b, j: (b, 0, 0, j, 0)),
        out_shape=jax.ShapeDtypeStruct((bx, 2, ns, FFT_FAST, cw), F32),
        compiler_params=_cparams(("parallel", "parallel")),
        name="dft_slow_fwd",
    )(fmat, x3)


def _dft_slow_inv_kernel(f_ref, b_ref, o_ref, *, bx):
    @pl.when(pl.program_id(0) < bx)
    def _():
        for bl in range(o_ref.shape[1]):
            bb = jnp.concatenate([b_ref[0, 0, :, bl, :], b_ref[0, 1, :, bl, :]], axis=0).astype(BF16)
            o_ref[:, bl, :] = jnp.dot(f_ref[...], bb, preferred_element_type=F32)

    @pl.when(pl.program_id(0) >= bx)
    def _():
        o_ref[...] = jnp.zeros_like(o_ref)


def dft_slow_inv(fmat, bv):
    bx, _, ns, nf, cw = bv.shape
    na = fmat.shape[0]
    kb = SUBLANES
    return pl.pallas_call(
        functools.partial(_dft_slow_inv_kernel, bx=bx),
        grid=(bx + 1, nf // kb),
        in_specs=[pl.BlockSpec((na, 2 * ns), lambda b, j: (0, 0)),
                  pl.BlockSpec((1, 2, ns, kb, cw), lambda b, j: (jnp.minimum(b, bx - 1), 0, 0, j, 0))],
        out_specs=pl.BlockSpec((na, kb, cw), lambda b, j: (b, j, 0)),
        out_shape=jax.ShapeDtypeStruct(((bx + 1) * na, nf, cw), F32),
        compiler_params=_cparams(("parallel", "parallel")),
        name="dft_slow_inv",
    )(fmat, bv)


FFT_GROUP = 8


def _dft_fast_filter_kernel(m_ref, a_ref, sc_ref, o_ref):
    sc = sc_ref[...]
    for j in range(FFT_GROUP):
        af = jnp.concatenate([a_ref[0, 0, j], a_ref[0, 1, j]], axis=0).astype(BF16)
        ab = jnp.concatenate([a_ref[1, 0, j], a_ref[1, 1, j]], axis=0).astype(BF16)
        xf = jnp.dot(m_ref[j], af, preferred_element_type=F32)
        xb = jnp.dot(m_ref[j], ab, preferred_element_type=F32)
        o_ref[0, j] = (xf[0:FFT_FAST] + xb[0:FFT_FAST]) * sc
        o_ref[1, j] = (xf[FFT_FAST:] - xb[FFT_FAST:]) * sc


def dft_fast_filter(m_fwd, a5, scale):
    _, _, ns, nf, cw = a5.shape
    g = FFT_GROUP
    return pl.pallas_call(
        _dft_fast_filter_kernel,
        grid=(ns // g,),
        in_specs=[pl.BlockSpec((g, 2 * nf, 2 * nf), lambda i: (i, 0, 0)),
                  pl.BlockSpec((2, 2, g, nf, cw), lambda i: (0, 0, i, 0, 0)),
                  pl.BlockSpec((1, cw), lambda i: (0, 0))],
        out_specs=pl.BlockSpec((2, g, nf, cw), lambda i: (0, i, 0, 0)),
        out_shape=jax.ShapeDtypeStruct((2, ns, nf, cw), F32),
        compiler_params=_cparams(("parallel",)),
        name="dft_fast_filter",
    )(m_fwd, a5, scale)


def _dft_fast_conv_kernel(mf_ref, mi_ref, a_ref, k_ref, o_ref):
    for j in range(FFT_GROUP):
        a = jnp.concatenate([a_ref[0, 0, j], a_ref[0, 1, j]], axis=0).astype(BF16)
        x = jnp.dot(mf_ref[j], a, preferred_element_type=F32)
        xr, xi = x[0:FFT_FAST], x[FFT_FAST:]
        kr, ki = k_ref[0, j], k_ref[1, j]
        y = jnp.concatenate([xr * kr - xi * ki, xr * ki + xi * kr], axis=0).astype(BF16)
        bv = jnp.dot(mi_ref[j], y, preferred_element_type=F32)
        o_ref[0, 0, j] = bv[0:FFT_FAST]
        o_ref[0, 1, j] = bv[FFT_FAST:]


def dft_fast_conv(m_fwd, m_inv, a5, kf):
    bx, _, ns, nf, cw = a5.shape
    g = FFT_GROUP
    return pl.pallas_call(
        _dft_fast_conv_kernel,
        grid=(bx, ns // g),
        in_specs=[pl.BlockSpec((g, 2 * nf, 2 * nf), lambda b, i: (i, 0, 0)),
                  pl.BlockSpec((g, 2 * nf, 2 * nf), lambda b, i: (i, 0, 0)),
                  pl.BlockSpec((1, 2, g, nf, cw), lambda b, i: (b, 0, i, 0, 0)),
                  pl.BlockSpec((2, g, nf, cw), lambda b, i: (0, i, 0, 0))],
        out_specs=pl.BlockSpec((1, 2, g, nf, cw), lambda b, i: (b, 0, i, 0, 0)),
        out_shape=jax.ShapeDtypeStruct((bx, 2, ns, nf, cw), F32),
        compiler_params=_cparams(("parallel", "parallel")),
        name="dft_fast_conv",
    )(m_fwd, m_inv, a5, kf)


def hyena_long_conv(vx_rows, bsz, length, hfb, scale):
    cw = vx_rows.shape[-1]
    ns = 2 * length // FFT_FAST
    f_fwd, f_inv, m_fwd, m_inv = _dft_consts(length)
    fa = dft_slow_fwd(f_fwd[:, :ns // 2], hfb.reshape(-1, FFT_FAST, cw), 2)
    kf = dft_fast_filter(m_fwd, fa, scale)
    xa = dft_slow_fwd(f_fwd[:, :ns // 2], vx_rows.reshape(-1, FFT_FAST, cw), bsz)
    bv = dft_fast_conv(m_fwd, m_inv, xa, kf)
    return dft_slow_inv(f_inv[:ns // 2], bv).reshape((bsz + 1) * length, cw)


def _hy_ctx_conv_kernel(v_ref, g_ref, y_hbm_ref, o_ref, *, lc):
    del y_hbm_ref
    nb = lc // SUBLANES

    def body(gi, acc):
        base = pl.multiple_of((nb - 1 - gi) * SUBLANES, SUBLANES)
        win = g_ref[pl.ds(base, lc + SUBLANES), :]
        for j in range(SUBLANES):
            vrow = v_ref[pl.ds(gi * SUBLANES + j, 1), :]
            acc = acc + win[SUBLANES - 1 - j:SUBLANES - 1 - j + lc, :] * vrow
        return acc

    o_ref[...] = lax.fori_loop(0, nb, body, jnp.zeros((lc, HY_W), F32))


def hy_ctx_conv(vx, gwin, yconv, bsz, tpb, lc):
    ctx_tile = lambda b: (bsz * (tpb - 1) + b, 0)
    return pl.pallas_call(
        functools.partial(_hy_ctx_conv_kernel, lc=lc),
        grid=(bsz,),
        in_specs=[pl.BlockSpec((lc, HY_W), ctx_tile),
                  pl.BlockSpec((2 * lc, HY_W), lambda b: (0, 0)),
                  pl.BlockSpec(memory_space=pl.ANY)],
        out_specs=pl.BlockSpec((lc, HY_W), ctx_tile),
        out_shape=jax.ShapeDtypeStruct(yconv.shape, F32),
        input_output_aliases={2: 0},
        compiler_params=_cparams(("parallel",)),
        name="hy_ctx_conv",
    )(vx, gwin, yconv)


ROUTE_E1, ROUTE_E2, ROUTE_W1, ROUTE_W2, ROUTE_R1, ROUTE_R2 = 0, 1, 2, 3, 4, 5
NEG_BIG = -1e30


def _gelu_tanh(x):
    return 0.5 * x * (1.0 + jnp.tanh(math.sqrt(2.0 / math.pi) * (x + 0.044715 * (x * x * x))))


def _lane_pick(slab, lane, idx):
    return jnp.sum(jnp.where(lane == idx, slab, 0.0), axis=-1, keepdims=True)


def _merge_kernel(x_ref, mod_ref, n1_ref, n2_ref, wg0_ref, wg1_ref, wg2_ref, wg3_ref, zhg_ref, zrt_ref, zs5_ref,
                  hgf_ref, hgb_ref, rtf_ref, rtb_ref, s5f_ref, s5b_ref, yc_ref, vx_ref, x0_ref,
                  vec_ref, glu1_ref, glu2_ref, wb_ref, wo_ref, wr_ref, br_ref,
                  xo_ref, h2_ref, route_ref, cnt_ref, base_ref, *, d):
    i = pl.program_id(0)
    w = BRANCH_W
    x = x_ref[...]
    mod = mod_ref[0]
    h = _modulated_norm(x, n1_ref[...], mod[:, 0:d], mod[:, d:2 * d]).astype(BF16)

    hd = w // HG_HEADS
    hr = lax.broadcasted_iota(jnp.int32, (w, w), 0) // hd
    hc = lax.broadcasted_iota(jnp.int32, (w, w), 1) // hd
    hm = jnp.where(hr == hc, 1.0, 0.0).astype(BF16)
    inv = 1.0 / hd

    hg = hgf_ref[...] + hgb_ref[...]
    ms = _split_dot(hg * hg, hm) * inv
    br_hg = hg * lax.rsqrt(ms + EPS) * vec_ref[0:1, :] * _silu(zhg_ref[...])

    rt = rtf_ref[...] + rtb_ref[...]
    cen = rt - _split_dot(rt, hm) * inv
    var = _split_dot(cen * cen, hm) * inv
    br_rt = cen * lax.rsqrt(var + EPS) * vec_ref[1:2, :] * _silu(zrt_ref[...])

    vx = vx_ref[...]
    br_hy = (yc_ref[...] + vx * vec_ref[3:4, :]) * x0_ref[...]

    y5 = _gelu_tanh(s5f_ref[...] + s5b_ref[...] + zs5_ref[...] * vec_ref[2:3, :]).astype(BF16)
    br_s5 = (jnp.dot(y5, glu1_ref[...], preferred_element_type=F32)
             * _sigmoid(jnp.dot(y5, glu2_ref[...], preferred_element_type=F32)))

    merged = jnp.zeros((TM, d), F32)
    for n, (br, wg_ref) in enumerate(zip((br_hg, br_hy, br_rt, br_s5), (wg0_ref, wg1_ref, wg2_ref, wg3_ref))):
        gate = _sigmoid(jnp.dot(h, wg_ref[...], preferred_element_type=F32))
        merged = merged + gate * jnp.dot(br.astype(BF16), wb_ref[n], preferred_element_type=F32)
    mix = jnp.dot(merged.astype(BF16), wo_ref[...], preferred_element_type=F32)
    x_mid = x + mod[:, 2 * d:3 * d] * mix
    xo_ref[...] = x_mid
    h2 = _modulated_norm(x_mid, n2_ref[...], mod[:, 3 * d:4 * d], mod[:, 4 * d:5 * d])
    h2_ref[...] = h2

    logits = _dot3(h2, wr_ref[...]) + br_ref[...]
    lane = lax.broadcasted_iota(jnp.int32, (TM, LANES), 1).astype(F32)
    gl = jnp.where(lane < MOE_GROUPS, logits, NEG_BIG)
    gmax = jnp.max(gl, axis=-1, keepdims=True)
    grp_p = 1.0 / jnp.sum(jnp.exp(gl - gmax), axis=-1, keepdims=True)
    gi = jnp.min(jnp.where(gl == gmax, lane, float(LANES)), axis=-1, keepdims=True)
    lo = MOE_GROUPS + MOE_PER_GROUP * gi
    el = jnp.where((lane >= lo) & (lane < lo + MOE_PER_GROUP), logits, NEG_BIG)
    m1 = jnp.max(el, axis=-1, keepdims=True)
    i1 = jnp.min(jnp.where(el == m1, lane, float(LANES)), axis=-1, keepdims=True)
    el2 = jnp.where(lane == i1, NEG_BIG, el)
    m2 = jnp.max(el2, axis=-1, keepdims=True)
    i2 = jnp.min(jnp.where(el2 == m2, lane, float(LANES)), axis=-1, keepdims=True)
    ratio = jnp.exp(m2 - m1)
    wt1 = grp_p / (1.0 + ratio)
    wt2 = grp_p * ratio / (1.0 + ratio)
    e1 = i1 - MOE_GROUPS
    e2 = i2 - MOE_GROUPS

    @pl.when(i == 0)
    def _():
        base_ref[...] = jnp.zeros_like(base_ref)

    oh1 = jnp.where(lane == e1, 1.0, 0.0)
    oh2 = jnp.where(lane == e2, 1.0, 0.0)
    ri = lax.broadcasted_iota(jnp.int32, (TM, TM), 0)
    ci = lax.broadcasted_iota(jnp.int32, (TM, TM), 1)
    below = jnp.where(ci < ri, 1.0, 0.0).astype(BF16)
    c1 = jnp.dot(below, oh1.astype(BF16), preferred_element_type=F32)
    c2 = jnp.dot(below, oh2.astype(BF16), preferred_element_type=F32)
    tot1 = jnp.sum(oh1, axis=0, keepdims=True)
    tot2 = jnp.sum(oh2, axis=0, keepdims=True)
    base = base_ref[...]
    r1 = jnp.sum(oh1 * (base + c1), axis=-1, keepdims=True)
    r2 = jnp.sum(oh2 * (base + tot1 + c2), axis=-1, keepdims=True)
    base = base + tot1 + tot2
    base_ref[...] = base
    cnt_ref[...] = jnp.broadcast_to(base, cnt_ref.shape)
    route = jnp.zeros((TM, LANES), F32)
    for idx, val in ((ROUTE_E1, e1), (ROUTE_E2, e2), (ROUTE_W1, wt1), (ROUTE_W2, wt2),
                     (ROUTE_R1, r1), (ROUTE_R2, r2)):
        route = jnp.where(lane == idx, val, route)
    route_ref[...] = route


def merge(x_all, mod3, n1, n2, w_in_bf, layer, z, hgf, hgb, rtf, rtb, s5f, s5b, yconv, vx, x0c,
          vecs, glu1, glu2, w_branch, w_out, w_router, b_router, bsz, tpb):
    m, d = x_all.shape
    w = BRANCH_W
    row = lambda width: pl.BlockSpec((TM, width), lambda i: (i, 0))
    srow = pl.BlockSpec((TM, w), lambda i: (_split_tile(i, bsz, tpb), 0))
    zcol = lambda cb: pl.BlockSpec((TM, w), lambda i: (i, cb))
    full = lambda shape: pl.BlockSpec(shape, lambda i: (0,) * len(shape))
    gate_w = lambda n: pl.BlockSpec((None, d, d), lambda i: (layer, 0, N_MIX // d + n))
    return pl.pallas_call(
        functools.partial(_merge_kernel, d=d),
        grid=(m // TM,),
        in_specs=[row(d),
                  pl.BlockSpec((1, 1, 6 * d), lambda i: (_seg_of_tile(i, tpb), 0, 0)),
                  full((1, d)), full((1, d)), gate_w(0), gate_w(1), gate_w(2), gate_w(3),
                  zcol(CB_HG_G), zcol(CB_RT_G), zcol(CB_S5)]
                 + [row(w)] * 6 + [srow, srow, row(w)]
                 + [full((SUBLANES, w)), full((w, w)), full((w, w)), full((N_BRANCH, w, d)), full((d, d)),
                    full((d, LANES)), full((1, LANES))],
        out_specs=[row(d), row(d), row(LANES), full((SUBLANES, LANES))],
        out_shape=[jax.ShapeDtypeStruct((m, d), F32), jax.ShapeDtypeStruct((m, d), F32),
                   jax.ShapeDtypeStruct((m, LANES), F32), jax.ShapeDtypeStruct((SUBLANES, LANES), F32)],
        scratch_shapes=[pltpu.VMEM((1, LANES), F32)],
        compiler_params=_cparams(("arbitrary",)),
        name="merge",
    )(x_all, mod3, n1, n2, w_in_bf, w_in_bf, w_in_bf, w_in_bf, z, z, z,
      hgf, hgb, rtf, rtb, s5f, s5b, yconv, vx, x0c,
      vecs, glu1, glu2, w_branch, w_out, w_router, b_router)


DMA_UNROLL = 8


def _row_copy(src_ref, src_row, dst_ref, dst_row, sem):
    return pltpu.make_async_copy(src_ref.at[pl.ds(src_row, 1), :], dst_ref.at[pl.ds(dst_row, 1), :], sem)


def _dispatch_kernel(dest_ref, h_ref, zero_ref, buf_ref, sem):
    del zero_ref
    base = pl.program_id(0) * TM

    def issue(r, carry):
        for k in range(2):
            _row_copy(h_ref, r, buf_ref, dest_ref[(base + r) * 2 + k], sem).start(priority=k)
        return carry

    lax.fori_loop(0, TM, issue, 0, unroll=DMA_UNROLL)
    for k in range(2):
        pltpu.make_async_copy(h_ref, buf_ref.at[pl.ds(0, TM), :], sem).wait()


def moe_dispatch(dest, h2, n_rows):
    m, d = h2.shape
    grid_spec = pltpu.PrefetchScalarGridSpec(
        num_scalar_prefetch=1,
        grid=(m // TM,),
        in_specs=[pl.BlockSpec((TM, d), lambda i, dest: (i, 0)),
                  pl.BlockSpec(memory_space=pl.ANY)],
        out_specs=pl.BlockSpec(memory_space=pl.ANY),
        scratch_shapes=[pltpu.SemaphoreType.DMA(())],
    )
    return pl.pallas_call(
        _dispatch_kernel,
        grid_spec=grid_spec,
        out_shape=jax.ShapeDtypeStruct((n_rows, d), F32),
        input_output_aliases={2: 0},
        compiler_params=_cparams(("arbitrary",)),
        name="moe_dispatch",
    )(dest, h2, jnp.zeros((n_rows, d), F32))


def _expert_kernel(plan_ref, x_ref, w1_ref, w3_ref, w2_ref, y_ref, *, n_blocks):
    used = pl.program_id(0) < plan_ref[n_blocks]

    @pl.when(used)
    def _():
        xb = x_ref[...].astype(BF16)
        a = jnp.dot(xb, w1_ref[...].astype(BF16), preferred_element_type=F32)
        g = jnp.dot(xb, w3_ref[...].astype(BF16), preferred_element_type=F32)
        y_ref[...] = jnp.dot((_silu(a) * g).astype(BF16), w2_ref[...].astype(BF16),
                             preferred_element_type=F32)

    @pl.when(jnp.logical_not(used))
    def _():
        y_ref[...] = jnp.zeros_like(y_ref)


def moe_experts(plan, buf, w1, w3, w2, layer):
    n_rows, d = buf.shape
    hid = w1.shape[-1]
    n_blocks = n_rows // MOE_BLOCK
    blk = lambda j, plan: jnp.minimum(j, plan[n_blocks] - 1)
    grid_spec = pltpu.PrefetchScalarGridSpec(
        num_scalar_prefetch=1,
        grid=(n_blocks,),
        in_specs=[pl.BlockSpec((MOE_BLOCK, d), lambda j, plan: (blk(j, plan), 0)),
                  pl.BlockSpec((None, None, d, hid), lambda j, plan: (layer, plan[blk(j, plan)], 0, 0)),
                  pl.BlockSpec((None, None, d, hid), lambda j, plan: (layer, plan[blk(j, plan)], 0, 0)),
                  pl.BlockSpec((None, None, hid, d), lambda j, plan: (layer, plan[blk(j, plan)], 0, 0))],
        out_specs=pl.BlockSpec((MOE_BLOCK, d), lambda j, plan: (j, 0)),
    )
    return pl.pallas_call(
        functools.partial(_expert_kernel, n_blocks=n_blocks),
        grid_spec=grid_spec,
        out_shape=jax.ShapeDtypeStruct((n_rows, d), F32),
        compiler_params=_cparams(("arbitrary",)),
        name="moe_experts",
    )(plan, buf, w1, w3, w2)


def _combine_kernel(dest_ref, x_ref, route_ref, mod_ref, gfin_ref, y_ref, o_ref, gath, sem, *, d, tpb, final):
    base = (pl.program_id(0) * tpb + pl.program_id(1)) * TM

    def issue(r, carry):
        for k in range(2):
            _row_copy(y_ref, dest_ref[(base + r) * 2 + k], gath.at[k], r, sem).start(priority=k)
        return carry

    lax.fori_loop(0, TM, issue, 0, unroll=DMA_UNROLL)
    for k in range(2):
        pltpu.make_async_copy(y_ref.at[pl.ds(0, TM), :], gath.at[k], sem).wait()
    route = route_ref[...]
    lane = lax.broadcasted_iota(jnp.int32, (TM, LANES), 1)
    moe = _lane_pick(route, lane, ROUTE_W1) * gath[0] + _lane_pick(route, lane, ROUTE_W2) * gath[1]
    x_new = x_ref[...] + mod_ref[0][:, 5 * d:6 * d] * moe
    if final:
        ms = jnp.mean(x_new * x_new, axis=-1, keepdims=True)
        x_new = x_new * lax.rsqrt(ms + EPS) * gfin_ref[...]
    o_ref[...] = x_new


def moe_combine(dest, x_mid, route, mod3, gain_final, ybuf, bsz, tpb, final):
    m, d = x_mid.shape
    nr = tpb - 1 if final else tpb
    tile = lambda b, r: b * tpb + r
    grid_spec = pltpu.PrefetchScalarGridSpec(
        num_scalar_prefetch=1,
        grid=(bsz, nr),
        in_specs=[pl.BlockSpec((TM, d), lambda b, r, dest: (tile(b, r), 0)),
                  pl.BlockSpec((TM, LANES), lambda b, r, dest: (tile(b, r), 0)),
                  pl.BlockSpec((1, 1, 6 * d), lambda b, r, dest: (_seg_of_tile(tile(b, r), tpb), 0, 0)),
                  pl.BlockSpec((1, d), lambda b, r, dest: (0, 0)),
                  pl.BlockSpec(memory_space=pl.ANY)],
        out_specs=pl.BlockSpec((TM, d), lambda b, r, dest: (b * nr + r, 0)),
        scratch_shapes=[pltpu.VMEM((2, TM, d), F32), pltpu.SemaphoreType.DMA(())],
    )
    return pl.pallas_call(
        functools.partial(_combine_kernel, d=d, tpb=tpb, final=final),
        grid_spec=grid_spec,
        out_shape=jax.ShapeDtypeStruct((bsz * nr * TM, d), F32),
        compiler_params=_cparams(("arbitrary", "arbitrary")),
        name="moe_combine",
    )(dest, x_mid, route, mod3, gain_final, ybuf)


def moe_plan(route, counts_row, n_blocks):
    counts = counts_row[:MOE_EXPERTS].astype(jnp.int32)
    padded = (counts + MOE_BLOCK - 1) // MOE_BLOCK * MOE_BLOCK
    pad_end = jnp.cumsum(padded)
    pad_start = pad_end - padded
    expert = route[:, ROUTE_E1:ROUTE_E2 + 1].astype(jnp.int32)
    rank = route[:, ROUTE_R1:ROUTE_R2 + 1].astype(jnp.int32)
    is_e = expert[..., None] == jnp.arange(MOE_EXPERTS, dtype=jnp.int32)
    dest = (jnp.sum(jnp.where(is_e, pad_start, 0), axis=-1) + rank).reshape(-1)
    block_row0 = jnp.arange(n_blocks, dtype=jnp.int32) * MOE_BLOCK
    block_expert = jnp.minimum(jnp.sum((pad_end[None, :] <= block_row0[:, None]).astype(jnp.int32), axis=1),
                               MOE_EXPERTS - 1)
    n_used = pad_end[MOE_EXPERTS - 1:] // MOE_BLOCK
    return dest, jnp.concatenate([block_expert, n_used])


def _rope_tables(length, lc):
    half = RET_DK // 2
    freqs = ROPE_BASE ** (-jnp.arange(0, half, 2, dtype=F32) / half)
    t = jnp.arange(length)
    row_pos = (t // GRID_W).astype(F32)[:, None]
    col_pos = (t % GRID_W).astype(F32)[:, None]
    lane = jnp.arange(RET_HEADS * RET_DK)
    fr = freqs[lane % (half // 2)][None, :]
    ang = jnp.where(((lane % RET_DK) < half)[None, :], row_pos, col_pos) * fr
    cos = jnp.concatenate([jnp.cos(ang), jnp.ones((lc, lane.shape[0]), F32)], axis=0)
    sin = jnp.concatenate([jnp.sin(ang), jnp.zeros((lc, lane.shape[0]), F32)], axis=0)
    return cos, sin


def _rotate_half_cols(wq):
    quarter = RET_DK // 4
    lane = np.arange(wq.shape[1])
    first = (lane % (2 * quarter)) < quarter
    src = np.where(first, lane + quarter, lane - quarter)
    sign = jnp.asarray(np.where(first, -1.0, 1.0), F32)
    return wq[:, src] * sign


def kernel(x, c, ctx, c_ctx, w_mod, b_mod, norm1, norm2, w_in, hg_lb_logits, hg_norm, hy_conv_w, hy_conv_b, hy_filt_w1, hy_filt_b1, hy_filt_freq1, hy_filt_w2, hy_filt_b2, hy_filt_freq2, hy_filt_w3, hy_skip, ret_decay_logit, ret_norm, s5_lam_re, s5_lam_im, s5_log_dt, s5_b_re, s5_b_im, s5_c_re, s5_c_im, s5_d, s5_glu_w1, s5_glu_w2, w_branch, w_out, moe_w_grp, moe_b_grp, moe_w_exp, moe_b_exp, moe_w1, moe_w3, moe_w2, norm_final):
    bsz, length, d = x.shape
    lc = ctx.shape[1]
    depth = w_mod.shape[0]
    assert lc == TM and length % TM == 0 and (2 * length) % (FFT_FAST * FFT_GROUP) == 0
    tt = length + lc
    tpb = tt // TM
    m = bsz * tt
    n_blocks = -(-(m * 2) // MOE_BLOCK) + MOE_EXPERTS
    w = BRANCH_W

    x_all = jnp.concatenate([x, ctx], axis=1).reshape(m, d)
    cmat = jnp.zeros((SUBLANES, d), F32).at[0].set(c_ctx).at[1:1 + bsz].set(c)
    mods = modvec(cmat, w_mod, b_mod)
    lb_sm = jax.nn.softmax(hg_lb_logits.astype(F32), axis=0)
    lb_all = jnp.cumsum(lb_sm, axis=0) - lb_sm[0]
    cos_tab, sin_tab = _rope_tables(length, lc)
    log_gamma = jax.nn.log_sigmoid(ret_decay_logit.astype(F32))
    gfin = norm_final.reshape(1, d)

    w_in_bf = w_in.astype(BF16)
    rq0 = 5 * w + 3 * HY_W
    nq = RET_HEADS * RET_DK

    for l in range(depth):
        w_rot = jnp.concatenate([_rotate_half_cols(w_in[l, :, rq0:rq0 + nq]),
                                 _rotate_half_cols(w_in[l, :, rq0 + nq:rq0 + 2 * nq])], axis=1).astype(BF16)
        mod3 = mods[l].reshape(SUBLANES, 1, 6 * d)

        z = inproj(x_all, mod3, norm1[l].reshape(1, d), w_in_bf, l, w_rot, tpb)
        hgf, hgb = hgrn(z, lb_all[l], bsz, tpb)
        rtf, rtb = retention(z, cos_tab, sin_tab, log_gamma[l], bsz, tpb)
        s5f, s5b = s5(z, *s5_params(s5_lam_re[l], s5_lam_im[l], s5_log_dt[l], s5_b_re[l], s5_b_im[l],
                                    s5_c_re[l], s5_c_im[l]), bsz, tpb)

        x0c, vx = hy_prep(z, hy_conv_w[l], hy_conv_b[l], bsz, tpb)
        fargs = (hy_filt_w1[l], hy_filt_b1[l], hy_filt_freq1[l], hy_filt_w2[l], hy_filt_b2[l],
                 hy_filt_freq2[l], hy_filt_w3[l])
        hfb, ss = hy_filter(length, *fargs)
        y_lat = hyena_long_conv(vx, bsz, length, hfb, lax.rsqrt(ss + EPS))
        hfbc, ssc = hy_filter(lc, *fargs)
        gwin = jnp.concatenate([hfbc[1, 1:][::-1], hfbc[0], jnp.zeros((1, HY_W), F32)], axis=0) \
            * lax.rsqrt(ssc + EPS)
        yconv = hy_ctx_conv(vx, gwin, y_lat, bsz, tpb, lc)

        vecs = jnp.zeros((SUBLANES, w), F32).at[0].set(hg_norm[l]).at[1].set(ret_norm[l]) \
            .at[2].set(s5_d[l]).at[3].set(hy_skip[l])
        w_router = jnp.zeros((d, LANES), F32).at[:, :MOE_GROUPS].set(moe_w_grp[l]) \
            .at[:, MOE_GROUPS:MOE_GROUPS + MOE_EXPERTS].set(moe_w_exp[l])
        b_router = jnp.zeros((1, LANES), F32).at[0, :MOE_GROUPS].set(moe_b_grp[l]) \
            .at[0, MOE_GROUPS:MOE_GROUPS + MOE_EXPERTS].set(moe_b_exp[l])
        x_mid, h2, route, counts = merge(
            x_all, mod3, norm1[l].reshape(1, d), norm2[l].reshape(1, d), w_in_bf, l, z,
            hgf, hgb, rtf, rtb, s5f, s5b, yconv, vx, x0c, vecs,
            s5_glu_w1[l].astype(BF16), s5_glu_w2[l].astype(BF16), w_branch[l].astype(BF16),
            w_out[l].astype(BF16), w_router, b_router, bsz, tpb)

        dest, block_expert = moe_plan(route, counts[0], n_blocks)
        buf = moe_dispatch(dest, h2, n_blocks * MOE_BLOCK)
        ybuf = moe_experts(block_expert, buf, moe_w1, moe_w3, moe_w2, l)
        x_all = moe_combine(dest, x_mid, route, mod3, gfin, ybuf, bsz, tpb, l == depth - 1)

    return x_all.reshape(bsz, length, d)
```

```python
import functools
import math

import numpy as np
import jax
import jax.numpy as jnp
from jax import lax
from jax.experimental import pallas as pl
from jax.experimental.pallas import tpu as pltpu

F32 = jnp.float32
BF16 = jnp.bfloat16
HIGHEST = lax.Precision.HIGHEST

GRID_W = 64
N_BRANCH = 4
BRANCH_W = 256
HG_HEADS = 4
HG_CHUNK = 16
HY_W = 256
HY_EMB = 33
HY_BANDS = 16
HY_ORDER = 64
HY_FAST_DECAY = 0.3
HY_SLOW_DECAY = 1.5
HY_TARGET = 1e-2
RET_HEADS = 4
RET_DK = 32
RET_DV = 64
ROPE_BASE = 10000.0
S5_GROUP_CH = 16
S5_GROUPS = 16
S5_STATE = 64
MOE_GROUPS = 4
MOE_PER_GROUP = 8
MOE_EXPERTS = 32
MOE_HIDDEN = 512
MOE_BLOCK = 256
EPS = 1e-6

LANES = 128
SUBLANES = 8
TM = 256
FFT_FAST = 128
VMEM_LIMIT = 56 * 1024 * 1024

N_MIX = 3072
N_ZCOLS = N_MIX + 2 * RET_HEADS * RET_DK
CB_HG_Q, CB_HG_FF, CB_HG_FB, CB_HG_I, CB_HG_G = 3, 4, 5, 6, 7
CB_RT_V, CB_RT_G, CB_S5 = 9, 10, 11
CB128_RT_Q, CB128_RT_K, CB128_RT_QR, CB128_RT_KR = 16, 17, 24, 25


def _cparams(sem):
    return pltpu.CompilerParams(dimension_semantics=sem, vmem_limit_bytes=VMEM_LIMIT)


def _sigmoid(x):
    return 1.0 / (1.0 + jnp.exp(-x))


def _silu(x):
    return x * _sigmoid(x)


def _dot(a, b):
    return jnp.dot(a.astype(BF16), b.astype(BF16), preferred_element_type=F32)


def _dot_nt(a, b):
    return lax.dot_general(a.astype(BF16), b.astype(BF16), (((1,), (1,)), ((), ())),
                           preferred_element_type=F32)


def _dot_hi(a, b):
    return jnp.dot(a, b, precision=HIGHEST, preferred_element_type=F32)


def _dot3(a, b):
    a_hi = a.astype(BF16)
    a_lo = (a - a_hi.astype(F32)).astype(BF16)
    b_hi = b.astype(BF16)
    b_lo = (b - b_hi.astype(F32)).astype(BF16)
    return (jnp.dot(a_hi, b_hi, preferred_element_type=F32)
            + jnp.dot(a_lo, b_hi, preferred_element_type=F32)
            + jnp.dot(a_hi, b_lo, preferred_element_type=F32))


def _split_dot(x, m_bf16):
    hi = x.astype(BF16)
    lo = (x - hi.astype(F32)).astype(BF16)
    return (jnp.dot(hi, m_bf16, preferred_element_type=F32)
            + jnp.dot(lo, m_bf16, preferred_element_type=F32))


def _split_dot_left(m_bf16, x):
    hi = x.astype(BF16)
    lo = (x - hi.astype(F32)).astype(BF16)
    return (jnp.dot(m_bf16, hi, preferred_element_type=F32)
            + jnp.dot(m_bf16, lo, preferred_element_type=F32))


def _seg_of_tile(i, tpb):
    return jnp.where(i % tpb == tpb - 1, 0, 1 + i // tpb)


def _split_tile(i, bsz, tpb):
    b, r = i // tpb, i % tpb
    return jnp.where(r == tpb - 1, bsz * (tpb - 1) + b, b * (tpb - 1) + r)


def _scan_tiles(b, s, tpb):
    last = tpb - 1
    fwd = jnp.where(s == 0, last, s - 1)
    bwd = jnp.where(s == 0, last, last - s)
    return b * tpb + fwd, b * tpb + bwd


def _modvec_kernel(c_ref, w_ref, b_ref, o_ref):
    c = c_ref[...]
    o_ref[0] = _dot_hi(_silu(c), w_ref[0]) + b_ref[0]


def modvec(cmat, w_mod, b_mod):
    depth, d, n6 = w_mod.shape
    nb = n6 // d
    return pl.pallas_call(
        _modvec_kernel,
        grid=(depth, nb),
        in_specs=[pl.BlockSpec((SUBLANES, d), lambda l, j: (0, 0)),
                  pl.BlockSpec((1, d, d), lambda l, j: (l, 0, j)),
                  pl.BlockSpec((1, 1, d), lambda l, j: (l, 0, j))],
        out_specs=pl.BlockSpec((1, SUBLANES, d), lambda l, j: (l, 0, j)),
        out_shape=jax.ShapeDtypeStruct((depth, SUBLANES, n6), F32),
        compiler_params=_cparams(("arbitrary", "arbitrary")),
        name="modvec",
    )(cmat, w_mod, b_mod.reshape(depth, 1, n6))


def _modulated_norm(x, gain, shift, scale):
    ms = jnp.mean(x * x, axis=-1, keepdims=True)
    return (x * lax.rsqrt(ms + EPS) * gain) * (1.0 + scale) + shift


def _inproj_kernel(x_ref, mod_ref, g_ref, w_ref, wrot_ref, z_ref, *, d):
    h = _modulated_norm(x_ref[...], g_ref[...], mod_ref[0, :, 0:d], mod_ref[0, :, d:2 * d]).astype(BF16)
    hy0 = 5 * BRANCH_W
    hy1 = hy0 + 3 * HY_W
    z_ref[:, 0:hy1 - hy0] = jnp.dot(h, w_ref[:, hy0:hy1], preferred_element_type=F32)
    z_ref[:, hy1 - hy0:hy1] = jnp.dot(h, w_ref[:, 0:hy0], preferred_element_type=F32)
    z_ref[:, hy1:N_MIX] = jnp.dot(h, w_ref[:, hy1:N_MIX], preferred_element_type=F32)
    z_ref[:, N_MIX:N_ZCOLS] = jnp.dot(h, wrot_ref[...], preferred_element_type=F32)


def inproj(x_all, mod3, gain, w_in_bf, layer, w_rot, tpb):
    m, d = x_all.shape
    return pl.pallas_call(
        functools.partial(_inproj_kernel, d=d),
        grid=(m // TM,),
        in_specs=[pl.BlockSpec((TM, d), lambda i: (i, 0)),
                  pl.BlockSpec((1, 1, 6 * d), lambda i: (_seg_of_tile(i, tpb), 0, 0)),
                  pl.BlockSpec((1, d), lambda i: (0, 0)),
                  pl.BlockSpec((None, d, N_MIX), lambda i: (layer, 0, 0)),
                  pl.BlockSpec((d, N_ZCOLS - N_MIX), lambda i: (0, 0))],
        out_specs=pl.BlockSpec((TM, N_ZCOLS), lambda i: (i, 0)),
        out_shape=jax.ShapeDtypeStruct((m, N_ZCOLS), F32),
        compiler_params=_cparams(("parallel",)),
        name="inproj",
    )(x_all, mod3, gain, w_in_bf, w_rot)


def _head_masked_rows(x, nchunk, c, hd):
    w = x.shape[-1]
    x3 = x.reshape(nchunk, c, w)
    lane_head = lax.broadcasted_iota(jnp.int32, (1, 1, w), 2) // hd
    parts = [jnp.where(lane_head == h, x3, 0.0) for h in range(w // hd)]
    return jnp.concatenate(parts, axis=1).reshape(nchunk * (w // hd) * c, w)


def _hgrn_direction(q_ref, f_ref, v_ref, lb, tri_ref, o_ref, st_ref, b_s, k_s, qe_s, ke_s, reverse):
    c = HG_CHUNK
    w = BRANCH_W
    nchunk = TM // c
    nh = HG_HEADS
    hd = w // nh
    half = c // 2
    z = f_ref[...]
    sg = _sigmoid(z)
    log2_f = jnp.log2(lb + (1.0 - lb) * sg)
    k = (1.0 - lb) * (1.0 - sg)
    b2 = _split_dot_left(tri_ref[0], log2_f)
    rem2 = _split_dot_left(tri_ref[1], log2_f)
    b_s[...] = b2
    k_s[...] = k
    qe_s[...] = _head_masked_rows(q_ref[...] * jnp.exp2(b2), nchunk, c, hd).astype(BF16)
    ke_s[...] = _head_masked_rows(k * jnp.exp2(rem2), nchunk, c, hd).astype(BF16)

    head_r = lax.broadcasted_iota(jnp.int32, (w, w), 0) // hd
    head_c = lax.broadcasted_iota(jnp.int32, (w, w), 1) // hd
    hm = jnp.where(head_r == head_c, 1.0, 0.0).astype(BF16)
    rows = lax.broadcasted_iota(jnp.int32, (half, 1), 0)

    pieces = []
    for s in range(c):
        for g in range(2):
            lo, hi = g * half, (g + 1) * half - 1
            if (hi < s) if not reverse else (lo > s):
                continue
            full = (lo >= s) if not reverse else (hi <= s)
            pieces.append((s, g, full))

    def chunk_step(n):
        cidx = (nchunk - 1 - n) if reverse else n
        off = pl.multiple_of(cidx * c, c)
        eoff = pl.multiple_of(cidx * (nh * c), nh * c)
        qc = q_ref[pl.ds(off, c), :]
        bc = b_s[pl.ds(off, c), :]
        kc = k_s[pl.ds(off, c), :]
        vc = v_ref[pl.ds(off, c), :]
        parts = []
        for s, g, full in pieces:
            sl = slice(g * half, (g + 1) * half)
            e = jnp.exp2(bc[sl] - bc[s:s + 1, :])
            if not full:
                t = rows + g * half
                e = jnp.where((t <= s) if reverse else (t >= s), e, 0.0)
            parts.append((qc[sl] * kc[s:s + 1, :]) * e)
        p = jnp.concatenate(parts, axis=0).astype(BF16)
        r = jnp.dot(p, hm, preferred_element_type=F32)
        og = [jnp.zeros((half, w), F32), jnp.zeros((half, w), F32)]
        for i, (s, g, _) in enumerate(pieces):
            og[g] = og[g] + r[i * half:(i + 1) * half, :] * vc[s:s + 1, :]
        st = st_ref[...]
        ri = _dot_nt(qe_s[pl.ds(eoff, nh * c), :], st)
        inter = jnp.concatenate([ri[h * c:(h + 1) * c, :] for h in range(nh)], axis=1)
        o_ref[pl.ds(off, c), :] = jnp.concatenate(og, axis=0) + inter
        vexp = jnp.concatenate([vc[:, h * hd:(h + 1) * hd] for h in range(nh)], axis=0)
        kv = lax.dot_general(vexp.astype(BF16), ke_s[pl.ds(eoff, nh * c), :], (((0,), (0,)), ((), ())),
                             preferred_element_type=F32)
        blast = bc[0:1, :] if reverse else bc[c - 1:c, :]
        st_ref[...] = jnp.exp2(blast) * st + kv

    return chunk_step


def _hgrn_kernel(qf_ref, ff_ref, vf_ref, qb_ref, fb_ref, vb_ref, lb_ref, trif_ref, trib_ref, of_ref, ob_ref,
                 stf_ref, stb_ref, bf_s, kf_s, qef_s, kef_s, bb_s, kb_s, qeb_s, keb_s):
    @pl.when(pl.program_id(1) == 0)
    def _():
        stf_ref[...] = jnp.zeros_like(stf_ref)
        stb_ref[...] = jnp.zeros_like(stb_ref)

    step_f = _hgrn_direction(qf_ref, ff_ref, vf_ref, lb_ref[0:1, :], trif_ref, of_ref, stf_ref,
                             bf_s, kf_s, qef_s, kef_s, False)
    step_b = _hgrn_direction(qb_ref, fb_ref, vb_ref, lb_ref[1:2, :], trib_ref, ob_ref, stb_ref,
                             bb_s, kb_s, qeb_s, keb_s, True)

    def body(n, carry):
        step_f(n)
        step_b(n)
        return carry

    lax.fori_loop(0, TM // HG_CHUNK, body, 0, unroll=2)


def hgrn(z, lb, bsz, tpb):
    m = z.shape[0]
    w = BRANCH_W
    hd = w // HG_HEADS
    ri = jnp.arange(TM)[:, None]
    ci = jnp.arange(TM)[None, :]
    same = (ri // HG_CHUNK) == (ci // HG_CHUNK)
    tri = jnp.stack([same & (ci <= ri), same & (ci > ri), same & (ci >= ri), same & (ci < ri)])
    tri = tri.astype(BF16).reshape(2, 2, TM, TM)

    def spec(cb, d):
        return pl.BlockSpec((TM, w), lambda b, s: (_scan_tiles(b, s, tpb)[d], cb))

    out_spec = [pl.BlockSpec((TM, w), lambda b, s: (_scan_tiles(b, s, tpb)[0], 0)),
                pl.BlockSpec((TM, w), lambda b, s: (_scan_tiles(b, s, tpb)[1], 0))]
    return pl.pallas_call(
        _hgrn_kernel,
        grid=(bsz, tpb),
        in_specs=[spec(CB_HG_Q, 0), spec(CB_HG_FF, 0), spec(CB_HG_I, 0),
                  spec(CB_HG_Q, 1), spec(CB_HG_FB, 1), spec(CB_HG_I, 1),
                  pl.BlockSpec((2, w), lambda b, s: (0, 0)),
                  pl.BlockSpec((None, 2, TM, TM), lambda b, s: (0, 0, 0, 0)),
                  pl.BlockSpec((None, 2, TM, TM), lambda b, s: (1, 0, 0, 0))],
        out_specs=out_spec,
        out_shape=[jax.ShapeDtypeStruct((m, w), F32)] * 2,
        scratch_shapes=[pltpu.VMEM((hd, w), F32), pltpu.VMEM((hd, w), F32)]
                       + [pltpu.VMEM((TM, w), F32), pltpu.VMEM((TM, w), F32),
                          pltpu.VMEM((HG_HEADS * TM, w), BF16), pltpu.VMEM((HG_HEADS * TM, w), BF16)] * 2,
        compiler_params=_cparams(("arbitrary", "arbitrary")),
        name="hgrn2",
    )(z, z, z, z, z, z, lb, tri, tri)


def _ret_direction(q_ref, k_ref, qr_ref, kr_ref, v_ref, cos_ref, sin_ref, lgq, lg_ref, d, o_ref, st_ref,
                   reverse):
    nq = RET_HEADS * RET_DK
    nv = RET_HEADS * RET_DV
    cos = cos_ref[...]
    sin = sin_ref[...]
    qx = q_ref[...] * cos + qr_ref[...] * sin
    kx = (k_ref[...] * cos + kr_ref[...] * sin) * (RET_DK ** -0.5)
    v = v_ref[...]
    t = lax.broadcasted_iota(jnp.int32, (TM, 1), 0).astype(F32)
    pos = (TM - 1.0 - t) if reverse else t
    qd = qx * jnp.exp((pos + 1.0) * lgq)
    kd = kx * jnp.exp((TM - 1.0 - pos) * lgq)
    ti = lax.broadcasted_iota(jnp.int32, (TM, TM), 0)
    si = lax.broadcasted_iota(jnp.int32, (TM, TM), 1)
    rel = ((si - ti) if reverse else (ti - si)).astype(F32)
    live = rel >= 0.0
    relc = jnp.where(live, rel, 0.0)
    lane_q = lax.broadcasted_iota(jnp.int32, (1, nq), 1) // RET_DK
    lane_v = lax.broadcasted_iota(jnp.int32, (1, nv), 1) // RET_DV
    st = st_ref[...]
    o = _dot_nt(qd, st)
    vb = v.astype(BF16)
    for h in range(RET_HEADS):
        dmat = jnp.where(live, jnp.exp(relc * lg_ref[d, h]), 0.0)
        sc = _dot_nt(jnp.where(lane_q == h, qx, 0.0), kx) * dmat
        oh = jnp.dot(sc.astype(BF16), vb, preferred_element_type=F32)
        o = o + jnp.where(lane_v == h, oh, 0.0)
    o_ref[...] = o
    kv = _dot(v.T, kd)
    hr = lax.broadcasted_iota(jnp.int32, (nv, nq), 0) // RET_DV
    hc = lax.broadcasted_iota(jnp.int32, (nv, nq), 1) // RET_DK
    st_ref[...] = st * jnp.exp(float(TM) * lgq) + jnp.where(hr == hc, kv, 0.0)


def _ret_kernel(lg_ref, qf, kf, qrf, krf, vf, cosf, sinf, qb, kb, qrb, krb, vb, cosb, sinb, lgq_ref,
                of_ref, ob_ref, stf_ref, stb_ref):
    @pl.when(pl.program_id(1) == 0)
    def _():
        stf_ref[...] = jnp.zeros_like(stf_ref)
        stb_ref[...] = jnp.zeros_like(stb_ref)

    _ret_direction(qf, kf, qrf, krf, vf, cosf, sinf, lgq_ref[0:1, :], lg_ref, 0, of_ref, stf_ref, False)
    _ret_direction(qb, kb, qrb, krb, vb, cosb, sinb, lgq_ref[1:2, :], lg_ref, 1, ob_ref, stb_ref, True)


def retention(z, cos_tab, sin_tab, log_gamma, bsz, tpb):
    m = z.shape[0]
    nq = RET_HEADS * RET_DK
    nv = RET_HEADS * RET_DV
    lgq = jnp.repeat(log_gamma, RET_DK, axis=1)

    def zspec(cb, width, d):
        return pl.BlockSpec((TM, width), lambda b, s: (_scan_tiles(b, s, tpb)[d], cb))

    def tspec(d):
        return pl.BlockSpec((TM, nq), lambda b, s: (_scan_tiles(0, s, tpb)[d], 0))

    def side(d):
        return [zspec(CB128_RT_Q, nq, d), zspec(CB128_RT_K, nq, d), zspec(CB128_RT_QR, nq, d),
                zspec(CB128_RT_KR, nq, d), zspec(CB_RT_V, nv, d), tspec(d), tspec(d)]

    return pl.pallas_call(
        _ret_kernel,
        grid=(bsz, tpb),
        in_specs=[pl.BlockSpec(memory_space=pltpu.SMEM)] + side(0) + side(1)
                 + [pl.BlockSpec((2, nq), lambda b, s: (0, 0))],
        out_specs=[pl.BlockSpec((TM, nv), lambda b, s: (_scan_tiles(b, s, tpb)[0], 0)),
                   pl.BlockSpec((TM, nv), lambda b, s: (_scan_tiles(b, s, tpb)[1], 0))],
        out_shape=[jax.ShapeDtypeStruct((m, nv), F32)] * 2,
        scratch_shapes=[pltpu.VMEM((nv, nq), F32), pltpu.VMEM((nv, nq), F32)],
        compiler_params=_cparams(("arbitrary", "arbitrary")),
        name="retention",
    )(log_gamma, z, z, z, z, z, cos_tab, sin_tab, z, z, z, z, z, cos_tab, sin_tab, lgq)


S5_NP = S5_GROUPS * S5_STATE


def _s5_direction(u_ref, bre_ref, bim_ref, cre_ref, cim_ref, tab_ref, o_ref, car_ref, xr_s, xi_s, reverse):
    nb = TM // SUBLANES
    ub = u_ref[...].astype(BF16)
    xr = jnp.dot(ub, bre_ref[0], preferred_element_type=F32).reshape(nb, SUBLANES, S5_NP)
    xi = jnp.dot(ub, bim_ref[0], preferred_element_type=F32).reshape(nb, SUBLANES, S5_NP)
    sub = lax.broadcasted_iota(jnp.int32, (1, SUBLANES, 1), 1)
    for k, row in ((1, 8), (2, 9), (4, 10)):
        ar = tab_ref[0, 0, row:row + 1, :]
        ai = tab_ref[0, 1, row:row + 1, :]
        shift = (SUBLANES - k) if reverse else k
        keep = (sub < SUBLANES - k) if reverse else (sub >= k)
        sr = jnp.where(keep, pltpu.roll(xr, shift, 1), 0.0)
        si = jnp.where(keep, pltpu.roll(xi, shift, 1), 0.0)
        xr, xi = xr + ar * sr - ai * si, xi + ar * si + ai * sr
    xr_s[...] = xr.reshape(TM, S5_NP)
    xi_s[...] = xi.reshape(TM, S5_NP)
    pr = tab_ref[0, 0, 0:SUBLANES, :]
    pi = tab_ref[0, 1, 0:SUBLANES, :]

    def body(n, carry):
        cr, ci = carry
        blk = (nb - 1 - n) if reverse else n
        off = pl.multiple_of(blk * SUBLANES, SUBLANES)
        br = xr_s[pl.ds(off, SUBLANES), :] + pr * cr - pi * ci
        bi = xi_s[pl.ds(off, SUBLANES), :] + pr * ci + pi * cr
        xr_s[pl.ds(off, SUBLANES), :] = br
        xi_s[pl.ds(off, SUBLANES), :] = bi
        if reverse:
            return br[0:1, :], bi[0:1, :]
        return br[SUBLANES - 1:SUBLANES, :], bi[SUBLANES - 1:SUBLANES, :]

    cr, ci = lax.fori_loop(0, nb, body, (car_ref[0:1, :], car_ref[1:2, :]))
    car_ref[0:1, :] = cr
    car_ref[1:2, :] = ci
    o_ref[...] = (jnp.dot(xr_s[...].astype(BF16), cre_ref[0], preferred_element_type=F32)
                  - jnp.dot(xi_s[...].astype(BF16), cim_ref[0], preferred_element_type=F32))


def _s5_kernel(uf, ub, bre, bim, cre, cim, tabf, tabb, bre2, bim2, cre2, cim2, of_ref, ob_ref,
               carf, carb, xr_s, xi_s):
    @pl.when(pl.program_id(1) == 0)
    def _():
        carf[...] = jnp.zeros_like(carf)
        carb[...] = jnp.zeros_like(carb)

    _s5_direction(uf, bre, bim, cre, cim, tabf, of_ref, carf, xr_s, xi_s, False)
    _s5_direction(ub, bre2, bim2, cre2, cim2, tabb, ob_ref, carb, xr_s, xi_s, True)


def s5(z, bblk_re, bblk_im, cblk_re, cblk_im, tab, bsz, tpb):
    m = z.shape[0]
    w = BRANCH_W

    def pspec(shape, d):
        return pl.BlockSpec((1,) + shape, lambda b, s: (d,) + (0,) * len(shape))

    def uspec(d):
        return pl.BlockSpec((TM, w), lambda b, s: (_scan_tiles(b, s, tpb)[d], CB_S5))

    return pl.pallas_call(
        _s5_kernel,
        grid=(bsz, tpb),
        in_specs=[uspec(0), uspec(1),
                  pspec((w, S5_NP), 0), pspec((w, S5_NP), 0), pspec((S5_NP, w), 0), pspec((S5_NP, w), 0),
                  pspec((2, 16, S5_NP), 0), pspec((2, 16, S5_NP), 1),
                  pspec((w, S5_NP), 1), pspec((w, S5_NP), 1), pspec((S5_NP, w), 1), pspec((S5_NP, w), 1)],
        out_specs=[pl.BlockSpec((TM, w), lambda b, s: (_scan_tiles(b, s, tpb)[0], 0)),
                   pl.BlockSpec((TM, w), lambda b, s: (_scan_tiles(b, s, tpb)[1], 0))],
        out_shape=[jax.ShapeDtypeStruct((m, w), F32)] * 2,
        scratch_shapes=[pltpu.VMEM((2, S5_NP), F32), pltpu.VMEM((2, S5_NP), F32),
                        pltpu.VMEM((TM, S5_NP), F32), pltpu.VMEM((TM, S5_NP), F32)],
        compiler_params=_cparams(("arbitrary", "arbitrary")),
        name="s5",
    )(z, z, bblk_re, bblk_im, cblk_re, cblk_im, tab, tab, bblk_re, bblk_im, cblk_re, cblk_im)


def s5_params(lam_re, lam_im, log_dt, b_re, b_im, c_re, c_im):
    lr = jnp.minimum(lam_re.astype(F32), -1e-4)
    li = lam_im.astype(F32)
    dt = jnp.exp(log_dt.astype(F32))[..., None]

    def apow(k):
        mag = jnp.exp(k * lr * dt)
        return mag * jnp.cos(k * li * dt), mag * jnp.sin(k * li * dt)

    ar, ai = apow(1.0)
    den = lr * lr + li * li
    gr = ((ar - 1.0) * lr + ai * li) / den
    gi = (ai * lr - (ar - 1.0) * li) / den
    bbr = gr[..., None] * b_re - gi[..., None] * b_im
    bbi = gr[..., None] * b_im + gi[..., None] * b_re
    eye = jnp.eye(S5_GROUPS, dtype=F32)

    def blk_in(bb):
        t = jnp.einsum('dgpc,gh->dgchp', bb, eye)
        return t.reshape(2, S5_GROUPS * S5_GROUP_CH, S5_NP).astype(BF16)

    def blk_out(cc):
        t = jnp.einsum('dgcp,gh->dgphc', cc.astype(F32), eye)
        return t.reshape(2, S5_NP, S5_GROUPS * S5_GROUP_CH).astype(BF16)

    rows_r, rows_i = [], []
    for j in range(SUBLANES):
        pr_f, pi_f = apow(float(j + 1))
        pr_b, pi_b = apow(float(SUBLANES - j))
        rows_r.append(jnp.stack([pr_f[0], pr_b[1]]))
        rows_i.append(jnp.stack([pi_f[0], pi_b[1]]))
    for k in (1.0, 2.0, 4.0):
        pr_k, pi_k = apow(k)
        rows_r.append(pr_k)
        rows_i.append(pi_k)
    zero = jnp.zeros_like(rows_r[0])
    while len(rows_r) < 16:
        rows_r.append(zero)
        rows_i.append(zero)
    tab_r = jnp.stack(rows_r, axis=1).reshape(2, 16, S5_NP)
    tab_i = jnp.stack(rows_i, axis=1).reshape(2, 16, S5_NP)
    tab = jnp.stack([tab_r, tab_i], axis=1)
    return blk_in(bbr), blk_in(bbi), blk_out(c_re), blk_out(c_im), tab


def _hy_prep_kernel(u_ref, prev_ref, next_ref, w_ref, b_ref, x0_ref, vx_ref, *, tpb):
    i = pl.program_id(0)
    r = i % tpb
    first = (r == 0) | (r == tpb - 1)
    last = r >= tpb - 2
    u = u_ref[...]
    w = w_ref[...]
    up_row = jnp.where(first, 0.0, prev_ref[SUBLANES - 1:SUBLANES, :])
    dn_row = jnp.where(last, 0.0, next_ref[0:1, :])
    row = lax.broadcasted_iota(jnp.int32, (TM, 1), 0)
    um = jnp.where(row == 0, up_row, pltpu.roll(u, 1, 0))
    un = jnp.where(row == TM - 1, dn_row, pltpu.roll(u, TM - 1, 0))
    y = um * w[0:1, :] + u * w[1:2, :] + un * w[2:3, :] + b_ref[...]
    x0_ref[...] = y[:, 0:HY_W]
    vx_ref[...] = y[:, 2 * HY_W:3 * HY_W] * y[:, HY_W:2 * HY_W]


def hy_prep(z, conv_w, conv_b, bsz, tpb):
    m = z.shape[0]
    w3 = 3 * HY_W
    rb = TM // SUBLANES
    nrb = m // SUBLANES
    return pl.pallas_call(
        functools.partial(_hy_prep_kernel, tpb=tpb),
        grid=(m // TM,),
        in_specs=[pl.BlockSpec((TM, w3), lambda i: (i, 0)),
                  pl.BlockSpec((SUBLANES, w3), lambda i: (jnp.maximum(i * rb - 1, 0), 0)),
                  pl.BlockSpec((SUBLANES, w3), lambda i: (jnp.minimum((i + 1) * rb, nrb - 1), 0)),
                  pl.BlockSpec((3, w3), lambda i: (0, 0)),
                  pl.BlockSpec((1, w3), lambda i: (0, 0))],
        out_specs=[pl.BlockSpec((TM, HY_W), lambda i: (i, 0)),
                   pl.BlockSpec((TM, HY_W), lambda i: (_split_tile(i, bsz, tpb), 0))],
        out_shape=[jax.ShapeDtypeStruct((m, HY_W), F32)] * 2,
        compiler_params=_cparams(("parallel",)),
        name="hy_prep",
    )(z, z, z, conv_w, conv_b.reshape(1, w3))


def _hy_filter_kernel(zf_ref, w1_ref, b1_ref, f1_ref, w2_ref, b2_ref, f2_ref, w3_ref, dl_ref,
                      hfb_ref, ss_ref):
    i = pl.program_id(0)
    zf = zf_ref[...]
    hid = jnp.sin(f1_ref[...] * (_dot_hi(zf, w1_ref[...]) + b1_ref[...]))
    hid = jnp.sin(f2_ref[...] * (_dot_hi(hid, w2_ref[...]) + b2_ref[...]))
    filt = _dot_hi(hid, w3_ref[...])
    window = jnp.exp(-zf[:, 0:1] * dl_ref[...])
    hf = filt[:, 0:HY_W] * window
    row = lax.broadcasted_iota(jnp.int32, (TM, 1), 0) + i * TM
    hb = jnp.where(row == 0, 0.0, filt[:, HY_W:2 * HY_W] * window)
    hfb_ref[0] = hf
    hfb_ref[1] = hb
    part = jnp.sum(hf * hf + hb * hb, axis=0, keepdims=True)

    @pl.when(i == 0)
    def _():
        ss_ref[...] = jnp.zeros_like(ss_ref)

    ss_ref[...] = ss_ref[...] + part


def hy_filter(length, w1, b1, fr1, w2, b2, fr2, w3):
    t = jnp.linspace(0.0, 1.0, length, dtype=F32)[:, None]
    wv = (2.0 * math.pi / length) * jnp.arange(length, dtype=F32)[:, None]
    f = jnp.linspace(1e-4, HY_BANDS - 1.0, HY_BANDS, dtype=F32)[None, :]
    zfeat = jnp.concatenate([t, jnp.cos(f * wv), -jnp.sin(f * wv)], axis=-1)
    zfeat = jnp.pad(zfeat, ((0, 0), (0, LANES - HY_EMB)))
    w1p = jnp.pad(w1, ((0, LANES - HY_EMB), (0, 0)))
    deltas = jnp.abs(jnp.linspace(math.log(HY_TARGET) / HY_SLOW_DECAY,
                                  math.log(HY_TARGET) / HY_FAST_DECAY, HY_W, dtype=F32))[None, :]
    full = lambda shape: pl.BlockSpec(shape, lambda i: (0,) * len(shape))
    return pl.pallas_call(
        _hy_filter_kernel,
        grid=(length // TM,),
        in_specs=[pl.BlockSpec((TM, LANES), lambda i: (i, 0)),
                  full((LANES, HY_ORDER)), full((1, HY_ORDER)), full((1, HY_ORDER)),
                  full((HY_ORDER, HY_ORDER)), full((1, HY_ORDER)), full((1, HY_ORDER)),
                  full((HY_ORDER, 2 * HY_W)), full((1, HY_W))],
        out_specs=[pl.BlockSpec((2, TM, HY_W), lambda i: (0, i, 0)),
                   pl.BlockSpec((1, HY_W), lambda i: (0, 0))],
        out_shape=[jax.ShapeDtypeStruct((2, length, HY_W), F32), jax.ShapeDtypeStruct((1, HY_W), F32)],
        compiler_params=_cparams(("arbitrary",)),
        name="hy_filter",
    )(zfeat, w1p, b1.reshape(1, -1), fr1.reshape(1, -1), w2, b2.reshape(1, -1), fr2.reshape(1, -1), w3, deltas)


def _dft_consts(length):
    n = 2 * length
    ns = n // FFT_FAST
    a = jnp.arange(ns, dtype=jnp.int32)
    ang = ((a[:, None] * a[None, :]) % ns).astype(F32) * (2.0 * math.pi / ns)
    f_fwd = jnp.concatenate([jnp.cos(ang), -jnp.sin(ang)], axis=0)
    f_inv = jnp.concatenate([jnp.cos(ang), -jnp.sin(ang)], axis=1)
    b = jnp.arange(FFT_FAST, dtype=jnp.int32)
    kk = a[:, None, None] + ns * b[None, :, None]
    idx = (kk * b[None, None, :]) % n
    th = idx.astype(F32) * (2.0 * math.pi / n)
    mr, mi = jnp.cos(th), -jnp.sin(th)
    m_fwd = jnp.concatenate([jnp.concatenate([mr, -mi], axis=2),
                             jnp.concatenate([mi, mr], axis=2)], axis=1)
    ir = jnp.transpose(mr, (0, 2, 1)) / n
    ii = jnp.transpose(-mi, (0, 2, 1)) / n
    m_inv = jnp.concatenate([jnp.concatenate([ir, -ii], axis=2),
                             jnp.concatenate([ii, ir], axis=2)], axis=1)
    return tuple(t.astype(BF16) for t in (f_fwd, f_inv, m_fwd, m_inv))


def _dft_slow_fwd_kernel(f_ref, x_ref, o_ref):
    ns = o_ref.shape[2]
    for bl in range(x_ref.shape[1]):
        r = jnp.dot(f_ref[...], x_ref[:, bl, :].astype(BF16), preferred_element_type=F32)
        o_ref[0, 0, :, bl, :] = r[0:ns]
        o_ref[0, 1, :, bl, :] = r[ns:]


def dft_slow_fwd(fmat, x3, bx):
    two_ns, na = fmat.shape
    ns = two_ns // 2
    cw = x3.shape[-1]
    kb = SUBLANES
    return pl.pallas_call(
        _dft_slow_fwd_kernel,
        grid=(bx, FFT_FAST // kb),
        in_specs=[pl.BlockSpec((two_ns, na), lambda b, j: (0, 0)),
                  pl.BlockSpec((na, kb, cw), lambda b, j: (b, j, 0))],
        out_specs=pl.BlockSpec((1, 2, ns, kb, cw), lambda b, j: (b, 0, 0, j, 0)),
        out_shape=jax.ShapeDtypeStruct((bx, 2, ns, FFT_FAST, cw), F32),
        compiler_params=_cparams(("parallel", "parallel")),
        name="dft_slow_fwd",
    )(fmat, x3)


def _dft_slow_inv_kernel(f_ref, b_ref, o_ref, *, bx):
    @pl.when(pl.program_id(0) < bx)
    def _():
        for bl in range(o_ref.shape[1]):
            bb = jnp.concatenate([b_ref[0, 0, :, bl, :], b_ref[0, 1, :, bl, :]], axis=0).astype(BF16)
            o_ref[:, bl, :] = jnp.dot(f_ref[...], bb, preferred_element_type=F32)

    @pl.when(pl.program_id(0) >= bx)
    def _():
        o_ref[...] = jnp.zeros_like(o_ref)


def dft_slow_inv(fmat, bv):
    bx, _, ns, nf, cw = bv.shape
    na = fmat.shape[0]
    kb = SUBLANES
    return pl.pallas_call(
        functools.partial(_dft_slow_inv_kernel, bx=bx),
        grid=(bx + 1, nf // kb),
        in_specs=[pl.BlockSpec((na, 2 * ns), lambda b, j: (0, 0)),
                  pl.BlockSpec((1, 2, ns, kb, cw), lambda b, j: (jnp.minimum(b, bx - 1), 0, 0, j, 0))],
        out_specs=pl.BlockSpec((na, kb, cw), lambda b, j: (b, j, 0)),
        out_shape=jax.ShapeDtypeStruct(((bx + 1) * na, nf, cw), F32),
        compiler_params=_cparams(("parallel", "parallel")),
        name="dft_slow_inv",
    )(fmat, bv)


FFT_GROUP = 8


def _dft_fast_filter_kernel(m_ref, a_ref, sc_ref, o_ref):
    sc = sc_ref[...]
    for j in range(FFT_GROUP):
        af = jnp.concatenate([a_ref[0, 0, j], a_ref[0, 1, j]], axis=0).astype(BF16)
        ab = jnp.concatenate([a_ref[1, 0, j], a_ref[1, 1, j]], axis=0).astype(BF16)
        xf = jnp.dot(m_ref[j], af, preferred_element_type=F32)
        xb = jnp.dot(m_ref[j], ab, preferred_element_type=F32)
        o_ref[0, j] = (xf[0:FFT_FAST] + xb[0:FFT_FAST]) * sc
        o_ref[1, j] = (xf[FFT_FAST:] - xb[FFT_FAST:]) * sc


def dft_fast_filter(m_fwd, a5, scale):
    _, _, ns, nf, cw = a5.shape
    g = FFT_GROUP
    return pl.pallas_call(
        _dft_fast_filter_kernel,
        grid=(ns // g,),
        in_specs=[pl.BlockSpec((g, 2 * nf, 2 * nf), lambda i: (i, 0, 0)),
                  pl.BlockSpec((2, 2, g, nf, cw), lambda i: (0, 0, i, 0, 0)),
                  pl.BlockSpec((1, cw), lambda i: (0, 0))],
        out_specs=pl.BlockSpec((2, g, nf, cw), lambda i: (0, i, 0, 0)),
        out_shape=jax.ShapeDtypeStruct((2, ns, nf, cw), F32),
        compiler_params=_cparams(("parallel",)),
        name="dft_fast_filter",
    )(m_fwd, a5, scale)


def _dft_fast_conv_kernel(mf_ref, mi_ref, a_ref, k_ref, o_ref):
    for j in range(FFT_GROUP):
        a = jnp.concatenate([a_ref[0, 0, j], a_ref[0, 1, j]], axis=0).astype(BF16)
        x = jnp.dot(mf_ref[j], a, preferred_element_type=F32)
        xr, xi = x[0:FFT_FAST], x[FFT_FAST:]
        kr, ki = k_ref[0, j], k_ref[1, j]
        y = jnp.concatenate([xr * kr - xi * ki, xr * ki + xi * kr], axis=0).astype(BF16)
        bv = jnp.dot(mi_ref[j], y, preferred_element_type=F32)
        o_ref[0, 0, j] = bv[0:FFT_FAST]
        o_ref[0, 1, j] = bv[FFT_FAST:]


def dft_fast_conv(m_fwd, m_inv, a5, kf):
    bx, _, ns, nf, cw = a5.shape
    g = FFT_GROUP
    return pl.pallas_call(
        _dft_fast_conv_kernel,
        grid=(bx, ns // g),
        in_specs=[pl.BlockSpec((g, 2 * nf, 2 * nf), lambda b, i: (i, 0, 0)),
                  pl.BlockSpec((g, 2 * nf, 2 * nf), lambda b, i: (i, 0, 0)),
                  pl.BlockSpec((1, 2, g, nf, cw), lambda b, i: (b, 0, i, 0, 0)),
                  pl.BlockSpec((2, g, nf, cw), lambda b, i: (0, i, 0, 0))],
        out_specs=pl.BlockSpec((1, 2, g, nf, cw), lambda b, i: (b, 0, i, 0, 0)),
        out_shape=jax.ShapeDtypeStruct((bx, 2, ns, nf, cw), F32),
        compiler_params=_cparams(("parallel", "parallel")),
        name="dft_fast_conv",
    )(m_fwd, m_inv, a5, kf)


def hyena_long_conv(vx_rows, bsz, length, hfb, scale):
    cw = vx_rows.shape[-1]
    ns = 2 * length // FFT_FAST
    f_fwd, f_inv, m_fwd, m_inv = _dft_consts(length)
    fa = dft_slow_fwd(f_fwd[:, :ns // 2], hfb.reshape(-1, FFT_FAST, cw), 2)
    kf = dft_fast_filter(m_fwd, fa, scale)
    xa = dft_slow_fwd(f_fwd[:, :ns // 2], vx_rows.reshape(-1, FFT_FAST, cw), bsz)
    bv = dft_fast_conv(m_fwd, m_inv, xa, kf)
    return dft_slow_inv(f_inv[:ns // 2], bv).reshape((bsz + 1) * length, cw)


def _hy_ctx_conv_kernel(v_ref, g_ref, y_hbm_ref, o_ref, *, lc):
    del y_hbm_ref
    nb = lc // SUBLANES

    def body(gi, acc):
        base = pl.multiple_of((nb - 1 - gi) * SUBLANES, SUBLANES)
        win = g_ref[pl.ds(base, lc + SUBLANES), :]
        for j in range(SUBLANES):
            vrow = v_ref[pl.ds(gi * SUBLANES + j, 1), :]
            acc = acc + win[SUBLANES - 1 - j:SUBLANES - 1 - j + lc, :] * vrow
        return acc

    o_ref[...] = lax.fori_loop(0, nb, body, jnp.zeros((lc, HY_W), F32))


def hy_ctx_conv(vx, gwin, yconv, bsz, tpb, lc):
    ctx_tile = lambda b: (bsz * (tpb - 1) + b, 0)
    return pl.pallas_call(
        functools.partial(_hy_ctx_conv_kernel, lc=lc),
        grid=(bsz,),
        in_specs=[pl.BlockSpec((lc, HY_W), ctx_tile),
                  pl.BlockSpec((2 * lc, HY_W), lambda b: (0, 0)),
                  pl.BlockSpec(memory_space=pl.ANY)],
        out_specs=pl.BlockSpec((lc, HY_W), ctx_tile),
        out_shape=jax.ShapeDtypeStruct(yconv.shape, F32),
        input_output_aliases={2: 0},
        compiler_params=_cparams(("parallel",)),
        name="hy_ctx_conv",
    )(vx, gwin, yconv)


ROUTE_E1, ROUTE_E2, ROUTE_W1, ROUTE_W2, ROUTE_R1, ROUTE_R2 = 0, 1, 2, 3, 4, 5
NEG_BIG = -1e30


def _gelu_tanh(x):
    return 0.5 * x * (1.0 + jnp.tanh(math.sqrt(2.0 / math.pi) * (x + 0.044715 * (x * x * x))))


def _lane_pick(slab, lane, idx):
    return jnp.sum(jnp.where(lane == idx, slab, 0.0), axis=-1, keepdims=True)


def _merge_kernel(x_ref, mod_ref, n1_ref, n2_ref, wg0_ref, wg1_ref, wg2_ref, wg3_ref, zhg_ref, zrt_ref, zs5_ref,
                  hgf_ref, hgb_ref, rtf_ref, rtb_ref, s5f_ref, s5b_ref, yc_ref, vx_ref, x0_ref,
                  vec_ref, glu1_ref, glu2_ref, wb_ref, wo_ref, wr_ref, br_ref,
                  xo_ref, h2_ref, route_ref, cnt_ref, base_ref, *, d):
    i = pl.program_id(0)
    w = BRANCH_W
    x = x_ref[...]
    mod = mod_ref[0]
    h = _modulated_norm(x, n1_ref[...], mod[:, 0:d], mod[:, d:2 * d]).astype(BF16)

    hd = w // HG_HEADS
    hr = lax.broadcasted_iota(jnp.int32, (w, w), 0) // hd
    hc = lax.broadcasted_iota(jnp.int32, (w, w), 1) // hd
    hm = jnp.where(hr == hc, 1.0, 0.0).astype(BF16)
    inv = 1.0 / hd

    hg = hgf_ref[...] + hgb_ref[...]
    ms = _split_dot(hg * hg, hm) * inv
    br_hg = hg * lax.rsqrt(ms + EPS) * vec_ref[0:1, :] * _silu(zhg_ref[...])

    rt = rtf_ref[...] + rtb_ref[...]
    cen = rt - _split_dot(rt, hm) * inv
    var = _split_dot(cen * cen, hm) * inv
    br_rt = cen * lax.rsqrt(var + EPS) * vec_ref[1:2, :] * _silu(zrt_ref[...])

    vx = vx_ref[...]
    br_hy = (yc_ref[...] + vx * vec_ref[3:4, :]) * x0_ref[...]

    y5 = _gelu_tanh(s5f_ref[...] + s5b_ref[...] + zs5_ref[...] * vec_ref[2:3, :]).astype(BF16)
    br_s5 = (jnp.dot(y5, glu1_ref[...], preferred_element_type=F32)
             * _sigmoid(jnp.dot(y5, glu2_ref[...], preferred_element_type=F32)))

    merged = jnp.zeros((TM, d), F32)
    for n, (br, wg_ref) in enumerate(zip((br_hg, br_hy, br_rt, br_s5), (wg0_ref, wg1_ref, wg2_ref, wg3_ref))):
        gate = _sigmoid(jnp.dot(h, wg_ref[...], preferred_element_type=F32))
        merged = merged + gate * jnp.dot(br.astype(BF16), wb_ref[n], preferred_element_type=F32)
    mix = jnp.dot(merged.astype(BF16), wo_ref[...], preferred_element_type=F32)
    x_mid = x + mod[:, 2 * d:3 * d] * mix
    xo_ref[...] = x_mid
    h2 = _modulated_norm(x_mid, n2_ref[...], mod[:, 3 * d:4 * d], mod[:, 4 * d:5 * d])
    h2_ref[...] = h2

    logits = _dot3(h2, wr_ref[...]) + br_ref[...]
    lane = lax.broadcasted_iota(jnp.int32, (TM, LANES), 1).astype(F32)
    gl = jnp.where(lane < MOE_GROUPS, logits, NEG_BIG)
    gmax = jnp.max(gl, axis=-1, keepdims=True)
    grp_p = 1.0 / jnp.sum(jnp.exp(gl - gmax), axis=-1, keepdims=True)
    gi = jnp.min(jnp.where(gl == gmax, lane, float(LANES)), axis=-1, keepdims=True)
    lo = MOE_GROUPS + MOE_PER_GROUP * gi
    el = jnp.where((lane >= lo) & (lane < lo + MOE_PER_GROUP), logits, NEG_BIG)
    m1 = jnp.max(el, axis=-1, keepdims=True)
    i1 = jnp.min(jnp.where(el == m1, lane, float(LANES)), axis=-1, keepdims=True)
    el2 = jnp.where(lane == i1, NEG_BIG, el)
    m2 = jnp.max(el2, axis=-1, keepdims=True)
    i2 = jnp.min(jnp.where(el2 == m2, lane, float(LANES)), axis=-1, keepdims=True)
    ratio = jnp.exp(m2 - m1)
    wt1 = grp_p / (1.0 + ratio)
    wt2 = grp_p * ratio / (1.0 + ratio)
    e1 = i1 - MOE_GROUPS
    e2 = i2 - MOE_GROUPS

    @pl.when(i == 0)
    def _():
        base_ref[...] = jnp.zeros_like(base_ref)

    oh1 = jnp.where(lane == e1, 1.0, 0.0)
    oh2 = jnp.where(lane == e2, 1.0, 0.0)
    ri = lax.broadcasted_iota(jnp.int32, (TM, TM), 0)
    ci = lax.broadcasted_iota(jnp.int32, (TM, TM), 1)
    below = jnp.where(ci < ri, 1.0, 0.0).astype(BF16)
    c1 = jnp.dot(below, oh1.astype(BF16), preferred_element_type=F32)
    c2 = jnp.dot(below, oh2.astype(BF16), preferred_element_type=F32)
    tot1 = jnp.sum(oh1, axis=0, keepdims=True)
    tot2 = jnp.sum(oh2, axis=0, keepdims=True)
    base = base_ref[...]
    r1 = jnp.sum(oh1 * (base + c1), axis=-1, keepdims=True)
    r2 = jnp.sum(oh2 * (base + tot1 + c2), axis=-1, keepdims=True)
    base = base + tot1 + tot2
    base_ref[...] = base
    cnt_ref[...] = jnp.broadcast_to(base, cnt_ref.shape)
    route = jnp.zeros((TM, LANES), F32)
    for idx, val in ((ROUTE_E1, e1), (ROUTE_E2, e2), (ROUTE_W1, wt1), (ROUTE_W2, wt2),
                     (ROUTE_R1, r1), (ROUTE_R2, r2)):
        route = jnp.where(lane == idx, val, route)
    route_ref[...] = route


def merge(x_all, mod3, n1, n2, w_in_bf, layer, z, hgf, hgb, rtf, rtb, s5f, s5b, yconv, vx, x0c,
          vecs, glu1, glu2, w_branch, w_out, w_router, b_router, bsz, tpb):
    m, d = x_all.shape
    w = BRANCH_W
    row = lambda width: pl.BlockSpec((TM, width), lambda i: (i, 0))
    srow = pl.BlockSpec((TM, w), lambda i: (_split_tile(i, bsz, tpb), 0))
    zcol = lambda cb: pl.BlockSpec((TM, w), lambda i: (i, cb))
    full = lambda shape: pl.BlockSpec(shape, lambda i: (0,) * len(shape))
    gate_w = lambda n: pl.BlockSpec((None, d, d), lambda i: (layer, 0, N_MIX // d + n))
    return pl.pallas_call(
        functools.partial(_merge_kernel, d=d),
        grid=(m // TM,),
        in_specs=[row(d),
                  pl.BlockSpec((1, 1, 6 * d), lambda i: (_seg_of_tile(i, tpb), 0, 0)),
                  full((1, d)), full((1, d)), gate_w(0), gate_w(1), gate_w(2), gate_w(3),
                  zcol(CB_HG_G), zcol(CB_RT_G), zcol(CB_S5)]
                 + [row(w)] * 6 + [srow, srow, row(w)]
                 + [full((SUBLANES, w)), full((w, w)), full((w, w)), full((N_BRANCH, w, d)), full((d, d)),
                    full((d, LANES)), full((1, LANES))],
        out_specs=[row(d), row(d), row(LANES), full((SUBLANES, LANES))],
        out_shape=[jax.ShapeDtypeStruct((m, d), F32), jax.ShapeDtypeStruct((m, d), F32),
                   jax.ShapeDtypeStruct((m, LANES), F32), jax.ShapeDtypeStruct((SUBLANES, LANES), F32)],
        scratch_shapes=[pltpu.VMEM((1, LANES), F32)],
        compiler_params=_cparams(("arbitrary",)),
        name="merge",
    )(x_all, mod3, n1, n2, w_in_bf, w_in_bf, w_in_bf, w_in_bf, z, z, z,
      hgf, hgb, rtf, rtb, s5f, s5b, yconv, vx, x0c,
      vecs, glu1, glu2, w_branch, w_out, w_router, b_router)


DMA_UNROLL = 8


def _row_copy(src_ref, src_row, dst_ref, dst_row, sem):
    return pltpu.make_async_copy(src_ref.at[pl.ds(src_row, 1), :], dst_ref.at[pl.ds(dst_row, 1), :], sem)


def _gather_block(src_ref, h_hbm, xbuf, sem, blk, slot):
    base = blk * MOE_BLOCK
    for r in range(MOE_BLOCK):
        _row_copy(h_hbm, src_ref[base + r], xbuf.at[slot], r, sem.at[slot]).start(priority=r % 2)


def _wait_block(h_hbm, xbuf, sem, slot):
    pltpu.make_async_copy(h_hbm.at[pl.ds(0, MOE_BLOCK), :], xbuf.at[slot], sem.at[slot]).wait()


def _expert_kernel(plan_ref, src_ref, h_hbm, w1_ref, w3_ref, w2_ref, y_ref, xbuf, sem, *, n_blocks):
    j = pl.program_id(0)
    n_used = plan_ref[n_blocks]
    used = j < n_used
    slot = j % 2

    @pl.when(j == 0)
    def _():
        _gather_block(src_ref, h_hbm, xbuf, sem, 0, 0)

    @pl.when(used)
    def _():
        _wait_block(h_hbm, xbuf, sem, slot)
        _gather_block(src_ref, h_hbm, xbuf, sem, jnp.minimum(j + 1, n_used - 1), 1 - slot)
        xb = xbuf[slot].astype(BF16)
        a = jnp.dot(xb, w1_ref[...].astype(BF16), preferred_element_type=F32)
        g = jnp.dot(xb, w3_ref[...].astype(BF16), preferred_element_type=F32)
        y_ref[...] = jnp.dot((_silu(a) * g).astype(BF16), w2_ref[...].astype(BF16),
                             preferred_element_type=F32)

        @pl.when(j == n_used - 1)
        def _():
            _wait_block(h_hbm, xbuf, sem, 1 - slot)

    @pl.when(jnp.logical_not(used))
    def _():
        y_ref[...] = jnp.zeros_like(y_ref)


def moe_experts(plan, src, h2, w1, w3, w2, layer, n_blocks):
    d = h2.shape[1]
    hid = w1.shape[-1]
    blk = lambda j, plan: jnp.minimum(j, plan[n_blocks] - 1)
    wspec = lambda shape: pl.BlockSpec((None, None) + shape,
                                       lambda j, plan, src: (layer, plan[blk(j, plan)], 0, 0))
    grid_spec = pltpu.PrefetchScalarGridSpec(
        num_scalar_prefetch=2,
        grid=(n_blocks,),
        in_specs=[pl.BlockSpec(memory_space=pl.ANY), wspec((d, hid)), wspec((d, hid)), wspec((hid, d))],
        out_specs=pl.BlockSpec((MOE_BLOCK, d), lambda j, plan, src: (j, 0)),
        scratch_shapes=[pltpu.VMEM((2, MOE_BLOCK, d), F32), pltpu.SemaphoreType.DMA((2,))],
    )
    return pl.pallas_call(
        functools.partial(_expert_kernel, n_blocks=n_blocks),
        grid_spec=grid_spec,
        out_shape=jax.ShapeDtypeStruct((n_blocks * MOE_BLOCK, d), F32),
        compiler_params=_cparams(("arbitrary",)),
        name="moe_experts",
    )(plan, src, h2, w1, w3, w2)


def _combine_kernel(dest_ref, x_ref, route_ref, mod_ref, gfin_ref, y_ref, o_ref, gath, sem, *, d, tpb, final):
    base = (pl.program_id(0) * tpb + pl.program_id(1)) * TM

    def issue(r, carry):
        for k in range(2):
            _row_copy(y_ref, dest_ref[(base + r) * 2 + k], gath.at[k], r, sem).start(priority=k)
        return carry

    lax.fori_loop(0, TM, issue, 0, unroll=DMA_UNROLL)
    for k in range(2):
        pltpu.make_async_copy(y_ref.at[pl.ds(0, TM), :], gath.at[k], sem).wait()
    route = route_ref[...]
    lane = lax.broadcasted_iota(jnp.int32, (TM, LANES), 1)
    moe = _lane_pick(route, lane, ROUTE_W1) * gath[0] + _lane_pick(route, lane, ROUTE_W2) * gath[1]
    x_new = x_ref[...] + mod_ref[0][:, 5 * d:6 * d] * moe
    if final:
        ms = jnp.mean(x_new * x_new, axis=-1, keepdims=True)
        x_new = x_new * lax.rsqrt(ms + EPS) * gfin_ref[...]
    o_ref[...] = x_new


def moe_combine(dest, x_mid, route, mod3, gain_final, ybuf, bsz, tpb, final):
    m, d = x_mid.shape
    nr = tpb - 1 if final else tpb
    tile = lambda b, r: b * tpb + r
    grid_spec = pltpu.PrefetchScalarGridSpec(
        num_scalar_prefetch=1,
        grid=(bsz, nr),
        in_specs=[pl.BlockSpec((TM, d), lambda b, r, dest: (tile(b, r), 0)),
                  pl.BlockSpec((TM, LANES), lambda b, r, dest: (tile(b, r), 0)),
                  pl.BlockSpec((1, 1, 6 * d), lambda b, r, dest: (_seg_of_tile(tile(b, r), tpb), 0, 0)),
                  pl.BlockSpec((1, d), lambda b, r, dest: (0, 0)),
                  pl.BlockSpec(memory_space=pl.ANY)],
        out_specs=pl.BlockSpec((TM, d), lambda b, r, dest: (b * nr + r, 0)),
        scratch_shapes=[pltpu.VMEM((2, TM, d), F32), pltpu.SemaphoreType.DMA(())],
    )
    return pl.pallas_call(
        functools.partial(_combine_kernel, d=d, tpb=tpb, final=final),
        grid_spec=grid_spec,
        out_shape=jax.ShapeDtypeStruct((bsz * nr * TM, d), F32),
        compiler_params=_cparams(("arbitrary", "arbitrary")),
        name="moe_combine",
    )(dest, x_mid, route, mod3, gain_final, ybuf)


def moe_plan(route, counts_row, n_blocks):
    counts = counts_row[:MOE_EXPERTS].astype(jnp.int32)
    padded = (counts + MOE_BLOCK - 1) // MOE_BLOCK * MOE_BLOCK
    pad_end = jnp.cumsum(padded)
    pad_start = pad_end - padded
    expert = route[:, ROUTE_E1:ROUTE_E2 + 1].astype(jnp.int32)
    rank = route[:, ROUTE_R1:ROUTE_R2 + 1].astype(jnp.int32)
    is_e = expert[..., None] == jnp.arange(MOE_EXPERTS, dtype=jnp.int32)
    dest = (jnp.sum(jnp.where(is_e, pad_start, 0), axis=-1) + rank).reshape(-1)
    block_row0 = jnp.arange(n_blocks, dtype=jnp.int32) * MOE_BLOCK
    block_expert = jnp.minimum(jnp.sum((pad_end[None, :] <= block_row0[:, None]).astype(jnp.int32), axis=1),
                               MOE_EXPERTS - 1)
    n_used = pad_end[MOE_EXPERTS - 1:] // MOE_BLOCK
    token = jnp.arange(dest.shape[0], dtype=jnp.int32) // 2
    src = jnp.zeros((n_blocks * MOE_BLOCK,), jnp.int32).at[dest].set(token, unique_indices=True)
    return dest, src, jnp.concatenate([block_expert, n_used])


def _rope_tables(length, lc):
    half = RET_DK // 2
    freqs = ROPE_BASE ** (-jnp.arange(0, half, 2, dtype=F32) / half)
    t = jnp.arange(length)
    row_pos = (t // GRID_W).astype(F32)[:, None]
    col_pos = (t % GRID_W).astype(F32)[:, None]
    lane = jnp.arange(RET_HEADS * RET_DK)
    fr = freqs[lane % (half // 2)][None, :]
    ang = jnp.where(((lane % RET_DK) < half)[None, :], row_pos, col_pos) * fr
    cos = jnp.concatenate([jnp.cos(ang), jnp.ones((lc, lane.shape[0]), F32)], axis=0)
    sin = jnp.concatenate([jnp.sin(ang), jnp.zeros((lc, lane.shape[0]), F32)], axis=0)
    return cos, sin


def _rotate_half_cols(wq):
    quarter = RET_DK // 4
    lane = np.arange(wq.shape[1])
    first = (lane % (2 * quarter)) < quarter
    src = np.where(first, lane + quarter, lane - quarter)
    sign = jnp.asarray(np.where(first, -1.0, 1.0), F32)
    return wq[:, src] * sign


def kernel(x, c, ctx, c_ctx, w_mod, b_mod, norm1, norm2, w_in, hg_lb_logits, hg_norm, hy_conv_w, hy_conv_b, hy_filt_w1, hy_filt_b1, hy_filt_freq1, hy_filt_w2, hy_filt_b2, hy_filt_freq2, hy_filt_w3, hy_skip, ret_decay_logit, ret_norm, s5_lam_re, s5_lam_im, s5_log_dt, s5_b_re, s5_b_im, s5_c_re, s5_c_im, s5_d, s5_glu_w1, s5_glu_w2, w_branch, w_out, moe_w_grp, moe_b_grp, moe_w_exp, moe_b_exp, moe_w1, moe_w3, moe_w2, norm_final):
    bsz, length, d = x.shape
    lc = ctx.shape[1]
    depth = w_mod.shape[0]
    assert lc == TM and length % TM == 0 and (2 * length) % (FFT_FAST * FFT_GROUP) == 0
    tt = length + lc
    tpb = tt // TM
    m = bsz * tt
    n_blocks = -(-(m * 2) // MOE_BLOCK) + MOE_EXPERTS
    w = BRANCH_W

    x_all = jnp.concatenate([x, ctx], axis=1).reshape(m, d)
    cmat = jnp.zeros((SUBLANES, d), F32).at[0].set(c_ctx).at[1:1 + bsz].set(c)
    mods = modvec(cmat, w_mod, b_mod)
    lb_sm = jax.nn.softmax(hg_lb_logits.astype(F32), axis=0)
    lb_all = jnp.cumsum(lb_sm, axis=0) - lb_sm[0]
    cos_tab, sin_tab = _rope_tables(length, lc)
    log_gamma = jax.nn.log_sigmoid(ret_decay_logit.astype(F32))
    gfin = norm_final.reshape(1, d)

    w_in_bf = w_in.astype(BF16)
    rq0 = 5 * w + 3 * HY_W
    nq = RET_HEADS * RET_DK

    for l in range(depth):
        w_rot = jnp.concatenate([_rotate_half_cols(w_in[l, :, rq0:rq0 + nq]),
                                 _rotate_half_cols(w_in[l, :, rq0 + nq:rq0 + 2 * nq])], axis=1).astype(BF16)
        mod3 = mods[l].reshape(SUBLANES, 1, 6 * d)

        z = inproj(x_all, mod3, norm1[l].reshape(1, d), w_in_bf, l, w_rot, tpb)
        hgf, hgb = hgrn(z, lb_all[l], bsz, tpb)
        rtf, rtb = retention(z, cos_tab, sin_tab, log_gamma[l], bsz, tpb)
        s5f, s5b = s5(z, *s5_params(s5_lam_re[l], s5_lam_im[l], s5_log_dt[l], s5_b_re[l], s5_b_im[l],
                                    s5_c_re[l], s5_c_im[l]), bsz, tpb)

        x0c, vx = hy_prep(z, hy_conv_w[l], hy_conv_b[l], bsz, tpb)
        fargs = (hy_filt_w1[l], hy_filt_b1[l], hy_filt_freq1[l], hy_filt_w2[l], hy_filt_b2[l],
                 hy_filt_freq2[l], hy_filt_w3[l])
        hfb, ss = hy_filter(length, *fargs)
        y_lat = hyena_long_conv(vx, bsz, length, hfb, lax.rsqrt(ss + EPS))
        hfbc, ssc = hy_filter(lc, *fargs)
        gwin = jnp.concatenate([hfbc[1, 1:][::-1], hfbc[0], jnp.zeros((1, HY_W), F32)], axis=0) \
            * lax.rsqrt(ssc + EPS)
        yconv = hy_ctx_conv(vx, gwin, y_lat, bsz, tpb, lc)

        vecs = jnp.zeros((SUBLANES, w), F32).at[0].set(hg_norm[l]).at[1].set(ret_norm[l]) \
            .at[2].set(s5_d[l]).at[3].set(hy_skip[l])
        w_router = jnp.zeros((d, LANES), F32).at[:, :MOE_GROUPS].set(moe_w_grp[l]) \
            .at[:, MOE_GROUPS:MOE_GROUPS + MOE_EXPERTS].set(moe_w_exp[l])
        b_router = jnp.zeros((1, LANES), F32).at[0, :MOE_GROUPS].set(moe_b_grp[l]) \
            .at[0, MOE_GROUPS:MOE_GROUPS + MOE_EXPERTS].set(moe_b_exp[l])
        x_mid, h2, route, counts = merge(
            x_all, mod3, norm1[l].reshape(1, d), norm2[l].reshape(1, d), w_in_bf, l, z,
            hgf, hgb, rtf, rtb, s5f, s5b, yconv, vx, x0c, vecs,
            s5_glu_w1[l].astype(BF16), s5_glu_w2[l].astype(BF16), w_branch[l].astype(BF16),
            w_out[l].astype(BF16), w_router, b_router, bsz, tpb)

        dest, src, plan = moe_plan(route, counts[0], n_blocks)
        ybuf = moe_experts(plan, src, h2, moe_w1, moe_w3, moe_w2, l, n_blocks)
        x_all = moe_combine(dest, x_mid, route, mod3, gfin, ybuf, bsz, tpb, l == depth - 1)

    return x_all.reshape(bsz, length, d)
```

```python
import functools
import math

import numpy as np
import jax
import jax.numpy as jnp
from jax import lax
from jax.experimental import pallas as pl
from jax.experimental.pallas import tpu as pltpu

F32 = jnp.float32
BF16 = jnp.bfloat16
HIGHEST = lax.Precision.HIGHEST

GRID_W = 64
N_BRANCH = 4
BRANCH_W = 256
HG_HEADS = 4
HG_CHUNK = 16
HY_W = 256
HY_EMB = 33
HY_BANDS = 16
HY_ORDER = 64
HY_FAST_DECAY = 0.3
HY_SLOW_DECAY = 1.5
HY_TARGET = 1e-2
RET_HEADS = 4
RET_DK = 32
RET_DV = 64
ROPE_BASE = 10000.0
S5_GROUP_CH = 16
S5_GROUPS = 16
S5_STATE = 64
MOE_GROUPS = 4
MOE_PER_GROUP = 8
MOE_EXPERTS = 32
MOE_HIDDEN = 512
MOE_BLOCK = 256
EPS = 1e-6

LANES = 128
SUBLANES = 8
TM = 256
FFT_FAST = 128
VMEM_LIMIT = 56 * 1024 * 1024

N_MIX = 3072
N_ZCOLS = N_MIX + 2 * RET_HEADS * RET_DK
CB_HG_Q, CB_HG_FF, CB_HG_FB, CB_HG_I, CB_HG_G = 3, 4, 5, 6, 7
CB_RT_V, CB_RT_G, CB_S5 = 9, 10, 11
CB128_RT_Q, CB128_RT_K, CB128_RT_QR, CB128_RT_KR = 16, 17, 24, 25


def _cparams(sem):
    return pltpu.CompilerParams(dimension_semantics=sem, vmem_limit_bytes=VMEM_LIMIT)


def _sigmoid(x):
    return 1.0 / (1.0 + jnp.exp(-x))


def _silu(x):
    return x * _sigmoid(x)


def _dot(a, b):
    return jnp.dot(a.astype(BF16), b.astype(BF16), preferred_element_type=F32)


def _dot_nt(a, b):
    return lax.dot_general(a.astype(BF16), b.astype(BF16), (((1,), (1,)), ((), ())),
                           preferred_element_type=F32)


def _dot_hi(a, b):
    return jnp.dot(a, b, precision=HIGHEST, preferred_element_type=F32)


def _dot3(a, b):
    a_hi = a.astype(BF16)
    a_lo = (a - a_hi.astype(F32)).astype(BF16)
    b_hi = b.astype(BF16)
    b_lo = (b - b_hi.astype(F32)).astype(BF16)
    return (jnp.dot(a_hi, b_hi, preferred_element_type=F32)
            + jnp.dot(a_lo, b_hi, preferred_element_type=F32)
            + jnp.dot(a_hi, b_lo, preferred_element_type=F32))


def _split_dot(x, m_bf16):
    hi = x.astype(BF16)
    lo = (x - hi.astype(F32)).astype(BF16)
    return (jnp.dot(hi, m_bf16, preferred_element_type=F32)
            + jnp.dot(lo, m_bf16, preferred_element_type=F32))


def _split_dot_left(m_bf16, x):
    hi = x.astype(BF16)
    lo = (x - hi.astype(F32)).astype(BF16)
    return (jnp.dot(m_bf16, hi, preferred_element_type=F32)
            + jnp.dot(m_bf16, lo, preferred_element_type=F32))


def _seg_of_tile(i, tpb):
    return jnp.where(i % tpb == tpb - 1, 0, 1 + i // tpb)


def _split_tile(i, bsz, tpb):
    b, r = i // tpb, i % tpb
    return jnp.where(r == tpb - 1, bsz * (tpb - 1) + b, b * (tpb - 1) + r)


def _scan_tiles(b, s, tpb):
    last = tpb - 1
    fwd = jnp.where(s == 0, last, s - 1)
    bwd = jnp.where(s == 0, last, last - s)
    return b * tpb + fwd, b * tpb + bwd


def _modvec_kernel(c_ref, w_ref, b_ref, o_ref):
    c = c_ref[...]
    o_ref[0] = _dot_hi(_silu(c), w_ref[0]) + b_ref[0]


def modvec(cmat, w_mod, b_mod):
    depth, d, n6 = w_mod.shape
    nb = n6 // d
    return pl.pallas_call(
        _modvec_kernel,
        grid=(depth, nb),
        in_specs=[pl.BlockSpec((SUBLANES, d), lambda l, j: (0, 0)),
                  pl.BlockSpec((1, d, d), lambda l, j: (l, 0, j)),
                  pl.BlockSpec((1, 1, d), lambda l, j: (l, 0, j))],
        out_specs=pl.BlockSpec((1, SUBLANES, d), lambda l, j: (l, 0, j)),
        out_shape=jax.ShapeDtypeStruct((depth, SUBLANES, n6), F32),
        compiler_params=_cparams(("arbitrary", "arbitrary")),
        name="modvec",
    )(cmat, w_mod, b_mod.reshape(depth, 1, n6))


def _modulated_norm(x, gain, shift, scale):
    ms = jnp.mean(x * x, axis=-1, keepdims=True)
    return (x * lax.rsqrt(ms + EPS) * gain) * (1.0 + scale) + shift


def _inproj_kernel(x_ref, mod_ref, g_ref, w_ref, wrot_ref, z_ref, *, d):
    h = _modulated_norm(x_ref[...], g_ref[...], mod_ref[0, :, 0:d], mod_ref[0, :, d:2 * d]).astype(BF16)
    hy0 = 5 * BRANCH_W
    hy1 = hy0 + 3 * HY_W
    z_ref[:, 0:hy1 - hy0] = jnp.dot(h, w_ref[:, hy0:hy1], preferred_element_type=F32)
    z_ref[:, hy1 - hy0:hy1] = jnp.dot(h, w_ref[:, 0:hy0], preferred_element_type=F32)
    z_ref[:, hy1:N_MIX] = jnp.dot(h, w_ref[:, hy1:N_MIX], preferred_element_type=F32)
    z_ref[:, N_MIX:N_ZCOLS] = jnp.dot(h, wrot_ref[...], preferred_element_type=F32)


def inproj(x_all, mod3, gain, w_in_bf, layer, w_rot, tpb):
    m, d = x_all.shape
    return pl.pallas_call(
        functools.partial(_inproj_kernel, d=d),
        grid=(m // TM,),
        in_specs=[pl.BlockSpec((TM, d), lambda i: (i, 0)),
                  pl.BlockSpec((1, 1, 6 * d), lambda i: (_seg_of_tile(i, tpb), 0, 0)),
                  pl.BlockSpec((1, d), lambda i: (0, 0)),
                  pl.BlockSpec((None, d, N_MIX), lambda i: (layer, 0, 0)),
                  pl.BlockSpec((d, N_ZCOLS - N_MIX), lambda i: (0, 0))],
        out_specs=pl.BlockSpec((TM, N_ZCOLS), lambda i: (i, 0)),
        out_shape=jax.ShapeDtypeStruct((m, N_ZCOLS), F32),
        compiler_params=_cparams(("parallel",)),
        name="inproj",
    )(x_all, mod3, gain, w_in_bf, w_rot)


def _head_masked_rows(x, nchunk, c, hd):
    w = x.shape[-1]
    x3 = x.reshape(nchunk, c, w)
    lane_head = lax.broadcasted_iota(jnp.int32, (1, 1, w), 2) // hd
    parts = [jnp.where(lane_head == h, x3, 0.0) for h in range(w // hd)]
    return jnp.concatenate(parts, axis=1).reshape(nchunk * (w // hd) * c, w)


def _hgrn_direction(q_ref, f_ref, v_ref, lb, tri_ref, o_ref, st_ref, b_s, k_s, qe_s, ke_s, reverse):
    c = HG_CHUNK
    w = BRANCH_W
    nchunk = TM // c
    nh = HG_HEADS
    hd = w // nh
    half = c // 2
    z = f_ref[...]
    sg = _sigmoid(z)
    log2_f = jnp.log2(lb + (1.0 - lb) * sg)
    k = (1.0 - lb) * (1.0 - sg)
    b2 = _split_dot_left(tri_ref[0], log2_f)
    rem2 = _split_dot_left(tri_ref[1], log2_f)
    b_s[...] = b2
    k_s[...] = k
    qe_s[...] = _head_masked_rows(q_ref[...] * jnp.exp2(b2), nchunk, c, hd).astype(BF16)
    ke_s[...] = _head_masked_rows(k * jnp.exp2(rem2), nchunk, c, hd).astype(BF16)

    head_r = lax.broadcasted_iota(jnp.int32, (w, w), 0) // hd
    head_c = lax.broadcasted_iota(jnp.int32, (w, w), 1) // hd
    hm = jnp.where(head_r == head_c, 1.0, 0.0).astype(BF16)
    rows = lax.broadcasted_iota(jnp.int32, (half, 1), 0)

    pieces = []
    for s in range(c):
        for g in range(2):
            lo, hi = g * half, (g + 1) * half - 1
            if (hi < s) if not reverse else (lo > s):
                continue
            full = (lo >= s) if not reverse else (hi <= s)
            pieces.append((s, g, full))

    def chunk_step(n):
        cidx = (nchunk - 1 - n) if reverse else n
        off = pl.multiple_of(cidx * c, c)
        eoff = pl.multiple_of(cidx * (nh * c), nh * c)
        qc = q_ref[pl.ds(off, c), :]
        bc = b_s[pl.ds(off, c), :]
        kc = k_s[pl.ds(off, c), :]
        vc = v_ref[pl.ds(off, c), :]
        parts = []
        for s, g, full in pieces:
            sl = slice(g * half, (g + 1) * half)
            e = jnp.exp2(bc[sl] - bc[s:s + 1, :])
            if not full:
                t = rows + g * half
                e = jnp.where((t <= s) if reverse else (t >= s), e, 0.0)
            parts.append((qc[sl] * kc[s:s + 1, :]) * e)
        p = jnp.concatenate(parts, axis=0).astype(BF16)
        r = jnp.dot(p, hm, preferred_element_type=F32)
        og = [jnp.zeros((half, w), F32), jnp.zeros((half, w), F32)]
        for i, (s, g, _) in enumerate(pieces):
            og[g] = og[g] + r[i * half:(i + 1) * half, :] * vc[s:s + 1, :]
        st = st_ref[...]
        ri = _dot_nt(qe_s[pl.ds(eoff, nh * c), :], st)
        inter = jnp.concatenate([ri[h * c:(h + 1) * c, :] for h in range(nh)], axis=1)
        o_ref[pl.ds(off, c), :] = jnp.concatenate(og, axis=0) + inter
        vexp = jnp.concatenate([vc[:, h * hd:(h + 1) * hd] for h in range(nh)], axis=0)
        kv = lax.dot_general(vexp.astype(BF16), ke_s[pl.ds(eoff, nh * c), :], (((0,), (0,)), ((), ())),
                             preferred_element_type=F32)
        blast = bc[0:1, :] if reverse else bc[c - 1:c, :]
        st_ref[...] = jnp.exp2(blast) * st + kv

    return chunk_step


def _hgrn_kernel(qf_ref, ff_ref, vf_ref, qb_ref, fb_ref, vb_ref, lb_ref, trif_ref, trib_ref, of_ref, ob_ref,
                 stf_ref, stb_ref, bf_s, kf_s, qef_s, kef_s, bb_s, kb_s, qeb_s, keb_s):
    @pl.when(pl.program_id(1) == 0)
    def _():
        stf_ref[...] = jnp.zeros_like(stf_ref)
        stb_ref[...] = jnp.zeros_like(stb_ref)

    step_f = _hgrn_direction(qf_ref, ff_ref, vf_ref, lb_ref[0:1, :], trif_ref, of_ref, stf_ref,
                             bf_s, kf_s, qef_s, kef_s, False)
    step_b = _hgrn_direction(qb_ref, fb_ref, vb_ref, lb_ref[1:2, :], trib_ref, ob_ref, stb_ref,
                             bb_s, kb_s, qeb_s, keb_s, True)

    def body(n, carry):
        step_f(n)
        step_b(n)
        return carry

    lax.fori_loop(0, TM // HG_CHUNK, body, 0, unroll=8)


def hgrn(z, lb, bsz, tpb):
    m = z.shape[0]
    w = BRANCH_W
    hd = w // HG_HEADS
    ri = jnp.arange(TM)[:, None]
    ci = jnp.arange(TM)[None, :]
    same = (ri // HG_CHUNK) == (ci // HG_CHUNK)
    tri = jnp.stack([same & (ci <= ri), same & (ci > ri), same & (ci >= ri), same & (ci < ri)])
    tri = tri.astype(BF16).reshape(2, 2, TM, TM)

    def spec(cb, d):
        return pl.BlockSpec((TM, w), lambda b, s: (_scan_tiles(b, s, tpb)[d], cb))

    out_spec = [pl.BlockSpec((TM, w), lambda b, s: (_scan_tiles(b, s, tpb)[0], 0)),
                pl.BlockSpec((TM, w), lambda b, s: (_scan_tiles(b, s, tpb)[1], 0))]
    return pl.pallas_call(
        _hgrn_kernel,
        grid=(bsz, tpb),
        in_specs=[spec(CB_HG_Q, 0), spec(CB_HG_FF, 0), spec(CB_HG_I, 0),
                  spec(CB_HG_Q, 1), spec(CB_HG_FB, 1), spec(CB_HG_I, 1),
                  pl.BlockSpec((2, w), lambda b, s: (0, 0)),
                  pl.BlockSpec((None, 2, TM, TM), lambda b, s: (0, 0, 0, 0)),
                  pl.BlockSpec((None, 2, TM, TM), lambda b, s: (1, 0, 0, 0))],
        out_specs=out_spec,
        out_shape=[jax.ShapeDtypeStruct((m, w), F32)] * 2,
        scratch_shapes=[pltpu.VMEM((hd, w), F32), pltpu.VMEM((hd, w), F32)]
                       + [pltpu.VMEM((TM, w), F32), pltpu.VMEM((TM, w), F32),
                          pltpu.VMEM((HG_HEADS * TM, w), BF16), pltpu.VMEM((HG_HEADS * TM, w), BF16)] * 2,
        compiler_params=_cparams(("arbitrary", "arbitrary")),
        name="hgrn2",
    )(z, z, z, z, z, z, lb, tri, tri)


def _ret_direction(q_ref, k_ref, qr_ref, kr_ref, v_ref, cos_ref, sin_ref, lgq, lg_ref, d, o_ref, st_ref,
                   reverse):
    nq = RET_HEADS * RET_DK
    nv = RET_HEADS * RET_DV
    cos = cos_ref[...]
    sin = sin_ref[...]
    qx = q_ref[...] * cos + qr_ref[...] * sin
    kx = (k_ref[...] * cos + kr_ref[...] * sin) * (RET_DK ** -0.5)
    v = v_ref[...]
    t = lax.broadcasted_iota(jnp.int32, (TM, 1), 0).astype(F32)
    pos = (TM - 1.0 - t) if reverse else t
    qd = qx * jnp.exp((pos + 1.0) * lgq)
    kd = kx * jnp.exp((TM - 1.0 - pos) * lgq)
    ti = lax.broadcasted_iota(jnp.int32, (TM, TM), 0)
    si = lax.broadcasted_iota(jnp.int32, (TM, TM), 1)
    rel = ((si - ti) if reverse else (ti - si)).astype(F32)
    live = rel >= 0.0
    relc = jnp.where(live, rel, 0.0)
    lane_q = lax.broadcasted_iota(jnp.int32, (1, nq), 1) // RET_DK
    lane_v = lax.broadcasted_iota(jnp.int32, (1, nv), 1) // RET_DV
    st = st_ref[...]
    o = _dot_nt(qd, st)
    vb = v.astype(BF16)
    for h in range(RET_HEADS):
        dmat = jnp.where(live, jnp.exp(relc * lg_ref[d, h]), 0.0)
        sc = _dot_nt(jnp.where(lane_q == h, qx, 0.0), kx) * dmat
        oh = jnp.dot(sc.astype(BF16), vb, preferred_element_type=F32)
        o = o + jnp.where(lane_v == h, oh, 0.0)
    o_ref[...] = o
    kv = _dot(v.T, kd)
    hr = lax.broadcasted_iota(jnp.int32, (nv, nq), 0) // RET_DV
    hc = lax.broadcasted_iota(jnp.int32, (nv, nq), 1) // RET_DK
    st_ref[...] = st * jnp.exp(float(TM) * lgq) + jnp.where(hr == hc, kv, 0.0)


def _ret_kernel(lg_ref, qf, kf, qrf, krf, vf, cosf, sinf, qb, kb, qrb, krb, vb, cosb, sinb, lgq_ref,
                of_ref, ob_ref, stf_ref, stb_ref):
    @pl.when(pl.program_id(1) == 0)
    def _():
        stf_ref[...] = jnp.zeros_like(stf_ref)
        stb_ref[...] = jnp.zeros_like(stb_ref)

    _ret_direction(qf, kf, qrf, krf, vf, cosf, sinf, lgq_ref[0:1, :], lg_ref, 0, of_ref, stf_ref, False)
    _ret_direction(qb, kb, qrb, krb, vb, cosb, sinb, lgq_ref[1:2, :], lg_ref, 1, ob_ref, stb_ref, True)


def retention(z, cos_tab, sin_tab, log_gamma, bsz, tpb):
    m = z.shape[0]
    nq = RET_HEADS * RET_DK
    nv = RET_HEADS * RET_DV
    lgq = jnp.repeat(log_gamma, RET_DK, axis=1)

    def zspec(cb, width, d):
        return pl.BlockSpec((TM, width), lambda b, s: (_scan_tiles(b, s, tpb)[d], cb))

    def tspec(d):
        return pl.BlockSpec((TM, nq), lambda b, s: (_scan_tiles(0, s, tpb)[d], 0))

    def side(d):
        return [zspec(CB128_RT_Q, nq, d), zspec(CB128_RT_K, nq, d), zspec(CB128_RT_QR, nq, d),
                zspec(CB128_RT_KR, nq, d), zspec(CB_RT_V, nv, d), tspec(d), tspec(d)]

    return pl.pallas_call(
        _ret_kernel,
        grid=(bsz, tpb),
        in_specs=[pl.BlockSpec(memory_space=pltpu.SMEM)] + side(0) + side(1)
                 + [pl.BlockSpec((2, nq), lambda b, s: (0, 0))],
        out_specs=[pl.BlockSpec((TM, nv), lambda b, s: (_scan_tiles(b, s, tpb)[0], 0)),
                   pl.BlockSpec((TM, nv), lambda b, s: (_scan_tiles(b, s, tpb)[1], 0))],
        out_shape=[jax.ShapeDtypeStruct((m, nv), F32)] * 2,
        scratch_shapes=[pltpu.VMEM((nv, nq), F32), pltpu.VMEM((nv, nq), F32)],
        compiler_params=_cparams(("arbitrary", "arbitrary")),
        name="retention",
    )(log_gamma, z, z, z, z, z, cos_tab, sin_tab, z, z, z, z, z, cos_tab, sin_tab, lgq)


S5_NP = S5_GROUPS * S5_STATE


def _s5_direction(u_ref, bre_ref, bim_ref, cre_ref, cim_ref, tab_ref, o_ref, car_ref, xr_s, xi_s, reverse):
    nb = TM // SUBLANES
    ub = u_ref[...].astype(BF16)
    xr = jnp.dot(ub, bre_ref[0], preferred_element_type=F32).reshape(nb, SUBLANES, S5_NP)
    xi = jnp.dot(ub, bim_ref[0], preferred_element_type=F32).reshape(nb, SUBLANES, S5_NP)
    sub = lax.broadcasted_iota(jnp.int32, (1, SUBLANES, 1), 1)
    for k, row in ((1, 8), (2, 9), (4, 10)):
        shift = (SUBLANES - k) if reverse else k
        keep = (sub < SUBLANES - k) if reverse else (sub >= k)
        ar = jnp.where(keep, tab_ref[0, 0, row:row + 1, :], 0.0)
        ai = jnp.where(keep, tab_ref[0, 1, row:row + 1, :], 0.0)
        sr = pltpu.roll(xr, shift, 1)
        si = pltpu.roll(xi, shift, 1)
        xr, xi = xr + ar * sr - ai * si, xi + ar * si + ai * sr
    xr_s[...] = xr.reshape(TM, S5_NP)
    xi_s[...] = xi.reshape(TM, S5_NP)
    pr = tab_ref[0, 0, 0:SUBLANES, :]
    pi = tab_ref[0, 1, 0:SUBLANES, :]

    def body(n, carry):
        cr, ci = carry
        blk = (nb - 1 - n) if reverse else n
        off = pl.multiple_of(blk * SUBLANES, SUBLANES)
        br = xr_s[pl.ds(off, SUBLANES), :] + pr * cr - pi * ci
        bi = xi_s[pl.ds(off, SUBLANES), :] + pr * ci + pi * cr
        xr_s[pl.ds(off, SUBLANES), :] = br
        xi_s[pl.ds(off, SUBLANES), :] = bi
        if reverse:
            return br[0:1, :], bi[0:1, :]
        return br[SUBLANES - 1:SUBLANES, :], bi[SUBLANES - 1:SUBLANES, :]

    cr, ci = lax.fori_loop(0, nb, body, (car_ref[0:1, :], car_ref[1:2, :]))
    car_ref[0:1, :] = cr
    car_ref[1:2, :] = ci
    o_ref[...] = (jnp.dot(xr_s[...].astype(BF16), cre_ref[0], preferred_element_type=F32)
                  - jnp.dot(xi_s[...].astype(BF16), cim_ref[0], preferred_element_type=F32))


def _s5_kernel(uf, ub, bre, bim, cre, cim, tabf, tabb, bre2, bim2, cre2, cim2, of_ref, ob_ref,
               carf, carb, xr_s, xi_s):
    @pl.when(pl.program_id(1) == 0)
    def _():
        carf[...] = jnp.zeros_like(carf)
        carb[...] = jnp.zeros_like(carb)

    _s5_direction(uf, bre, bim, cre, cim, tabf, of_ref, carf, xr_s, xi_s, False)
    _s5_direction(ub, bre2, bim2, cre2, cim2, tabb, ob_ref, carb, xr_s, xi_s, True)


def s5(z, bblk_re, bblk_im, cblk_re, cblk_im, tab, bsz, tpb):
    m = z.shape[0]
    w = BRANCH_W

    def pspec(shape, d):
        return pl.BlockSpec((1,) + shape, lambda b, s: (d,) + (0,) * len(shape))

    def uspec(d):
        return pl.BlockSpec((TM, w), lambda b, s: (_scan_tiles(b, s, tpb)[d], CB_S5))

    return pl.pallas_call(
        _s5_kernel,
        grid=(bsz, tpb),
        in_specs=[uspec(0), uspec(1),
                  pspec((w, S5_NP), 0), pspec((w, S5_NP), 0), pspec((S5_NP, w), 0), pspec((S5_NP, w), 0),
                  pspec((2, 16, S5_NP), 0), pspec((2, 16, S5_NP), 1),
                  pspec((w, S5_NP), 1), pspec((w, S5_NP), 1), pspec((S5_NP, w), 1), pspec((S5_NP, w), 1)],
        out_specs=[pl.BlockSpec((TM, w), lambda b, s: (_scan_tiles(b, s, tpb)[0], 0)),
                   pl.BlockSpec((TM, w), lambda b, s: (_scan_tiles(b, s, tpb)[1], 0))],
        out_shape=[jax.ShapeDtypeStruct((m, w), F32)] * 2,
        scratch_shapes=[pltpu.VMEM((2, S5_NP), F32), pltpu.VMEM((2, S5_NP), F32),
                        pltpu.VMEM((TM, S5_NP), F32), pltpu.VMEM((TM, S5_NP), F32)],
        compiler_params=_cparams(("arbitrary", "arbitrary")),
        name="s5",
    )(z, z, bblk_re, bblk_im, cblk_re, cblk_im, tab, tab, bblk_re, bblk_im, cblk_re, cblk_im)


def s5_params(lam_re, lam_im, log_dt, b_re, b_im, c_re, c_im):
    lr = jnp.minimum(lam_re.astype(F32), -1e-4)
    li = lam_im.astype(F32)
    dt = jnp.exp(log_dt.astype(F32))[..., None]

    def apow(k):
        mag = jnp.exp(k * lr * dt)
        return mag * jnp.cos(k * li * dt), mag * jnp.sin(k * li * dt)

    ar, ai = apow(1.0)
    den = lr * lr + li * li
    gr = ((ar - 1.0) * lr + ai * li) / den
    gi = (ai * lr - (ar - 1.0) * li) / den
    bbr = gr[..., None] * b_re - gi[..., None] * b_im
    bbi = gr[..., None] * b_im + gi[..., None] * b_re
    eye = jnp.eye(S5_GROUPS, dtype=F32)

    def blk_in(bb):
        t = jnp.einsum('dgpc,gh->dgchp', bb, eye)
        return t.reshape(2, S5_GROUPS * S5_GROUP_CH, S5_NP).astype(BF16)

    def blk_out(cc):
        t = jnp.einsum('dgcp,gh->dgphc', cc.astype(F32), eye)
        return t.reshape(2, S5_NP, S5_GROUPS * S5_GROUP_CH).astype(BF16)

    rows_r, rows_i = [], []
    for j in range(SUBLANES):
        pr_f, pi_f = apow(float(j + 1))
        pr_b, pi_b = apow(float(SUBLANES - j))
        rows_r.append(jnp.stack([pr_f[0], pr_b[1]]))
        rows_i.append(jnp.stack([pi_f[0], pi_b[1]]))
    for k in (1.0, 2.0, 4.0):
        pr_k, pi_k = apow(k)
        rows_r.append(pr_k)
        rows_i.append(pi_k)
    zero = jnp.zeros_like(rows_r[0])
    while len(rows_r) < 16:
        rows_r.append(zero)
        rows_i.append(zero)
    tab_r = jnp.stack(rows_r, axis=1).reshape(2, 16, S5_NP)
    tab_i = jnp.stack(rows_i, axis=1).reshape(2, 16, S5_NP)
    tab = jnp.stack([tab_r, tab_i], axis=1)
    return blk_in(bbr), blk_in(bbi), blk_out(c_re), blk_out(c_im), tab


def _hy_prep_kernel(u_ref, prev_ref, next_ref, w_ref, b_ref, x0_ref, vx_ref, *, tpb):
    i = pl.program_id(0)
    r = i % tpb
    first = (r == 0) | (r == tpb - 1)
    last = r >= tpb - 2
    u = u_ref[...]
    w = w_ref[...]
    up_row = jnp.where(first, 0.0, prev_ref[SUBLANES - 1:SUBLANES, :])
    dn_row = jnp.where(last, 0.0, next_ref[0:1, :])
    row = lax.broadcasted_iota(jnp.int32, (TM, 1), 0)
    um = jnp.where(row == 0, up_row, pltpu.roll(u, 1, 0))
    un = jnp.where(row == TM - 1, dn_row, pltpu.roll(u, TM - 1, 0))
    y = um * w[0:1, :] + u * w[1:2, :] + un * w[2:3, :] + b_ref[...]
    x0_ref[...] = y[:, 0:HY_W]
    vx_ref[...] = y[:, 2 * HY_W:3 * HY_W] * y[:, HY_W:2 * HY_W]


def hy_prep(z, conv_w, conv_b, bsz, tpb):
    m = z.shape[0]
    w3 = 3 * HY_W
    rb = TM // SUBLANES
    nrb = m // SUBLANES
    return pl.pallas_call(
        functools.partial(_hy_prep_kernel, tpb=tpb),
        grid=(m // TM,),
        in_specs=[pl.BlockSpec((TM, w3), lambda i: (i, 0)),
                  pl.BlockSpec((SUBLANES, w3), lambda i: (jnp.maximum(i * rb - 1, 0), 0)),
                  pl.BlockSpec((SUBLANES, w3), lambda i: (jnp.minimum((i + 1) * rb, nrb - 1), 0)),
                  pl.BlockSpec((3, w3), lambda i: (0, 0)),
                  pl.BlockSpec((1, w3), lambda i: (0, 0))],
        out_specs=[pl.BlockSpec((TM, HY_W), lambda i: (i, 0)),
                   pl.BlockSpec((TM, HY_W), lambda i: (_split_tile(i, bsz, tpb), 0))],
        out_shape=[jax.ShapeDtypeStruct((m, HY_W), F32)] * 2,
        compiler_params=_cparams(("parallel",)),
        name="hy_prep",
    )(z, z, z, conv_w, conv_b.reshape(1, w3))


def _hy_filter_kernel(zf_ref, w1_ref, b1_ref, f1_ref, w2_ref, b2_ref, f2_ref, w3_ref, dl_ref,
                      hfb_ref, ss_ref):
    i = pl.program_id(0)
    zf = zf_ref[...]
    hid = jnp.sin(f1_ref[...] * (_dot_hi(zf, w1_ref[...]) + b1_ref[...]))
    hid = jnp.sin(f2_ref[...] * (_dot_hi(hid, w2_ref[...]) + b2_ref[...]))
    filt = _dot_hi(hid, w3_ref[...])
    window = jnp.exp(-zf[:, 0:1] * dl_ref[...])
    hf = filt[:, 0:HY_W] * window
    row = lax.broadcasted_iota(jnp.int32, (TM, 1), 0) + i * TM
    hb = jnp.where(row == 0, 0.0, filt[:, HY_W:2 * HY_W] * window)
    hfb_ref[0] = hf
    hfb_ref[1] = hb
    part = jnp.sum(hf * hf + hb * hb, axis=0, keepdims=True)

    @pl.when(i == 0)
    def _():
        ss_ref[...] = jnp.zeros_like(ss_ref)

    ss_ref[...] = ss_ref[...] + part


def hy_filter(length, w1, b1, fr1, w2, b2, fr2, w3):
    t = jnp.linspace(0.0, 1.0, length, dtype=F32)[:, None]
    wv = (2.0 * math.pi / length) * jnp.arange(length, dtype=F32)[:, None]
    f = jnp.linspace(1e-4, HY_BANDS - 1.0, HY_BANDS, dtype=F32)[None, :]
    zfeat = jnp.concatenate([t, jnp.cos(f * wv), -jnp.sin(f * wv)], axis=-1)
    zfeat = jnp.pad(zfeat, ((0, 0), (0, LANES - HY_EMB)))
    w1p = jnp.pad(w1, ((0, LANES - HY_EMB), (0, 0)))
    deltas = jnp.abs(jnp.linspace(math.log(HY_TARGET) / HY_SLOW_DECAY,
                                  math.log(HY_TARGET) / HY_FAST_DECAY, HY_W, dtype=F32))[None, :]
    full = lambda shape: pl.BlockSpec(shape, lambda i: (0,) * len(shape))
    return pl.pallas_call(
        _hy_filter_kernel,
        grid=(length // TM,),
        in_specs=[pl.BlockSpec((TM, LANES), lambda i: (i, 0)),
                  full((LANES, HY_ORDER)), full((1, HY_ORDER)), full((1, HY_ORDER)),
                  full((HY_ORDER, HY_ORDER)), full((1, HY_ORDER)), full((1, HY_ORDER)),
                  full((HY_ORDER, 2 * HY_W)), full((1, HY_W))],
        out_specs=[pl.BlockSpec((2, TM, HY_W), lambda i: (0, i, 0)),
                   pl.BlockSpec((1, HY_W), lambda i: (0, 0))],
        out_shape=[jax.ShapeDtypeStruct((2, length, HY_W), F32), jax.ShapeDtypeStruct((1, HY_W), F32)],
        compiler_params=_cparams(("arbitrary",)),
        name="hy_filter",
    )(zfeat, w1p, b1.reshape(1, -1), fr1.reshape(1, -1), w2, b2.reshape(1, -1), fr2.reshape(1, -1), w3, deltas)


def _dft_consts(length):
    n = 2 * length
    ns = n // FFT_FAST
    a = jnp.arange(ns, dtype=jnp.int32)
    ang = ((a[:, None] * a[None, :]) % ns).astype(F32) * (2.0 * math.pi / ns)
    f_fwd = jnp.concatenate([jnp.cos(ang), -jnp.sin(ang)], axis=0)
    f_inv = jnp.concatenate([jnp.cos(ang), -jnp.sin(ang)], axis=1)
    b = jnp.arange(FFT_FAST, dtype=jnp.int32)
    th1 = (a[:, None] * b[None, :]).astype(F32) * (2.0 * math.pi / n)
    th2 = ((b[:, None] * b[None, :]) % FFT_FAST).astype(F32) * (2.0 * math.pi / FFT_FAST)
    c1, s1 = jnp.cos(th1)[:, None, :], jnp.sin(th1)[:, None, :]
    c2, s2 = jnp.cos(th2)[None, :, :], jnp.sin(th2)[None, :, :]
    mr = c1 * c2 - s1 * s2
    mi = -(s1 * c2 + c1 * s2)
    m_fwd = jnp.concatenate([jnp.concatenate([mr, -mi], axis=2),
                             jnp.concatenate([mi, mr], axis=2)], axis=1)
    ir = jnp.transpose(mr, (0, 2, 1)) / n
    ii = jnp.transpose(-mi, (0, 2, 1)) / n
    m_inv = jnp.concatenate([jnp.concatenate([ir, -ii], axis=2),
                             jnp.concatenate([ii, ir], axis=2)], axis=1)
    return tuple(t.astype(BF16) for t in (f_fwd, f_inv, m_fwd, m_inv))


def _dft_slow_fwd_kernel(f_ref, x_ref, o_ref):
    ns = o_ref.shape[2]
    for bl in range(x_ref.shape[1]):
        r = jnp.dot(f_ref[...], x_ref[:, bl, :].astype(BF16), preferred_element_type=F32)
        o_ref[0, 0, :, bl, :] = r[0:ns]
        o_ref[0, 1, :, bl, :] = r[ns:]


def dft_slow_fwd(fmat, x3, bx):
    two_ns, na = fmat.shape
    ns = two_ns // 2
    cw = x3.shape[-1]
    kb = SUBLANES
    return pl.pallas_call(
        _dft_slow_fwd_kernel,
        grid=(bx, FFT_FAST // kb),
        in_specs=[pl.BlockSpec((two_ns, na), lambda b, j: (0, 0)),
                  pl.BlockSpec((na, kb, cw), lambda b, j: (b, j, 0))],
        out_specs=pl.BlockSpec((1, 2, ns, kb, cw), lambda b, j: (b, 0, 0, j, 0)),
        out_shape=jax.ShapeDtypeStruct((bx, 2, ns, FFT_FAST, cw), F32),
        compiler_params=_cparams(("parallel", "parallel")),
        name="dft_slow_fwd",
    )(fmat, x3)


def _dft_slow_inv_kernel(f_ref, b_ref, o_ref, *, bx):
    @pl.when(pl.program_id(0) < bx)
    def _():
        for bl in range(o_ref.shape[1]):
            bb = jnp.concatenate([b_ref[0, 0, :, bl, :], b_ref[0, 1, :, bl, :]], axis=0).astype(BF16)
            o_ref[:, bl, :] = jnp.dot(f_ref[...], bb, preferred_element_type=F32)

    @pl.when(pl.program_id(0) >= bx)
    def _():
        o_ref[...] = jnp.zeros_like(o_ref)


def dft_slow_inv(fmat, bv):
    bx, _, ns, nf, cw = bv.shape
    na = fmat.shape[0]
    kb = SUBLANES
    return pl.pallas_call(
        functools.partial(_dft_slow_inv_kernel, bx=bx),
        grid=(bx + 1, nf // kb),
        in_specs=[pl.BlockSpec((na, 2 * ns), lambda b, j: (0, 0)),
                  pl.BlockSpec((1, 2, ns, kb, cw), lambda b, j: (jnp.minimum(b, bx - 1), 0, 0, j, 0))],
        out_specs=pl.BlockSpec((na, kb, cw), lambda b, j: (b, j, 0)),
        out_shape=jax.ShapeDtypeStruct(((bx + 1) * na, nf, cw), F32),
        compiler_params=_cparams(("parallel", "parallel")),
        name="dft_slow_inv",
    )(fmat, bv)


FFT_GROUP = 8


def _dft_fast_filter_kernel(m_ref, a_ref, sc_ref, o_ref):
    sc = sc_ref[...]
    for j in range(FFT_GROUP):
        af = jnp.concatenate([a_ref[0, 0, j], a_ref[0, 1, j]], axis=0).astype(BF16)
        ab = jnp.concatenate([a_ref[1, 0, j], a_ref[1, 1, j]], axis=0).astype(BF16)
        xf = jnp.dot(m_ref[j], af, preferred_element_type=F32)
        xb = jnp.dot(m_ref[j], ab, preferred_element_type=F32)
        o_ref[0, j] = (xf[0:FFT_FAST] + xb[0:FFT_FAST]) * sc
        o_ref[1, j] = (xf[FFT_FAST:] - xb[FFT_FAST:]) * sc


def dft_fast_filter(m_fwd, a5, scale):
    _, _, ns, nf, cw = a5.shape
    g = FFT_GROUP
    return pl.pallas_call(
        _dft_fast_filter_kernel,
        grid=(ns // g,),
        in_specs=[pl.BlockSpec((g, 2 * nf, 2 * nf), lambda i: (i, 0, 0)),
                  pl.BlockSpec((2, 2, g, nf, cw), lambda i: (0, 0, i, 0, 0)),
                  pl.BlockSpec((1, cw), lambda i: (0, 0))],
        out_specs=pl.BlockSpec((2, g, nf, cw), lambda i: (0, i, 0, 0)),
        out_shape=jax.ShapeDtypeStruct((2, ns, nf, cw), F32),
        compiler_params=_cparams(("parallel",)),
        name="dft_fast_filter",
    )(m_fwd, a5, scale)


def _dft_fast_conv_kernel(mf_ref, mi_ref, a_ref, k_ref, o_ref):
    for j in range(FFT_GROUP):
        a = jnp.concatenate([a_ref[0, 0, j], a_ref[0, 1, j]], axis=0).astype(BF16)
        x = jnp.dot(mf_ref[j], a, preferred_element_type=F32)
        xr, xi = x[0:FFT_FAST], x[FFT_FAST:]
        kr, ki = k_ref[0, j], k_ref[1, j]
        y = jnp.concatenate([xr * kr - xi * ki, xr * ki + xi * kr], axis=0).astype(BF16)
        bv = jnp.dot(mi_ref[j], y, preferred_element_type=F32)
        o_ref[0, 0, j] = bv[0:FFT_FAST]
        o_ref[0, 1, j] = bv[FFT_FAST:]


def dft_fast_conv(m_fwd, m_inv, a5, kf):
    bx, _, ns, nf, cw = a5.shape
    g = FFT_GROUP
    return pl.pallas_call(
        _dft_fast_conv_kernel,
        grid=(bx, ns // g),
        in_specs=[pl.BlockSpec((g, 2 * nf, 2 * nf), lambda b, i: (i, 0, 0)),
                  pl.BlockSpec((g, 2 * nf, 2 * nf), lambda b, i: (i, 0, 0)),
                  pl.BlockSpec((1, 2, g, nf, cw), lambda b, i: (b, 0, i, 0, 0)),
                  pl.BlockSpec((2, g, nf, cw), lambda b, i: (0, i, 0, 0))],
        out_specs=pl.BlockSpec((1, 2, g, nf, cw), lambda b, i: (b, 0, i, 0, 0)),
        out_shape=jax.ShapeDtypeStruct((bx, 2, ns, nf, cw), F32),
        compiler_params=_cparams(("parallel", "parallel")),
        name="dft_fast_conv",
    )(m_fwd, m_inv, a5, kf)


def hyena_long_conv(vx_rows, bsz, length, hfb, scale):
    cw = vx_rows.shape[-1]
    ns = 2 * length // FFT_FAST
    f_fwd, f_inv, m_fwd, m_inv = _dft_consts(length)
    fa = dft_slow_fwd(f_fwd[:, :ns // 2], hfb.reshape(-1, FFT_FAST, cw), 2)
    kf = dft_fast_filter(m_fwd, fa, scale)
    xa = dft_slow_fwd(f_fwd[:, :ns // 2], vx_rows.reshape(-1, FFT_FAST, cw), bsz)
    bv = dft_fast_conv(m_fwd, m_inv, xa, kf)
    return dft_slow_inv(f_inv[:ns // 2], bv).reshape((bsz + 1) * length, cw)


def _hy_ctx_conv_kernel(v_ref, g_ref, y_hbm_ref, o_ref, *, lc):
    del y_hbm_ref
    nb = lc // SUBLANES

    def body(gi, acc):
        base = pl.multiple_of((nb - 1 - gi) * SUBLANES, SUBLANES)
        win = g_ref[pl.ds(base, lc + SUBLANES), :]
        for j in range(SUBLANES):
            vrow = v_ref[pl.ds(gi * SUBLANES + j, 1), :]
            acc = acc + win[SUBLANES - 1 - j:SUBLANES - 1 - j + lc, :] * vrow
        return acc

    o_ref[...] = lax.fori_loop(0, nb, body, jnp.zeros((lc, HY_W), F32))


def hy_ctx_conv(vx, gwin, yconv, bsz, tpb, lc):
    ctx_tile = lambda b: (bsz * (tpb - 1) + b, 0)
    return pl.pallas_call(
        functools.partial(_hy_ctx_conv_kernel, lc=lc),
        grid=(bsz,),
        in_specs=[pl.BlockSpec((lc, HY_W), ctx_tile),
                  pl.BlockSpec((2 * lc, HY_W), lambda b: (0, 0)),
                  pl.BlockSpec(memory_space=pl.ANY)],
        out_specs=pl.BlockSpec((lc, HY_W), ctx_tile),
        out_shape=jax.ShapeDtypeStruct(yconv.shape, F32),
        input_output_aliases={2: 0},
        compiler_params=_cparams(("parallel",)),
        name="hy_ctx_conv",
    )(vx, gwin, yconv)


ROUTE_E1, ROUTE_E2, ROUTE_W1, ROUTE_W2, ROUTE_R1, ROUTE_R2 = 0, 1, 2, 3, 4, 5
NEG_BIG = -1e30


def _gelu_tanh(x):
    return 0.5 * x * (1.0 + jnp.tanh(math.sqrt(2.0 / math.pi) * (x + 0.044715 * (x * x * x))))


def _lane_pick(slab, lane, idx):
    return jnp.sum(jnp.where(lane == idx, slab, 0.0), axis=-1, keepdims=True)


def _merge_kernel(x_ref, mod_ref, n1_ref, n2_ref, wg0_ref, wg1_ref, wg2_ref, wg3_ref, zhg_ref, zrt_ref, zs5_ref,
                  hgf_ref, hgb_ref, rtf_ref, rtb_ref, s5f_ref, s5b_ref, yc_ref, vx_ref, x0_ref,
                  vec_ref, glu1_ref, glu2_ref, wb_ref, wo_ref, wr_ref, br_ref,
                  xo_ref, h2_ref, route_ref, cnt_ref, base_ref, *, d):
    i = pl.program_id(0)
    w = BRANCH_W
    x = x_ref[...]
    mod = mod_ref[0]
    h = _modulated_norm(x, n1_ref[...], mod[:, 0:d], mod[:, d:2 * d]).astype(BF16)

    hd = w // HG_HEADS
    hr = lax.broadcasted_iota(jnp.int32, (w, w), 0) // hd
    hc = lax.broadcasted_iota(jnp.int32, (w, w), 1) // hd
    hm = jnp.where(hr == hc, 1.0, 0.0).astype(BF16)
    inv = 1.0 / hd

    hg = hgf_ref[...] + hgb_ref[...]
    ms = _split_dot(hg * hg, hm) * inv
    br_hg = hg * lax.rsqrt(ms + EPS) * vec_ref[0:1, :] * _silu(zhg_ref[...])

    rt = rtf_ref[...] + rtb_ref[...]
    cen = rt - _split_dot(rt, hm) * inv
    var = _split_dot(cen * cen, hm) * inv
    br_rt = cen * lax.rsqrt(var + EPS) * vec_ref[1:2, :] * _silu(zrt_ref[...])

    vx = vx_ref[...]
    br_hy = (yc_ref[...] + vx * vec_ref[3:4, :]) * x0_ref[...]

    y5 = _gelu_tanh(s5f_ref[...] + s5b_ref[...] + zs5_ref[...] * vec_ref[2:3, :]).astype(BF16)
    br_s5 = (jnp.dot(y5, glu1_ref[...], preferred_element_type=F32)
             * _sigmoid(jnp.dot(y5, glu2_ref[...], preferred_element_type=F32)))

    merged = jnp.zeros((TM, d), F32)
    for n, (br, wg_ref) in enumerate(zip((br_hg, br_hy, br_rt, br_s5), (wg0_ref, wg1_ref, wg2_ref, wg3_ref))):
        gate = _sigmoid(jnp.dot(h, wg_ref[...], preferred_element_type=F32))
        merged = merged + gate * jnp.dot(br.astype(BF16), wb_ref[n], preferred_element_type=F32)
    mix = jnp.dot(merged.astype(BF16), wo_ref[...], preferred_element_type=F32)
    x_mid = x + mod[:, 2 * d:3 * d] * mix
    xo_ref[...] = x_mid
    h2 = _modulated_norm(x_mid, n2_ref[...], mod[:, 3 * d:4 * d], mod[:, 4 * d:5 * d])
    h2_ref[...] = h2

    logits = _dot3(h2, wr_ref[...]) + br_ref[...]
    lane = lax.broadcasted_iota(jnp.int32, (TM, LANES), 1).astype(F32)
    gl = jnp.where(lane < MOE_GROUPS, logits, NEG_BIG)
    gmax = jnp.max(gl, axis=-1, keepdims=True)
    grp_p = 1.0 / jnp.sum(jnp.exp(gl - gmax), axis=-1, keepdims=True)
    gi = jnp.min(jnp.where(gl == gmax, lane, float(LANES)), axis=-1, keepdims=True)
    lo = MOE_GROUPS + MOE_PER_GROUP * gi
    el = jnp.where((lane >= lo) & (lane < lo + MOE_PER_GROUP), logits, NEG_BIG)
    m1 = jnp.max(el, axis=-1, keepdims=True)
    i1 = jnp.min(jnp.where(el == m1, lane, float(LANES)), axis=-1, keepdims=True)
    el2 = jnp.where(lane == i1, NEG_BIG, el)
    m2 = jnp.max(el2, axis=-1, keepdims=True)
    i2 = jnp.min(jnp.where(el2 == m2, lane, float(LANES)), axis=-1, keepdims=True)
    ratio = jnp.exp(m2 - m1)
    wt1 = grp_p / (1.0 + ratio)
    wt2 = grp_p * ratio / (1.0 + ratio)
    e1 = i1 - MOE_GROUPS
    e2 = i2 - MOE_GROUPS

    @pl.when(i == 0)
    def _():
        base_ref[...] = jnp.zeros_like(base_ref)

    oh1 = jnp.where(lane == e1, 1.0, 0.0)
    oh2 = jnp.where(lane == e2, 1.0, 0.0)
    ri = lax.broadcasted_iota(jnp.int32, (TM, TM), 0)
    ci = lax.broadcasted_iota(jnp.int32, (TM, TM), 1)
    below = jnp.where(ci < ri, 1.0, 0.0).astype(BF16)
    c1 = jnp.dot(below, oh1.astype(BF16), preferred_element_type=F32)
    c2 = jnp.dot(below, oh2.astype(BF16), preferred_element_type=F32)
    tot1 = jnp.sum(oh1, axis=0, keepdims=True)
    tot2 = jnp.sum(oh2, axis=0, keepdims=True)
    base = base_ref[...]
    r1 = jnp.sum(oh1 * (base + c1), axis=-1, keepdims=True)
    r2 = jnp.sum(oh2 * (base + tot1 + c2), axis=-1, keepdims=True)
    base = base + tot1 + tot2
    base_ref[...] = base
    cnt_ref[...] = jnp.broadcast_to(base, cnt_ref.shape)
    route = jnp.zeros((TM, LANES), F32)
    for idx, val in ((ROUTE_E1, e1), (ROUTE_E2, e2), (ROUTE_W1, wt1), (ROUTE_W2, wt2),
                     (ROUTE_R1, r1), (ROUTE_R2, r2)):
        route = jnp.where(lane == idx, val, route)
    route_ref[...] = route


def merge(x_all, mod3, n1, n2, w_in_bf, layer, z, hgf, hgb, rtf, rtb, s5f, s5b, yconv, vx, x0c,
          vecs, glu1, glu2, w_branch, w_out, w_router, b_router, bsz, tpb):
    m, d = x_all.shape
    w = BRANCH_W
    row = lambda width: pl.BlockSpec((TM, width), lambda i: (i, 0))
    srow = pl.BlockSpec((TM, w), lambda i: (_split_tile(i, bsz, tpb), 0))
    zcol = lambda cb: pl.BlockSpec((TM, w), lambda i: (i, cb))
    full = lambda shape: pl.BlockSpec(shape, lambda i: (0,) * len(shape))
    gate_w = lambda n: pl.BlockSpec((None, d, d), lambda i: (layer, 0, N_MIX // d + n))
    return pl.pallas_call(
        functools.partial(_merge_kernel, d=d),
        grid=(m // TM,),
        in_specs=[row(d),
                  pl.BlockSpec((1, 1, 6 * d), lambda i: (_seg_of_tile(i, tpb), 0, 0)),
                  full((1, d)), full((1, d)), gate_w(0), gate_w(1), gate_w(2), gate_w(3),
                  zcol(CB_HG_G), zcol(CB_RT_G), zcol(CB_S5)]
                 + [row(w)] * 6 + [srow, srow, row(w)]
                 + [full((SUBLANES, w)), full((w, w)), full((w, w)), full((N_BRANCH, w, d)), full((d, d)),
                    full((d, LANES)), full((1, LANES))],
        out_specs=[row(d), row(d), row(LANES), full((SUBLANES, LANES))],
        out_shape=[jax.ShapeDtypeStruct((m, d), F32), jax.ShapeDtypeStruct((m, d), F32),
                   jax.ShapeDtypeStruct((m, LANES), F32), jax.ShapeDtypeStruct((SUBLANES, LANES), F32)],
        scratch_shapes=[pltpu.VMEM((1, LANES), F32)],
        compiler_params=_cparams(("arbitrary",)),
        name="merge",
    )(x_all, mod3, n1, n2, w_in_bf, w_in_bf, w_in_bf, w_in_bf, z, z, z,
      hgf, hgb, rtf, rtb, s5f, s5b, yconv, vx, x0c,
      vecs, glu1, glu2, w_branch, w_out, w_router, b_router)


DMA_UNROLL = 8


def _row_copy(src_ref, src_row, dst_ref, dst_row, sem):
    return pltpu.make_async_copy(src_ref.at[pl.ds(src_row, 1), :], dst_ref.at[pl.ds(dst_row, 1), :], sem)


def _dispatch_kernel(dest_ref, h_ref, zero_ref, buf_ref, sem):
    del zero_ref
    base = pl.program_id(0) * TM

    def issue(r, carry):
        for k in range(2):
            _row_copy(h_ref, r, buf_ref, dest_ref[(base + r) * 2 + k], sem).start(priority=k)
        return carry

    lax.fori_loop(0, TM, issue, 0, unroll=DMA_UNROLL)
    for k in range(2):
        pltpu.make_async_copy(h_ref, buf_ref.at[pl.ds(0, TM), :], sem).wait()


def moe_dispatch(dest, h2, n_rows):
    m, d = h2.shape
    grid_spec = pltpu.PrefetchScalarGridSpec(
        num_scalar_prefetch=1,
        grid=(m // TM,),
        in_specs=[pl.BlockSpec((TM, d), lambda i, dest: (i, 0)),
                  pl.BlockSpec(memory_space=pl.ANY)],
        out_specs=pl.BlockSpec(memory_space=pl.ANY),
        scratch_shapes=[pltpu.SemaphoreType.DMA(())],
    )
    return pl.pallas_call(
        _dispatch_kernel,
        grid_spec=grid_spec,
        out_shape=jax.ShapeDtypeStruct((n_rows, d), F32),
        input_output_aliases={2: 0},
        compiler_params=_cparams(("arbitrary",)),
        name="moe_dispatch",
    )(dest, h2, jnp.zeros((n_rows, d), F32))


def _expert_kernel(plan_ref, x_ref, w1_ref, w3_ref, w2_ref, y_ref, *, n_blocks):
    used = pl.program_id(0) < plan_ref[n_blocks]

    @pl.when(used)
    def _():
        xb = x_ref[...].astype(BF16)
        a = jnp.dot(xb, w1_ref[...].astype(BF16), preferred_element_type=F32)
        g = jnp.dot(xb, w3_ref[...].astype(BF16), preferred_element_type=F32)
        y_ref[...] = jnp.dot((_silu(a) * g).astype(BF16), w2_ref[...].astype(BF16),
                             preferred_element_type=F32)

    @pl.when(jnp.logical_not(used))
    def _():
        y_ref[...] = jnp.zeros_like(y_ref)


def moe_experts(plan, buf, w1, w3, w2, layer):
    n_rows, d = buf.shape
    hid = w1.shape[-1]
    n_blocks = n_rows // MOE_BLOCK
    blk = lambda j, plan: jnp.minimum(j, plan[n_blocks] - 1)
    grid_spec = pltpu.PrefetchScalarGridSpec(
        num_scalar_prefetch=1,
        grid=(n_blocks,),
        in_specs=[pl.BlockSpec((MOE_BLOCK, d), lambda j, plan: (blk(j, plan), 0)),
                  pl.BlockSpec((None, None, d, hid), lambda j, plan: (layer, plan[blk(j, plan)], 0, 0)),
                  pl.BlockSpec((None, None, d, hid), lambda j, plan: (layer, plan[blk(j, plan)], 0, 0)),
                  pl.BlockSpec((None, None, hid, d), lambda j, plan: (layer, plan[blk(j, plan)], 0, 0))],
        out_specs=pl.BlockSpec((MOE_BLOCK, d), lambda j, plan: (j, 0)),
    )
    return pl.pallas_call(
        functools.partial(_expert_kernel, n_blocks=n_blocks),
        grid_spec=grid_spec,
        out_shape=jax.ShapeDtypeStruct((n_rows, d), F32),
        compiler_params=_cparams(("arbitrary",)),
        name="moe_experts",
    )(plan, buf, w1, w3, w2)


def _combine_kernel(dest_ref, x_ref, route_ref, mod_ref, gfin_ref, y_ref, o_ref, gath, sem, *,
                    d, bsz, tpb, nr, final):
    step = pl.program_id(0) * nr + pl.program_id(1)
    slot = step % 2

    def start_gather(s, sl):
        base = ((s // nr) * tpb + s % nr) * TM

        def issue(r, carry):
            for k in range(2):
                _row_copy(y_ref, dest_ref[(base + r) * 2 + k], gath.at[sl, k], r, sem.at[sl]).start(priority=k)
            return carry

        lax.fori_loop(0, TM, issue, 0, unroll=DMA_UNROLL)

    @pl.when(step == 0)
    def _():
        start_gather(0, 0)

    @pl.when(step + 1 < bsz * nr)
    def _():
        start_gather(step + 1, 1 - slot)

    for k in range(2):
        pltpu.make_async_copy(y_ref.at[pl.ds(0, TM), :], gath.at[slot, k], sem.at[slot]).wait()
    route = route_ref[...]
    lane = lax.broadcasted_iota(jnp.int32, (TM, LANES), 1)
    moe = (_lane_pick(route, lane, ROUTE_W1) * gath[slot, 0]
           + _lane_pick(route, lane, ROUTE_W2) * gath[slot, 1])
    x_new = x_ref[...] + mod_ref[0][:, 5 * d:6 * d] * moe
    if final:
        ms = jnp.mean(x_new * x_new, axis=-1, keepdims=True)
        x_new = x_new * lax.rsqrt(ms + EPS) * gfin_ref[...]
    o_ref[...] = x_new


def moe_combine(dest, x_mid, route, mod3, gain_final, ybuf, bsz, tpb, final):
    m, d = x_mid.shape
    nr = tpb - 1 if final else tpb
    tile = lambda b, r: b * tpb + r
    grid_spec = pltpu.PrefetchScalarGridSpec(
        num_scalar_prefetch=1,
        grid=(bsz, nr),
        in_specs=[pl.BlockSpec((TM, d), lambda b, r, dest: (tile(b, r), 0)),
                  pl.BlockSpec((TM, LANES), lambda b, r, dest: (tile(b, r), 0)),
                  pl.BlockSpec((1, 1, 6 * d), lambda b, r, dest: (_seg_of_tile(tile(b, r), tpb), 0, 0)),
                  pl.BlockSpec((1, d), lambda b, r, dest: (0, 0)),
                  pl.BlockSpec(memory_space=pl.ANY)],
        out_specs=pl.BlockSpec((TM, d), lambda b, r, dest: (b * nr + r, 0)),
        scratch_shapes=[pltpu.VMEM((2, 2, TM, d), F32), pltpu.SemaphoreType.DMA((2,))],
    )
    return pl.pallas_call(
        functools.partial(_combine_kernel, d=d, bsz=bsz, tpb=tpb, nr=nr, final=final),
        grid_spec=grid_spec,
        out_shape=jax.ShapeDtypeStruct((bsz * nr * TM, d), F32),
        compiler_params=_cparams(("arbitrary", "arbitrary")),
        name="moe_combine",
    )(dest, x_mid, route, mod3, gain_final, ybuf)


def moe_plan(route, counts_row, n_blocks):
    counts = counts_row[:MOE_EXPERTS].astype(jnp.int32)
    padded = (counts + MOE_BLOCK - 1) // MOE_BLOCK * MOE_BLOCK
    pad_end = jnp.cumsum(padded)
    pad_start = pad_end - padded
    expert = route[:, ROUTE_E1:ROUTE_E2 + 1].astype(jnp.int32)
    rank = route[:, ROUTE_R1:ROUTE_R2 + 1].astype(jnp.int32)
    is_e = expert[..., None] == jnp.arange(MOE_EXPERTS, dtype=jnp.int32)
    dest = (jnp.sum(jnp.where(is_e, pad_start, 0), axis=-1) + rank).reshape(-1)
    block_row0 = jnp.arange(n_blocks, dtype=jnp.int32) * MOE_BLOCK
    block_expert = jnp.minimum(jnp.sum((pad_end[None, :] <= block_row0[:, None]).astype(jnp.int32), axis=1),
                               MOE_EXPERTS - 1)
    n_used = pad_end[MOE_EXPERTS - 1:] // MOE_BLOCK
    return dest, jnp.concatenate([block_expert, n_used])


def _rope_tables(length, lc):
    half = RET_DK // 2
    freqs = ROPE_BASE ** (-jnp.arange(0, half, 2, dtype=F32) / half)
    t = jnp.arange(length)
    row_pos = (t // GRID_W).astype(F32)[:, None]
    col_pos = (t % GRID_W).astype(F32)[:, None]
    lane = jnp.arange(RET_HEADS * RET_DK)
    fr = freqs[lane % (half // 2)][None, :]
    ang = jnp.where(((lane % RET_DK) < half)[None, :], row_pos, col_pos) * fr
    cos = jnp.concatenate([jnp.cos(ang), jnp.ones((lc, lane.shape[0]), F32)], axis=0)
    sin = jnp.concatenate([jnp.sin(ang), jnp.zeros((lc, lane.shape[0]), F32)], axis=0)
    return cos, sin


def _rotate_half_cols(wq):
    quarter = RET_DK // 4
    lane = np.arange(wq.shape[1])
    first = (lane % (2 * quarter)) < quarter
    src = np.where(first, lane + quarter, lane - quarter)
    sign = jnp.asarray(np.where(first, -1.0, 1.0), F32)
    return wq[:, src] * sign


def kernel(x, c, ctx, c_ctx, w_mod, b_mod, norm1, norm2, w_in, hg_lb_logits, hg_norm, hy_conv_w, hy_conv_b, hy_filt_w1, hy_filt_b1, hy_filt_freq1, hy_filt_w2, hy_filt_b2, hy_filt_freq2, hy_filt_w3, hy_skip, ret_decay_logit, ret_norm, s5_lam_re, s5_lam_im, s5_log_dt, s5_b_re, s5_b_im, s5_c_re, s5_c_im, s5_d, s5_glu_w1, s5_glu_w2, w_branch, w_out, moe_w_grp, moe_b_grp, moe_w_exp, moe_b_exp, moe_w1, moe_w3, moe_w2, norm_final):
    bsz, length, d = x.shape
    lc = ctx.shape[1]
    depth = w_mod.shape[0]
    assert lc == TM and length % TM == 0 and (2 * length) % (FFT_FAST * FFT_GROUP) == 0
    tt = length + lc
    tpb = tt // TM
    m = bsz * tt
    n_blocks = -(-(m * 2) // MOE_BLOCK) + MOE_EXPERTS
    w = BRANCH_W

    x_all = jnp.concatenate([x, ctx], axis=1).reshape(m, d)
    cmat = jnp.zeros((SUBLANES, d), F32).at[0].set(c_ctx).at[1:1 + bsz].set(c)
    mods = modvec(cmat, w_mod, b_mod)
    lb_sm = jax.nn.softmax(hg_lb_logits.astype(F32), axis=0)
    lb_all = jnp.cumsum(lb_sm, axis=0) - lb_sm[0]
    cos_tab, sin_tab = _rope_tables(length, lc)
    log_gamma = jax.nn.log_sigmoid(ret_decay_logit.astype(F32))
    gfin = norm_final.reshape(1, d)

    w_in_bf = w_in.astype(BF16)
    rq0 = 5 * w + 3 * HY_W
    nq = RET_HEADS * RET_DK

    for l in range(depth):
        w_rot = jnp.concatenate([_rotate_half_cols(w_in[l, :, rq0:rq0 + nq]),
                                 _rotate_half_cols(w_in[l, :, rq0 + nq:rq0 + 2 * nq])], axis=1).astype(BF16)
        mod3 = mods[l].reshape(SUBLANES, 1, 6 * d)

        z = inproj(x_all, mod3, norm1[l].reshape(1, d), w_in_bf, l, w_rot, tpb)
        hgf, hgb = hgrn(z, lb_all[l], bsz, tpb)
        rtf, rtb = retention(z, cos_tab, sin_tab, log_gamma[l], bsz, tpb)
        s5f, s5b = s5(z, *s5_params(s5_lam_re[l], s5_lam_im[l], s5_log_dt[l], s5_b_re[l], s5_b_im[l],
                                    s5_c_re[l], s5_c_im[l]), bsz, tpb)

        x0c, vx = hy_prep(z, hy_conv_w[l], hy_conv_b[l], bsz, tpb)
        fargs = (hy_filt_w1[l], hy_filt_b1[l], hy_filt_freq1[l], hy_filt_w2[l], hy_filt_b2[l],
                 hy_filt_freq2[l], hy_filt_w3[l])
        hfb, ss = hy_filter(length, *fargs)
        y_lat = hyena_long_conv(vx, bsz, length, hfb, lax.rsqrt(ss + EPS))
        hfbc, ssc = hy_filter(lc, *fargs)
        gwin = jnp.concatenate([hfbc[1, 1:][::-1], hfbc[0], jnp.zeros((1, HY_W), F32)], axis=0) \
            * lax.rsqrt(ssc + EPS)
        yconv = hy_ctx_conv(vx, gwin, y_lat, bsz, tpb, lc)

        vecs = jnp.zeros((SUBLANES, w), F32).at[0].set(hg_norm[l]).at[1].set(ret_norm[l]) \
            .at[2].set(s5_d[l]).at[3].set(hy_skip[l])
        w_router = jnp.zeros((d, LANES), F32).at[:, :MOE_GROUPS].set(moe_w_grp[l]) \
            .at[:, MOE_GROUPS:MOE_GROUPS + MOE_EXPERTS].set(moe_w_exp[l])
        b_router = jnp.zeros((1, LANES), F32).at[0, :MOE_GROUPS].set(moe_b_grp[l]) \
            .at[0, MOE_GROUPS:MOE_GROUPS + MOE_EXPERTS].set(moe_b_exp[l])
        x_mid, h2, route, counts = merge(
            x_all, mod3, norm1[l].reshape(1, d), norm2[l].reshape(1, d), w_in_bf, l, z,
            hgf, hgb, rtf, rtb, s5f, s5b, yconv, vx, x0c, vecs,
            s5_glu_w1[l].astype(BF16), s5_glu_w2[l].astype(BF16), w_branch[l].astype(BF16),
            w_out[l].astype(BF16), w_router, b_router, bsz, tpb)

        dest, block_expert = moe_plan(route, counts[0], n_blocks)
        buf = moe_dispatch(dest, h2, n_blocks * MOE_BLOCK)
        ybuf = moe_experts(block_expert, buf, moe_w1, moe_w3, moe_w2, l)
        x_all = moe_combine(dest, x_mid, route, mod3, gfin, ybuf, bsz, tpb, l == depth - 1)

    return x_all.reshape(bsz, length, d)
```

```python
import functools
import math

import numpy as np
import jax
import jax.numpy as jnp
from jax import lax
from jax.experimental import pallas as pl
from jax.experimental.pallas import tpu as pltpu

F32 = jnp.float32
BF16 = jnp.bfloat16
HIGHEST = lax.Precision.HIGHEST

GRID_W = 64
N_BRANCH = 4
BRANCH_W = 256
HG_HEADS = 4
HG_CHUNK = 16
HY_W = 256
HY_EMB = 33
HY_BANDS = 16
HY_ORDER = 64
HY_FAST_DECAY = 0.3
HY_SLOW_DECAY = 1.5
HY_TARGET = 1e-2
RET_HEADS = 4
RET_DK = 32
RET_DV = 64
ROPE_BASE = 10000.0
S5_GROUP_CH = 16
S5_GROUPS = 16
S5_STATE = 64
MOE_GROUPS = 4
MOE_PER_GROUP = 8
MOE_EXPERTS = 32
MOE_HIDDEN = 512
MOE_BLOCK = 256
EPS = 1e-6

LANES = 128
SUBLANES = 8
TM = 256
FFT_FAST = 128
VMEM_LIMIT = 56 * 1024 * 1024

N_MIX = 3072
N_ZCOLS = N_MIX + 2 * RET_HEADS * RET_DK
CB_HG_Q, CB_HG_FF, CB_HG_FB, CB_HG_I, CB_HG_G = 3, 4, 5, 6, 7
CB_RT_V, CB_RT_G, CB_S5 = 9, 10, 11
CB128_RT_Q, CB128_RT_K, CB128_RT_QR, CB128_RT_KR = 16, 17, 24, 25


def _cparams(sem):
    return pltpu.CompilerParams(dimension_semantics=sem, vmem_limit_bytes=VMEM_LIMIT)


def _sigmoid(x):
    return 1.0 / (1.0 + jnp.exp(-x))


def _silu(x):
    return x * _sigmoid(x)


def _dot(a, b):
    return jnp.dot(a.astype(BF16), b.astype(BF16), preferred_element_type=F32)


def _dot_nt(a, b):
    return lax.dot_general(a.astype(BF16), b.astype(BF16), (((1,), (1,)), ((), ())),
                           preferred_element_type=F32)


def _dot_hi(a, b):
    return jnp.dot(a, b, precision=HIGHEST, preferred_element_type=F32)


def _dot3(a, b):
    a_hi = a.astype(BF16)
    a_lo = (a - a_hi.astype(F32)).astype(BF16)
    b_hi = b.astype(BF16)
    b_lo = (b - b_hi.astype(F32)).astype(BF16)
    return (jnp.dot(a_hi, b_hi, preferred_element_type=F32)
            + jnp.dot(a_lo, b_hi, preferred_element_type=F32)
            + jnp.dot(a_hi, b_lo, preferred_element_type=F32))


def _split_dot(x, m_bf16):
    hi = x.astype(BF16)
    lo = (x - hi.astype(F32)).astype(BF16)
    return (jnp.dot(hi, m_bf16, preferred_element_type=F32)
            + jnp.dot(lo, m_bf16, preferred_element_type=F32))


def _split_dot_left(m_bf16, x):
    hi = x.astype(BF16)
    lo = (x - hi.astype(F32)).astype(BF16)
    return (jnp.dot(m_bf16, hi, preferred_element_type=F32)
            + jnp.dot(m_bf16, lo, preferred_element_type=F32))


def _seg_of_tile(i, tpb):
    return jnp.where(i % tpb == tpb - 1, 0, 1 + i // tpb)


def _split_tile(i, bsz, tpb):
    b, r = i // tpb, i % tpb
    return jnp.where(r == tpb - 1, bsz * (tpb - 1) + b, b * (tpb - 1) + r)


def _scan_tiles(b, s, tpb):
    last = tpb - 1
    fwd = jnp.where(s == 0, last, s - 1)
    bwd = jnp.where(s == 0, last, last - s)
    return b * tpb + fwd, b * tpb + bwd


def _modvec_kernel(c_ref, w_ref, b_ref, o_ref):
    c = c_ref[...]
    o_ref[0] = _dot_hi(_silu(c), w_ref[0]) + b_ref[0]


def modvec(cmat, w_mod, b_mod):
    depth, d, n6 = w_mod.shape
    nb = n6 // d
    return pl.pallas_call(
        _modvec_kernel,
        grid=(depth, nb),
        in_specs=[pl.BlockSpec((SUBLANES, d), lambda l, j: (0, 0)),
                  pl.BlockSpec((1, d, d), lambda l, j: (l, 0, j)),
                  pl.BlockSpec((1, 1, d), lambda l, j: (l, 0, j))],
        out_specs=pl.BlockSpec((1, SUBLANES, d), lambda l, j: (l, 0, j)),
        out_shape=jax.ShapeDtypeStruct((depth, SUBLANES, n6), F32),
        compiler_params=_cparams(("arbitrary", "arbitrary")),
        name="modvec",
    )(cmat, w_mod, b_mod.reshape(depth, 1, n6))


def _modulated_norm(x, gain, shift, scale):
    ms = jnp.mean(x * x, axis=-1, keepdims=True)
    return (x * lax.rsqrt(ms + EPS) * gain) * (1.0 + scale) + shift


def _inproj_kernel(x_ref, mod_ref, g_ref, w_ref, wrot_ref, z_ref, *, d):
    h = _modulated_norm(x_ref[...], g_ref[...], mod_ref[0, :, 0:d], mod_ref[0, :, d:2 * d]).astype(BF16)
    hy0 = 5 * BRANCH_W
    hy1 = hy0 + 3 * HY_W
    z_ref[:, 0:hy1 - hy0] = jnp.dot(h, w_ref[:, hy0:hy1], preferred_element_type=F32)
    z_ref[:, hy1 - hy0:hy1] = jnp.dot(h, w_ref[:, 0:hy0], preferred_element_type=F32)
    z_ref[:, hy1:N_MIX] = jnp.dot(h, w_ref[:, hy1:N_MIX], preferred_element_type=F32)
    z_ref[:, N_MIX:N_ZCOLS] = jnp.dot(h, wrot_ref[...], preferred_element_type=F32)


def inproj(x_all, mod3, gain, w_in_bf, layer, w_rot, tpb):
    m, d = x_all.shape
    return pl.pallas_call(
        functools.partial(_inproj_kernel, d=d),
        grid=(m // TM,),
        in_specs=[pl.BlockSpec((TM, d), lambda i: (i, 0)),
                  pl.BlockSpec((1, 1, 6 * d), lambda i: (_seg_of_tile(i, tpb), 0, 0)),
                  pl.BlockSpec((1, d), lambda i: (0, 0)),
                  pl.BlockSpec((None, d, N_MIX), lambda i: (layer, 0, 0)),
                  pl.BlockSpec((d, N_ZCOLS - N_MIX), lambda i: (0, 0))],
        out_specs=pl.BlockSpec((TM, N_ZCOLS), lambda i: (i, 0)),
        out_shape=jax.ShapeDtypeStruct((m, N_ZCOLS), F32),
        compiler_params=_cparams(("parallel",)),
        name="inproj",
    )(x_all, mod3, gain, w_in_bf, w_rot)


def _head_masked_rows(x, nchunk, c, hd):
    w = x.shape[-1]
    x3 = x.reshape(nchunk, c, w)
    lane_head = lax.broadcasted_iota(jnp.int32, (1, 1, w), 2) // hd
    parts = [jnp.where(lane_head == h, x3, 0.0) for h in range(w // hd)]
    return jnp.concatenate(parts, axis=1).reshape(nchunk * (w // hd) * c, w)


def _hgrn_direction(q_ref, f_ref, v_ref, lb, tri_ref, o_ref, st_ref, b_s, k_s, qe_s, ke_s, reverse):
    c = HG_CHUNK
    w = BRANCH_W
    nchunk = TM // c
    nh = HG_HEADS
    hd = w // nh
    half = c // 2
    z = f_ref[...]
    sg = _sigmoid(z)
    log2_f = jnp.log2(lb + (1.0 - lb) * sg)
    k = (1.0 - lb) * (1.0 - sg)
    b2 = _split_dot_left(tri_ref[0], log2_f)
    rem2 = _split_dot_left(tri_ref[1], log2_f)
    b_s[...] = b2
    k_s[...] = k
    qe_s[...] = _head_masked_rows(q_ref[...] * jnp.exp2(b2), nchunk, c, hd).astype(BF16)
    ke_s[...] = _head_masked_rows(k * jnp.exp2(rem2), nchunk, c, hd).astype(BF16)

    head_r = lax.broadcasted_iota(jnp.int32, (w, w), 0) // hd
    head_c = lax.broadcasted_iota(jnp.int32, (w, w), 1) // hd
    hm = jnp.where(head_r == head_c, 1.0, 0.0).astype(BF16)
    rows = lax.broadcasted_iota(jnp.int32, (half, 1), 0)

    pieces = []
    for s in range(c):
        for g in range(2):
            lo, hi = g * half, (g + 1) * half - 1
            if (hi < s) if not reverse else (lo > s):
                continue
            full = (lo >= s) if not reverse else (hi <= s)
            pieces.append((s, g, full))

    def chunk_step(n):
        cidx = (nchunk - 1 - n) if reverse else n
        off = pl.multiple_of(cidx * c, c)
        eoff = pl.multiple_of(cidx * (nh * c), nh * c)
        qc = q_ref[pl.ds(off, c), :]
        bc = b_s[pl.ds(off, c), :]
        kc = k_s[pl.ds(off, c), :]
        vc = v_ref[pl.ds(off, c), :]
        parts = []
        for s, g, full in pieces:
            sl = slice(g * half, (g + 1) * half)
            e = jnp.exp2(bc[sl] - bc[s:s + 1, :])
            if not full:
                t = rows + g * half
                e = jnp.where((t <= s) if reverse else (t >= s), e, 0.0)
            parts.append((qc[sl] * kc[s:s + 1, :]) * e)
        p = jnp.concatenate(parts, axis=0).astype(BF16)
        r = jnp.dot(p, hm, preferred_element_type=F32)
        og = [jnp.zeros((half, w), F32), jnp.zeros((half, w), F32)]
        for i, (s, g, _) in enumerate(pieces):
            og[g] = og[g] + r[i * half:(i + 1) * half, :] * vc[s:s + 1, :]
        st = st_ref[...]
        ri = _dot_nt(qe_s[pl.ds(eoff, nh * c), :], st)
        inter = jnp.concatenate([ri[h * c:(h + 1) * c, :] for h in range(nh)], axis=1)
        o_ref[pl.ds(off, c), :] = jnp.concatenate(og, axis=0) + inter
        vexp = jnp.concatenate([vc[:, h * hd:(h + 1) * hd] for h in range(nh)], axis=0)
        kv = lax.dot_general(vexp.astype(BF16), ke_s[pl.ds(eoff, nh * c), :], (((0,), (0,)), ((), ())),
                             preferred_element_type=F32)
        blast = bc[0:1, :] if reverse else bc[c - 1:c, :]
        st_ref[...] = jnp.exp2(blast) * st + kv

    return chunk_step


def _hgrn_kernel(qf_ref, ff_ref, vf_ref, qb_ref, fb_ref, vb_ref, lb_ref, trif_ref, trib_ref, of_ref, ob_ref,
                 stf_ref, stb_ref, bf_s, kf_s, qef_s, kef_s, bb_s, kb_s, qeb_s, keb_s):
    @pl.when(pl.program_id(1) == 0)
    def _():
        stf_ref[...] = jnp.zeros_like(stf_ref)
        stb_ref[...] = jnp.zeros_like(stb_ref)

    step_f = _hgrn_direction(qf_ref, ff_ref, vf_ref, lb_ref[0:1, :], trif_ref, of_ref, stf_ref,
                             bf_s, kf_s, qef_s, kef_s, False)
    step_b = _hgrn_direction(qb_ref, fb_ref, vb_ref, lb_ref[1:2, :], trib_ref, ob_ref, stb_ref,
                             bb_s, kb_s, qeb_s, keb_s, True)

    def body(n, carry):
        step_f(n)
        step_b(n)
        return carry

    lax.fori_loop(0, TM // HG_CHUNK, body, 0, unroll=8)


def hgrn(z, lb, bsz, tpb):
    m = z.shape[0]
    w = BRANCH_W
    hd = w // HG_HEADS
    ri = jnp.arange(TM)[:, None]
    ci = jnp.arange(TM)[None, :]
    same = (ri // HG_CHUNK) == (ci // HG_CHUNK)
    tri = jnp.stack([same & (ci <= ri), same & (ci > ri), same & (ci >= ri), same & (ci < ri)])
    tri = tri.astype(BF16).reshape(2, 2, TM, TM)

    def spec(cb, d):
        return pl.BlockSpec((TM, w), lambda b, s: (_scan_tiles(b, s, tpb)[d], cb))

    out_spec = [pl.BlockSpec((TM, w), lambda b, s: (_scan_tiles(b, s, tpb)[0], 0)),
                pl.BlockSpec((TM, w), lambda b, s: (_scan_tiles(b, s, tpb)[1], 0))]
    return pl.pallas_call(
        _hgrn_kernel,
        grid=(bsz, tpb),
        in_specs=[spec(CB_HG_Q, 0), spec(CB_HG_FF, 0), spec(CB_HG_I, 0),
                  spec(CB_HG_Q, 1), spec(CB_HG_FB, 1), spec(CB_HG_I, 1),
                  pl.BlockSpec((2, w), lambda b, s: (0, 0)),
                  pl.BlockSpec((None, 2, TM, TM), lambda b, s: (0, 0, 0, 0)),
                  pl.BlockSpec((None, 2, TM, TM), lambda b, s: (1, 0, 0, 0))],
        out_specs=out_spec,
        out_shape=[jax.ShapeDtypeStruct((m, w), F32)] * 2,
        scratch_shapes=[pltpu.VMEM((hd, w), F32), pltpu.VMEM((hd, w), F32)]
                       + [pltpu.VMEM((TM, w), F32), pltpu.VMEM((TM, w), F32),
                          pltpu.VMEM((HG_HEADS * TM, w), BF16), pltpu.VMEM((HG_HEADS * TM, w), BF16)] * 2,
        compiler_params=_cparams(("arbitrary", "arbitrary")),
        name="hgrn2",
    )(z, z, z, z, z, z, lb, tri, tri)


def _ret_direction(q_ref, k_ref, qr_ref, kr_ref, v_ref, cos_ref, sin_ref, lgq, lg_ref, d, o_ref, st_ref,
                   reverse):
    nq = RET_HEADS * RET_DK
    nv = RET_HEADS * RET_DV
    cos = cos_ref[...]
    sin = sin_ref[...]
    qx = q_ref[...] * cos + qr_ref[...] * sin
    kx = (k_ref[...] * cos + kr_ref[...] * sin) * (RET_DK ** -0.5)
    v = v_ref[...]
    t = lax.broadcasted_iota(jnp.int32, (TM, 1), 0).astype(F32)
    pos = (TM - 1.0 - t) if reverse else t
    qd = qx * jnp.exp((pos + 1.0) * lgq)
    kd = kx * jnp.exp((TM - 1.0 - pos) * lgq)
    ti = lax.broadcasted_iota(jnp.int32, (TM, TM), 0)
    si = lax.broadcasted_iota(jnp.int32, (TM, TM), 1)
    rel = ((si - ti) if reverse else (ti - si)).astype(F32)
    live = rel >= 0.0
    relc = jnp.where(live, rel, 0.0)
    lane_q = lax.broadcasted_iota(jnp.int32, (1, nq), 1) // RET_DK
    lane_v = lax.broadcasted_iota(jnp.int32, (1, nv), 1) // RET_DV
    st = st_ref[...]
    o = _dot_nt(qd, st)
    vb = v.astype(BF16)
    for h in range(RET_HEADS):
        dmat = jnp.where(live, jnp.exp(relc * lg_ref[d, h]), 0.0)
        sc = _dot_nt(jnp.where(lane_q == h, qx, 0.0), kx) * dmat
        oh = jnp.dot(sc.astype(BF16), vb, preferred_element_type=F32)
        o = o + jnp.where(lane_v == h, oh, 0.0)
    o_ref[...] = o
    kv = _dot(v.T, kd)
    hr = lax.broadcasted_iota(jnp.int32, (nv, nq), 0) // RET_DV
    hc = lax.broadcasted_iota(jnp.int32, (nv, nq), 1) // RET_DK
    st_ref[...] = st * jnp.exp(float(TM) * lgq) + jnp.where(hr == hc, kv, 0.0)


def _ret_kernel(lg_ref, qf, kf, qrf, krf, vf, cosf, sinf, qb, kb, qrb, krb, vb, cosb, sinb, lgq_ref,
                of_ref, ob_ref, stf_ref, stb_ref):
    @pl.when(pl.program_id(1) == 0)
    def _():
        stf_ref[...] = jnp.zeros_like(stf_ref)
        stb_ref[...] = jnp.zeros_like(stb_ref)

    _ret_direction(qf, kf, qrf, krf, vf, cosf, sinf, lgq_ref[0:1, :], lg_ref, 0, of_ref, stf_ref, False)
    _ret_direction(qb, kb, qrb, krb, vb, cosb, sinb, lgq_ref[1:2, :], lg_ref, 1, ob_ref, stb_ref, True)


def retention(z, cos_tab, sin_tab, log_gamma, bsz, tpb):
    m = z.shape[0]
    nq = RET_HEADS * RET_DK
    nv = RET_HEADS * RET_DV
    lgq = jnp.repeat(log_gamma, RET_DK, axis=1)

    def zspec(cb, width, d):
        return pl.BlockSpec((TM, width), lambda b, s: (_scan_tiles(b, s, tpb)[d], cb))

    def tspec(d):
        return pl.BlockSpec((TM, nq), lambda b, s: (_scan_tiles(0, s, tpb)[d], 0))

    def side(d):
        return [zspec(CB128_RT_Q, nq, d), zspec(CB128_RT_K, nq, d), zspec(CB128_RT_QR, nq, d),
                zspec(CB128_RT_KR, nq, d), zspec(CB_RT_V, nv, d), tspec(d), tspec(d)]

    return pl.pallas_call(
        _ret_kernel,
        grid=(bsz, tpb),
        in_specs=[pl.BlockSpec(memory_space=pltpu.SMEM)] + side(0) + side(1)
                 + [pl.BlockSpec((2, nq), lambda b, s: (0, 0))],
        out_specs=[pl.BlockSpec((TM, nv), lambda b, s: (_scan_tiles(b, s, tpb)[0], 0)),
                   pl.BlockSpec((TM, nv), lambda b, s: (_scan_tiles(b, s, tpb)[1], 0))],
        out_shape=[jax.ShapeDtypeStruct((m, nv), F32)] * 2,
        scratch_shapes=[pltpu.VMEM((nv, nq), F32), pltpu.VMEM((nv, nq), F32)],
        compiler_params=_cparams(("arbitrary", "arbitrary")),
        name="retention",
    )(log_gamma, z, z, z, z, z, cos_tab, sin_tab, z, z, z, z, z, cos_tab, sin_tab, lgq)


S5_NP = S5_GROUPS * S5_STATE


def _s5_direction(u_ref, bre_ref, bim_ref, cre_ref, cim_ref, tab_ref, o_ref, car_ref, xr_s, xi_s, reverse):
    nb = TM // SUBLANES
    ub = u_ref[...].astype(BF16)
    xr = jnp.dot(ub, bre_ref[0], preferred_element_type=F32).reshape(nb, SUBLANES, S5_NP)
    xi = jnp.dot(ub, bim_ref[0], preferred_element_type=F32).reshape(nb, SUBLANES, S5_NP)
    sub = lax.broadcasted_iota(jnp.int32, (1, SUBLANES, 1), 1)
    for k, row in ((1, 8), (2, 9), (4, 10)):
        shift = (SUBLANES - k) if reverse else k
        keep = (sub < SUBLANES - k) if reverse else (sub >= k)
        ar = jnp.where(keep, tab_ref[0, 0, row:row + 1, :], 0.0)
        ai = jnp.where(keep, tab_ref[0, 1, row:row + 1, :], 0.0)
        sr = pltpu.roll(xr, shift, 1)
        si = pltpu.roll(xi, shift, 1)
        xr, xi = xr + ar * sr - ai * si, xi + ar * si + ai * sr
    xr_s[...] = xr.reshape(TM, S5_NP)
    xi_s[...] = xi.reshape(TM, S5_NP)
    pr = tab_ref[0, 0, 0:SUBLANES, :]
    pi = tab_ref[0, 1, 0:SUBLANES, :]

    def body(n, carry):
        cr, ci = carry
        blk = (nb - 1 - n) if reverse else n
        off = pl.multiple_of(blk * SUBLANES, SUBLANES)
        br = xr_s[pl.ds(off, SUBLANES), :] + pr * cr - pi * ci
        bi = xi_s[pl.ds(off, SUBLANES), :] + pr * ci + pi * cr
        xr_s[pl.ds(off, SUBLANES), :] = br
        xi_s[pl.ds(off, SUBLANES), :] = bi
        if reverse:
            return br[0:1, :], bi[0:1, :]
        return br[SUBLANES - 1:SUBLANES, :], bi[SUBLANES - 1:SUBLANES, :]

    cr, ci = lax.fori_loop(0, nb, body, (car_ref[0:1, :], car_ref[1:2, :]))
    car_ref[0:1, :] = cr
    car_ref[1:2, :] = ci
    o_ref[...] = (jnp.dot(xr_s[...].astype(BF16), cre_ref[0], preferred_element_type=F32)
                  - jnp.dot(xi_s[...].astype(BF16), cim_ref[0], preferred_element_type=F32))


def _s5_kernel(uf, ub, bre, bim, cre, cim, tabf, tabb, bre2, bim2, cre2, cim2, of_ref, ob_ref,
               carf, carb, xr_s, xi_s):
    @pl.when(pl.program_id(1) == 0)
    def _():
        carf[...] = jnp.zeros_like(carf)
        carb[...] = jnp.zeros_like(carb)

    _s5_direction(uf, bre, bim, cre, cim, tabf, of_ref, carf, xr_s, xi_s, False)
    _s5_direction(ub, bre2, bim2, cre2, cim2, tabb, ob_ref, carb, xr_s, xi_s, True)


def s5(z, bblk_re, bblk_im, cblk_re, cblk_im, tab, bsz, tpb):
    m = z.shape[0]
    w = BRANCH_W

    def pspec(shape, d):
        return pl.BlockSpec((1,) + shape, lambda b, s: (d,) + (0,) * len(shape))

    def uspec(d):
        return pl.BlockSpec((TM, w), lambda b, s: (_scan_tiles(b, s, tpb)[d], CB_S5))

    return pl.pallas_call(
        _s5_kernel,
        grid=(bsz, tpb),
        in_specs=[uspec(0), uspec(1),
                  pspec((w, S5_NP), 0), pspec((w, S5_NP), 0), pspec((S5_NP, w), 0), pspec((S5_NP, w), 0),
                  pspec((2, 16, S5_NP), 0), pspec((2, 16, S5_NP), 1),
                  pspec((w, S5_NP), 1), pspec((w, S5_NP), 1), pspec((S5_NP, w), 1), pspec((S5_NP, w), 1)],
        out_specs=[pl.BlockSpec((TM, w), lambda b, s: (_scan_tiles(b, s, tpb)[0], 0)),
                   pl.BlockSpec((TM, w), lambda b, s: (_scan_tiles(b, s, tpb)[1], 0))],
        out_shape=[jax.ShapeDtypeStruct((m, w), F32)] * 2,
        scratch_shapes=[pltpu.VMEM((2, S5_NP), F32), pltpu.VMEM((2, S5_NP), F32),
                        pltpu.VMEM((TM, S5_NP), F32), pltpu.VMEM((TM, S5_NP), F32)],
        compiler_params=_cparams(("arbitrary", "arbitrary")),
        name="s5",
    )(z, z, bblk_re, bblk_im, cblk_re, cblk_im, tab, tab, bblk_re, bblk_im, cblk_re, cblk_im)


def s5_params(lam_re, lam_im, log_dt, b_re, b_im, c_re, c_im):
    lr = jnp.minimum(lam_re.astype(F32), -1e-4)
    li = lam_im.astype(F32)
    dt = jnp.exp(log_dt.astype(F32))[..., None]

    def apow(k):
        mag = jnp.exp(k * lr * dt)
        return mag * jnp.cos(k * li * dt), mag * jnp.sin(k * li * dt)

    ar, ai = apow(1.0)
    den = lr * lr + li * li
    gr = ((ar - 1.0) * lr + ai * li) / den
    gi = (ai * lr - (ar - 1.0) * li) / den
    bbr = gr[..., None] * b_re - gi[..., None] * b_im
    bbi = gr[..., None] * b_im + gi[..., None] * b_re
    eye = jnp.eye(S5_GROUPS, dtype=F32)

    def blk_in(bb):
        t = jnp.einsum('dgpc,gh->dgchp', bb, eye)
        return t.reshape(2, S5_GROUPS * S5_GROUP_CH, S5_NP).astype(BF16)

    def blk_out(cc):
        t = jnp.einsum('dgcp,gh->dgphc', cc.astype(F32), eye)
        return t.reshape(2, S5_NP, S5_GROUPS * S5_GROUP_CH).astype(BF16)

    rows_r, rows_i = [], []
    for j in range(SUBLANES):
        pr_f, pi_f = apow(float(j + 1))
        pr_b, pi_b = apow(float(SUBLANES - j))
        rows_r.append(jnp.stack([pr_f[0], pr_b[1]]))
        rows_i.append(jnp.stack([pi_f[0], pi_b[1]]))
    for k in (1.0, 2.0, 4.0):
        pr_k, pi_k = apow(k)
        rows_r.append(pr_k)
        rows_i.append(pi_k)
    zero = jnp.zeros_like(rows_r[0])
    while len(rows_r) < 16:
        rows_r.append(zero)
        rows_i.append(zero)
    tab_r = jnp.stack(rows_r, axis=1).reshape(2, 16, S5_NP)
    tab_i = jnp.stack(rows_i, axis=1).reshape(2, 16, S5_NP)
    tab = jnp.stack([tab_r, tab_i], axis=1)
    return blk_in(bbr), blk_in(bbi), blk_out(c_re), blk_out(c_im), tab


def _hy_prep_kernel(u_ref, prev_ref, next_ref, w_ref, b_ref, x0_ref, vx_ref, *, tpb):
    i = pl.program_id(0)
    r = i % tpb
    first = (r == 0) | (r == tpb - 1)
    last = r >= tpb - 2
    u = u_ref[...]
    w = w_ref[...]
    up_row = jnp.where(first, 0.0, prev_ref[SUBLANES - 1:SUBLANES, :])
    dn_row = jnp.where(last, 0.0, next_ref[0:1, :])
    row = lax.broadcasted_iota(jnp.int32, (TM, 1), 0)
    um = jnp.where(row == 0, up_row, pltpu.roll(u, 1, 0))
    un = jnp.where(row == TM - 1, dn_row, pltpu.roll(u, TM - 1, 0))
    y = um * w[0:1, :] + u * w[1:2, :] + un * w[2:3, :] + b_ref[...]
    x0_ref[...] = y[:, 0:HY_W]
    vx_ref[...] = y[:, 2 * HY_W:3 * HY_W] * y[:, HY_W:2 * HY_W]


def hy_prep(z, conv_w, conv_b, bsz, tpb):
    m = z.shape[0]
    w3 = 3 * HY_W
    rb = TM // SUBLANES
    nrb = m // SUBLANES
    return pl.pallas_call(
        functools.partial(_hy_prep_kernel, tpb=tpb),
        grid=(m // TM,),
        in_specs=[pl.BlockSpec((TM, w3), lambda i: (i, 0)),
                  pl.BlockSpec((SUBLANES, w3), lambda i: (jnp.maximum(i * rb - 1, 0), 0)),
                  pl.BlockSpec((SUBLANES, w3), lambda i: (jnp.minimum((i + 1) * rb, nrb - 1), 0)),
                  pl.BlockSpec((3, w3), lambda i: (0, 0)),
                  pl.BlockSpec((1, w3), lambda i: (0, 0))],
        out_specs=[pl.BlockSpec((TM, HY_W), lambda i: (i, 0)),
                   pl.BlockSpec((TM, HY_W), lambda i: (_split_tile(i, bsz, tpb), 0))],
        out_shape=[jax.ShapeDtypeStruct((m, HY_W), F32)] * 2,
        compiler_params=_cparams(("parallel",)),
        name="hy_prep",
    )(z, z, z, conv_w, conv_b.reshape(1, w3))


def _hy_filter_kernel(zf_ref, w1_ref, b1_ref, f1_ref, w2_ref, b2_ref, f2_ref, w3_ref, dl_ref,
                      hfb_ref, ss_ref):
    i = pl.program_id(0)
    zf = zf_ref[...]
    hid = jnp.sin(f1_ref[...] * (_dot_hi(zf, w1_ref[...]) + b1_ref[...]))
    hid = jnp.sin(f2_ref[...] * (_dot_hi(hid, w2_ref[...]) + b2_ref[...]))
    filt = _dot_hi(hid, w3_ref[...])
    window = jnp.exp(-zf[:, 0:1] * dl_ref[...])
    hf = filt[:, 0:HY_W] * window
    row = lax.broadcasted_iota(jnp.int32, (TM, 1), 0) + i * TM
    hb = jnp.where(row == 0, 0.0, filt[:, HY_W:2 * HY_W] * window)
    hfb_ref[0] = hf
    hfb_ref[1] = hb
    part = jnp.sum(hf * hf + hb * hb, axis=0, keepdims=True)

    @pl.when(i == 0)
    def _():
        ss_ref[...] = jnp.zeros_like(ss_ref)

    ss_ref[...] = ss_ref[...] + part


def hy_filter(length, w1, b1, fr1, w2, b2, fr2, w3):
    t = jnp.linspace(0.0, 1.0, length, dtype=F32)[:, None]
    wv = (2.0 * math.pi / length) * jnp.arange(length, dtype=F32)[:, None]
    f = jnp.linspace(1e-4, HY_BANDS - 1.0, HY_BANDS, dtype=F32)[None, :]
    zfeat = jnp.concatenate([t, jnp.cos(f * wv), -jnp.sin(f * wv)], axis=-1)
    zfeat = jnp.pad(zfeat, ((0, 0), (0, LANES - HY_EMB)))
    w1p = jnp.pad(w1, ((0, LANES - HY_EMB), (0, 0)))
    deltas = jnp.abs(jnp.linspace(math.log(HY_TARGET) / HY_SLOW_DECAY,
                                  math.log(HY_TARGET) / HY_FAST_DECAY, HY_W, dtype=F32))[None, :]
    full = lambda shape: pl.BlockSpec(shape, lambda i: (0,) * len(shape))
    return pl.pallas_call(
        _hy_filter_kernel,
        grid=(length // TM,),
        in_specs=[pl.BlockSpec((TM, LANES), lambda i: (i, 0)),
                  full((LANES, HY_ORDER)), full((1, HY_ORDER)), full((1, HY_ORDER)),
                  full((HY_ORDER, HY_ORDER)), full((1, HY_ORDER)), full((1, HY_ORDER)),
                  full((HY_ORDER, 2 * HY_W)), full((1, HY_W))],
        out_specs=[pl.BlockSpec((2, TM, HY_W), lambda i: (0, i, 0)),
                   pl.BlockSpec((1, HY_W), lambda i: (0, 0))],
        out_shape=[jax.ShapeDtypeStruct((2, length, HY_W), F32), jax.ShapeDtypeStruct((1, HY_W), F32)],
        compiler_params=_cparams(("arbitrary",)),
        name="hy_filter",
    )(zfeat, w1p, b1.reshape(1, -1), fr1.reshape(1, -1), w2, b2.reshape(1, -1), fr2.reshape(1, -1), w3, deltas)


def _dft_consts(length):
    n = 2 * length
    ns = n // FFT_FAST
    a = jnp.arange(ns, dtype=jnp.int32)
    ang = ((a[:, None] * a[None, :]) % ns).astype(F32) * (2.0 * math.pi / ns)
    f_fwd = jnp.concatenate([jnp.cos(ang), -jnp.sin(ang)], axis=0)
    f_inv = jnp.concatenate([jnp.cos(ang), -jnp.sin(ang)], axis=1)
    b = jnp.arange(FFT_FAST, dtype=jnp.int32)
    th1 = (a[:, None] * b[None, :]).astype(F32) * (2.0 * math.pi / n)
    th2 = ((b[:, None] * b[None, :]) % FFT_FAST).astype(F32) * (2.0 * math.pi / FFT_FAST)
    c1, s1 = jnp.cos(th1)[:, None, :], jnp.sin(th1)[:, None, :]
    c2, s2 = jnp.cos(th2)[None, :, :], jnp.sin(th2)[None, :, :]
    mr = c1 * c2 - s1 * s2
    mi = -(s1 * c2 + c1 * s2)
    m_fwd = jnp.concatenate([jnp.concatenate([mr, -mi], axis=2),
                             jnp.concatenate([mi, mr], axis=2)], axis=1)
    ir = jnp.transpose(mr, (0, 2, 1)) / n
    ii = jnp.transpose(-mi, (0, 2, 1)) / n
    m_inv = jnp.concatenate([jnp.concatenate([ir, -ii], axis=2),
                             jnp.concatenate([ii, ir], axis=2)], axis=1)
    return tuple(t.astype(BF16) for t in (f_fwd, f_inv, m_fwd, m_inv))


def _dft_slow_fwd_kernel(f_ref, x_ref, o_ref):
    ns = o_ref.shape[2]
    for bl in range(x_ref.shape[1]):
        r = jnp.dot(f_ref[...], x_ref[:, bl, :].astype(BF16), preferred_element_type=F32)
        o_ref[0, 0, :, bl, :] = r[0:ns]
        o_ref[0, 1, :, bl, :] = r[ns:]


def dft_slow_fwd(fmat, x3, bx):
    two_ns, na = fmat.shape
    ns = two_ns // 2
    cw = x3.shape[-1]
    kb = SUBLANES
    return pl.pallas_call(
        _dft_slow_fwd_kernel,
        grid=(bx, FFT_FAST // kb),
        in_specs=[pl.BlockSpec((two_ns, na), lambda b, j: (0, 0)),
                  pl.BlockSpec((na, kb, cw), lambda b, j: (b, j, 0))],
        out_specs=pl.BlockSpec((1, 2, ns, kb, cw), lambda b, j: (b, 0, 0, j, 0)),
        out_shape=jax.ShapeDtypeStruct((bx, 2, ns, FFT_FAST, cw), F32),
        compiler_params=_cparams(("parallel", "parallel")),
        name="dft_slow_fwd",
    )(fmat, x3)


def _dft_slow_inv_kernel(f_ref, b_ref, o_ref, *, bx):
    @pl.when(pl.program_id(0) < bx)
    def _():
        for bl in range(o_ref.shape[1]):
            bb = jnp.concatenate([b_ref[0, 0, :, bl, :], b_ref[0, 1, :, bl, :]], axis=0).astype(BF16)
            o_ref[:, bl, :] = jnp.dot(f_ref[...], bb, preferred_element_type=F32)

    @pl.when(pl.program_id(0) >= bx)
    def _():
        o_ref[...] = jnp.zeros_like(o_ref)


def dft_slow_inv(fmat, bv):
    bx, _, ns, nf, cw = bv.shape
    na = fmat.shape[0]
    kb = SUBLANES
    return pl.pallas_call(
        functools.partial(_dft_slow_inv_kernel, bx=bx),
        grid=(bx + 1, nf // kb),
        in_specs=[pl.BlockSpec((na, 2 * ns), lambda b, j: (0, 0)),
                  pl.BlockSpec((1, 2, ns, kb, cw), lambda b, j: (jnp.minimum(b, bx - 1), 0, 0, j, 0))],
        out_specs=pl.BlockSpec((na, kb, cw), lambda b, j: (b, j, 0)),
        out_shape=jax.ShapeDtypeStruct(((bx + 1) * na, nf, cw), F32),
        compiler_params=_cparams(("parallel", "parallel")),
        name="dft_slow_inv",
    )(fmat, bv)


FFT_GROUP = 8


def _dft_fast_filter_kernel(m_ref, a_ref, sc_ref, o_ref):
    sc = sc_ref[...]
    for j in range(FFT_GROUP):
        af = jnp.concatenate([a_ref[0, 0, j], a_ref[0, 1, j]], axis=0).astype(BF16)
        ab = jnp.concatenate([a_ref[1, 0, j], a_ref[1, 1, j]], axis=0).astype(BF16)
        xf = jnp.dot(m_ref[j], af, preferred_element_type=F32)
        xb = jnp.dot(m_ref[j], ab, preferred_element_type=F32)
        o_ref[0, j] = (xf[0:FFT_FAST] + xb[0:FFT_FAST]) * sc
        o_ref[1, j] = (xf[FFT_FAST:] - xb[FFT_FAST:]) * sc


def dft_fast_filter(m_fwd, a5, scale):
    _, _, ns, nf, cw = a5.shape
    g = FFT_GROUP
    return pl.pallas_call(
        _dft_fast_filter_kernel,
        grid=(ns // g,),
        in_specs=[pl.BlockSpec((g, 2 * nf, 2 * nf), lambda i: (i, 0, 0)),
                  pl.BlockSpec((2, 2, g, nf, cw), lambda i: (0, 0, i, 0, 0)),
                  pl.BlockSpec((1, cw), lambda i: (0, 0))],
        out_specs=pl.BlockSpec((2, g, nf, cw), lambda i: (0, i, 0, 0)),
        out_shape=jax.ShapeDtypeStruct((2, ns, nf, cw), F32),
        compiler_params=_cparams(("parallel",)),
        name="dft_fast_filter",
    )(m_fwd, a5, scale)


def _dft_fast_conv_kernel(mf_ref, mi_ref, a_ref, k_ref, o_ref):
    for j in range(FFT_GROUP):
        a = jnp.concatenate([a_ref[0, 0, j], a_ref[0, 1, j]], axis=0).astype(BF16)
        x = jnp.dot(mf_ref[j], a, preferred_element_type=F32)
        xr, xi = x[0:FFT_FAST], x[FFT_FAST:]
        kr, ki = k_ref[0, j], k_ref[1, j]
        y = jnp.concatenate([xr * kr - xi * ki, xr * ki + xi * kr], axis=0).astype(BF16)
        bv = jnp.dot(mi_ref[j], y, preferred_element_type=F32)
        o_ref[0, 0, j] = bv[0:FFT_FAST]
        o_ref[0, 1, j] = bv[FFT_FAST:]


def dft_fast_conv(m_fwd, m_inv, a5, kf):
    bx, _, ns, nf, cw = a5.shape
    g = FFT_GROUP
    return pl.pallas_call(
        _dft_fast_conv_kernel,
        grid=(bx, ns // g),
        in_specs=[pl.BlockSpec((g, 2 * nf, 2 * nf), lambda b, i: (i, 0, 0)),
                  pl.BlockSpec((g, 2 * nf, 2 * nf), lambda b, i: (i, 0, 0)),
                  pl.BlockSpec((1, 2, g, nf, cw), lambda b, i: (b, 0, i, 0, 0)),
                  pl.BlockSpec((2, g, nf, cw), lambda b, i: (0, i, 0, 0))],
        out_specs=pl.BlockSpec((1, 2, g, nf, cw), lambda b, i: (b, 0, i, 0, 0)),
        out_shape=jax.ShapeDtypeStruct((bx, 2, ns, nf, cw), F32),
        compiler_params=_cparams(("parallel", "parallel")),
        name="dft_fast_conv",
    )(m_fwd, m_inv, a5, kf)


def hyena_long_conv(vx_rows, bsz, length, hfb, scale):
    cw = vx_rows.shape[-1]
    ns = 2 * length // FFT_FAST
    f_fwd, f_inv, m_fwd, m_inv = _dft_consts(length)
    fa = dft_slow_fwd(f_fwd[:, :ns // 2], hfb.reshape(-1, FFT_FAST, cw), 2)
    kf = dft_fast_filter(m_fwd, fa, scale)
    xa = dft_slow_fwd(f_fwd[:, :ns // 2], vx_rows.reshape(-1, FFT_FAST, cw), bsz)
    bv = dft_fast_conv(m_fwd, m_inv, xa, kf)
    return dft_slow_inv(f_inv[:ns // 2], bv).reshape((bsz + 1) * length, cw)


def _hy_ctx_conv_kernel(v_ref, g_ref, y_hbm_ref, o_ref, *, lc):
    del y_hbm_ref
    nb = lc // SUBLANES

    def body(gi, acc):
        base = pl.multiple_of((nb - 1 - gi) * SUBLANES, SUBLANES)
        win = g_ref[pl.ds(base, lc + SUBLANES), :]
        for j in range(SUBLANES):
            vrow = v_ref[pl.ds(gi * SUBLANES + j, 1), :]
            acc = acc + win[SUBLANES - 1 - j:SUBLANES - 1 - j + lc, :] * vrow
        return acc

    o_ref[...] = lax.fori_loop(0, nb, body, jnp.zeros((lc, HY_W), F32))


def hy_ctx_conv(vx, gwin, yconv, bsz, tpb, lc):
    ctx_tile = lambda b: (bsz * (tpb - 1) + b, 0)
    return pl.pallas_call(
        functools.partial(_hy_ctx_conv_kernel, lc=lc),
        grid=(bsz,),
        in_specs=[pl.BlockSpec((lc, HY_W), ctx_tile),
                  pl.BlockSpec((2 * lc, HY_W), lambda b: (0, 0)),
                  pl.BlockSpec(memory_space=pl.ANY)],
        out_specs=pl.BlockSpec((lc, HY_W), ctx_tile),
        out_shape=jax.ShapeDtypeStruct(yconv.shape, F32),
        input_output_aliases={2: 0},
        compiler_params=_cparams(("parallel",)),
        name="hy_ctx_conv",
    )(vx, gwin, yconv)


ROUTE_E1, ROUTE_E2, ROUTE_W1, ROUTE_W2, ROUTE_R1, ROUTE_R2 = 0, 1, 2, 3, 4, 5
NEG_BIG = -1e30


def _gelu_tanh(x):
    return 0.5 * x * (1.0 + jnp.tanh(math.sqrt(2.0 / math.pi) * (x + 0.044715 * (x * x * x))))


def _lane_pick(slab, lane, idx):
    return jnp.sum(jnp.where(lane == idx, slab, 0.0), axis=-1, keepdims=True)


def _route_tile(logits, valid, base_ref, cnt_ref, route_ref, rt_ref):
    lane = lax.broadcasted_iota(jnp.int32, (TM, LANES), 1).astype(F32)
    gl = jnp.where(lane < MOE_GROUPS, logits, NEG_BIG)
    gmax = jnp.max(gl, axis=-1, keepdims=True)
    yield
    grp_p = 1.0 / jnp.sum(jnp.exp(gl - gmax), axis=-1, keepdims=True)
    gi = jnp.min(jnp.where(gl == gmax, lane, float(LANES)), axis=-1, keepdims=True)
    yield
    lo = MOE_GROUPS + MOE_PER_GROUP * gi
    el = jnp.where((lane >= lo) & (lane < lo + MOE_PER_GROUP), logits, NEG_BIG)
    m1 = jnp.max(el, axis=-1, keepdims=True)
    yield
    i1 = jnp.min(jnp.where(el == m1, lane, float(LANES)), axis=-1, keepdims=True)
    yield
    el2 = jnp.where(lane == i1, NEG_BIG, el)
    m2 = jnp.max(el2, axis=-1, keepdims=True)
    yield
    i2 = jnp.min(jnp.where(el2 == m2, lane, float(LANES)), axis=-1, keepdims=True)
    yield
    ratio = jnp.exp(m2 - m1)
    wt1 = grp_p / (1.0 + ratio)
    wt2 = grp_p * ratio / (1.0 + ratio)
    e1 = i1 - MOE_GROUPS
    e2 = i2 - MOE_GROUPS

    oh1 = jnp.where(lane == e1, valid, 0.0)
    oh2 = jnp.where(lane == e2, valid, 0.0)
    ri = lax.broadcasted_iota(jnp.int32, (TM, TM), 0)
    ci = lax.broadcasted_iota(jnp.int32, (TM, TM), 1)
    below = jnp.where(ci < ri, 1.0, 0.0).astype(BF16)
    c1 = jnp.dot(below, oh1.astype(BF16), preferred_element_type=F32)
    c2 = jnp.dot(below, oh2.astype(BF16), preferred_element_type=F32)
    tot1 = jnp.sum(oh1, axis=0, keepdims=True)
    tot2 = jnp.sum(oh2, axis=0, keepdims=True)
    yield
    base = base_ref[...]
    r1 = jnp.sum(oh1 * (base + c1), axis=-1, keepdims=True)
    r2 = jnp.sum(oh2 * (base + tot1 + c2), axis=-1, keepdims=True)
    base = base + tot1 + tot2
    base_ref[...] = base
    cnt_ref[...] = jnp.broadcast_to(base, cnt_ref.shape)
    route = jnp.zeros((TM, LANES), F32)
    for idx, val in ((ROUTE_E1, e1), (ROUTE_E2, e2), (ROUTE_W1, wt1), (ROUTE_W2, wt2),
                     (ROUTE_R1, r1), (ROUTE_R2, r2)):
        route = jnp.where(lane == idx, val, route)
    route_ref[...] = route
    rt_ref[...] = route.T[0:SUBLANES, :]


def _merge_kernel(x_ref, mod_ref, n1_ref, n2_ref, wg0_ref, wg1_ref, wg2_ref, wg3_ref, zhg_ref, zrt_ref, zs5_ref,
                  hgf_ref, hgb_ref, rtf_ref, rtb_ref, s5f_ref, s5b_ref, yc_ref, vx_ref, x0_ref,
                  vec_ref, glu1_ref, glu2_ref, wb_ref, wo_ref, wr_ref, br_ref,
                  xo_ref, h2_ref, route_ref, rt_ref, cnt_ref, base_ref, lg_s, *, d):
    i = pl.program_id(0)

    @pl.when(i == 0)
    def _():
        base_ref[...] = jnp.zeros_like(base_ref)
        lg_s[...] = jnp.zeros_like(lg_s)

    routing = _route_tile(lg_s[...], jnp.where(i > 0, 1.0, 0.0), base_ref, cnt_ref, route_ref, rt_ref)
    advance_routing = lambda: next(routing, None)

    w = BRANCH_W
    x = x_ref[...]
    mod = mod_ref[0]
    h = _modulated_norm(x, n1_ref[...], mod[:, 0:d], mod[:, d:2 * d]).astype(BF16)
    advance_routing()

    hd = w // HG_HEADS
    hr = lax.broadcasted_iota(jnp.int32, (w, w), 0) // hd
    hc = lax.broadcasted_iota(jnp.int32, (w, w), 1) // hd
    hm = jnp.where(hr == hc, 1.0, 0.0).astype(BF16)
    inv = 1.0 / hd

    hg = hgf_ref[...] + hgb_ref[...]
    ms = _split_dot(hg * hg, hm) * inv
    br_hg = hg * lax.rsqrt(ms + EPS) * vec_ref[0:1, :] * _silu(zhg_ref[...])
    advance_routing()

    rt = rtf_ref[...] + rtb_ref[...]
    cen = rt - _split_dot(rt, hm) * inv
    var = _split_dot(cen * cen, hm) * inv
    br_rt = cen * lax.rsqrt(var + EPS) * vec_ref[1:2, :] * _silu(zrt_ref[...])
    advance_routing()

    vx = vx_ref[...]
    br_hy = (yc_ref[...] + vx * vec_ref[3:4, :]) * x0_ref[...]

    y5 = _gelu_tanh(s5f_ref[...] + s5b_ref[...] + zs5_ref[...] * vec_ref[2:3, :]).astype(BF16)
    br_s5 = (jnp.dot(y5, glu1_ref[...], preferred_element_type=F32)
             * _sigmoid(jnp.dot(y5, glu2_ref[...], preferred_element_type=F32)))
    advance_routing()

    merged = jnp.zeros((TM, d), F32)
    for n, (br, wg_ref) in enumerate(zip((br_hg, br_hy, br_rt, br_s5), (wg0_ref, wg1_ref, wg2_ref, wg3_ref))):
        gate = _sigmoid(jnp.dot(h, wg_ref[...], preferred_element_type=F32))
        merged = merged + gate * jnp.dot(br.astype(BF16), wb_ref[n], preferred_element_type=F32)
        advance_routing()
    for _ in routing:
        pass
    mix = jnp.dot(merged.astype(BF16), wo_ref[...], preferred_element_type=F32)
    x_mid = x + mod[:, 2 * d:3 * d] * mix
    xo_ref[...] = x_mid
    h2 = _modulated_norm(x_mid, n2_ref[...], mod[:, 3 * d:4 * d], mod[:, 4 * d:5 * d])
    h2_ref[...] = h2

    lg_s[...] = _dot3(h2, wr_ref[...]) + br_ref[...]


def merge(x_all, mod3, n1, n2, w_in_bf, layer, z, hgf, hgb, rtf, rtb, s5f, s5b, yconv, vx, x0c,
          vecs, glu1, glu2, w_branch, w_out, w_router, b_router, bsz, tpb):
    m, d = x_all.shape
    w = BRANCH_W
    nt = m // TM
    cur = lambda i: jnp.minimum(i, nt - 1)
    prev = lambda i: jnp.maximum(i - 1, 0)
    row = lambda width: pl.BlockSpec((TM, width), lambda i: (cur(i), 0))
    srow = pl.BlockSpec((TM, w), lambda i: (_split_tile(cur(i), bsz, tpb), 0))
    zcol = lambda cb: pl.BlockSpec((TM, w), lambda i: (cur(i), cb))
    full = lambda shape: pl.BlockSpec(shape, lambda i: (0,) * len(shape))
    gate_w = lambda n: pl.BlockSpec((None, d, d), lambda i: (layer, 0, N_MIX // d + n))
    return pl.pallas_call(
        functools.partial(_merge_kernel, d=d),
        grid=(nt + 1,),
        in_specs=[row(d),
                  pl.BlockSpec((1, 1, 6 * d), lambda i: (_seg_of_tile(cur(i), tpb), 0, 0)),
                  full((1, d)), full((1, d)), gate_w(0), gate_w(1), gate_w(2), gate_w(3),
                  zcol(CB_HG_G), zcol(CB_RT_G), zcol(CB_S5)]
                 + [row(w)] * 6 + [srow, srow, row(w)]
                 + [full((SUBLANES, w)), full((w, w)), full((w, w)), full((N_BRANCH, w, d)), full((d, d)),
                    full((d, LANES)), full((1, LANES))],
        out_specs=[row(d), row(d),
                   pl.BlockSpec((TM, LANES), lambda i: (prev(i), 0)),
                   pl.BlockSpec((None, SUBLANES, TM), lambda i: (prev(i), 0, 0)),
                   full((SUBLANES, LANES))],
        out_shape=[jax.ShapeDtypeStruct((m, d), F32), jax.ShapeDtypeStruct((m, d), F32),
                   jax.ShapeDtypeStruct((m, LANES), F32), jax.ShapeDtypeStruct((nt, SUBLANES, TM), F32),
                   jax.ShapeDtypeStruct((SUBLANES, LANES), F32)],
        scratch_shapes=[pltpu.VMEM((1, LANES), F32), pltpu.VMEM((TM, LANES), F32)],
        compiler_params=_cparams(("arbitrary",)),
        name="merge",
    )(x_all, mod3, n1, n2, w_in_bf, w_in_bf, w_in_bf, w_in_bf, z, z, z,
      hgf, hgb, rtf, rtb, s5f, s5b, yconv, vx, x0c,
      vecs, glu1, glu2, w_branch, w_out, w_router, b_router)


DMA_UNROLL = 8


def _row_copy(src_ref, src_row, dst_ref, dst_row, sem):
    return pltpu.make_async_copy(src_ref.at[pl.ds(src_row, 1), :], dst_ref.at[pl.ds(dst_row, 1), :], sem)


DISPATCH_ROWS = 3 * TM


def _dispatch_kernel(dest_ref, h_ref, zero_ref, buf_ref, sem, *, m):
    del zero_ref
    rows = h_ref.shape[0]
    base = pl.program_id(0) * rows

    def issue(r, carry):
        for k in range(2):
            _row_copy(h_ref, r, buf_ref, dest_ref[k * m + base + r], sem).start(priority=k)
        return carry

    lax.fori_loop(0, rows, issue, 0, unroll=DMA_UNROLL)
    for k in range(2):
        pltpu.make_async_copy(h_ref, buf_ref.at[pl.ds(0, rows), :], sem).wait()


def moe_dispatch(dest, h2, n_rows):
    m, d = h2.shape
    assert m % DISPATCH_ROWS == 0
    grid_spec = pltpu.PrefetchScalarGridSpec(
        num_scalar_prefetch=1,
        grid=(m // DISPATCH_ROWS,),
        in_specs=[pl.BlockSpec((DISPATCH_ROWS, d), lambda i, dest: (i, 0)),
                  pl.BlockSpec(memory_space=pl.ANY)],
        out_specs=pl.BlockSpec(memory_space=pl.ANY),
        scratch_shapes=[pltpu.SemaphoreType.DMA(())],
    )
    return pl.pallas_call(
        functools.partial(_dispatch_kernel, m=m),
        grid_spec=grid_spec,
        out_shape=jax.ShapeDtypeStruct((n_rows, d), F32),
        input_output_aliases={2: 0},
        compiler_params=_cparams(("arbitrary",)),
        name="moe_dispatch",
    )(dest, h2, jnp.zeros((n_rows, d), F32))


def _expert_kernel(plan_ref, x_ref, w1_ref, w3_ref, w2_ref, y_ref, *, n_blocks):
    used = pl.program_id(0) < plan_ref[n_blocks]

    @pl.when(used)
    def _():
        xb = x_ref[...].astype(BF16)
        a = jnp.dot(xb, w1_ref[...].astype(BF16), preferred_element_type=F32)
        g = jnp.dot(xb, w3_ref[...].astype(BF16), preferred_element_type=F32)
        y_ref[...] = jnp.dot((_silu(a) * g).astype(BF16), w2_ref[...].astype(BF16),
                             preferred_element_type=F32)

    @pl.when(jnp.logical_not(used))
    def _():
        y_ref[...] = jnp.zeros_like(y_ref)


def moe_experts(plan, buf, w1, w3, w2, layer):
    n_rows, d = buf.shape
    hid = w1.shape[-1]
    n_blocks = n_rows // MOE_BLOCK
    blk = lambda j, plan: jnp.minimum(j, plan[n_blocks] - 1)
    grid_spec = pltpu.PrefetchScalarGridSpec(
        num_scalar_prefetch=1,
        grid=(n_blocks,),
        in_specs=[pl.BlockSpec((MOE_BLOCK, d), lambda j, plan: (blk(j, plan), 0)),
                  pl.BlockSpec((None, None, d, hid), lambda j, plan: (layer, plan[blk(j, plan)], 0, 0)),
                  pl.BlockSpec((None, None, d, hid), lambda j, plan: (layer, plan[blk(j, plan)], 0, 0)),
                  pl.BlockSpec((None, None, hid, d), lambda j, plan: (layer, plan[blk(j, plan)], 0, 0))],
        out_specs=pl.BlockSpec((MOE_BLOCK, d), lambda j, plan: (j, 0)),
    )
    return pl.pallas_call(
        functools.partial(_expert_kernel, n_blocks=n_blocks),
        grid_spec=grid_spec,
        out_shape=jax.ShapeDtypeStruct((n_rows, d), F32),
        compiler_params=_cparams(("arbitrary",)),
        name="moe_experts",
    )(plan, buf, w1, w3, w2)


def _combine_kernel(dest_ref, x_ref, route_ref, mod_ref, gfin_ref, y_ref, o_ref, gath, sem, *,
                    d, m, bsz, tpb, nr, final):
    step = pl.program_id(0) * nr + pl.program_id(1)
    slot = step % 2

    def start_gather(s, sl):
        base = ((s // nr) * tpb + s % nr) * TM

        def issue(r, carry):
            for k in range(2):
                _row_copy(y_ref, dest_ref[k * m + base + r], gath.at[sl, k], r, sem.at[sl]).start(priority=k)
            return carry

        lax.fori_loop(0, TM, issue, 0, unroll=DMA_UNROLL)

    @pl.when(step == 0)
    def _():
        start_gather(0, 0)

    @pl.when(step + 1 < bsz * nr)
    def _():
        start_gather(step + 1, 1 - slot)

    for k in range(2):
        pltpu.make_async_copy(y_ref.at[pl.ds(0, TM), :], gath.at[slot, k], sem.at[slot]).wait()
    route = route_ref[...]
    lane = lax.broadcasted_iota(jnp.int32, (TM, LANES), 1)
    moe = (_lane_pick(route, lane, ROUTE_W1) * gath[slot, 0]
           + _lane_pick(route, lane, ROUTE_W2) * gath[slot, 1])
    x_new = x_ref[...] + mod_ref[0][:, 5 * d:6 * d] * moe
    if final:
        ms = jnp.mean(x_new * x_new, axis=-1, keepdims=True)
        x_new = x_new * lax.rsqrt(ms + EPS) * gfin_ref[...]
    o_ref[...] = x_new


def moe_combine(dest, x_mid, route, mod3, gain_final, ybuf, bsz, tpb, final):
    m, d = x_mid.shape
    nr = tpb - 1 if final else tpb
    tile = lambda b, r: b * tpb + r
    grid_spec = pltpu.PrefetchScalarGridSpec(
        num_scalar_prefetch=1,
        grid=(bsz, nr),
        in_specs=[pl.BlockSpec((TM, d), lambda b, r, dest: (tile(b, r), 0)),
                  pl.BlockSpec((TM, LANES), lambda b, r, dest: (tile(b, r), 0)),
                  pl.BlockSpec((1, 1, 6 * d), lambda b, r, dest: (_seg_of_tile(tile(b, r), tpb), 0, 0)),
                  pl.BlockSpec((1, d), lambda b, r, dest: (0, 0)),
                  pl.BlockSpec(memory_space=pl.ANY)],
        out_specs=pl.BlockSpec((TM, d), lambda b, r, dest: (b * nr + r, 0)),
        scratch_shapes=[pltpu.VMEM((2, 2, TM, d), F32), pltpu.SemaphoreType.DMA((2,))],
    )
    return pl.pallas_call(
        functools.partial(_combine_kernel, d=d, m=m, bsz=bsz, tpb=tpb, nr=nr, final=final),
        grid_spec=grid_spec,
        out_shape=jax.ShapeDtypeStruct((bsz * nr * TM, d), F32),
        compiler_params=_cparams(("arbitrary", "arbitrary")),
        name="moe_combine",
    )(dest, x_mid, route, mod3, gain_final, ybuf)


def moe_plan(rt, counts_row, n_blocks):
    counts = counts_row[:MOE_EXPERTS].astype(jnp.int32)
    padded = (counts + MOE_BLOCK - 1) // MOE_BLOCK * MOE_BLOCK
    pad_end = jnp.cumsum(padded)
    pad_start = pad_end - padded
    by_choice = lambda f: jnp.transpose(rt[:, f:f + 2, :], (1, 0, 2)).reshape(2, -1).astype(jnp.int32)
    expert = by_choice(ROUTE_E1)
    rank = by_choice(ROUTE_R1)
    is_e = expert[None] == jnp.arange(MOE_EXPERTS, dtype=jnp.int32)[:, None, None]
    dest = (jnp.sum(jnp.where(is_e, pad_start[:, None, None], 0), axis=0) + rank).reshape(-1)
    block_row0 = jnp.arange(n_blocks, dtype=jnp.int32) * MOE_BLOCK
    block_expert = jnp.minimum(jnp.sum((pad_end[None, :] <= block_row0[:, None]).astype(jnp.int32), axis=1),
                               MOE_EXPERTS - 1)
    n_used = pad_end[MOE_EXPERTS - 1:] // MOE_BLOCK
    return dest, jnp.concatenate([block_expert, n_used])


def _rope_tables(length, lc):
    half = RET_DK // 2
    freqs = ROPE_BASE ** (-jnp.arange(0, half, 2, dtype=F32) / half)
    t = jnp.arange(length)
    row_pos = (t // GRID_W).astype(F32)[:, None]
    col_pos = (t % GRID_W).astype(F32)[:, None]
    lane = jnp.arange(RET_HEADS * RET_DK)
    fr = freqs[lane % (half // 2)][None, :]
    ang = jnp.where(((lane % RET_DK) < half)[None, :], row_pos, col_pos) * fr
    cos = jnp.concatenate([jnp.cos(ang), jnp.ones((lc, lane.shape[0]), F32)], axis=0)
    sin = jnp.concatenate([jnp.sin(ang), jnp.zeros((lc, lane.shape[0]), F32)], axis=0)
    return cos, sin


def _rotate_half_cols(wq):
    quarter = RET_DK // 4
    lane = np.arange(wq.shape[1])
    first = (lane % (2 * quarter)) < quarter
    src = np.where(first, lane + quarter, lane - quarter)
    sign = jnp.asarray(np.where(first, -1.0, 1.0), F32)
    return wq[:, src] * sign


def kernel(x, c, ctx, c_ctx, w_mod, b_mod, norm1, norm2, w_in, hg_lb_logits, hg_norm, hy_conv_w, hy_conv_b, hy_filt_w1, hy_filt_b1, hy_filt_freq1, hy_filt_w2, hy_filt_b2, hy_filt_freq2, hy_filt_w3, hy_skip, ret_decay_logit, ret_norm, s5_lam_re, s5_lam_im, s5_log_dt, s5_b_re, s5_b_im, s5_c_re, s5_c_im, s5_d, s5_glu_w1, s5_glu_w2, w_branch, w_out, moe_w_grp, moe_b_grp, moe_w_exp, moe_b_exp, moe_w1, moe_w3, moe_w2, norm_final):
    bsz, length, d = x.shape
    lc = ctx.shape[1]
    depth = w_mod.shape[0]
    assert lc == TM and length % TM == 0 and (2 * length) % (FFT_FAST * FFT_GROUP) == 0
    tt = length + lc
    tpb = tt // TM
    m = bsz * tt
    n_blocks = -(-(m * 2) // MOE_BLOCK) + MOE_EXPERTS
    w = BRANCH_W

    x_all = jnp.concatenate([x, ctx], axis=1).reshape(m, d)
    cmat = jnp.zeros((SUBLANES, d), F32).at[0].set(c_ctx).at[1:1 + bsz].set(c)
    mods = modvec(cmat, w_mod, b_mod)
    lb_sm = jax.nn.softmax(hg_lb_logits.astype(F32), axis=0)
    lb_all = jnp.cumsum(lb_sm, axis=0) - lb_sm[0]
    cos_tab, sin_tab = _rope_tables(length, lc)
    log_gamma = jax.nn.log_sigmoid(ret_decay_logit.astype(F32))
    gfin = norm_final.reshape(1, d)

    w_in_bf = w_in.astype(BF16)
    rq0 = 5 * w + 3 * HY_W
    nq = RET_HEADS * RET_DK

    for l in range(depth):
        w_rot = jnp.concatenate([_rotate_half_cols(w_in[l, :, rq0:rq0 + nq]),
                                 _rotate_half_cols(w_in[l, :, rq0 + nq:rq0 + 2 * nq])], axis=1).astype(BF16)
        mod3 = mods[l].reshape(SUBLANES, 1, 6 * d)

        z = inproj(x_all, mod3, norm1[l].reshape(1, d), w_in_bf, l, w_rot, tpb)
        hgf, hgb = hgrn(z, lb_all[l], bsz, tpb)
        rtf, rtb = retention(z, cos_tab, sin_tab, log_gamma[l], bsz, tpb)
        s5f, s5b = s5(z, *s5_params(s5_lam_re[l], s5_lam_im[l], s5_log_dt[l], s5_b_re[l], s5_b_im[l],
                                    s5_c_re[l], s5_c_im[l]), bsz, tpb)

        x0c, vx = hy_prep(z, hy_conv_w[l], hy_conv_b[l], bsz, tpb)
        fargs = (hy_filt_w1[l], hy_filt_b1[l], hy_filt_freq1[l], hy_filt_w2[l], hy_filt_b2[l],
                 hy_filt_freq2[l], hy_filt_w3[l])
        hfb, ss = hy_filter(length, *fargs)
        y_lat = hyena_long_conv(vx, bsz, length, hfb, lax.rsqrt(ss + EPS))
        hfbc, ssc = hy_filter(lc, *fargs)
        gwin = jnp.concatenate([hfbc[1, 1:][::-1], hfbc[0], jnp.zeros((1, HY_W), F32)], axis=0) \
            * lax.rsqrt(ssc + EPS)
        yconv = hy_ctx_conv(vx, gwin, y_lat, bsz, tpb, lc)

        vecs = jnp.zeros((SUBLANES, w), F32).at[0].set(hg_norm[l]).at[1].set(ret_norm[l]) \
            .at[2].set(s5_d[l]).at[3].set(hy_skip[l])
        w_router = jnp.zeros((d, LANES), F32).at[:, :MOE_GROUPS].set(moe_w_grp[l]) \
            .at[:, MOE_GROUPS:MOE_GROUPS + MOE_EXPERTS].set(moe_w_exp[l])
        b_router = jnp.zeros((1, LANES), F32).at[0, :MOE_GROUPS].set(moe_b_grp[l]) \
            .at[0, MOE_GROUPS:MOE_GROUPS + MOE_EXPERTS].set(moe_b_exp[l])
        x_mid, h2, route, rt, counts = merge(
            x_all, mod3, norm1[l].reshape(1, d), norm2[l].reshape(1, d), w_in_bf, l, z,
            hgf, hgb, rtf, rtb, s5f, s5b, yconv, vx, x0c, vecs,
            s5_glu_w1[l].astype(BF16), s5_glu_w2[l].astype(BF16), w_branch[l].astype(BF16),
            w_out[l].astype(BF16), w_router, b_router, bsz, tpb)

        dest, block_expert = moe_plan(rt, counts[0], n_blocks)
        buf = moe_dispatch(dest, h2, n_blocks * MOE_BLOCK)
        ybuf = moe_experts(block_expert, buf, moe_w1, moe_w3, moe_w2, l)
        x_all = moe_combine(dest, x_mid, route, mod3, gfin, ybuf, bsz, tpb, l == depth - 1)

    return x_all.reshape(bsz, length, d)
```

```python
import functools
import math

import numpy as np
import jax
import jax.numpy as jnp
from jax import lax
from jax.experimental import pallas as pl
from jax.experimental.pallas import tpu as pltpu

F32 = jnp.float32
BF16 = jnp.bfloat16
HIGHEST = lax.Precision.HIGHEST

GRID_W = 64
N_BRANCH = 4
BRANCH_W = 256
HG_HEADS = 4
HG_CHUNK = 16
HY_W = 256
HY_EMB = 33
HY_BANDS = 16
HY_ORDER = 64
HY_FAST_DECAY = 0.3
HY_SLOW_DECAY = 1.5
HY_TARGET = 1e-2
RET_HEADS = 4
RET_DK = 32
RET_DV = 64
ROPE_BASE = 10000.0
S5_GROUP_CH = 16
S5_GROUPS = 16
S5_STATE = 64
MOE_GROUPS = 4
MOE_PER_GROUP = 8
MOE_EXPERTS = 32
MOE_HIDDEN = 512
MOE_BLOCK = 256
EPS = 1e-6

LANES = 128
SUBLANES = 8
TM = 256
FFT_FAST = 128
VMEM_LIMIT = 56 * 1024 * 1024

N_MIX = 3072
N_ZCOLS = N_MIX + 2 * RET_HEADS * RET_DK
CB_HG_Q, CB_HG_FF, CB_HG_FB, CB_HG_I, CB_HG_G = 3, 4, 5, 6, 7
CB_RT_V, CB_RT_G, CB_S5 = 9, 10, 11
CB128_RT_Q, CB128_RT_K, CB128_RT_QR, CB128_RT_KR = 16, 17, 24, 25


def _cparams(sem):
    return pltpu.CompilerParams(dimension_semantics=sem, vmem_limit_bytes=VMEM_LIMIT)


def _sigmoid(x):
    return 1.0 / (1.0 + jnp.exp(-x))


def _silu(x):
    return x * _sigmoid(x)


def _dot(a, b):
    return jnp.dot(a.astype(BF16), b.astype(BF16), preferred_element_type=F32)


def _dot_nt(a, b):
    return lax.dot_general(a.astype(BF16), b.astype(BF16), (((1,), (1,)), ((), ())),
                           preferred_element_type=F32)


def _dot_hi(a, b):
    return jnp.dot(a, b, precision=HIGHEST, preferred_element_type=F32)


def _dot3(a, b):
    a_hi = a.astype(BF16)
    a_lo = (a - a_hi.astype(F32)).astype(BF16)
    b_hi = b.astype(BF16)
    b_lo = (b - b_hi.astype(F32)).astype(BF16)
    return (jnp.dot(a_hi, b_hi, preferred_element_type=F32)
            + jnp.dot(a_lo, b_hi, preferred_element_type=F32)
            + jnp.dot(a_hi, b_lo, preferred_element_type=F32))


def _split_dot(x, m_bf16):
    hi = x.astype(BF16)
    lo = (x - hi.astype(F32)).astype(BF16)
    return (jnp.dot(hi, m_bf16, preferred_element_type=F32)
            + jnp.dot(lo, m_bf16, preferred_element_type=F32))


def _split_dot_left(m_bf16, x):
    hi = x.astype(BF16)
    lo = (x - hi.astype(F32)).astype(BF16)
    return (jnp.dot(m_bf16, hi, preferred_element_type=F32)
            + jnp.dot(m_bf16, lo, preferred_element_type=F32))


def _seg_of_tile(i, tpb):
    return jnp.where(i % tpb == tpb - 1, 0, 1 + i // tpb)


def _split_tile(i, bsz, tpb):
    b, r = i // tpb, i % tpb
    return jnp.where(r == tpb - 1, bsz * (tpb - 1) + b, b * (tpb - 1) + r)


def _scan_tiles(b, s, tpb):
    last = tpb - 1
    fwd = jnp.where(s == 0, last, s - 1)
    bwd = jnp.where(s == 0, last, last - s)
    return b * tpb + fwd, b * tpb + bwd


def _modvec_kernel(c_ref, w_ref, b_ref, o_ref):
    c = c_ref[...]
    o_ref[0] = _dot_hi(_silu(c), w_ref[0]) + b_ref[0]


def modvec(cmat, w_mod, b_mod):
    depth, d, n6 = w_mod.shape
    nb = n6 // d
    return pl.pallas_call(
        _modvec_kernel,
        grid=(depth, nb),
        in_specs=[pl.BlockSpec((SUBLANES, d), lambda l, j: (0, 0)),
                  pl.BlockSpec((1, d, d), lambda l, j: (l, 0, j)),
                  pl.BlockSpec((1, 1, d), lambda l, j: (l, 0, j))],
        out_specs=pl.BlockSpec((1, SUBLANES, d), lambda l, j: (l, 0, j)),
        out_shape=jax.ShapeDtypeStruct((depth, SUBLANES, n6), F32),
        compiler_params=_cparams(("arbitrary", "arbitrary")),
        name="modvec",
    )(cmat, w_mod, b_mod.reshape(depth, 1, n6))


def _modulated_norm(x, gain, shift, scale):
    ms = jnp.mean(x * x, axis=-1, keepdims=True)
    return (x * lax.rsqrt(ms + EPS) * gain) * (1.0 + scale) + shift


def _inproj_kernel(x_ref, mod_ref, g_ref, w_ref, wrot_ref, z_ref, *, d):
    h = _modulated_norm(x_ref[...], g_ref[...], mod_ref[0, :, 0:d], mod_ref[0, :, d:2 * d]).astype(BF16)
    hy0 = 5 * BRANCH_W
    hy1 = hy0 + 3 * HY_W
    z_ref[:, 0:hy1 - hy0] = jnp.dot(h, w_ref[:, hy0:hy1], preferred_element_type=F32)
    z_ref[:, hy1 - hy0:hy1] = jnp.dot(h, w_ref[:, 0:hy0], preferred_element_type=F32)
    z_ref[:, hy1:N_MIX] = jnp.dot(h, w_ref[:, hy1:N_MIX], preferred_element_type=F32)
    z_ref[:, N_MIX:N_ZCOLS] = jnp.dot(h, wrot_ref[...], preferred_element_type=F32)


def inproj(x_all, mod3, gain, w_in_bf, layer, w_rot, tpb):
    m, d = x_all.shape
    return pl.pallas_call(
        functools.partial(_inproj_kernel, d=d),
        grid=(m // TM,),
        in_specs=[pl.BlockSpec((TM, d), lambda i: (i, 0)),
                  pl.BlockSpec((1, 1, 6 * d), lambda i: (_seg_of_tile(i, tpb), 0, 0)),
                  pl.BlockSpec((1, d), lambda i: (0, 0)),
                  pl.BlockSpec((None, d, N_MIX), lambda i: (layer, 0, 0)),
                  pl.BlockSpec((d, N_ZCOLS - N_MIX), lambda i: (0, 0))],
        out_specs=pl.BlockSpec((TM, N_ZCOLS), lambda i: (i, 0)),
        out_shape=jax.ShapeDtypeStruct((m, N_ZCOLS), F32),
        compiler_params=_cparams(("parallel",)),
        name="inproj",
    )(x_all, mod3, gain, w_in_bf, w_rot)


def _head_masked_rows(x, nchunk, c, hd):
    w = x.shape[-1]
    x3 = x.reshape(nchunk, c, w)
    lane_head = lax.broadcasted_iota(jnp.int32, (1, 1, w), 2) // hd
    parts = [jnp.where(lane_head == h, x3, 0.0) for h in range(w // hd)]
    return jnp.concatenate(parts, axis=1).reshape(nchunk * (w // hd) * c, w)


def _hgrn_direction(q_ref, f_ref, v_ref, lb, tri_ref, o_ref, st_ref, b_s, k_s, qe_s, ke_s, reverse):
    c = HG_CHUNK
    w = BRANCH_W
    nchunk = TM // c
    nh = HG_HEADS
    hd = w // nh
    half = c // 2
    z = f_ref[...]
    sg = _sigmoid(z)
    log2_f = jnp.log2(lb + (1.0 - lb) * sg)
    k = (1.0 - lb) * (1.0 - sg)
    b2 = _split_dot_left(tri_ref[0], log2_f)
    rem2 = _split_dot_left(tri_ref[1], log2_f)
    b_s[...] = b2
    k_s[...] = k
    qe_s[...] = _head_masked_rows(q_ref[...] * jnp.exp2(b2), nchunk, c, hd).astype(BF16)
    ke_s[...] = _head_masked_rows(k * jnp.exp2(rem2), nchunk, c, hd).astype(BF16)

    head_r = lax.broadcasted_iota(jnp.int32, (w, w), 0) // hd
    head_c = lax.broadcasted_iota(jnp.int32, (w, w), 1) // hd
    hm = jnp.where(head_r == head_c, 1.0, 0.0).astype(BF16)
    rows = lax.broadcasted_iota(jnp.int32, (half, 1), 0)

    pieces = []
    for s in range(c):
        for g in range(2):
            lo, hi = g * half, (g + 1) * half - 1
            if (hi < s) if not reverse else (lo > s):
                continue
            full = (lo >= s) if not reverse else (hi <= s)
            pieces.append((s, g, full))

    def chunk_step(n):
        cidx = (nchunk - 1 - n) if reverse else n
        off = pl.multiple_of(cidx * c, c)
        eoff = pl.multiple_of(cidx * (nh * c), nh * c)
        qc = q_ref[pl.ds(off, c), :]
        bc = b_s[pl.ds(off, c), :]
        kc = k_s[pl.ds(off, c), :]
        vc = v_ref[pl.ds(off, c), :]
        parts = []
        for s, g, full in pieces:
            sl = slice(g * half, (g + 1) * half)
            e = jnp.exp2(bc[sl] - bc[s:s + 1, :])
            if not full:
                t = rows + g * half
                e = jnp.where((t <= s) if reverse else (t >= s), e, 0.0)
            parts.append((qc[sl] * kc[s:s + 1, :]) * e)
        p = jnp.concatenate(parts, axis=0).astype(BF16)
        r = jnp.dot(p, hm, preferred_element_type=F32)
        og = [jnp.zeros((half, w), F32), jnp.zeros((half, w), F32)]
        for i, (s, g, _) in enumerate(pieces):
            og[g] = og[g] + r[i * half:(i + 1) * half, :] * vc[s:s + 1, :]
        st = st_ref[...]
        ri = _dot_nt(qe_s[pl.ds(eoff, nh * c), :], st)
        inter = jnp.concatenate([ri[h * c:(h + 1) * c, :] for h in range(nh)], axis=1)
        o_ref[pl.ds(off, c), :] = jnp.concatenate(og, axis=0) + inter
        vexp = jnp.concatenate([vc[:, h * hd:(h + 1) * hd] for h in range(nh)], axis=0)
        kv = lax.dot_general(vexp.astype(BF16), ke_s[pl.ds(eoff, nh * c), :], (((0,), (0,)), ((), ())),
                             preferred_element_type=F32)
        blast = bc[0:1, :] if reverse else bc[c - 1:c, :]
        st_ref[...] = jnp.exp2(blast) * st + kv

    return chunk_step


def _hgrn_kernel(qf_ref, ff_ref, vf_ref, qb_ref, fb_ref, vb_ref, lb_ref, trif_ref, trib_ref, of_ref, ob_ref,
                 stf_ref, stb_ref, bf_s, kf_s, qef_s, kef_s, bb_s, kb_s, qeb_s, keb_s):
    @pl.when(pl.program_id(1) == 0)
    def _():
        stf_ref[...] = jnp.zeros_like(stf_ref)
        stb_ref[...] = jnp.zeros_like(stb_ref)

    step_f = _hgrn_direction(qf_ref, ff_ref, vf_ref, lb_ref[0:1, :], trif_ref, of_ref, stf_ref,
                             bf_s, kf_s, qef_s, kef_s, False)
    step_b = _hgrn_direction(qb_ref, fb_ref, vb_ref, lb_ref[1:2, :], trib_ref, ob_ref, stb_ref,
                             bb_s, kb_s, qeb_s, keb_s, True)

    def body(n, carry):
        step_f(n)
        step_b(n)
        return carry

    lax.fori_loop(0, TM // HG_CHUNK, body, 0, unroll=8)


def hgrn(z, lb, bsz, tpb):
    m = z.shape[0]
    w = BRANCH_W
    hd = w // HG_HEADS
    ri = jnp.arange(TM)[:, None]
    ci = jnp.arange(TM)[None, :]
    same = (ri // HG_CHUNK) == (ci // HG_CHUNK)
    tri = jnp.stack([same & (ci <= ri), same & (ci > ri), same & (ci >= ri), same & (ci < ri)])
    tri = tri.astype(BF16).reshape(2, 2, TM, TM)

    def spec(cb, d):
        return pl.BlockSpec((TM, w), lambda b, s: (_scan_tiles(b, s, tpb)[d], cb))

    out_spec = [pl.BlockSpec((TM, w), lambda b, s: (_scan_tiles(b, s, tpb)[0], 0)),
                pl.BlockSpec((TM, w), lambda b, s: (_scan_tiles(b, s, tpb)[1], 0))]
    return pl.pallas_call(
        _hgrn_kernel,
        grid=(bsz, tpb),
        in_specs=[spec(CB_HG_Q, 0), spec(CB_HG_FF, 0), spec(CB_HG_I, 0),
                  spec(CB_HG_Q, 1), spec(CB_HG_FB, 1), spec(CB_HG_I, 1),
                  pl.BlockSpec((2, w), lambda b, s: (0, 0)),
                  pl.BlockSpec((None, 2, TM, TM), lambda b, s: (0, 0, 0, 0)),
                  pl.BlockSpec((None, 2, TM, TM), lambda b, s: (1, 0, 0, 0))],
        out_specs=out_spec,
        out_shape=[jax.ShapeDtypeStruct((m, w), F32)] * 2,
        scratch_shapes=[pltpu.VMEM((hd, w), F32), pltpu.VMEM((hd, w), F32)]
                       + [pltpu.VMEM((TM, w), F32), pltpu.VMEM((TM, w), F32),
                          pltpu.VMEM((HG_HEADS * TM, w), BF16), pltpu.VMEM((HG_HEADS * TM, w), BF16)] * 2,
        compiler_params=_cparams(("arbitrary", "arbitrary")),
        name="hgrn2",
    )(z, z, z, z, z, z, lb, tri, tri)


def _ret_direction(q_ref, k_ref, qr_ref, kr_ref, v_ref, cos_ref, sin_ref, lgq, lg_ref, d, o_ref, st_ref,
                   reverse):
    nq = RET_HEADS * RET_DK
    nv = RET_HEADS * RET_DV
    cos = cos_ref[...]
    sin = sin_ref[...]
    qx = q_ref[...] * cos + qr_ref[...] * sin
    kx = (k_ref[...] * cos + kr_ref[...] * sin) * (RET_DK ** -0.5)
    v = v_ref[...]
    t = lax.broadcasted_iota(jnp.int32, (TM, 1), 0).astype(F32)
    pos = (TM - 1.0 - t) if reverse else t
    qd = qx * jnp.exp((pos + 1.0) * lgq)
    kd = kx * jnp.exp((TM - 1.0 - pos) * lgq)
    ti = lax.broadcasted_iota(jnp.int32, (TM, TM), 0)
    si = lax.broadcasted_iota(jnp.int32, (TM, TM), 1)
    rel = ((si - ti) if reverse else (ti - si)).astype(F32)
    live = rel >= 0.0
    relc = jnp.where(live, rel, 0.0)
    lane_q = lax.broadcasted_iota(jnp.int32, (1, nq), 1) // RET_DK
    lane_v = lax.broadcasted_iota(jnp.int32, (1, nv), 1) // RET_DV
    st = st_ref[...]
    o = _dot_nt(qd, st)
    vb = v.astype(BF16)
    for h in range(RET_HEADS):
        dmat = jnp.where(live, jnp.exp(relc * lg_ref[d, h]), 0.0)
        sc = _dot_nt(jnp.where(lane_q == h, qx, 0.0), kx) * dmat
        oh = jnp.dot(sc.astype(BF16), vb, preferred_element_type=F32)
        o = o + jnp.where(lane_v == h, oh, 0.0)
    o_ref[...] = o
    kv = _dot(v.T, kd)
    hr = lax.broadcasted_iota(jnp.int32, (nv, nq), 0) // RET_DV
    hc = lax.broadcasted_iota(jnp.int32, (nv, nq), 1) // RET_DK
    st_ref[...] = st * jnp.exp(float(TM) * lgq) + jnp.where(hr == hc, kv, 0.0)


def _ret_kernel(lg_ref, qf, kf, qrf, krf, vf, cosf, sinf, qb, kb, qrb, krb, vb, cosb, sinb, lgq_ref,
                of_ref, ob_ref, stf_ref, stb_ref):
    @pl.when(pl.program_id(1) == 0)
    def _():
        stf_ref[...] = jnp.zeros_like(stf_ref)
        stb_ref[...] = jnp.zeros_like(stb_ref)

    _ret_direction(qf, kf, qrf, krf, vf, cosf, sinf, lgq_ref[0:1, :], lg_ref, 0, of_ref, stf_ref, False)
    _ret_direction(qb, kb, qrb, krb, vb, cosb, sinb, lgq_ref[1:2, :], lg_ref, 1, ob_ref, stb_ref, True)


def retention(z, cos_tab, sin_tab, log_gamma, bsz, tpb):
    m = z.shape[0]
    nq = RET_HEADS * RET_DK
    nv = RET_HEADS * RET_DV
    lgq = jnp.repeat(log_gamma, RET_DK, axis=1)

    def zspec(cb, width, d):
        return pl.BlockSpec((TM, width), lambda b, s: (_scan_tiles(b, s, tpb)[d], cb))

    def tspec(d):
        return pl.BlockSpec((TM, nq), lambda b, s: (_scan_tiles(0, s, tpb)[d], 0))

    def side(d):
        return [zspec(CB128_RT_Q, nq, d), zspec(CB128_RT_K, nq, d), zspec(CB128_RT_QR, nq, d),
                zspec(CB128_RT_KR, nq, d), zspec(CB_RT_V, nv, d), tspec(d), tspec(d)]

    return pl.pallas_call(
        _ret_kernel,
        grid=(bsz, tpb),
        in_specs=[pl.BlockSpec(memory_space=pltpu.SMEM)] + side(0) + side(1)
                 + [pl.BlockSpec((2, nq), lambda b, s: (0, 0))],
        out_specs=[pl.BlockSpec((TM, nv), lambda b, s: (_scan_tiles(b, s, tpb)[0], 0)),
                   pl.BlockSpec((TM, nv), lambda b, s: (_scan_tiles(b, s, tpb)[1], 0))],
        out_shape=[jax.ShapeDtypeStruct((m, nv), F32)] * 2,
        scratch_shapes=[pltpu.VMEM((nv, nq), F32), pltpu.VMEM((nv, nq), F32)],
        compiler_params=_cparams(("arbitrary", "arbitrary")),
        name="retention",
    )(log_gamma, z, z, z, z, z, cos_tab, sin_tab, z, z, z, z, z, cos_tab, sin_tab, lgq)


S5_NP = S5_GROUPS * S5_STATE


S5_T = SUBLANES
S5_ROW_APPLY, S5_ROW_END, S5_ROW_GROUP, S5_ROW_STEP = 0, 8, 16, 24


def _s5g_direction(u_ref, bre_ref, bim_ref, cre_ref, cim_ref, kt_ref, tab_ref, o_ref, car_ref, reverse):
    ng = TM // S5_T
    nsg = ng // SUBLANES
    np_ = S5_NP
    w = BRANCH_W
    sub = lax.broadcasted_iota(jnp.int32, (1, SUBLANES, 1), 1)

    def neighbour(x3, k):
        shift = (SUBLANES - k) if reverse else k
        keep = (sub < SUBLANES - k) if reverse else (sub >= k)
        return jnp.where(keep, pltpu.roll(x3, shift, 1), 0.0)

    u3 = u_ref[...]
    u2 = u3.reshape(TM, w)
    ub = u2.astype(BF16)
    y = jnp.dot(ub, kt_ref[0, 0], preferred_element_type=F32)
    for tau in range(1, S5_T):
        y = y + jnp.dot(neighbour(u3, tau).reshape(TM, w).astype(BF16), kt_ref[0, tau],
                        preferred_element_type=F32)

    bur = jnp.dot(ub, bre_ref[0], preferred_element_type=F32).reshape(ng, S5_T, np_)
    bui = jnp.dot(ub, bim_ref[0], preferred_element_type=F32).reshape(ng, S5_T, np_)
    wr = tab_ref[0, 0, S5_ROW_END:S5_ROW_END + S5_T, :]
    wi = tab_ref[0, 1, S5_ROW_END:S5_ROW_END + S5_T, :]
    gsel = (lax.broadcasted_iota(jnp.int32, (ng, TM), 1) // S5_T
            == lax.broadcasted_iota(jnp.int32, (ng, TM), 0))
    gsel = jnp.where(gsel, 1.0, 0.0).astype(BF16)
    er = jnp.dot(gsel, (wr * bur - wi * bui).reshape(TM, np_).astype(BF16), preferred_element_type=F32)
    ei = jnp.dot(gsel, (wr * bui + wi * bur).reshape(TM, np_).astype(BF16), preferred_element_type=F32)

    sr = er.reshape(nsg, SUBLANES, np_)
    si = ei.reshape(nsg, SUBLANES, np_)
    for n, k in enumerate((1, 2, 4)):
        ar = tab_ref[0, 0, S5_ROW_STEP + n:S5_ROW_STEP + n + 1, :]
        ai = tab_ref[0, 1, S5_ROW_STEP + n:S5_ROW_STEP + n + 1, :]
        nr, ni = neighbour(sr, k), neighbour(si, k)
        sr, si = sr + ar * nr - ai * ni, si + ar * ni + ai * nr
    pr = tab_ref[0, 0, S5_ROW_GROUP:S5_ROW_GROUP + SUBLANES, :]
    pi = tab_ref[0, 1, S5_ROW_GROUP:S5_ROW_GROUP + SUBLANES, :]
    cin_r, cin_i = car_ref[0:1, :], car_ref[1:2, :]
    cr, ci = cin_r, cin_i
    blocks_r, blocks_i = [None] * nsg, [None] * nsg
    for sg in (reversed(range(nsg)) if reverse else range(nsg)):
        br = sr[sg] + pr * cr - pi * ci
        bi = si[sg] + pr * ci + pi * cr
        blocks_r[sg], blocks_i[sg] = br, bi
        cr, ci = (br[0:1, :], bi[0:1, :]) if reverse else (br[SUBLANES - 1:, :], bi[SUBLANES - 1:, :])
    car_ref[0:1, :] = cr
    car_ref[1:2, :] = ci
    end_r = jnp.concatenate(blocks_r, axis=0)
    end_i = jnp.concatenate(blocks_i, axis=0)
    grow = lax.broadcasted_iota(jnp.int32, (ng, 1), 0)
    edge = (ng - 1) if reverse else 0
    shift = (ng - 1) if reverse else 1
    pre_r = jnp.where(grow == edge, cin_r, pltpu.roll(end_r, shift, 0))
    pre_i = jnp.where(grow == edge, cin_i, pltpu.roll(end_i, shift, 0))

    ar = tab_ref[0, 0, S5_ROW_APPLY:S5_ROW_APPLY + S5_T, :][None]
    ai = tab_ref[0, 1, S5_ROW_APPLY:S5_ROW_APPLY + S5_T, :][None]
    pre_r3 = pre_r[:, None, :]
    pre_i3 = pre_i[:, None, :]
    xr = (ar * pre_r3 - ai * pre_i3).reshape(TM, np_).astype(BF16)
    xi = (ar * pre_i3 + ai * pre_r3).reshape(TM, np_).astype(BF16)
    o_ref[...] = (y + jnp.dot(xr, cre_ref[0], preferred_element_type=F32)
                  - jnp.dot(xi, cim_ref[0], preferred_element_type=F32))


def _s5g_kernel(uf, ub, bre, bim, cre, cim, ktf, tabf, bre2, bim2, cre2, cim2, ktb, tabb, of_ref, ob_ref,
                carf, carb):
    @pl.when(pl.program_id(1) == 0)
    def _():
        carf[...] = jnp.zeros_like(carf)
        carb[...] = jnp.zeros_like(carb)

    _s5g_direction(uf, bre, bim, cre, cim, ktf, tabf, of_ref, carf, False)
    _s5g_direction(ub, bre2, bim2, cre2, cim2, ktb, tabb, ob_ref, carb, True)


def s5g(z, bblk_re, bblk_im, cblk_re, cblk_im, ktoep, tab, bsz, tpb):
    m = z.shape[0]
    w = BRANCH_W
    ng = TM // S5_T
    z3 = z.reshape(m // S5_T, S5_T, z.shape[1])

    def pspec(shape, d):
        return pl.BlockSpec((1,) + shape, lambda b, s: (d,) + (0,) * len(shape))

    def uspec(d):
        return pl.BlockSpec((ng, S5_T, w), lambda b, s: (_scan_tiles(b, s, tpb)[d], 0, CB_S5))

    def side(d):
        return [pspec((w, S5_NP), d), pspec((w, S5_NP), d), pspec((S5_NP, w), d), pspec((S5_NP, w), d),
                pspec((S5_T, w, w), d), pspec((2, 32, S5_NP), d)]

    return pl.pallas_call(
        _s5g_kernel,
        grid=(bsz, tpb),
        in_specs=[uspec(0), uspec(1)] + side(0) + side(1),
        out_specs=[pl.BlockSpec((TM, w), lambda b, s: (_scan_tiles(b, s, tpb)[0], 0)),
                   pl.BlockSpec((TM, w), lambda b, s: (_scan_tiles(b, s, tpb)[1], 0))],
        out_shape=[jax.ShapeDtypeStruct((m, w), F32)] * 2,
        scratch_shapes=[pltpu.VMEM((2, S5_NP), F32), pltpu.VMEM((2, S5_NP), F32)],
        compiler_params=_cparams(("arbitrary", "arbitrary")),
        name="s5",
    )(z3, z3, bblk_re, bblk_im, cblk_re, cblk_im, ktoep, tab,
      bblk_re, bblk_im, cblk_re, cblk_im, ktoep, tab)


def s5g_params(lam_re, lam_im, log_dt, b_re, b_im, c_re, c_im):
    lr = jnp.minimum(lam_re.astype(F32), -1e-4)
    li = lam_im.astype(F32)
    dt = jnp.exp(log_dt.astype(F32))[..., None]

    def apow(k):
        mag = jnp.exp(k * lr * dt)
        return mag * jnp.cos(k * li * dt), mag * jnp.sin(k * li * dt)

    ar, ai = apow(1.0)
    den = lr * lr + li * li
    gr = ((ar - 1.0) * lr + ai * li) / den
    gi = (ai * lr - (ar - 1.0) * li) / den
    bbr = gr[..., None] * b_re - gi[..., None] * b_im
    bbi = gr[..., None] * b_im + gi[..., None] * b_re
    cr, ci = c_re.astype(F32), c_im.astype(F32)
    eye = jnp.eye(S5_GROUPS, dtype=F32)
    nch = S5_GROUPS * S5_GROUP_CH

    def blk_in(bb):
        return jnp.einsum('dgpc,gh->dgchp', bb, eye).reshape(2, nch, S5_NP).astype(BF16)

    def blk_out(cc):
        return jnp.einsum('dgcp,gh->dgphc', cc, eye).reshape(2, S5_NP, nch).astype(BF16)

    lags = []
    for tau in range(S5_T):
        pr, pi = apow(float(tau))
        car = cr * pr[:, :, None, :] - ci * pi[:, :, None, :]
        cai = cr * pi[:, :, None, :] + ci * pr[:, :, None, :]
        k = (jnp.einsum('dgop,dgpi->dgio', car, bbr, precision=HIGHEST)
             - jnp.einsum('dgop,dgpi->dgio', cai, bbi, precision=HIGHEST))
        lags.append(jnp.einsum('dgio,gh->dgiho', k, eye).reshape(2, nch, nch))
    ktoep = jnp.stack(lags, axis=1).astype(BF16)

    def row(kf, kb):
        pf, qf = apow(float(kf))
        pb, qb = apow(float(kb))
        return jnp.stack([pf[0], pb[1]]), jnp.stack([qf[0], qb[1]])

    t = S5_T
    rows = ([row(j + 1, t - j) for j in range(t)]
            + [row(t - 1 - j, j) for j in range(t)]
            + [row(t * (j + 1), t * (t - j)) for j in range(SUBLANES)]
            + [row(t * k, t * k) for k in (1, 2, 4)])
    zero = jnp.zeros_like(rows[0][0])
    rows = rows + [(zero, zero)] * (32 - len(rows))
    tab_r = jnp.stack([r for r, _ in rows], axis=1).reshape(2, 32, S5_NP)
    tab_i = jnp.stack([q for _, q in rows], axis=1).reshape(2, 32, S5_NP)
    tab = jnp.stack([tab_r, tab_i], axis=1)
    return blk_in(bbr), blk_in(bbi), blk_out(cr), blk_out(ci), ktoep, tab


def _hy_prep_kernel(u_ref, prev_ref, next_ref, w_ref, b_ref, x0_ref, vx_ref, *, tpb):
    i = pl.program_id(0)
    r = i % tpb
    first = (r == 0) | (r == tpb - 1)
    last = r >= tpb - 2
    u = u_ref[...]
    w = w_ref[...]
    up_row = jnp.where(first, 0.0, prev_ref[SUBLANES - 1:SUBLANES, :])
    dn_row = jnp.where(last, 0.0, next_ref[0:1, :])
    row = lax.broadcasted_iota(jnp.int32, (TM, 1), 0)
    um = jnp.where(row == 0, up_row, pltpu.roll(u, 1, 0))
    un = jnp.where(row == TM - 1, dn_row, pltpu.roll(u, TM - 1, 0))
    y = um * w[0:1, :] + u * w[1:2, :] + un * w[2:3, :] + b_ref[...]
    x0_ref[...] = y[:, 0:HY_W]
    vx_ref[...] = y[:, 2 * HY_W:3 * HY_W] * y[:, HY_W:2 * HY_W]


def hy_prep(z, conv_w, conv_b, bsz, tpb):
    m = z.shape[0]
    w3 = 3 * HY_W
    rb = TM // SUBLANES
    nrb = m // SUBLANES
    return pl.pallas_call(
        functools.partial(_hy_prep_kernel, tpb=tpb),
        grid=(m // TM,),
        in_specs=[pl.BlockSpec((TM, w3), lambda i: (i, 0)),
                  pl.BlockSpec((SUBLANES, w3), lambda i: (jnp.maximum(i * rb - 1, 0), 0)),
                  pl.BlockSpec((SUBLANES, w3), lambda i: (jnp.minimum((i + 1) * rb, nrb - 1), 0)),
                  pl.BlockSpec((3, w3), lambda i: (0, 0)),
                  pl.BlockSpec((1, w3), lambda i: (0, 0))],
        out_specs=[pl.BlockSpec((TM, HY_W), lambda i: (i, 0)),
                   pl.BlockSpec((TM, HY_W), lambda i: (_split_tile(i, bsz, tpb), 0))],
        out_shape=[jax.ShapeDtypeStruct((m, HY_W), F32)] * 2,
        compiler_params=_cparams(("parallel",)),
        name="hy_prep",
    )(z, z, z, conv_w, conv_b.reshape(1, w3))


def _hy_filter_kernel(zf_ref, w1_ref, b1_ref, f1_ref, w2_ref, b2_ref, f2_ref, w3_ref, dl_ref,
                      hfb_ref, ss_ref):
    i = pl.program_id(0)
    zf = zf_ref[...]
    hid = jnp.sin(f1_ref[...] * (_dot_hi(zf, w1_ref[...]) + b1_ref[...]))
    hid = jnp.sin(f2_ref[...] * (_dot_hi(hid, w2_ref[...]) + b2_ref[...]))
    filt = _dot_hi(hid, w3_ref[...])
    window = jnp.exp(-zf[:, 0:1] * dl_ref[...])
    hf = filt[:, 0:HY_W] * window
    row = lax.broadcasted_iota(jnp.int32, (TM, 1), 0) + i * TM
    hb = jnp.where(row == 0, 0.0, filt[:, HY_W:2 * HY_W] * window)
    hfb_ref[0] = hf
    hfb_ref[1] = hb
    part = jnp.sum(hf * hf + hb * hb, axis=0, keepdims=True)

    @pl.when(i == 0)
    def _():
        ss_ref[...] = jnp.zeros_like(ss_ref)

    ss_ref[...] = ss_ref[...] + part


def hy_filter(length, w1, b1, fr1, w2, b2, fr2, w3):
    t = jnp.linspace(0.0, 1.0, length, dtype=F32)[:, None]
    wv = (2.0 * math.pi / length) * jnp.arange(length, dtype=F32)[:, None]
    f = jnp.linspace(1e-4, HY_BANDS - 1.0, HY_BANDS, dtype=F32)[None, :]
    zfeat = jnp.concatenate([t, jnp.cos(f * wv), -jnp.sin(f * wv)], axis=-1)
    zfeat = jnp.pad(zfeat, ((0, 0), (0, LANES - HY_EMB)))
    w1p = jnp.pad(w1, ((0, LANES - HY_EMB), (0, 0)))
    deltas = jnp.abs(jnp.linspace(math.log(HY_TARGET) / HY_SLOW_DECAY,
                                  math.log(HY_TARGET) / HY_FAST_DECAY, HY_W, dtype=F32))[None, :]
    full = lambda shape: pl.BlockSpec(shape, lambda i: (0,) * len(shape))
    return pl.pallas_call(
        _hy_filter_kernel,
        grid=(length // TM,),
        in_specs=[pl.BlockSpec((TM, LANES), lambda i: (i, 0)),
                  full((LANES, HY_ORDER)), full((1, HY_ORDER)), full((1, HY_ORDER)),
                  full((HY_ORDER, HY_ORDER)), full((1, HY_ORDER)), full((1, HY_ORDER)),
                  full((HY_ORDER, 2 * HY_W)), full((1, HY_W))],
        out_specs=[pl.BlockSpec((2, TM, HY_W), lambda i: (0, i, 0)),
                   pl.BlockSpec((1, HY_W), lambda i: (0, 0))],
        out_shape=[jax.ShapeDtypeStruct((2, length, HY_W), F32), jax.ShapeDtypeStruct((1, HY_W), F32)],
        compiler_params=_cparams(("arbitrary",)),
        name="hy_filter",
    )(zfeat, w1p, b1.reshape(1, -1), fr1.reshape(1, -1), w2, b2.reshape(1, -1), fr2.reshape(1, -1), w3, deltas)


def _dft_consts(length):
    n = 2 * length
    ns = n // FFT_FAST
    a = jnp.arange(ns, dtype=jnp.int32)
    ang = ((a[:, None] * a[None, :]) % ns).astype(F32) * (2.0 * math.pi / ns)
    f_fwd = jnp.concatenate([jnp.cos(ang), -jnp.sin(ang)], axis=0)
    f_inv = jnp.concatenate([jnp.cos(ang), -jnp.sin(ang)], axis=1)
    b = jnp.arange(FFT_FAST, dtype=jnp.int32)
    th1 = (a[:, None] * b[None, :]).astype(F32) * (2.0 * math.pi / n)
    th2 = ((b[:, None] * b[None, :]) % FFT_FAST).astype(F32) * (2.0 * math.pi / FFT_FAST)
    c1, s1 = jnp.cos(th1)[:, None, :], jnp.sin(th1)[:, None, :]
    c2, s2 = jnp.cos(th2)[None, :, :], jnp.sin(th2)[None, :, :]
    mr = c1 * c2 - s1 * s2
    mi = -(s1 * c2 + c1 * s2)
    m_fwd = jnp.concatenate([jnp.concatenate([mr, -mi], axis=2),
                             jnp.concatenate([mi, mr], axis=2)], axis=1)
    ir = jnp.transpose(mr, (0, 2, 1)) / n
    ii = jnp.transpose(-mi, (0, 2, 1)) / n
    m_inv = jnp.concatenate([jnp.concatenate([ir, -ii], axis=2),
                             jnp.concatenate([ii, ir], axis=2)], axis=1)
    return tuple(t.astype(BF16) for t in (f_fwd, f_inv, m_fwd, m_inv))


def _dft_slow_fwd_kernel(f_ref, x_ref, o_ref):
    ns = o_ref.shape[2]
    for bl in range(x_ref.shape[1]):
        r = jnp.dot(f_ref[...], x_ref[:, bl, :].astype(BF16), preferred_element_type=F32)
        o_ref[0, 0, :, bl, :] = r[0:ns]
        o_ref[0, 1, :, bl, :] = r[ns:]


def dft_slow_fwd(fmat, x3, bx):
    two_ns, na = fmat.shape
    ns = two_ns // 2
    cw = x3.shape[-1]
    kb = SUBLANES
    return pl.pallas_call(
        _dft_slow_fwd_kernel,
        grid=(bx, FFT_FAST // kb),
        in_specs=[pl.BlockSpec((two_ns, na), lambda b, j: (0, 0)),
                  pl.BlockSpec((na, kb, cw), lambda b, j: (b, j, 0))],
        out_specs=pl.BlockSpec((1, 2, ns, kb, cw), lambda b, j: (b, 0, 0, j, 0)),
        out_shape=jax.ShapeDtypeStruct((bx, 2, ns, FFT_FAST, cw), F32),
        compiler_params=_cparams(("parallel", "parallel")),
        name="dft_slow_fwd",
    )(fmat, x3)


def _dft_slow_inv_kernel(f_ref, b_ref, o_ref, *, bx):
    @pl.when(pl.program_id(0) < bx)
    def _():
        for bl in range(o_ref.shape[1]):
            bb = jnp.concatenate([b_ref[0, 0, :, bl, :], b_ref[0, 1, :, bl, :]], axis=0).astype(BF16)
            o_ref[:, bl, :] = jnp.dot(f_ref[...], bb, preferred_element_type=F32)

    @pl.when(pl.program_id(0) >= bx)
    def _():
        o_ref[...] = jnp.zeros_like(o_ref)


def dft_slow_inv(fmat, bv):
    bx, _, ns, nf, cw = bv.shape
    na = fmat.shape[0]
    kb = SUBLANES
    return pl.pallas_call(
        functools.partial(_dft_slow_inv_kernel, bx=bx),
        grid=(bx + 1, nf // kb),
        in_specs=[pl.BlockSpec((na, 2 * ns), lambda b, j: (0, 0)),
                  pl.BlockSpec((1, 2, ns, kb, cw), lambda b, j: (jnp.minimum(b, bx - 1), 0, 0, j, 0))],
        out_specs=pl.BlockSpec((na, kb, cw), lambda b, j: (b, j, 0)),
        out_shape=jax.ShapeDtypeStruct(((bx + 1) * na, nf, cw), F32),
        compiler_params=_cparams(("parallel", "parallel")),
        name="dft_slow_inv",
    )(fmat, bv)


FFT_GROUP = 8


def _dft_fast_filter_kernel(m_ref, a_ref, sc_ref, o_ref):
    sc = sc_ref[...]
    for j in range(FFT_GROUP):
        af = jnp.concatenate([a_ref[0, 0, j], a_ref[0, 1, j]], axis=0).astype(BF16)
        ab = jnp.concatenate([a_ref[1, 0, j], a_ref[1, 1, j]], axis=0).astype(BF16)
        xf = jnp.dot(m_ref[j], af, preferred_element_type=F32)
        xb = jnp.dot(m_ref[j], ab, preferred_element_type=F32)
        o_ref[0, j] = (xf[0:FFT_FAST] + xb[0:FFT_FAST]) * sc
        o_ref[1, j] = (xf[FFT_FAST:] - xb[FFT_FAST:]) * sc


def dft_fast_filter(m_fwd, a5, scale):
    _, _, ns, nf, cw = a5.shape
    g = FFT_GROUP
    return pl.pallas_call(
        _dft_fast_filter_kernel,
        grid=(ns // g,),
        in_specs=[pl.BlockSpec((g, 2 * nf, 2 * nf), lambda i: (i, 0, 0)),
                  pl.BlockSpec((2, 2, g, nf, cw), lambda i: (0, 0, i, 0, 0)),
                  pl.BlockSpec((1, cw), lambda i: (0, 0))],
        out_specs=pl.BlockSpec((2, g, nf, cw), lambda i: (0, i, 0, 0)),
        out_shape=jax.ShapeDtypeStruct((2, ns, nf, cw), F32),
        compiler_params=_cparams(("parallel",)),
        name="dft_fast_filter",
    )(m_fwd, a5, scale)


def _dft_fast_conv_kernel(mf_ref, mi_ref, a_ref, k_ref, o_ref):
    for j in range(FFT_GROUP):
        a = jnp.concatenate([a_ref[0, 0, j], a_ref[0, 1, j]], axis=0).astype(BF16)
        x = jnp.dot(mf_ref[j], a, preferred_element_type=F32)
        xr, xi = x[0:FFT_FAST], x[FFT_FAST:]
        kr, ki = k_ref[0, j], k_ref[1, j]
        y = jnp.concatenate([xr * kr - xi * ki, xr * ki + xi * kr], axis=0).astype(BF16)
        bv = jnp.dot(mi_ref[j], y, preferred_element_type=F32)
        o_ref[0, 0, j] = bv[0:FFT_FAST]
        o_ref[0, 1, j] = bv[FFT_FAST:]


def dft_fast_conv(m_fwd, m_inv, a5, kf):
    bx, _, ns, nf, cw = a5.shape
    g = FFT_GROUP
    return pl.pallas_call(
        _dft_fast_conv_kernel,
        grid=(bx, ns // g),
        in_specs=[pl.BlockSpec((g, 2 * nf, 2 * nf), lambda b, i: (i, 0, 0)),
                  pl.BlockSpec((g, 2 * nf, 2 * nf), lambda b, i: (i, 0, 0)),
                  pl.BlockSpec((1, 2, g, nf, cw), lambda b, i: (b, 0, i, 0, 0)),
                  pl.BlockSpec((2, g, nf, cw), lambda b, i: (0, i, 0, 0))],
        out_specs=pl.BlockSpec((1, 2, g, nf, cw), lambda b, i: (b, 0, i, 0, 0)),
        out_shape=jax.ShapeDtypeStruct((bx, 2, ns, nf, cw), F32),
        compiler_params=_cparams(("parallel", "parallel")),
        name="dft_fast_conv",
    )(m_fwd, m_inv, a5, kf)


def hyena_long_conv(vx_rows, bsz, length, hfb, scale):
    cw = vx_rows.shape[-1]
    ns = 2 * length // FFT_FAST
    f_fwd, f_inv, m_fwd, m_inv = _dft_consts(length)
    fa = dft_slow_fwd(f_fwd[:, :ns // 2], hfb.reshape(-1, FFT_FAST, cw), 2)
    kf = dft_fast_filter(m_fwd, fa, scale)
    xa = dft_slow_fwd(f_fwd[:, :ns // 2], vx_rows.reshape(-1, FFT_FAST, cw), bsz)
    bv = dft_fast_conv(m_fwd, m_inv, xa, kf)
    return dft_slow_inv(f_inv[:ns // 2], bv).reshape((bsz + 1) * length, cw)


def _hy_ctx_conv_kernel(v_ref, g_ref, y_hbm_ref, o_ref, *, lc):
    del y_hbm_ref
    nb = lc // SUBLANES

    def body(gi, acc):
        base = pl.multiple_of((nb - 1 - gi) * SUBLANES, SUBLANES)
        win = g_ref[pl.ds(base, lc + SUBLANES), :]
        for j in range(SUBLANES):
            vrow = v_ref[pl.ds(gi * SUBLANES + j, 1), :]
            acc = acc + win[SUBLANES - 1 - j:SUBLANES - 1 - j + lc, :] * vrow
        return acc

    o_ref[...] = lax.fori_loop(0, nb, body, jnp.zeros((lc, HY_W), F32))


def hy_ctx_conv(vx, gwin, yconv, bsz, tpb, lc):
    ctx_tile = lambda b: (bsz * (tpb - 1) + b, 0)
    return pl.pallas_call(
        functools.partial(_hy_ctx_conv_kernel, lc=lc),
        grid=(bsz,),
        in_specs=[pl.BlockSpec((lc, HY_W), ctx_tile),
                  pl.BlockSpec((2 * lc, HY_W), lambda b: (0, 0)),
                  pl.BlockSpec(memory_space=pl.ANY)],
        out_specs=pl.BlockSpec((lc, HY_W), ctx_tile),
        out_shape=jax.ShapeDtypeStruct(yconv.shape, F32),
        input_output_aliases={2: 0},
        compiler_params=_cparams(("parallel",)),
        name="hy_ctx_conv",
    )(vx, gwin, yconv)


ROUTE_E1, ROUTE_E2, ROUTE_W1, ROUTE_W2, ROUTE_R1, ROUTE_R2 = 0, 1, 2, 3, 4, 5
NEG_BIG = -1e30


def _gelu_tanh(x):
    return 0.5 * x * (1.0 + jnp.tanh(math.sqrt(2.0 / math.pi) * (x + 0.044715 * (x * x * x))))


def _lane_pick(slab, lane, idx):
    return jnp.sum(jnp.where(lane == idx, slab, 0.0), axis=-1, keepdims=True)


def _route_tile(logits, valid, base_ref, cnt_ref, route_ref, rt_ref):
    lane = lax.broadcasted_iota(jnp.int32, (TM, LANES), 1).astype(F32)
    gl = jnp.where(lane < MOE_GROUPS, logits, NEG_BIG)
    gmax = jnp.max(gl, axis=-1, keepdims=True)
    yield
    grp_p = 1.0 / jnp.sum(jnp.exp(gl - gmax), axis=-1, keepdims=True)
    gi = jnp.min(jnp.where(gl == gmax, lane, float(LANES)), axis=-1, keepdims=True)
    yield
    lo = MOE_GROUPS + MOE_PER_GROUP * gi
    el = jnp.where((lane >= lo) & (lane < lo + MOE_PER_GROUP), logits, NEG_BIG)
    m1 = jnp.max(el, axis=-1, keepdims=True)
    yield
    i1 = jnp.min(jnp.where(el == m1, lane, float(LANES)), axis=-1, keepdims=True)
    yield
    el2 = jnp.where(lane == i1, NEG_BIG, el)
    m2 = jnp.max(el2, axis=-1, keepdims=True)
    yield
    i2 = jnp.min(jnp.where(el2 == m2, lane, float(LANES)), axis=-1, keepdims=True)
    yield
    ratio = jnp.exp(m2 - m1)
    wt1 = grp_p / (1.0 + ratio)
    wt2 = grp_p * ratio / (1.0 + ratio)
    e1 = i1 - MOE_GROUPS
    e2 = i2 - MOE_GROUPS

    oh1 = jnp.where(lane == e1, valid, 0.0)
    oh2 = jnp.where(lane == e2, valid, 0.0)
    ri = lax.broadcasted_iota(jnp.int32, (TM, TM), 0)
    ci = lax.broadcasted_iota(jnp.int32, (TM, TM), 1)
    below = jnp.where(ci < ri, 1.0, 0.0).astype(BF16)
    c1 = jnp.dot(below, oh1.astype(BF16), preferred_element_type=F32)
    c2 = jnp.dot(below, oh2.astype(BF16), preferred_element_type=F32)
    tot1 = jnp.sum(oh1, axis=0, keepdims=True)
    tot2 = jnp.sum(oh2, axis=0, keepdims=True)
    yield
    base = base_ref[...]
    r1 = jnp.sum(oh1 * (base + c1), axis=-1, keepdims=True)
    r2 = jnp.sum(oh2 * (base + tot1 + c2), axis=-1, keepdims=True)
    base = base + tot1 + tot2
    base_ref[...] = base
    cnt_ref[...] = jnp.broadcast_to(base, cnt_ref.shape)
    route = jnp.zeros((TM, LANES), F32)
    for idx, val in ((ROUTE_E1, e1), (ROUTE_E2, e2), (ROUTE_W1, wt1), (ROUTE_W2, wt2),
                     (ROUTE_R1, r1), (ROUTE_R2, r2)):
        route = jnp.where(lane == idx, val, route)
    route_ref[...] = route
    rt_ref[...] = route.T[0:SUBLANES, :]


def _merge_kernel(x_ref, mod_ref, n1_ref, n2_ref, wg0_ref, wg1_ref, wg2_ref, wg3_ref, zhg_ref, zrt_ref, zs5_ref,
                  hgf_ref, hgb_ref, rtf_ref, rtb_ref, s5f_ref, s5b_ref, yc_ref, vx_ref, x0_ref,
                  vec_ref, glu1_ref, glu2_ref, wb_ref, wo_ref, wr_ref, br_ref,
                  xo_ref, h2_ref, route_ref, rt_ref, cnt_ref, base_ref, lg_s, *, d):
    i = pl.program_id(0)

    @pl.when(i == 0)
    def _():
        base_ref[...] = jnp.zeros_like(base_ref)
        lg_s[...] = jnp.zeros_like(lg_s)

    routing = _route_tile(lg_s[...], jnp.where(i > 0, 1.0, 0.0), base_ref, cnt_ref, route_ref, rt_ref)
    advance_routing = lambda: next(routing, None)

    w = BRANCH_W
    x = x_ref[...]
    mod = mod_ref[0]
    h = _modulated_norm(x, n1_ref[...], mod[:, 0:d], mod[:, d:2 * d]).astype(BF16)
    advance_routing()

    hd = w // HG_HEADS
    hr = lax.broadcasted_iota(jnp.int32, (w, w), 0) // hd
    hc = lax.broadcasted_iota(jnp.int32, (w, w), 1) // hd
    hm = jnp.where(hr == hc, 1.0, 0.0).astype(BF16)
    inv = 1.0 / hd

    hg = hgf_ref[...] + hgb_ref[...]
    ms = _split_dot(hg * hg, hm) * inv
    br_hg = hg * lax.rsqrt(ms + EPS) * vec_ref[0:1, :] * _silu(zhg_ref[...])
    advance_routing()

    rt = rtf_ref[...] + rtb_ref[...]
    cen = rt - _split_dot(rt, hm) * inv
    var = _split_dot(cen * cen, hm) * inv
    br_rt = cen * lax.rsqrt(var + EPS) * vec_ref[1:2, :] * _silu(zrt_ref[...])
    advance_routing()

    vx = vx_ref[...]
    br_hy = (yc_ref[...] + vx * vec_ref[3:4, :]) * x0_ref[...]

    y5 = _gelu_tanh(s5f_ref[...] + s5b_ref[...] + zs5_ref[...] * vec_ref[2:3, :]).astype(BF16)
    br_s5 = (jnp.dot(y5, glu1_ref[...], preferred_element_type=F32)
             * _sigmoid(jnp.dot(y5, glu2_ref[...], preferred_element_type=F32)))
    advance_routing()

    merged = jnp.zeros((TM, d), F32)
    for n, (br, wg_ref) in enumerate(zip((br_hg, br_hy, br_rt, br_s5), (wg0_ref, wg1_ref, wg2_ref, wg3_ref))):
        gate = _sigmoid(jnp.dot(h, wg_ref[...], preferred_element_type=F32))
        merged = merged + gate * jnp.dot(br.astype(BF16), wb_ref[n], preferred_element_type=F32)
        advance_routing()
    for _ in routing:
        pass
    mix = jnp.dot(merged.astype(BF16), wo_ref[...], preferred_element_type=F32)
    x_mid = x + mod[:, 2 * d:3 * d] * mix
    xo_ref[...] = x_mid
    h2 = _modulated_norm(x_mid, n2_ref[...], mod[:, 3 * d:4 * d], mod[:, 4 * d:5 * d])
    h2_ref[...] = h2

    lg_s[...] = _dot3(h2, wr_ref[...]) + br_ref[...]


def merge(x_all, mod3, n1, n2, w_in_bf, layer, z, hgf, hgb, rtf, rtb, s5f, s5b, yconv, vx, x0c,
          vecs, glu1, glu2, w_branch, w_out, w_router, b_router, bsz, tpb):
    m, d = x_all.shape
    w = BRANCH_W
    nt = m // TM
    cur = lambda i: jnp.minimum(i, nt - 1)
    prev = lambda i: jnp.maximum(i - 1, 0)
    row = lambda width: pl.BlockSpec((TM, width), lambda i: (cur(i), 0))
    srow = pl.BlockSpec((TM, w), lambda i: (_split_tile(cur(i), bsz, tpb), 0))
    zcol = lambda cb: pl.BlockSpec((TM, w), lambda i: (cur(i), cb))
    full = lambda shape: pl.BlockSpec(shape, lambda i: (0,) * len(shape))
    gate_w = lambda n: pl.BlockSpec((None, d, d), lambda i: (layer, 0, N_MIX // d + n))
    return pl.pallas_call(
        functools.partial(_merge_kernel, d=d),
        grid=(nt + 1,),
        in_specs=[row(d),
                  pl.BlockSpec((1, 1, 6 * d), lambda i: (_seg_of_tile(cur(i), tpb), 0, 0)),
                  full((1, d)), full((1, d)), gate_w(0), gate_w(1), gate_w(2), gate_w(3),
                  zcol(CB_HG_G), zcol(CB_RT_G), zcol(CB_S5)]
                 + [row(w)] * 6 + [srow, srow, row(w)]
                 + [full((SUBLANES, w)), full((w, w)), full((w, w)), full((N_BRANCH, w, d)), full((d, d)),
                    full((d, LANES)), full((1, LANES))],
        out_specs=[row(d), row(d),
                   pl.BlockSpec((TM, LANES), lambda i: (prev(i), 0)),
                   pl.BlockSpec((None, SUBLANES, TM), lambda i: (prev(i), 0, 0)),
                   full((SUBLANES, LANES))],
        out_shape=[jax.ShapeDtypeStruct((m, d), F32), jax.ShapeDtypeStruct((m, d), F32),
                   jax.ShapeDtypeStruct((m, LANES), F32), jax.ShapeDtypeStruct((nt, SUBLANES, TM), F32),
                   jax.ShapeDtypeStruct((SUBLANES, LANES), F32)],
        scratch_shapes=[pltpu.VMEM((1, LANES), F32), pltpu.VMEM((TM, LANES), F32)],
        compiler_params=_cparams(("arbitrary",)),
        name="merge",
    )(x_all, mod3, n1, n2, w_in_bf, w_in_bf, w_in_bf, w_in_bf, z, z, z,
      hgf, hgb, rtf, rtb, s5f, s5b, yconv, vx, x0c,
      vecs, glu1, glu2, w_branch, w_out, w_router, b_router)


DMA_UNROLL = 8


def _row_copy(src_ref, src_row, dst_ref, dst_row, sem):
    return pltpu.make_async_copy(src_ref.at[pl.ds(src_row, 1), :], dst_ref.at[pl.ds(dst_row, 1), :], sem)


DISPATCH_ROWS = 6 * TM


def _dispatch_kernel(dest_ref, h_ref, zero_ref, buf_ref, sem, *, m):
    del zero_ref
    rows = h_ref.shape[0]
    base = pl.program_id(0) * rows

    def issue(r, carry):
        for k in range(2):
            _row_copy(h_ref, r, buf_ref, dest_ref[k * m + base + r], sem).start(priority=k)
        return carry

    lax.fori_loop(0, rows, issue, 0, unroll=DMA_UNROLL)
    for k in range(2):
        pltpu.make_async_copy(h_ref, buf_ref.at[pl.ds(0, rows), :], sem).wait()


def moe_dispatch(dest, h2, n_rows):
    m, d = h2.shape
    assert m % DISPATCH_ROWS == 0
    grid_spec = pltpu.PrefetchScalarGridSpec(
        num_scalar_prefetch=1,
        grid=(m // DISPATCH_ROWS,),
        in_specs=[pl.BlockSpec((DISPATCH_ROWS, d), lambda i, dest: (i, 0)),
                  pl.BlockSpec(memory_space=pl.ANY)],
        out_specs=pl.BlockSpec(memory_space=pl.ANY),
        scratch_shapes=[pltpu.SemaphoreType.DMA(())],
    )
    return pl.pallas_call(
        functools.partial(_dispatch_kernel, m=m),
        grid_spec=grid_spec,
        out_shape=jax.ShapeDtypeStruct((n_rows, d), F32),
        input_output_aliases={2: 0},
        compiler_params=_cparams(("arbitrary",)),
        name="moe_dispatch",
    )(dest, h2, jnp.zeros((n_rows, d), F32))


def _expert_kernel(plan_ref, x_ref, w1_ref, w3_ref, w2_ref, y_ref, *, n_blocks):
    used = pl.program_id(0) < plan_ref[n_blocks]

    @pl.when(used)
    def _():
        xb = x_ref[...].astype(BF16)
        a = jnp.dot(xb, w1_ref[...].astype(BF16), preferred_element_type=F32)
        g = jnp.dot(xb, w3_ref[...].astype(BF16), preferred_element_type=F32)
        y_ref[...] = jnp.dot((_silu(a) * g).astype(BF16), w2_ref[...].astype(BF16),
                             preferred_element_type=F32)

    @pl.when(jnp.logical_not(used))
    def _():
        y_ref[...] = jnp.zeros_like(y_ref)


def moe_experts(plan, buf, w1, w3, w2, layer):
    n_rows, d = buf.shape
    hid = w1.shape[-1]
    n_blocks = n_rows // MOE_BLOCK
    blk = lambda j, plan: jnp.minimum(j, plan[n_blocks] - 1)
    grid_spec = pltpu.PrefetchScalarGridSpec(
        num_scalar_prefetch=1,
        grid=(n_blocks,),
        in_specs=[pl.BlockSpec((MOE_BLOCK, d), lambda j, plan: (blk(j, plan), 0)),
                  pl.BlockSpec((None, None, d, hid), lambda j, plan: (layer, plan[blk(j, plan)], 0, 0)),
                  pl.BlockSpec((None, None, d, hid), lambda j, plan: (layer, plan[blk(j, plan)], 0, 0)),
                  pl.BlockSpec((None, None, hid, d), lambda j, plan: (layer, plan[blk(j, plan)], 0, 0))],
        out_specs=pl.BlockSpec((MOE_BLOCK, d), lambda j, plan: (j, 0)),
    )
    return pl.pallas_call(
        functools.partial(_expert_kernel, n_blocks=n_blocks),
        grid_spec=grid_spec,
        out_shape=jax.ShapeDtypeStruct((n_rows, d), F32),
        compiler_params=_cparams(("arbitrary",)),
        name="moe_experts",
    )(plan, buf, w1, w3, w2)


def _combine_kernel(dest_ref, x_ref, route_ref, mod_ref, gfin_ref, y_ref, o_ref, gath, sem, *,
                    d, m, bsz, tpb, nr, final):
    step = pl.program_id(0) * nr + pl.program_id(1)
    slot = step % 2

    def start_gather(s, sl):
        base = ((s // nr) * tpb + s % nr) * TM

        def issue(r, carry):
            for k in range(2):
                _row_copy(y_ref, dest_ref[k * m + base + r], gath.at[sl, k], r, sem.at[sl]).start(priority=k)
            return carry

        lax.fori_loop(0, TM, issue, 0, unroll=DMA_UNROLL)

    @pl.when(step == 0)
    def _():
        start_gather(0, 0)

    @pl.when(step + 1 < bsz * nr)
    def _():
        start_gather(step + 1, 1 - slot)

    for k in range(2):
        pltpu.make_async_copy(y_ref.at[pl.ds(0, TM), :], gath.at[slot, k], sem.at[slot]).wait()
    route = route_ref[...]
    lane = lax.broadcasted_iota(jnp.int32, (TM, LANES), 1)
    moe = (_lane_pick(route, lane, ROUTE_W1) * gath[slot, 0]
           + _lane_pick(route, lane, ROUTE_W2) * gath[slot, 1])
    x_new = x_ref[...] + mod_ref[0][:, 5 * d:6 * d] * moe
    if final:
        ms = jnp.mean(x_new * x_new, axis=-1, keepdims=True)
        x_new = x_new * lax.rsqrt(ms + EPS) * gfin_ref[...]
    o_ref[...] = x_new


def moe_combine(dest, x_mid, route, mod3, gain_final, ybuf, bsz, tpb, final):
    m, d = x_mid.shape
    nr = tpb - 1 if final else tpb
    tile = lambda b, r: b * tpb + r
    grid_spec = pltpu.PrefetchScalarGridSpec(
        num_scalar_prefetch=1,
        grid=(bsz, nr),
        in_specs=[pl.BlockSpec((TM, d), lambda b, r, dest: (tile(b, r), 0)),
                  pl.BlockSpec((TM, LANES), lambda b, r, dest: (tile(b, r), 0)),
                  pl.BlockSpec((1, 1, 6 * d), lambda b, r, dest: (_seg_of_tile(tile(b, r), tpb), 0, 0)),
                  pl.BlockSpec((1, d), lambda b, r, dest: (0, 0)),
                  pl.BlockSpec(memory_space=pl.ANY)],
        out_specs=pl.BlockSpec((TM, d), lambda b, r, dest: (b * nr + r, 0)),
        scratch_shapes=[pltpu.VMEM((2, 2, TM, d), F32), pltpu.SemaphoreType.DMA((2,))],
    )
    return pl.pallas_call(
        functools.partial(_combine_kernel, d=d, m=m, bsz=bsz, tpb=tpb, nr=nr, final=final),
        grid_spec=grid_spec,
        out_shape=jax.ShapeDtypeStruct((bsz * nr * TM, d), F32),
        compiler_params=_cparams(("arbitrary", "arbitrary")),
        name="moe_combine",
    )(dest, x_mid, route, mod3, gain_final, ybuf)


def moe_plan(rt, counts_row, n_blocks):
    counts = counts_row[:MOE_EXPERTS].astype(jnp.int32)
    padded = (counts + MOE_BLOCK - 1) // MOE_BLOCK * MOE_BLOCK
    pad_end = jnp.cumsum(padded)
    pad_start = pad_end - padded
    by_choice = lambda f: jnp.transpose(rt[:, f:f + 2, :], (1, 0, 2)).reshape(2, -1).astype(jnp.int32)
    expert = by_choice(ROUTE_E1)
    rank = by_choice(ROUTE_R1)
    is_e = expert[None] == jnp.arange(MOE_EXPERTS, dtype=jnp.int32)[:, None, None]
    dest = (jnp.sum(jnp.where(is_e, pad_start[:, None, None], 0), axis=0) + rank).reshape(-1)
    block_row0 = jnp.arange(n_blocks, dtype=jnp.int32) * MOE_BLOCK
    block_expert = jnp.minimum(jnp.sum((pad_end[None, :] <= block_row0[:, None]).astype(jnp.int32), axis=1),
                               MOE_EXPERTS - 1)
    n_used = pad_end[MOE_EXPERTS - 1:] // MOE_BLOCK
    return dest, jnp.concatenate([block_expert, n_used])


def _rope_tables(length, lc):
    half = RET_DK // 2
    freqs = ROPE_BASE ** (-jnp.arange(0, half, 2, dtype=F32) / half)
    t = jnp.arange(length)
    row_pos = (t // GRID_W).astype(F32)[:, None]
    col_pos = (t % GRID_W).astype(F32)[:, None]
    lane = jnp.arange(RET_HEADS * RET_DK)
    fr = freqs[lane % (half // 2)][None, :]
    ang = jnp.where(((lane % RET_DK) < half)[None, :], row_pos, col_pos) * fr
    cos = jnp.concatenate([jnp.cos(ang), jnp.ones((lc, lane.shape[0]), F32)], axis=0)
    sin = jnp.concatenate([jnp.sin(ang), jnp.zeros((lc, lane.shape[0]), F32)], axis=0)
    return cos, sin


def _rotate_half_cols(wq):
    quarter = RET_DK // 4
    lane = np.arange(wq.shape[1])
    first = (lane % (2 * quarter)) < quarter
    src = np.where(first, lane + quarter, lane - quarter)
    sign = jnp.asarray(np.where(first, -1.0, 1.0), F32)
    return wq[:, src] * sign


def kernel(x, c, ctx, c_ctx, w_mod, b_mod, norm1, norm2, w_in, hg_lb_logits, hg_norm, hy_conv_w, hy_conv_b, hy_filt_w1, hy_filt_b1, hy_filt_freq1, hy_filt_w2, hy_filt_b2, hy_filt_freq2, hy_filt_w3, hy_skip, ret_decay_logit, ret_norm, s5_lam_re, s5_lam_im, s5_log_dt, s5_b_re, s5_b_im, s5_c_re, s5_c_im, s5_d, s5_glu_w1, s5_glu_w2, w_branch, w_out, moe_w_grp, moe_b_grp, moe_w_exp, moe_b_exp, moe_w1, moe_w3, moe_w2, norm_final):
    bsz, length, d = x.shape
    lc = ctx.shape[1]
    depth = w_mod.shape[0]
    assert lc == TM and length % TM == 0 and (2 * length) % (FFT_FAST * FFT_GROUP) == 0
    tt = length + lc
    tpb = tt // TM
    m = bsz * tt
    n_blocks = -(-(m * 2) // MOE_BLOCK) + MOE_EXPERTS
    w = BRANCH_W

    x_all = jnp.concatenate([x, ctx], axis=1).reshape(m, d)
    cmat = jnp.zeros((SUBLANES, d), F32).at[0].set(c_ctx).at[1:1 + bsz].set(c)
    mods = modvec(cmat, w_mod, b_mod)
    lb_sm = jax.nn.softmax(hg_lb_logits.astype(F32), axis=0)
    lb_all = jnp.cumsum(lb_sm, axis=0) - lb_sm[0]
    cos_tab, sin_tab = _rope_tables(length, lc)
    log_gamma = jax.nn.log_sigmoid(ret_decay_logit.astype(F32))
    gfin = norm_final.reshape(1, d)

    w_in_bf = w_in.astype(BF16)
    rq0 = 5 * w + 3 * HY_W
    nq = RET_HEADS * RET_DK

    for l in range(depth):
        w_rot = jnp.concatenate([_rotate_half_cols(w_in[l, :, rq0:rq0 + nq]),
                                 _rotate_half_cols(w_in[l, :, rq0 + nq:rq0 + 2 * nq])], axis=1).astype(BF16)
        mod3 = mods[l].reshape(SUBLANES, 1, 6 * d)

        z = inproj(x_all, mod3, norm1[l].reshape(1, d), w_in_bf, l, w_rot, tpb)
        hgf, hgb = hgrn(z, lb_all[l], bsz, tpb)
        rtf, rtb = retention(z, cos_tab, sin_tab, log_gamma[l], bsz, tpb)
        s5f, s5b = s5g(z, *s5g_params(s5_lam_re[l], s5_lam_im[l], s5_log_dt[l], s5_b_re[l], s5_b_im[l],
                                      s5_c_re[l], s5_c_im[l]), bsz, tpb)

        x0c, vx = hy_prep(z, hy_conv_w[l], hy_conv_b[l], bsz, tpb)
        fargs = (hy_filt_w1[l], hy_filt_b1[l], hy_filt_freq1[l], hy_filt_w2[l], hy_filt_b2[l],
                 hy_filt_freq2[l], hy_filt_w3[l])
        hfb, ss = hy_filter(length, *fargs)
        y_lat = hyena_long_conv(vx, bsz, length, hfb, lax.rsqrt(ss + EPS))
        hfbc, ssc = hy_filter(lc, *fargs)
        gwin = jnp.concatenate([hfbc[1, 1:][::-1], hfbc[0], jnp.zeros((1, HY_W), F32)], axis=0) \
            * lax.rsqrt(ssc + EPS)
        yconv = hy_ctx_conv(vx, gwin, y_lat, bsz, tpb, lc)

        vecs = jnp.zeros((SUBLANES, w), F32).at[0].set(hg_norm[l]).at[1].set(ret_norm[l]) \
            .at[2].set(s5_d[l]).at[3].set(hy_skip[l])
        w_router = jnp.zeros((d, LANES), F32).at[:, :MOE_GROUPS].set(moe_w_grp[l]) \
            .at[:, MOE_GROUPS:MOE_GROUPS + MOE_EXPERTS].set(moe_w_exp[l])
        b_router = jnp.zeros((1, LANES), F32).at[0, :MOE_GROUPS].set(moe_b_grp[l]) \
            .at[0, MOE_GROUPS:MOE_GROUPS + MOE_EXPERTS].set(moe_b_exp[l])
        x_mid, h2, route, rt, counts = merge(
            x_all, mod3, norm1[l].reshape(1, d), norm2[l].reshape(1, d), w_in_bf, l, z,
            hgf, hgb, rtf, rtb, s5f, s5b, yconv, vx, x0c, vecs,
            s5_glu_w1[l].astype(BF16), s5_glu_w2[l].astype(BF16), w_branch[l].astype(BF16),
            w_out[l].astype(BF16), w_router, b_router, bsz, tpb)

        dest, block_expert = moe_plan(rt, counts[0], n_blocks)
        buf = moe_dispatch(dest, h2, n_blocks * MOE_BLOCK)
        ybuf = moe_experts(block_expert, buf, moe_w1, moe_w3, moe_w2, l)
        x_all = moe_combine(dest, x_mid, route, mod3, gfin, ybuf, bsz, tpb, l == depth - 1)

    return x_all.reshape(bsz, length, d)
```

```python
import functools
import math

import numpy as np
import jax
import jax.numpy as jnp
from jax import lax
from jax.experimental import pallas as pl
from jax.experimental.pallas import tpu as pltpu

F32 = jnp.float32
BF16 = jnp.bfloat16
HIGHEST = lax.Precision.HIGHEST

GRID_W = 64
N_BRANCH = 4
BRANCH_W = 256
HG_HEADS = 4
HG_CHUNK = 16
HY_W = 256
HY_EMB = 33
HY_BANDS = 16
HY_ORDER = 64
HY_FAST_DECAY = 0.3
HY_SLOW_DECAY = 1.5
HY_TARGET = 1e-2
RET_HEADS = 4
RET_DK = 32
RET_DV = 64
ROPE_BASE = 10000.0
S5_GROUP_CH = 16
S5_GROUPS = 16
S5_STATE = 64
MOE_GROUPS = 4
MOE_PER_GROUP = 8
MOE_EXPERTS = 32
MOE_HIDDEN = 512
MOE_BLOCK = 256
EPS = 1e-6

LANES = 128
SUBLANES = 8
TM = 256
FFT_FAST = 128
VMEM_LIMIT = 56 * 1024 * 1024

N_MIX = 3072
N_ZCOLS = N_MIX + 2 * RET_HEADS * RET_DK
CB_HG_Q, CB_HG_FF, CB_HG_FB, CB_HG_I, CB_HG_G = 3, 4, 5, 6, 7
CB_RT_V, CB_RT_G, CB_S5 = 9, 10, 11
CB128_RT_Q, CB128_RT_K, CB128_RT_QR, CB128_RT_KR = 16, 17, 24, 25


def _cparams(sem):
    return pltpu.CompilerParams(dimension_semantics=sem, vmem_limit_bytes=VMEM_LIMIT)


def _sigmoid(x):
    return 1.0 / (1.0 + jnp.exp(-x))


def _silu(x):
    return x * _sigmoid(x)


def _dot(a, b):
    return jnp.dot(a.astype(BF16), b.astype(BF16), preferred_element_type=F32)


def _dot_nt(a, b):
    return lax.dot_general(a.astype(BF16), b.astype(BF16), (((1,), (1,)), ((), ())),
                           preferred_element_type=F32)


def _dot_hi(a, b):
    return jnp.dot(a, b, precision=HIGHEST, preferred_element_type=F32)


def _dot3(a, b):
    a_hi = a.astype(BF16)
    a_lo = (a - a_hi.astype(F32)).astype(BF16)
    b_hi = b.astype(BF16)
    b_lo = (b - b_hi.astype(F32)).astype(BF16)
    return (jnp.dot(a_hi, b_hi, preferred_element_type=F32)
            + jnp.dot(a_lo, b_hi, preferred_element_type=F32)
            + jnp.dot(a_hi, b_lo, preferred_element_type=F32))


def _split_dot(x, m_bf16):
    hi = x.astype(BF16)
    lo = (x - hi.astype(F32)).astype(BF16)
    return (jnp.dot(hi, m_bf16, preferred_element_type=F32)
            + jnp.dot(lo, m_bf16, preferred_element_type=F32))


def _split_dot_left(m_bf16, x):
    hi = x.astype(BF16)
    lo = (x - hi.astype(F32)).astype(BF16)
    return (jnp.dot(m_bf16, hi, preferred_element_type=F32)
            + jnp.dot(m_bf16, lo, preferred_element_type=F32))


def _seg_of_tile(i, tpb):
    return jnp.where(i % tpb == tpb - 1, 0, 1 + i // tpb)


def _split_tile(i, bsz, tpb):
    b, r = i // tpb, i % tpb
    return jnp.where(r == tpb - 1, bsz * (tpb - 1) + b, b * (tpb - 1) + r)


def _scan_tiles(b, s, tpb):
    last = tpb - 1
    fwd = jnp.where(s == 0, last, s - 1)
    bwd = jnp.where(s == 0, last, last - s)
    return b * tpb + fwd, b * tpb + bwd


def _modvec_kernel(c_ref, w_ref, b_ref, o_ref):
    c = c_ref[...]
    o_ref[0] = _dot_hi(_silu(c), w_ref[0]) + b_ref[0]


def modvec(cmat, w_mod, b_mod):
    depth, d, n6 = w_mod.shape
    nb = n6 // d
    return pl.pallas_call(
        _modvec_kernel,
        grid=(depth, nb),
        in_specs=[pl.BlockSpec((SUBLANES, d), lambda l, j: (0, 0)),
                  pl.BlockSpec((1, d, d), lambda l, j: (l, 0, j)),
                  pl.BlockSpec((1, 1, d), lambda l, j: (l, 0, j))],
        out_specs=pl.BlockSpec((1, SUBLANES, d), lambda l, j: (l, 0, j)),
        out_shape=jax.ShapeDtypeStruct((depth, SUBLANES, n6), F32),
        compiler_params=_cparams(("arbitrary", "arbitrary")),
        name="modvec",
    )(cmat, w_mod, b_mod.reshape(depth, 1, n6))


def _modulated_norm(x, gain, shift, scale):
    ms = jnp.mean(x * x, axis=-1, keepdims=True)
    return (x * lax.rsqrt(ms + EPS) * gain) * (1.0 + scale) + shift


def _inproj_kernel(x_ref, mod_ref, g_ref, w_ref, wrot_ref, z_ref, *, d):
    h = _modulated_norm(x_ref[...], g_ref[...], mod_ref[0, :, 0:d], mod_ref[0, :, d:2 * d]).astype(BF16)
    hy0 = 5 * BRANCH_W
    hy1 = hy0 + 3 * HY_W
    z_ref[:, 0:hy1 - hy0] = jnp.dot(h, w_ref[:, hy0:hy1], preferred_element_type=F32)
    z_ref[:, hy1 - hy0:hy1] = jnp.dot(h, w_ref[:, 0:hy0], preferred_element_type=F32)
    z_ref[:, hy1:N_MIX] = jnp.dot(h, w_ref[:, hy1:N_MIX], preferred_element_type=F32)
    z_ref[:, N_MIX:N_ZCOLS] = jnp.dot(h, wrot_ref[...], preferred_element_type=F32)


def inproj(x_all, mod3, gain, w_in_bf, layer, w_rot, tpb):
    m, d = x_all.shape
    return pl.pallas_call(
        functools.partial(_inproj_kernel, d=d),
        grid=(m // TM,),
        in_specs=[pl.BlockSpec((TM, d), lambda i: (i, 0)),
                  pl.BlockSpec((1, 1, 6 * d), lambda i: (_seg_of_tile(i, tpb), 0, 0)),
                  pl.BlockSpec((1, d), lambda i: (0, 0)),
                  pl.BlockSpec((None, d, N_MIX), lambda i: (layer, 0, 0)),
                  pl.BlockSpec((d, N_ZCOLS - N_MIX), lambda i: (0, 0))],
        out_specs=pl.BlockSpec((TM, N_ZCOLS), lambda i: (i, 0)),
        out_shape=jax.ShapeDtypeStruct((m, N_ZCOLS), F32),
        compiler_params=_cparams(("parallel",)),
        name="inproj",
    )(x_all, mod3, gain, w_in_bf, w_rot)


def _head_masked_rows(x, nchunk, c, hd):
    w = x.shape[-1]
    x3 = x.reshape(nchunk, c, w)
    lane_head = lax.broadcasted_iota(jnp.int32, (1, 1, w), 2) // hd
    parts = [jnp.where(lane_head == h, x3, 0.0) for h in range(w // hd)]
    return jnp.concatenate(parts, axis=1).reshape(nchunk * (w // hd) * c, w)


def _hgrn_direction(q_ref, f_ref, v_ref, lb, tri_ref, o_ref, st_ref, b_s, k_s, qe_s, ke_s, reverse):
    c = HG_CHUNK
    w = BRANCH_W
    nchunk = TM // c
    nh = HG_HEADS
    hd = w // nh
    half = c // 2
    z = f_ref[...]
    sg = _sigmoid(z)
    log2_f = jnp.log2(lb + (1.0 - lb) * sg)
    k = (1.0 - lb) * (1.0 - sg)
    b2 = _split_dot_left(tri_ref[0], log2_f)
    rem2 = _split_dot_left(tri_ref[1], log2_f)
    b_s[...] = b2
    k_s[...] = k
    qe_s[...] = _head_masked_rows(q_ref[...] * jnp.exp2(b2), nchunk, c, hd).astype(BF16)
    ke_s[...] = _head_masked_rows(k * jnp.exp2(rem2), nchunk, c, hd).astype(BF16)

    head_r = lax.broadcasted_iota(jnp.int32, (w, w), 0) // hd
    head_c = lax.broadcasted_iota(jnp.int32, (w, w), 1) // hd
    hm = jnp.where(head_r == head_c, 1.0, 0.0).astype(BF16)
    rows = lax.broadcasted_iota(jnp.int32, (half, 1), 0)

    pieces = []
    for s in range(c):
        for g in range(2):
            lo, hi = g * half, (g + 1) * half - 1
            if (hi < s) if not reverse else (lo > s):
                continue
            full = (lo >= s) if not reverse else (hi <= s)
            pieces.append((s, g, full))

    def chunk_step(n):
        cidx = (nchunk - 1 - n) if reverse else n
        off = pl.multiple_of(cidx * c, c)
        eoff = pl.multiple_of(cidx * (nh * c), nh * c)
        qc = q_ref[pl.ds(off, c), :]
        bc = b_s[pl.ds(off, c), :]
        kc = k_s[pl.ds(off, c), :]
        vc = v_ref[pl.ds(off, c), :]
        parts = []
        for s, g, full in pieces:
            sl = slice(g * half, (g + 1) * half)
            e = jnp.exp2(bc[sl] - bc[s:s + 1, :])
            if not full:
                t = rows + g * half
                e = jnp.where((t <= s) if reverse else (t >= s), e, 0.0)
            parts.append((qc[sl] * kc[s:s + 1, :]) * e)
        p = jnp.concatenate(parts, axis=0).astype(BF16)
        r = jnp.dot(p, hm, preferred_element_type=F32)
        og = [jnp.zeros((half, w), F32), jnp.zeros((half, w), F32)]
        for i, (s, g, _) in enumerate(pieces):
            og[g] = og[g] + r[i * half:(i + 1) * half, :] * vc[s:s + 1, :]
        st = st_ref[...]
        ri = _dot_nt(qe_s[pl.ds(eoff, nh * c), :], st)
        inter = jnp.concatenate([ri[h * c:(h + 1) * c, :] for h in range(nh)], axis=1)
        o_ref[pl.ds(off, c), :] = jnp.concatenate(og, axis=0) + inter
        vexp = jnp.concatenate([vc[:, h * hd:(h + 1) * hd] for h in range(nh)], axis=0)
        kv = lax.dot_general(vexp.astype(BF16), ke_s[pl.ds(eoff, nh * c), :], (((0,), (0,)), ((), ())),
                             preferred_element_type=F32)
        blast = bc[0:1, :] if reverse else bc[c - 1:c, :]
        st_ref[...] = jnp.exp2(blast) * st + kv

    return chunk_step


def _hgrn_kernel(qf_ref, ff_ref, vf_ref, qb_ref, fb_ref, vb_ref, lb_ref, trif_ref, trib_ref, of_ref, ob_ref,
                 stf_ref, stb_ref, bf_s, kf_s, qef_s, kef_s, bb_s, kb_s, qeb_s, keb_s):
    @pl.when(pl.program_id(1) == 0)
    def _():
        stf_ref[...] = jnp.zeros_like(stf_ref)
        stb_ref[...] = jnp.zeros_like(stb_ref)

    step_f = _hgrn_direction(qf_ref, ff_ref, vf_ref, lb_ref[0:1, :], trif_ref, of_ref, stf_ref,
                             bf_s, kf_s, qef_s, kef_s, False)
    step_b = _hgrn_direction(qb_ref, fb_ref, vb_ref, lb_ref[1:2, :], trib_ref, ob_ref, stb_ref,
                             bb_s, kb_s, qeb_s, keb_s, True)

    def body(n, carry):
        step_f(n)
        step_b(n)
        return carry

    lax.fori_loop(0, TM // HG_CHUNK, body, 0, unroll=16)


def hgrn(z, lb, bsz, tpb):
    m = z.shape[0]
    w = BRANCH_W
    hd = w // HG_HEADS
    ri = jnp.arange(TM)[:, None]
    ci = jnp.arange(TM)[None, :]
    same = (ri // HG_CHUNK) == (ci // HG_CHUNK)
    tri = jnp.stack([same & (ci <= ri), same & (ci > ri), same & (ci >= ri), same & (ci < ri)])
    tri = tri.astype(BF16).reshape(2, 2, TM, TM)

    def spec(cb, d):
        return pl.BlockSpec((TM, w), lambda b, s: (_scan_tiles(b, s, tpb)[d], cb))

    out_spec = [pl.BlockSpec((TM, w), lambda b, s: (_scan_tiles(b, s, tpb)[0], 0)),
                pl.BlockSpec((TM, w), lambda b, s: (_scan_tiles(b, s, tpb)[1], 0))]
    return pl.pallas_call(
        _hgrn_kernel,
        grid=(bsz, tpb),
        in_specs=[spec(CB_HG_Q, 0), spec(CB_HG_FF, 0), spec(CB_HG_I, 0),
                  spec(CB_HG_Q, 1), spec(CB_HG_FB, 1), spec(CB_HG_I, 1),
                  pl.BlockSpec((2, w), lambda b, s: (0, 0)),
                  pl.BlockSpec((None, 2, TM, TM), lambda b, s: (0, 0, 0, 0)),
                  pl.BlockSpec((None, 2, TM, TM), lambda b, s: (1, 0, 0, 0))],
        out_specs=out_spec,
        out_shape=[jax.ShapeDtypeStruct((m, w), F32)] * 2,
        scratch_shapes=[pltpu.VMEM((hd, w), F32), pltpu.VMEM((hd, w), F32)]
                       + [pltpu.VMEM((TM, w), F32), pltpu.VMEM((TM, w), F32),
                          pltpu.VMEM((HG_HEADS * TM, w), BF16), pltpu.VMEM((HG_HEADS * TM, w), BF16)] * 2,
        compiler_params=_cparams(("arbitrary", "arbitrary")),
        name="hgrn2",
    )(z, z, z, z, z, z, lb, tri, tri)


def _ret_direction(q_ref, k_ref, qr_ref, kr_ref, v_ref, cos_ref, sin_ref, lgq, lg_ref, d, o_ref, st_ref,
                   reverse):
    nq = RET_HEADS * RET_DK
    nv = RET_HEADS * RET_DV
    cos = cos_ref[...]
    sin = sin_ref[...]
    qx = q_ref[...] * cos + qr_ref[...] * sin
    kx = (k_ref[...] * cos + kr_ref[...] * sin) * (RET_DK ** -0.5)
    v = v_ref[...]
    t = lax.broadcasted_iota(jnp.int32, (TM, 1), 0).astype(F32)
    pos = (TM - 1.0 - t) if reverse else t
    qd = qx * jnp.exp((pos + 1.0) * lgq)
    kd = kx * jnp.exp((TM - 1.0 - pos) * lgq)
    ti = lax.broadcasted_iota(jnp.int32, (TM, TM), 0)
    si = lax.broadcasted_iota(jnp.int32, (TM, TM), 1)
    rel = ((si - ti) if reverse else (ti - si)).astype(F32)
    live = rel >= 0.0
    relc = jnp.where(live, rel, 0.0)
    lane_q = lax.broadcasted_iota(jnp.int32, (1, nq), 1) // RET_DK
    lane_v = lax.broadcasted_iota(jnp.int32, (1, nv), 1) // RET_DV
    st = st_ref[...]
    o = _dot_nt(qd, st)
    vb = v.astype(BF16)
    for h in range(RET_HEADS):
        dmat = jnp.where(live, jnp.exp(relc * lg_ref[d, h]), 0.0)
        sc = _dot_nt(jnp.where(lane_q == h, qx, 0.0), kx) * dmat
        oh = jnp.dot(sc.astype(BF16), vb, preferred_element_type=F32)
        o = o + jnp.where(lane_v == h, oh, 0.0)
    o_ref[...] = o
    kv = _dot(v.T, kd)
    hr = lax.broadcasted_iota(jnp.int32, (nv, nq), 0) // RET_DV
    hc = lax.broadcasted_iota(jnp.int32, (nv, nq), 1) // RET_DK
    st_ref[...] = st * jnp.exp(float(TM) * lgq) + jnp.where(hr == hc, kv, 0.0)


def _ret_kernel(lg_ref, qf, kf, qrf, krf, vf, cosf, sinf, qb, kb, qrb, krb, vb, cosb, sinb, lgq_ref,
                of_ref, ob_ref, stf_ref, stb_ref):
    @pl.when(pl.program_id(1) == 0)
    def _():
        stf_ref[...] = jnp.zeros_like(stf_ref)
        stb_ref[...] = jnp.zeros_like(stb_ref)

    _ret_direction(qf, kf, qrf, krf, vf, cosf, sinf, lgq_ref[0:1, :], lg_ref, 0, of_ref, stf_ref, False)
    _ret_direction(qb, kb, qrb, krb, vb, cosb, sinb, lgq_ref[1:2, :], lg_ref, 1, ob_ref, stb_ref, True)


def retention(z, cos_tab, sin_tab, log_gamma, bsz, tpb):
    m = z.shape[0]
    nq = RET_HEADS * RET_DK
    nv = RET_HEADS * RET_DV
    lgq = jnp.repeat(log_gamma, RET_DK, axis=1)

    def zspec(cb, width, d):
        return pl.BlockSpec((TM, width), lambda b, s: (_scan_tiles(b, s, tpb)[d], cb))

    def tspec(d):
        return pl.BlockSpec((TM, nq), lambda b, s: (_scan_tiles(0, s, tpb)[d], 0))

    def side(d):
        return [zspec(CB128_RT_Q, nq, d), zspec(CB128_RT_K, nq, d), zspec(CB128_RT_QR, nq, d),
                zspec(CB128_RT_KR, nq, d), zspec(CB_RT_V, nv, d), tspec(d), tspec(d)]

    return pl.pallas_call(
        _ret_kernel,
        grid=(bsz, tpb),
        in_specs=[pl.BlockSpec(memory_space=pltpu.SMEM)] + side(0) + side(1)
                 + [pl.BlockSpec((2, nq), lambda b, s: (0, 0))],
        out_specs=[pl.BlockSpec((TM, nv), lambda b, s: (_scan_tiles(b, s, tpb)[0], 0)),
                   pl.BlockSpec((TM, nv), lambda b, s: (_scan_tiles(b, s, tpb)[1], 0))],
        out_shape=[jax.ShapeDtypeStruct((m, nv), F32)] * 2,
        scratch_shapes=[pltpu.VMEM((nv, nq), F32), pltpu.VMEM((nv, nq), F32)],
        compiler_params=_cparams(("arbitrary", "arbitrary")),
        name="retention",
    )(log_gamma, z, z, z, z, z, cos_tab, sin_tab, z, z, z, z, z, cos_tab, sin_tab, lgq)


S5_NP = S5_GROUPS * S5_STATE


S5_T = SUBLANES
S5_ROW_APPLY, S5_ROW_END, S5_ROW_GROUP, S5_ROW_STEP = 0, 8, 16, 24


def _s5g_direction(u_ref, bre_ref, bim_ref, cre_ref, cim_ref, kt_ref, tab_ref, o_ref, car_ref, reverse):
    ng = TM // S5_T
    nsg = ng // SUBLANES
    np_ = S5_NP
    w = BRANCH_W
    sub = lax.broadcasted_iota(jnp.int32, (1, SUBLANES, 1), 1)

    def neighbour(x3, k):
        shift = (SUBLANES - k) if reverse else k
        keep = (sub < SUBLANES - k) if reverse else (sub >= k)
        return jnp.where(keep, pltpu.roll(x3, shift, 1), 0.0)

    u3 = u_ref[...]
    u2 = u3.reshape(TM, w)
    ub = u2.astype(BF16)
    y = jnp.dot(ub, kt_ref[0, 0], preferred_element_type=F32)
    for tau in range(1, S5_T):
        y = y + jnp.dot(neighbour(u3, tau).reshape(TM, w).astype(BF16), kt_ref[0, tau],
                        preferred_element_type=F32)

    bur = jnp.dot(ub, bre_ref[0], preferred_element_type=F32).reshape(ng, S5_T, np_)
    bui = jnp.dot(ub, bim_ref[0], preferred_element_type=F32).reshape(ng, S5_T, np_)
    wr = tab_ref[0, 0, S5_ROW_END:S5_ROW_END + S5_T, :]
    wi = tab_ref[0, 1, S5_ROW_END:S5_ROW_END + S5_T, :]
    gsel = (lax.broadcasted_iota(jnp.int32, (ng, TM), 1) // S5_T
            == lax.broadcasted_iota(jnp.int32, (ng, TM), 0))
    gsel = jnp.where(gsel, 1.0, 0.0).astype(BF16)
    er = jnp.dot(gsel, (wr * bur - wi * bui).reshape(TM, np_).astype(BF16), preferred_element_type=F32)
    ei = jnp.dot(gsel, (wr * bui + wi * bur).reshape(TM, np_).astype(BF16), preferred_element_type=F32)

    sr = er.reshape(nsg, SUBLANES, np_)
    si = ei.reshape(nsg, SUBLANES, np_)
    for n, k in enumerate((1, 2, 4)):
        ar = tab_ref[0, 0, S5_ROW_STEP + n:S5_ROW_STEP + n + 1, :]
        ai = tab_ref[0, 1, S5_ROW_STEP + n:S5_ROW_STEP + n + 1, :]
        nr, ni = neighbour(sr, k), neighbour(si, k)
        sr, si = sr + ar * nr - ai * ni, si + ar * ni + ai * nr
    pr = tab_ref[0, 0, S5_ROW_GROUP:S5_ROW_GROUP + SUBLANES, :]
    pi = tab_ref[0, 1, S5_ROW_GROUP:S5_ROW_GROUP + SUBLANES, :]
    cin_r, cin_i = car_ref[0:1, :], car_ref[1:2, :]
    cr, ci = cin_r, cin_i
    blocks_r, blocks_i = [None] * nsg, [None] * nsg
    for sg in (reversed(range(nsg)) if reverse else range(nsg)):
        br = sr[sg] + pr * cr - pi * ci
        bi = si[sg] + pr * ci + pi * cr
        blocks_r[sg], blocks_i[sg] = br, bi
        cr, ci = (br[0:1, :], bi[0:1, :]) if reverse else (br[SUBLANES - 1:, :], bi[SUBLANES - 1:, :])
    car_ref[0:1, :] = cr
    car_ref[1:2, :] = ci
    end_r = jnp.concatenate(blocks_r, axis=0)
    end_i = jnp.concatenate(blocks_i, axis=0)
    grow = lax.broadcasted_iota(jnp.int32, (ng, 1), 0)
    edge = (ng - 1) if reverse else 0
    shift = (ng - 1) if reverse else 1
    pre_r = jnp.where(grow == edge, cin_r, pltpu.roll(end_r, shift, 0))
    pre_i = jnp.where(grow == edge, cin_i, pltpu.roll(end_i, shift, 0))

    ar = tab_ref[0, 0, S5_ROW_APPLY:S5_ROW_APPLY + S5_T, :][None]
    ai = tab_ref[0, 1, S5_ROW_APPLY:S5_ROW_APPLY + S5_T, :][None]
    pre_r3 = pre_r[:, None, :]
    pre_i3 = pre_i[:, None, :]
    xr = (ar * pre_r3 - ai * pre_i3).reshape(TM, np_).astype(BF16)
    xi = (ar * pre_i3 + ai * pre_r3).reshape(TM, np_).astype(BF16)
    o_ref[...] = (y + jnp.dot(xr, cre_ref[0], preferred_element_type=F32)
                  - jnp.dot(xi, cim_ref[0], preferred_element_type=F32))


def _s5g_kernel(uf, ub, bre, bim, cre, cim, ktf, tabf, bre2, bim2, cre2, cim2, ktb, tabb, of_ref, ob_ref,
                carf, carb):
    @pl.when(pl.program_id(1) == 0)
    def _():
        carf[...] = jnp.zeros_like(carf)
        carb[...] = jnp.zeros_like(carb)

    _s5g_direction(uf, bre, bim, cre, cim, ktf, tabf, of_ref, carf, False)
    _s5g_direction(ub, bre2, bim2, cre2, cim2, ktb, tabb, ob_ref, carb, True)


def s5g(z, bblk_re, bblk_im, cblk_re, cblk_im, ktoep, tab, bsz, tpb):
    m = z.shape[0]
    w = BRANCH_W
    ng = TM // S5_T
    z3 = z.reshape(m // S5_T, S5_T, z.shape[1])

    def pspec(shape, d):
        return pl.BlockSpec((1,) + shape, lambda b, s: (d,) + (0,) * len(shape))

    def uspec(d):
        return pl.BlockSpec((ng, S5_T, w), lambda b, s: (_scan_tiles(b, s, tpb)[d], 0, CB_S5))

    def side(d):
        return [pspec((w, S5_NP), d), pspec((w, S5_NP), d), pspec((S5_NP, w), d), pspec((S5_NP, w), d),
                pspec((S5_T, w, w), d), pspec((2, 32, S5_NP), d)]

    return pl.pallas_call(
        _s5g_kernel,
        grid=(bsz, tpb),
        in_specs=[uspec(0), uspec(1)] + side(0) + side(1),
        out_specs=[pl.BlockSpec((TM, w), lambda b, s: (_scan_tiles(b, s, tpb)[0], 0)),
                   pl.BlockSpec((TM, w), lambda b, s: (_scan_tiles(b, s, tpb)[1], 0))],
        out_shape=[jax.ShapeDtypeStruct((m, w), F32)] * 2,
        scratch_shapes=[pltpu.VMEM((2, S5_NP), F32), pltpu.VMEM((2, S5_NP), F32)],
        compiler_params=_cparams(("arbitrary", "arbitrary")),
        name="s5",
    )(z3, z3, bblk_re, bblk_im, cblk_re, cblk_im, ktoep, tab,
      bblk_re, bblk_im, cblk_re, cblk_im, ktoep, tab)


def s5g_params(lam_re, lam_im, log_dt, b_re, b_im, c_re, c_im):
    lr = jnp.minimum(lam_re.astype(F32), -1e-4)
    li = lam_im.astype(F32)
    dt = jnp.exp(log_dt.astype(F32))[..., None]

    def apow(k):
        mag = jnp.exp(k * lr * dt)
        return mag * jnp.cos(k * li * dt), mag * jnp.sin(k * li * dt)

    ar, ai = apow(1.0)
    den = lr * lr + li * li
    gr = ((ar - 1.0) * lr + ai * li) / den
    gi = (ai * lr - (ar - 1.0) * li) / den
    bbr = gr[..., None] * b_re - gi[..., None] * b_im
    bbi = gr[..., None] * b_im + gi[..., None] * b_re
    cr, ci = c_re.astype(F32), c_im.astype(F32)
    eye = jnp.eye(S5_GROUPS, dtype=F32)
    nch = S5_GROUPS * S5_GROUP_CH

    def blk_in(bb):
        return jnp.einsum('dgpc,gh->dgchp', bb, eye).reshape(2, nch, S5_NP).astype(BF16)

    def blk_out(cc):
        return jnp.einsum('dgcp,gh->dgphc', cc, eye).reshape(2, S5_NP, nch).astype(BF16)

    taus = jnp.arange(S5_T, dtype=F32)[:, None, None, None]
    pr, pi = apow(taus)
    car = cr * pr[:, :, :, None, :] - ci * pi[:, :, :, None, :]
    cai = cr * pi[:, :, :, None, :] + ci * pr[:, :, :, None, :]
    k = (jnp.einsum('tdgop,dgpi->dtgio', car, bbr, precision=HIGHEST)
         - jnp.einsum('tdgop,dgpi->dtgio', cai, bbi, precision=HIGHEST))
    ktoep = jnp.einsum('dtgio,gh->dtgiho', k, eye).reshape(2, S5_T, nch, nch).astype(BF16)

    t = S5_T
    j = np.arange(t)
    fwd = np.concatenate([j + 1, t - 1 - j, t * (j + 1), t * np.array([1, 2, 4])])
    bwd = np.concatenate([t - j, j, t * (t - j), t * np.array([1, 2, 4])])
    exps = np.zeros((2, 32), np.float32)
    exps[0, :fwd.size], exps[1, :bwd.size] = fwd, bwd
    used = (np.arange(32) < fwd.size).astype(np.float32)[None, :, None, None]
    e = jnp.asarray(exps)[:, :, None, None]
    mag = jnp.exp(e * (lr * dt)[:, None]) * used
    ang = e * (li * dt)[:, None]
    tab = jnp.stack([(mag * jnp.cos(ang)).reshape(2, 32, S5_NP),
                     (mag * jnp.sin(ang)).reshape(2, 32, S5_NP)], axis=1)
    return blk_in(bbr), blk_in(bbi), blk_out(cr), blk_out(ci), ktoep, tab


def _hy_prep_kernel(u_ref, prev_ref, next_ref, w_ref, b_ref, x0_ref, vx_ref, *, tpb):
    i = pl.program_id(0)
    r = i % tpb
    first = (r == 0) | (r == tpb - 1)
    last = r >= tpb - 2
    u = u_ref[...]
    w = w_ref[...]
    up_row = jnp.where(first, 0.0, prev_ref[SUBLANES - 1:SUBLANES, :])
    dn_row = jnp.where(last, 0.0, next_ref[0:1, :])
    row = lax.broadcasted_iota(jnp.int32, (TM, 1), 0)
    um = jnp.where(row == 0, up_row, pltpu.roll(u, 1, 0))
    un = jnp.where(row == TM - 1, dn_row, pltpu.roll(u, TM - 1, 0))
    y = um * w[0:1, :] + u * w[1:2, :] + un * w[2:3, :] + b_ref[...]
    x0_ref[...] = y[:, 0:HY_W]
    vx_ref[...] = y[:, 2 * HY_W:3 * HY_W] * y[:, HY_W:2 * HY_W]


def hy_prep(z, conv_w, conv_b, bsz, tpb):
    m = z.shape[0]
    w3 = 3 * HY_W
    rb = TM // SUBLANES
    nrb = m // SUBLANES
    return pl.pallas_call(
        functools.partial(_hy_prep_kernel, tpb=tpb),
        grid=(m // TM,),
        in_specs=[pl.BlockSpec((TM, w3), lambda i: (i, 0)),
                  pl.BlockSpec((SUBLANES, w3), lambda i: (jnp.maximum(i * rb - 1, 0), 0)),
                  pl.BlockSpec((SUBLANES, w3), lambda i: (jnp.minimum((i + 1) * rb, nrb - 1), 0)),
                  pl.BlockSpec((3, w3), lambda i: (0, 0)),
                  pl.BlockSpec((1, w3), lambda i: (0, 0))],
        out_specs=[pl.BlockSpec((TM, HY_W), lambda i: (i, 0)),
                   pl.BlockSpec((TM, HY_W), lambda i: (_split_tile(i, bsz, tpb), 0))],
        out_shape=[jax.ShapeDtypeStruct((m, HY_W), F32)] * 2,
        compiler_params=_cparams(("parallel",)),
        name="hy_prep",
    )(z, z, z, conv_w, conv_b.reshape(1, w3))


def _hy_filter_kernel(zf_ref, w1_ref, b1_ref, f1_ref, w2_ref, b2_ref, f2_ref, w3_ref, dl_ref,
                      hfb_ref, ss_ref):
    i = pl.program_id(0)
    zf = zf_ref[...]
    hid = jnp.sin(f1_ref[...] * (_dot_hi(zf, w1_ref[...]) + b1_ref[...]))
    hid = jnp.sin(f2_ref[...] * (_dot_hi(hid, w2_ref[...]) + b2_ref[...]))
    filt = _dot_hi(hid, w3_ref[...])
    window = jnp.exp(-zf[:, 0:1] * dl_ref[...])
    hf = filt[:, 0:HY_W] * window
    row = lax.broadcasted_iota(jnp.int32, (TM, 1), 0) + i * TM
    hb = jnp.where(row == 0, 0.0, filt[:, HY_W:2 * HY_W] * window)
    hfb_ref[0] = hf
    hfb_ref[1] = hb
    part = jnp.sum(hf * hf + hb * hb, axis=0, keepdims=True)

    @pl.when(i == 0)
    def _():
        ss_ref[...] = jnp.zeros_like(ss_ref)

    ss_ref[...] = ss_ref[...] + part


def hy_filter(length, w1, b1, fr1, w2, b2, fr2, w3):
    t = jnp.linspace(0.0, 1.0, length, dtype=F32)[:, None]
    wv = (2.0 * math.pi / length) * jnp.arange(length, dtype=F32)[:, None]
    f = jnp.linspace(1e-4, HY_BANDS - 1.0, HY_BANDS, dtype=F32)[None, :]
    zfeat = jnp.concatenate([t, jnp.cos(f * wv), -jnp.sin(f * wv)], axis=-1)
    zfeat = jnp.pad(zfeat, ((0, 0), (0, LANES - HY_EMB)))
    w1p = jnp.pad(w1, ((0, LANES - HY_EMB), (0, 0)))
    deltas = jnp.abs(jnp.linspace(math.log(HY_TARGET) / HY_SLOW_DECAY,
                                  math.log(HY_TARGET) / HY_FAST_DECAY, HY_W, dtype=F32))[None, :]
    full = lambda shape: pl.BlockSpec(shape, lambda i: (0,) * len(shape))
    return pl.pallas_call(
        _hy_filter_kernel,
        grid=(length // TM,),
        in_specs=[pl.BlockSpec((TM, LANES), lambda i: (i, 0)),
                  full((LANES, HY_ORDER)), full((1, HY_ORDER)), full((1, HY_ORDER)),
                  full((HY_ORDER, HY_ORDER)), full((1, HY_ORDER)), full((1, HY_ORDER)),
                  full((HY_ORDER, 2 * HY_W)), full((1, HY_W))],
        out_specs=[pl.BlockSpec((2, TM, HY_W), lambda i: (0, i, 0)),
                   pl.BlockSpec((1, HY_W), lambda i: (0, 0))],
        out_shape=[jax.ShapeDtypeStruct((2, length, HY_W), F32), jax.ShapeDtypeStruct((1, HY_W), F32)],
        compiler_params=_cparams(("arbitrary",)),
        name="hy_filter",
    )(zfeat, w1p, b1.reshape(1, -1), fr1.reshape(1, -1), w2, b2.reshape(1, -1), fr2.reshape(1, -1), w3, deltas)


def _dft_consts(length):
    n = 2 * length
    ns = n // FFT_FAST
    a = jnp.arange(ns, dtype=jnp.int32)
    ang = ((a[:, None] * a[None, :]) % ns).astype(F32) * (2.0 * math.pi / ns)
    f_fwd = jnp.concatenate([jnp.cos(ang), -jnp.sin(ang)], axis=0)
    f_inv = jnp.concatenate([jnp.cos(ang), -jnp.sin(ang)], axis=1)
    b = jnp.arange(FFT_FAST, dtype=jnp.int32)
    th1 = (a[:, None] * b[None, :]).astype(F32) * (2.0 * math.pi / n)
    th2 = ((b[:, None] * b[None, :]) % FFT_FAST).astype(F32) * (2.0 * math.pi / FFT_FAST)
    c1, s1 = jnp.cos(th1)[:, None, :], jnp.sin(th1)[:, None, :]
    c2, s2 = jnp.cos(th2)[None, :, :], jnp.sin(th2)[None, :, :]
    mr = c1 * c2 - s1 * s2
    mi = -(s1 * c2 + c1 * s2)
    m_fwd = jnp.concatenate([jnp.concatenate([mr, -mi], axis=2),
                             jnp.concatenate([mi, mr], axis=2)], axis=1)
    ir = jnp.transpose(mr, (0, 2, 1)) / n
    ii = jnp.transpose(-mi, (0, 2, 1)) / n
    m_inv = jnp.concatenate([jnp.concatenate([ir, -ii], axis=2),
                             jnp.concatenate([ii, ir], axis=2)], axis=1)
    return tuple(t.astype(BF16) for t in (f_fwd, f_inv, m_fwd, m_inv))


def _dft_slow_fwd_kernel(f_ref, x_ref, o_ref):
    ns = o_ref.shape[2]
    for bl in range(x_ref.shape[1]):
        r = jnp.dot(f_ref[...], x_ref[:, bl, :].astype(BF16), preferred_element_type=F32)
        o_ref[0, 0, :, bl, :] = r[0:ns]
        o_ref[0, 1, :, bl, :] = r[ns:]


def dft_slow_fwd(fmat, x3, bx):
    two_ns, na = fmat.shape
    ns = two_ns // 2
    cw = x3.shape[-1]
    kb = SUBLANES
    return pl.pallas_call(
        _dft_slow_fwd_kernel,
        grid=(bx, FFT_FAST // kb),
        in_specs=[pl.BlockSpec((two_ns, na), lambda b, j: (0, 0)),
                  pl.BlockSpec((na, kb, cw), lambda b, j: (b, j, 0))],
        out_specs=pl.BlockSpec((1, 2, ns, kb, cw), lambda b, j: (b, 0, 0, j, 0)),
        out_shape=jax.ShapeDtypeStruct((bx, 2, ns, FFT_FAST, cw), F32),
        compiler_params=_cparams(("parallel", "parallel")),
        name="dft_slow_fwd",
    )(fmat, x3)


def _dft_slow_inv_kernel(f_ref, b_ref, o_ref, *, bx):
    @pl.when(pl.program_id(0) < bx)
    def _():
        for bl in range(o_ref.shape[1]):
            bb = jnp.concatenate([b_ref[0, 0, :, bl, :], b_ref[0, 1, :, bl, :]], axis=0).astype(BF16)
            o_ref[:, bl, :] = jnp.dot(f_ref[...], bb, preferred_element_type=F32)

    @pl.when(pl.program_id(0) >= bx)
    def _():
        o_ref[...] = jnp.zeros_like(o_ref)


def dft_slow_inv(fmat, bv):
    bx, _, ns, nf, cw = bv.shape
    na = fmat.shape[0]
    kb = SUBLANES
    return pl.pallas_call(
        functools.partial(_dft_slow_inv_kernel, bx=bx),
        grid=(bx + 1, nf // kb),
        in_specs=[pl.BlockSpec((na, 2 * ns), lambda b, j: (0, 0)),
                  pl.BlockSpec((1, 2, ns, kb, cw), lambda b, j: (jnp.minimum(b, bx - 1), 0, 0, j, 0))],
        out_specs=pl.BlockSpec((na, kb, cw), lambda b, j: (b, j, 0)),
        out_shape=jax.ShapeDtypeStruct(((bx + 1) * na, nf, cw), F32),
        compiler_params=_cparams(("parallel", "parallel")),
        name="dft_slow_inv",
    )(fmat, bv)


FFT_GROUP = 8


def _dft_fast_filter_kernel(m_ref, a_ref, sc_ref, o_ref):
    sc = sc_ref[...]
    for j in range(FFT_GROUP):
        af = jnp.concatenate([a_ref[0, 0, j], a_ref[0, 1, j]], axis=0).astype(BF16)
        ab = jnp.concatenate([a_ref[1, 0, j], a_ref[1, 1, j]], axis=0).astype(BF16)
        xf = jnp.dot(m_ref[j], af, preferred_element_type=F32)
        xb = jnp.dot(m_ref[j], ab, preferred_element_type=F32)
        o_ref[0, j] = (xf[0:FFT_FAST] + xb[0:FFT_FAST]) * sc
        o_ref[1, j] = (xf[FFT_FAST:] - xb[FFT_FAST:]) * sc


def dft_fast_filter(m_fwd, a5, scale):
    _, _, ns, nf, cw = a5.shape
    g = FFT_GROUP
    return pl.pallas_call(
        _dft_fast_filter_kernel,
        grid=(ns // g,),
        in_specs=[pl.BlockSpec((g, 2 * nf, 2 * nf), lambda i: (i, 0, 0)),
                  pl.BlockSpec((2, 2, g, nf, cw), lambda i: (0, 0, i, 0, 0)),
                  pl.BlockSpec((1, cw), lambda i: (0, 0))],
        out_specs=pl.BlockSpec((2, g, nf, cw), lambda i: (0, i, 0, 0)),
        out_shape=jax.ShapeDtypeStruct((2, ns, nf, cw), F32),
        compiler_params=_cparams(("parallel",)),
        name="dft_fast_filter",
    )(m_fwd, a5, scale)


def _dft_fast_conv_kernel(mf_ref, mi_ref, a_ref, k_ref, o_ref):
    for j in range(FFT_GROUP):
        a = jnp.concatenate([a_ref[0, 0, j], a_ref[0, 1, j]], axis=0).astype(BF16)
        x = jnp.dot(mf_ref[j], a, preferred_element_type=F32)
        xr, xi = x[0:FFT_FAST], x[FFT_FAST:]
        kr, ki = k_ref[0, j], k_ref[1, j]
        y = jnp.concatenate([xr * kr - xi * ki, xr * ki + xi * kr], axis=0).astype(BF16)
        bv = jnp.dot(mi_ref[j], y, preferred_element_type=F32)
        o_ref[0, 0, j] = bv[0:FFT_FAST]
        o_ref[0, 1, j] = bv[FFT_FAST:]


def dft_fast_conv(m_fwd, m_inv, a5, kf):
    bx, _, ns, nf, cw = a5.shape
    g = FFT_GROUP
    return pl.pallas_call(
        _dft_fast_conv_kernel,
        grid=(bx, ns // g),
        in_specs=[pl.BlockSpec((g, 2 * nf, 2 * nf), lambda b, i: (i, 0, 0)),
                  pl.BlockSpec((g, 2 * nf, 2 * nf), lambda b, i: (i, 0, 0)),
                  pl.BlockSpec((1, 2, g, nf, cw), lambda b, i: (b, 0, i, 0, 0)),
                  pl.BlockSpec((2, g, nf, cw), lambda b, i: (0, i, 0, 0))],
        out_specs=pl.BlockSpec((1, 2, g, nf, cw), lambda b, i: (b, 0, i, 0, 0)),
        out_shape=jax.ShapeDtypeStruct((bx, 2, ns, nf, cw), F32),
        compiler_params=_cparams(("parallel", "parallel")),
        name="dft_fast_conv",
    )(m_fwd, m_inv, a5, kf)


def hyena_long_conv(vx_rows, bsz, length, hfb, scale):
    cw = vx_rows.shape[-1]
    ns = 2 * length // FFT_FAST
    f_fwd, f_inv, m_fwd, m_inv = _dft_consts(length)
    fa = dft_slow_fwd(f_fwd[:, :ns // 2], hfb.reshape(-1, FFT_FAST, cw), 2)
    kf = dft_fast_filter(m_fwd, fa, scale)
    xa = dft_slow_fwd(f_fwd[:, :ns // 2], vx_rows.reshape(-1, FFT_FAST, cw), bsz)
    bv = dft_fast_conv(m_fwd, m_inv, xa, kf)
    return dft_slow_inv(f_inv[:ns // 2], bv).reshape((bsz + 1) * length, cw)


def _hy_ctx_conv_kernel(v_ref, g_ref, y_hbm_ref, o_ref, *, lc):
    del y_hbm_ref
    nb = lc // SUBLANES

    def body(gi, acc):
        base = pl.multiple_of((nb - 1 - gi) * SUBLANES, SUBLANES)
        win = g_ref[pl.ds(base, lc + SUBLANES), :]
        for j in range(SUBLANES):
            vrow = v_ref[pl.ds(gi * SUBLANES + j, 1), :]
            acc = acc + win[SUBLANES - 1 - j:SUBLANES - 1 - j + lc, :] * vrow
        return acc

    o_ref[...] = lax.fori_loop(0, nb, body, jnp.zeros((lc, HY_W), F32))


def hy_ctx_conv(vx, gwin, yconv, bsz, tpb, lc):
    ctx_tile = lambda b: (bsz * (tpb - 1) + b, 0)
    return pl.pallas_call(
        functools.partial(_hy_ctx_conv_kernel, lc=lc),
        grid=(bsz,),
        in_specs=[pl.BlockSpec((lc, HY_W), ctx_tile),
                  pl.BlockSpec((2 * lc, HY_W), lambda b: (0, 0)),
                  pl.BlockSpec(memory_space=pl.ANY)],
        out_specs=pl.BlockSpec((lc, HY_W), ctx_tile),
        out_shape=jax.ShapeDtypeStruct(yconv.shape, F32),
        input_output_aliases={2: 0},
        compiler_params=_cparams(("parallel",)),
        name="hy_ctx_conv",
    )(vx, gwin, yconv)


ROUTE_E1, ROUTE_E2, ROUTE_W1, ROUTE_W2, ROUTE_R1, ROUTE_R2 = 0, 1, 2, 3, 4, 5
NEG_BIG = -1e30


def _gelu_tanh(x):
    return 0.5 * x * (1.0 + jnp.tanh(math.sqrt(2.0 / math.pi) * (x + 0.044715 * (x * x * x))))


def _lane_pick(slab, lane, idx):
    return jnp.sum(jnp.where(lane == idx, slab, 0.0), axis=-1, keepdims=True)


def _route_tile(logits, valid, base_ref, cnt_ref, route_ref, rt_ref):
    lane = lax.broadcasted_iota(jnp.int32, (TM, LANES), 1).astype(F32)
    gl = jnp.where(lane < MOE_GROUPS, logits, NEG_BIG)
    gmax = jnp.max(gl, axis=-1, keepdims=True)
    yield
    grp_p = 1.0 / jnp.sum(jnp.exp(gl - gmax), axis=-1, keepdims=True)
    gi = jnp.min(jnp.where(gl == gmax, lane, float(LANES)), axis=-1, keepdims=True)
    yield
    lo = MOE_GROUPS + MOE_PER_GROUP * gi
    el = jnp.where((lane >= lo) & (lane < lo + MOE_PER_GROUP), logits, NEG_BIG)
    m1 = jnp.max(el, axis=-1, keepdims=True)
    yield
    i1 = jnp.min(jnp.where(el == m1, lane, float(LANES)), axis=-1, keepdims=True)
    yield
    el2 = jnp.where(lane == i1, NEG_BIG, el)
    m2 = jnp.max(el2, axis=-1, keepdims=True)
    yield
    i2 = jnp.min(jnp.where(el2 == m2, lane, float(LANES)), axis=-1, keepdims=True)
    yield
    ratio = jnp.exp(m2 - m1)
    wt1 = grp_p / (1.0 + ratio)
    wt2 = grp_p * ratio / (1.0 + ratio)
    e1 = i1 - MOE_GROUPS
    e2 = i2 - MOE_GROUPS

    oh1 = jnp.where(lane == e1, valid, 0.0)
    oh2 = jnp.where(lane == e2, valid, 0.0)
    ri = lax.broadcasted_iota(jnp.int32, (TM, TM), 0)
    ci = lax.broadcasted_iota(jnp.int32, (TM, TM), 1)
    below = jnp.where(ci < ri, 1.0, 0.0).astype(BF16)
    c1 = jnp.dot(below, oh1.astype(BF16), preferred_element_type=F32)
    c2 = jnp.dot(below, oh2.astype(BF16), preferred_element_type=F32)
    tot1 = jnp.sum(oh1, axis=0, keepdims=True)
    tot2 = jnp.sum(oh2, axis=0, keepdims=True)
    yield
    base = base_ref[...]
    r1 = jnp.sum(oh1 * (base + c1), axis=-1, keepdims=True)
    r2 = jnp.sum(oh2 * (base + tot1 + c2), axis=-1, keepdims=True)
    base = base + tot1 + tot2
    base_ref[...] = base
    cnt_ref[...] = jnp.broadcast_to(base, cnt_ref.shape)
    route = jnp.zeros((TM, LANES), F32)
    for idx, val in ((ROUTE_E1, e1), (ROUTE_E2, e2), (ROUTE_W1, wt1), (ROUTE_W2, wt2),
                     (ROUTE_R1, r1), (ROUTE_R2, r2)):
        route = jnp.where(lane == idx, val, route)
    route_ref[...] = route
    rt_ref[...] = route.T[0:SUBLANES, :]


def _merge_kernel(x_ref, mod_ref, n1_ref, n2_ref, wg0_ref, wg1_ref, wg2_ref, wg3_ref, zhg_ref, zrt_ref, zs5_ref,
                  hgf_ref, hgb_ref, rtf_ref, rtb_ref, s5f_ref, s5b_ref, yc_ref, vx_ref, x0_ref,
                  vec_ref, glu1_ref, glu2_ref, wb_ref, wo_ref, wr_ref, br_ref,
                  xo_ref, h2_ref, route_ref, rt_ref, cnt_ref, base_ref, lg_s, *, d):
    i = pl.program_id(0)

    @pl.when(i == 0)
    def _():
        base_ref[...] = jnp.zeros_like(base_ref)
        lg_s[...] = jnp.zeros_like(lg_s)

    routing = _route_tile(lg_s[...], jnp.where(i > 0, 1.0, 0.0), base_ref, cnt_ref, route_ref, rt_ref)
    advance_routing = lambda: next(routing, None)

    w = BRANCH_W
    x = x_ref[...]
    mod = mod_ref[0]
    h = _modulated_norm(x, n1_ref[...], mod[:, 0:d], mod[:, d:2 * d]).astype(BF16)
    advance_routing()

    hd = w // HG_HEADS
    hr = lax.broadcasted_iota(jnp.int32, (w, w), 0) // hd
    hc = lax.broadcasted_iota(jnp.int32, (w, w), 1) // hd
    hm = jnp.where(hr == hc, 1.0, 0.0).astype(BF16)
    inv = 1.0 / hd

    hg = hgf_ref[...] + hgb_ref[...]
    ms = _split_dot(hg * hg, hm) * inv
    br_hg = hg * lax.rsqrt(ms + EPS) * vec_ref[0:1, :] * _silu(zhg_ref[...])
    advance_routing()

    rt = rtf_ref[...] + rtb_ref[...]
    cen = rt - _split_dot(rt, hm) * inv
    var = _split_dot(cen * cen, hm) * inv
    br_rt = cen * lax.rsqrt(var + EPS) * vec_ref[1:2, :] * _silu(zrt_ref[...])
    advance_routing()

    vx = vx_ref[...]
    br_hy = (yc_ref[...] + vx * vec_ref[3:4, :]) * x0_ref[...]

    y5 = _gelu_tanh(s5f_ref[...] + s5b_ref[...] + zs5_ref[...] * vec_ref[2:3, :]).astype(BF16)
    br_s5 = (jnp.dot(y5, glu1_ref[...], preferred_element_type=F32)
             * _sigmoid(jnp.dot(y5, glu2_ref[...], preferred_element_type=F32)))
    advance_routing()

    merged = jnp.zeros((TM, d), F32)
    for n, (br, wg_ref) in enumerate(zip((br_hg, br_hy, br_rt, br_s5), (wg0_ref, wg1_ref, wg2_ref, wg3_ref))):
        gate = _sigmoid(jnp.dot(h, wg_ref[...], preferred_element_type=F32))
        merged = merged + gate * jnp.dot(br.astype(BF16), wb_ref[n], preferred_element_type=F32)
        advance_routing()
    for _ in routing:
        pass
    mix = jnp.dot(merged.astype(BF16), wo_ref[...], preferred_element_type=F32)
    x_mid = x + mod[:, 2 * d:3 * d] * mix
    xo_ref[...] = x_mid
    h2 = _modulated_norm(x_mid, n2_ref[...], mod[:, 3 * d:4 * d], mod[:, 4 * d:5 * d])
    h2_ref[...] = h2

    lg_s[...] = _dot3(h2, wr_ref[...]) + br_ref[...]


def merge(x_all, mod3, n1, n2, w_in_bf, layer, z, hgf, hgb, rtf, rtb, s5f, s5b, yconv, vx, x0c,
          vecs, glu1, glu2, w_branch, w_out, w_router, b_router, bsz, tpb):
    m, d = x_all.shape
    w = BRANCH_W
    nt = m // TM
    cur = lambda i: jnp.minimum(i, nt - 1)
    prev = lambda i: jnp.maximum(i - 1, 0)
    row = lambda width: pl.BlockSpec((TM, width), lambda i: (cur(i), 0))
    srow = pl.BlockSpec((TM, w), lambda i: (_split_tile(cur(i), bsz, tpb), 0))
    zcol = lambda cb: pl.BlockSpec((TM, w), lambda i: (cur(i), cb))
    full = lambda shape: pl.BlockSpec(shape, lambda i: (0,) * len(shape))
    gate_w = lambda n: pl.BlockSpec((None, d, d), lambda i: (layer, 0, N_MIX // d + n))
    return pl.pallas_call(
        functools.partial(_merge_kernel, d=d),
        grid=(nt + 1,),
        in_specs=[row(d),
                  pl.BlockSpec((1, 1, 6 * d), lambda i: (_seg_of_tile(cur(i), tpb), 0, 0)),
                  full((1, d)), full((1, d)), gate_w(0), gate_w(1), gate_w(2), gate_w(3),
                  zcol(CB_HG_G), zcol(CB_RT_G), zcol(CB_S5)]
                 + [row(w)] * 6 + [srow, srow, row(w)]
                 + [full((SUBLANES, w)), full((w, w)), full((w, w)), full((N_BRANCH, w, d)), full((d, d)),
                    full((d, LANES)), full((1, LANES))],
        out_specs=[row(d), row(d),
                   pl.BlockSpec((TM, LANES), lambda i: (prev(i), 0)),
                   pl.BlockSpec((None, SUBLANES, TM), lambda i: (prev(i), 0, 0)),
                   full((SUBLANES, LANES))],
        out_shape=[jax.ShapeDtypeStruct((m, d), F32), jax.ShapeDtypeStruct((m, d), F32),
                   jax.ShapeDtypeStruct((m, LANES), F32), jax.ShapeDtypeStruct((nt, SUBLANES, TM), F32),
                   jax.ShapeDtypeStruct((SUBLANES, LANES), F32)],
        scratch_shapes=[pltpu.VMEM((1, LANES), F32), pltpu.VMEM((TM, LANES), F32)],
        compiler_params=_cparams(("arbitrary",)),
        name="merge",
    )(x_all, mod3, n1, n2, w_in_bf, w_in_bf, w_in_bf, w_in_bf, z, z, z,
      hgf, hgb, rtf, rtb, s5f, s5b, yconv, vx, x0c,
      vecs, glu1, glu2, w_branch, w_out, w_router, b_router)


DMA_UNROLL = 8


def _row_copy(src_ref, src_row, dst_ref, dst_row, sem):
    return pltpu.make_async_copy(src_ref.at[pl.ds(src_row, 1), :], dst_ref.at[pl.ds(dst_row, 1), :], sem)


DISPATCH_ROWS = 6 * TM


def _dispatch_kernel(dest_ref, h_ref, zero_ref, buf_ref, sem, *, m):
    del zero_ref
    rows = h_ref.shape[0]
    base = pl.program_id(0) * rows

    def issue(r, carry):
        for k in range(2):
            _row_copy(h_ref, r, buf_ref, dest_ref[k * m + base + r], sem).start(priority=k)
        return carry

    lax.fori_loop(0, rows, issue, 0, unroll=DMA_UNROLL)
    for k in range(2):
        pltpu.make_async_copy(h_ref, buf_ref.at[pl.ds(0, rows), :], sem).wait()


def moe_dispatch(dest, h2, n_rows):
    m, d = h2.shape
    assert m % DISPATCH_ROWS == 0
    grid_spec = pltpu.PrefetchScalarGridSpec(
        num_scalar_prefetch=1,
        grid=(m // DISPATCH_ROWS,),
        in_specs=[pl.BlockSpec((DISPATCH_ROWS, d), lambda i, dest: (i, 0)),
                  pl.BlockSpec(memory_space=pl.ANY)],
        out_specs=pl.BlockSpec(memory_space=pl.ANY),
        scratch_shapes=[pltpu.SemaphoreType.DMA(())],
    )
    return pl.pallas_call(
        functools.partial(_dispatch_kernel, m=m),
        grid_spec=grid_spec,
        out_shape=jax.ShapeDtypeStruct((n_rows, d), F32),
        input_output_aliases={2: 0},
        compiler_params=_cparams(("arbitrary",)),
        name="moe_dispatch",
    )(dest, h2, jnp.zeros((n_rows, d), F32))


def _expert_kernel(plan_ref, x_ref, w1_ref, w3_ref, w2_ref, y_ref, *, n_blocks):
    used = pl.program_id(0) < plan_ref[n_blocks]

    @pl.when(used)
    def _():
        xb = x_ref[...].astype(BF16)
        a = jnp.dot(xb, w1_ref[...].astype(BF16), preferred_element_type=F32)
        g = jnp.dot(xb, w3_ref[...].astype(BF16), preferred_element_type=F32)
        y_ref[...] = jnp.dot((_silu(a) * g).astype(BF16), w2_ref[...].astype(BF16),
                             preferred_element_type=F32)

    @pl.when(jnp.logical_not(used))
    def _():
        y_ref[...] = jnp.zeros_like(y_ref)


def moe_experts(plan, buf, w1, w3, w2, layer):
    n_rows, d = buf.shape
    hid = w1.shape[-1]
    n_blocks = n_rows // MOE_BLOCK
    blk = lambda j, plan: jnp.minimum(j, plan[n_blocks] - 1)
    grid_spec = pltpu.PrefetchScalarGridSpec(
        num_scalar_prefetch=1,
        grid=(n_blocks,),
        in_specs=[pl.BlockSpec((MOE_BLOCK, d), lambda j, plan: (blk(j, plan), 0)),
                  pl.BlockSpec((None, None, d, hid), lambda j, plan: (layer, plan[blk(j, plan)], 0, 0)),
                  pl.BlockSpec((None, None, d, hid), lambda j, plan: (layer, plan[blk(j, plan)], 0, 0)),
                  pl.BlockSpec((None, None, hid, d), lambda j, plan: (layer, plan[blk(j, plan)], 0, 0))],
        out_specs=pl.BlockSpec((MOE_BLOCK, d), lambda j, plan: (j, 0)),
    )
    return pl.pallas_call(
        functools.partial(_expert_kernel, n_blocks=n_blocks),
        grid_spec=grid_spec,
        out_shape=jax.ShapeDtypeStruct((n_rows, d), F32),
        compiler_params=_cparams(("arbitrary",)),
        name="moe_experts",
    )(plan, buf, w1, w3, w2)


def _combine_kernel(dest_ref, x_ref, route_ref, mod_ref, gfin_ref, y_ref, o_ref, gath, sem, *,
                    d, m, bsz, tpb, nr, final):
    step = pl.program_id(0) * nr + pl.program_id(1)
    slot = step % 2

    def start_gather(s, sl):
        base = ((s // nr) * tpb + s % nr) * TM

        def issue(r, carry):
            for k in range(2):
                _row_copy(y_ref, dest_ref[k * m + base + r], gath.at[sl, k], r, sem.at[sl]).start(priority=k)
            return carry

        lax.fori_loop(0, TM, issue, 0, unroll=DMA_UNROLL)

    @pl.when(step == 0)
    def _():
        start_gather(0, 0)

    @pl.when(step + 1 < bsz * nr)
    def _():
        start_gather(step + 1, 1 - slot)

    for k in range(2):
        pltpu.make_async_copy(y_ref.at[pl.ds(0, TM), :], gath.at[slot, k], sem.at[slot]).wait()
    route = route_ref[...]
    lane = lax.broadcasted_iota(jnp.int32, (TM, LANES), 1)
    moe = (_lane_pick(route, lane, ROUTE_W1) * gath[slot, 0]
           + _lane_pick(route, lane, ROUTE_W2) * gath[slot, 1])
    x_new = x_ref[...] + mod_ref[0][:, 5 * d:6 * d] * moe
    if final:
        ms = jnp.mean(x_new * x_new, axis=-1, keepdims=True)
        x_new = x_new * lax.rsqrt(ms + EPS) * gfin_ref[...]
    o_ref[...] = x_new


def moe_combine(dest, x_mid, route, mod3, gain_final, ybuf, bsz, tpb, final):
    m, d = x_mid.shape
    nr = tpb - 1 if final else tpb
    tile = lambda b, r: b * tpb + r
    grid_spec = pltpu.PrefetchScalarGridSpec(
        num_scalar_prefetch=1,
        grid=(bsz, nr),
        in_specs=[pl.BlockSpec((TM, d), lambda b, r, dest: (tile(b, r), 0)),
                  pl.BlockSpec((TM, LANES), lambda b, r, dest: (tile(b, r), 0)),
                  pl.BlockSpec((1, 1, 6 * d), lambda b, r, dest: (_seg_of_tile(tile(b, r), tpb), 0, 0)),
                  pl.BlockSpec((1, d), lambda b, r, dest: (0, 0)),
                  pl.BlockSpec(memory_space=pl.ANY)],
        out_specs=pl.BlockSpec((TM, d), lambda b, r, dest: (b * nr + r, 0)),
        scratch_shapes=[pltpu.VMEM((2, 2, TM, d), F32), pltpu.SemaphoreType.DMA((2,))],
    )
    return pl.pallas_call(
        functools.partial(_combine_kernel, d=d, m=m, bsz=bsz, tpb=tpb, nr=nr, final=final),
        grid_spec=grid_spec,
        out_shape=jax.ShapeDtypeStruct((bsz * nr * TM, d), F32),
        compiler_params=_cparams(("arbitrary", "arbitrary")),
        name="moe_combine",
    )(dest, x_mid, route, mod3, gain_final, ybuf)


def moe_plan(rt, counts_row, n_blocks):
    counts = counts_row[:MOE_EXPERTS].astype(jnp.int32)
    padded = (counts + MOE_BLOCK - 1) // MOE_BLOCK * MOE_BLOCK
    pad_end = jnp.cumsum(padded)
    pad_start = pad_end - padded
    by_choice = lambda f: jnp.transpose(rt[:, f:f + 2, :], (1, 0, 2)).reshape(2, -1).astype(jnp.int32)
    expert = by_choice(ROUTE_E1)
    rank = by_choice(ROUTE_R1)
    is_e = expert[None] == jnp.arange(MOE_EXPERTS, dtype=jnp.int32)[:, None, None]
    dest = (jnp.sum(jnp.where(is_e, pad_start[:, None, None], 0), axis=0) + rank).reshape(-1)
    block_row0 = jnp.arange(n_blocks, dtype=jnp.int32) * MOE_BLOCK
    block_expert = jnp.minimum(jnp.sum((pad_end[None, :] <= block_row0[:, None]).astype(jnp.int32), axis=1),
                               MOE_EXPERTS - 1)
    n_used = pad_end[MOE_EXPERTS - 1:] // MOE_BLOCK
    return dest, jnp.concatenate([block_expert, n_used])


def _rope_tables(length, lc):
    half = RET_DK // 2
    freqs = ROPE_BASE ** (-jnp.arange(0, half, 2, dtype=F32) / half)
    t = jnp.arange(length)
    row_pos = (t // GRID_W).astype(F32)[:, None]
    col_pos = (t % GRID_W).astype(F32)[:, None]
    lane = jnp.arange(RET_HEADS * RET_DK)
    fr = freqs[lane % (half // 2)][None, :]
    ang = jnp.where(((lane % RET_DK) < half)[None, :], row_pos, col_pos) * fr
    cos = jnp.concatenate([jnp.cos(ang), jnp.ones((lc, lane.shape[0]), F32)], axis=0)
    sin = jnp.concatenate([jnp.sin(ang), jnp.zeros((lc, lane.shape[0]), F32)], axis=0)
    return cos, sin


def _rotate_half_cols(wq):
    quarter = RET_DK // 4
    lane = np.arange(wq.shape[1])
    first = (lane % (2 * quarter)) < quarter
    src = np.where(first, lane + quarter, lane - quarter)
    sign = jnp.asarray(np.where(first, -1.0, 1.0), F32)
    return wq[:, src] * sign


def kernel(x, c, ctx, c_ctx, w_mod, b_mod, norm1, norm2, w_in, hg_lb_logits, hg_norm, hy_conv_w, hy_conv_b, hy_filt_w1, hy_filt_b1, hy_filt_freq1, hy_filt_w2, hy_filt_b2, hy_filt_freq2, hy_filt_w3, hy_skip, ret_decay_logit, ret_norm, s5_lam_re, s5_lam_im, s5_log_dt, s5_b_re, s5_b_im, s5_c_re, s5_c_im, s5_d, s5_glu_w1, s5_glu_w2, w_branch, w_out, moe_w_grp, moe_b_grp, moe_w_exp, moe_b_exp, moe_w1, moe_w3, moe_w2, norm_final):
    bsz, length, d = x.shape
    lc = ctx.shape[1]
    depth = w_mod.shape[0]
    assert lc == TM and length % TM == 0 and (2 * length) % (FFT_FAST * FFT_GROUP) == 0
    tt = length + lc
    tpb = tt // TM
    m = bsz * tt
    n_blocks = -(-(m * 2) // MOE_BLOCK) + MOE_EXPERTS
    w = BRANCH_W

    x_all = jnp.concatenate([x, ctx], axis=1).reshape(m, d)
    cmat = jnp.zeros((SUBLANES, d), F32).at[0].set(c_ctx).at[1:1 + bsz].set(c)
    mods = modvec(cmat, w_mod, b_mod)
    lb_sm = jax.nn.softmax(hg_lb_logits.astype(F32), axis=0)
    lb_all = jnp.cumsum(lb_sm, axis=0) - lb_sm[0]
    cos_tab, sin_tab = _rope_tables(length, lc)
    log_gamma = jax.nn.log_sigmoid(ret_decay_logit.astype(F32))
    gfin = norm_final.reshape(1, d)

    w_in_bf = w_in.astype(BF16)
    rq0 = 5 * w + 3 * HY_W
    nq = RET_HEADS * RET_DK

    for l in range(depth):
        w_rot = jnp.concatenate([_rotate_half_cols(w_in[l, :, rq0:rq0 + nq]),
                                 _rotate_half_cols(w_in[l, :, rq0 + nq:rq0 + 2 * nq])], axis=1).astype(BF16)
        mod3 = mods[l].reshape(SUBLANES, 1, 6 * d)

        z = inproj(x_all, mod3, norm1[l].reshape(1, d), w_in_bf, l, w_rot, tpb)
        hgf, hgb = hgrn(z, lb_all[l], bsz, tpb)
        rtf, rtb = retention(z, cos_tab, sin_tab, log_gamma[l], bsz, tpb)
        s5f, s5b = s5g(z, *s5g_params(s5_lam_re[l], s5_lam_im[l], s5_log_dt[l], s5_b_re[l], s5_b_im[l],
                                      s5_c_re[l], s5_c_im[l]), bsz, tpb)

        x0c, vx = hy_prep(z, hy_conv_w[l], hy_conv_b[l], bsz, tpb)
        fargs = (hy_filt_w1[l], hy_filt_b1[l], hy_filt_freq1[l], hy_filt_w2[l], hy_filt_b2[l],
                 hy_filt_freq2[l], hy_filt_w3[l])
        hfb, ss = hy_filter(length, *fargs)
        y_lat = hyena_long_conv(vx, bsz, length, hfb, lax.rsqrt(ss + EPS))
        hfbc, ssc = hy_filter(lc, *fargs)
        gwin = jnp.concatenate([hfbc[1, 1:][::-1], hfbc[0], jnp.zeros((1, HY_W), F32)], axis=0) \
            * lax.rsqrt(ssc + EPS)
        yconv = hy_ctx_conv(vx, gwin, y_lat, bsz, tpb, lc)

        vecs = jnp.zeros((SUBLANES, w), F32).at[0].set(hg_norm[l]).at[1].set(ret_norm[l]) \
            .at[2].set(s5_d[l]).at[3].set(hy_skip[l])
        w_router = jnp.zeros((d, LANES), F32).at[:, :MOE_GROUPS].set(moe_w_grp[l]) \
            .at[:, MOE_GROUPS:MOE_GROUPS + MOE_EXPERTS].set(moe_w_exp[l])
        b_router = jnp.zeros((1, LANES), F32).at[0, :MOE_GROUPS].set(moe_b_grp[l]) \
            .at[0, MOE_GROUPS:MOE_GROUPS + MOE_EXPERTS].set(moe_b_exp[l])
        x_mid, h2, route, rt, counts = merge(
            x_all, mod3, norm1[l].reshape(1, d), norm2[l].reshape(1, d), w_in_bf, l, z,
            hgf, hgb, rtf, rtb, s5f, s5b, yconv, vx, x0c, vecs,
            s5_glu_w1[l].astype(BF16), s5_glu_w2[l].astype(BF16), w_branch[l].astype(BF16),
            w_out[l].astype(BF16), w_router, b_router, bsz, tpb)

        dest, block_expert = moe_plan(rt, counts[0], n_blocks)
        buf = moe_dispatch(dest, h2, n_blocks * MOE_BLOCK)
        ybuf = moe_experts(block_expert, buf, moe_w1, moe_w3, moe_w2, l)
        x_all = moe_combine(dest, x_mid, route, mod3, gfin, ybuf, bsz, tpb, l == depth - 1)

    return x_all.reshape(bsz, length, d)
```

```python
import functools
import math

import numpy as np
import jax
import jax.numpy as jnp
from jax import lax
from jax.experimental import pallas as pl
from jax.experimental.pallas import tpu as pltpu

F32 = jnp.float32
BF16 = jnp.bfloat16
HIGHEST = lax.Precision.HIGHEST

GRID_W = 64
N_BRANCH = 4
BRANCH_W = 256
HG_HEADS = 4
HG_CHUNK = 16
HY_W = 256
HY_EMB = 33
HY_BANDS = 16
HY_ORDER = 64
HY_FAST_DECAY = 0.3
HY_SLOW_DECAY = 1.5
HY_TARGET = 1e-2
RET_HEADS = 4
RET_DK = 32
RET_DV = 64
ROPE_BASE = 10000.0
S5_GROUP_CH = 16
S5_GROUPS = 16
S5_STATE = 64
MOE_GROUPS = 4
MOE_PER_GROUP = 8
MOE_EXPERTS = 32
MOE_HIDDEN = 512
MOE_BLOCK = 256
EPS = 1e-6

LANES = 128
SUBLANES = 8
TM = 256
FFT_FAST = 128
VMEM_LIMIT = 56 * 1024 * 1024

N_MIX = 3072
N_ZCOLS = N_MIX + 2 * RET_HEADS * RET_DK
CB_HG_Q, CB_HG_FF, CB_HG_FB, CB_HG_I, CB_HG_G = 3, 4, 5, 6, 7
CB_RT_V, CB_RT_G, CB_S5 = 9, 10, 11
CB128_RT_Q, CB128_RT_K, CB128_RT_QR, CB128_RT_KR = 16, 17, 24, 25


def _cparams(sem):
    return pltpu.CompilerParams(dimension_semantics=sem, vmem_limit_bytes=VMEM_LIMIT)


def _sigmoid(x):
    return 1.0 / (1.0 + jnp.exp(-x))


def _silu(x):
    return x * _sigmoid(x)


def _dot(a, b):
    return jnp.dot(a.astype(BF16), b.astype(BF16), preferred_element_type=F32)


def _dot_nt(a, b):
    return lax.dot_general(a.astype(BF16), b.astype(BF16), (((1,), (1,)), ((), ())),
                           preferred_element_type=F32)


def _dot_hi(a, b):
    return jnp.dot(a, b, precision=HIGHEST, preferred_element_type=F32)


def _dot3(a, b):
    a_hi = a.astype(BF16)
    a_lo = (a - a_hi.astype(F32)).astype(BF16)
    b_hi = b.astype(BF16)
    b_lo = (b - b_hi.astype(F32)).astype(BF16)
    return (jnp.dot(a_hi, b_hi, preferred_element_type=F32)
            + jnp.dot(a_lo, b_hi, preferred_element_type=F32)
            + jnp.dot(a_hi, b_lo, preferred_element_type=F32))


def _split_dot(x, m_bf16):
    hi = x.astype(BF16)
    lo = (x - hi.astype(F32)).astype(BF16)
    return (jnp.dot(hi, m_bf16, preferred_element_type=F32)
            + jnp.dot(lo, m_bf16, preferred_element_type=F32))


def _split_dot_left(m_bf16, x):
    hi = x.astype(BF16)
    lo = (x - hi.astype(F32)).astype(BF16)
    return (jnp.dot(m_bf16, hi, preferred_element_type=F32)
            + jnp.dot(m_bf16, lo, preferred_element_type=F32))


def _seg_of_tile(i, tpb):
    return jnp.where(i % tpb == tpb - 1, 0, 1 + i // tpb)


def _split_tile(i, bsz, tpb):
    b, r = i // tpb, i % tpb
    return jnp.where(r == tpb - 1, bsz * (tpb - 1) + b, b * (tpb - 1) + r)


def _scan_tiles(b, s, tpb):
    last = tpb - 1
    fwd = jnp.where(s == 0, last, s - 1)
    bwd = jnp.where(s == 0, last, last - s)
    return b * tpb + fwd, b * tpb + bwd


def _modvec_kernel(c_ref, w_ref, b_ref, o_ref):
    c = c_ref[...]
    o_ref[0] = _dot_hi(_silu(c), w_ref[0]) + b_ref[0]


def modvec(cmat, w_mod, b_mod):
    depth, d, n6 = w_mod.shape
    nb = n6 // d
    return pl.pallas_call(
        _modvec_kernel,
        grid=(depth, nb),
        in_specs=[pl.BlockSpec((SUBLANES, d), lambda l, j: (0, 0)),
                  pl.BlockSpec((1, d, d), lambda l, j: (l, 0, j)),
                  pl.BlockSpec((1, 1, d), lambda l, j: (l, 0, j))],
        out_specs=pl.BlockSpec((1, SUBLANES, d), lambda l, j: (l, 0, j)),
        out_shape=jax.ShapeDtypeStruct((depth, SUBLANES, n6), F32),
        compiler_params=_cparams(("arbitrary", "arbitrary")),
        name="modvec",
    )(cmat, w_mod, b_mod.reshape(depth, 1, n6))


def _modulated_norm(x, gain, shift, scale):
    ms = jnp.mean(x * x, axis=-1, keepdims=True)
    return (x * lax.rsqrt(ms + EPS) * gain) * (1.0 + scale) + shift


def _inproj_kernel(x_ref, mod_ref, g_ref, w_ref, wrot_ref, z_ref, *, d):
    h = _modulated_norm(x_ref[...], g_ref[...], mod_ref[0, :, 0:d], mod_ref[0, :, d:2 * d]).astype(BF16)
    hy0 = 5 * BRANCH_W
    hy1 = hy0 + 3 * HY_W
    z_ref[:, 0:hy1 - hy0] = jnp.dot(h, w_ref[:, hy0:hy1], preferred_element_type=F32)
    z_ref[:, hy1 - hy0:hy1] = jnp.dot(h, w_ref[:, 0:hy0], preferred_element_type=F32)
    z_ref[:, hy1:N_MIX] = jnp.dot(h, w_ref[:, hy1:N_MIX], preferred_element_type=F32)
    z_ref[:, N_MIX:N_ZCOLS] = jnp.dot(h, wrot_ref[...], preferred_element_type=F32)


def inproj(x_all, mod3, gain, w_in_bf, layer, w_rot, tpb):
    m, d = x_all.shape
    return pl.pallas_call(
        functools.partial(_inproj_kernel, d=d),
        grid=(m // TM,),
        in_specs=[pl.BlockSpec((TM, d), lambda i: (i, 0)),
                  pl.BlockSpec((1, 1, 6 * d), lambda i: (_seg_of_tile(i, tpb), 0, 0)),
                  pl.BlockSpec((1, d), lambda i: (0, 0)),
                  pl.BlockSpec((None, d, N_MIX), lambda i: (layer, 0, 0)),
                  pl.BlockSpec((d, N_ZCOLS - N_MIX), lambda i: (0, 0))],
        out_specs=pl.BlockSpec((TM, N_ZCOLS), lambda i: (i, 0)),
        out_shape=jax.ShapeDtypeStruct((m, N_ZCOLS), F32),
        compiler_params=_cparams(("parallel",)),
        name="inproj",
    )(x_all, mod3, gain, w_in_bf, w_rot)


def _head_masked_rows(x, nchunk, c, hd):
    w = x.shape[-1]
    x3 = x.reshape(nchunk, c, w)
    lane_head = lax.broadcasted_iota(jnp.int32, (1, 1, w), 2) // hd
    parts = [jnp.where(lane_head == h, x3, 0.0) for h in range(w // hd)]
    return jnp.concatenate(parts, axis=1).reshape(nchunk * (w // hd) * c, w)


def _hgrn_direction(q_ref, f_ref, v_ref, lb, tri_ref, o_ref, st_ref, b_s, k_s, qe_s, ke_s, reverse):
    c = HG_CHUNK
    w = BRANCH_W
    nchunk = TM // c
    nh = HG_HEADS
    hd = w // nh
    half = c // 2
    z = f_ref[...]
    sg = _sigmoid(z)
    log2_f = jnp.log2(lb + (1.0 - lb) * sg)
    k = (1.0 - lb) * (1.0 - sg)
    b2 = _split_dot_left(tri_ref[0], log2_f)
    rem2 = _split_dot_left(tri_ref[1], log2_f)
    b_s[...] = b2
    k_s[...] = k
    qe_s[...] = _head_masked_rows(q_ref[...] * jnp.exp2(b2), nchunk, c, hd).astype(BF16)
    ke_s[...] = _head_masked_rows(k * jnp.exp2(rem2), nchunk, c, hd).astype(BF16)

    head_r = lax.broadcasted_iota(jnp.int32, (w, w), 0) // hd
    head_c = lax.broadcasted_iota(jnp.int32, (w, w), 1) // hd
    hm = jnp.where(head_r == head_c, 1.0, 0.0).astype(BF16)
    rows = lax.broadcasted_iota(jnp.int32, (half, 1), 0)

    pieces = []
    for s in range(c):
        for g in range(2):
            lo, hi = g * half, (g + 1) * half - 1
            if (hi < s) if not reverse else (lo > s):
                continue
            full = (lo >= s) if not reverse else (hi <= s)
            pieces.append((s, g, full))

    def chunk_step(n):
        cidx = (nchunk - 1 - n) if reverse else n
        off = pl.multiple_of(cidx * c, c)
        eoff = pl.multiple_of(cidx * (nh * c), nh * c)
        qc = q_ref[pl.ds(off, c), :]
        bc = b_s[pl.ds(off, c), :]
        kc = k_s[pl.ds(off, c), :]
        vc = v_ref[pl.ds(off, c), :]
        parts = []
        for s, g, full in pieces:
            sl = slice(g * half, (g + 1) * half)
            e = jnp.exp2(bc[sl] - bc[s:s + 1, :])
            if not full:
                t = rows + g * half
                e = jnp.where((t <= s) if reverse else (t >= s), e, 0.0)
            parts.append((qc[sl] * kc[s:s + 1, :]) * e)
        p = jnp.concatenate(parts, axis=0).astype(BF16)
        r = jnp.dot(p, hm, preferred_element_type=F32)
        og = [jnp.zeros((half, w), F32), jnp.zeros((half, w), F32)]
        for i, (s, g, _) in enumerate(pieces):
            og[g] = og[g] + r[i * half:(i + 1) * half, :] * vc[s:s + 1, :]
        st = st_ref[...]
        ri = _dot_nt(qe_s[pl.ds(eoff, nh * c), :], st)
        inter = jnp.concatenate([ri[h * c:(h + 1) * c, :] for h in range(nh)], axis=1)
        o_ref[pl.ds(off, c), :] = jnp.concatenate(og, axis=0) + inter
        vexp = jnp.concatenate([vc[:, h * hd:(h + 1) * hd] for h in range(nh)], axis=0)
        kv = lax.dot_general(vexp.astype(BF16), ke_s[pl.ds(eoff, nh * c), :], (((0,), (0,)), ((), ())),
                             preferred_element_type=F32)
        blast = bc[0:1, :] if reverse else bc[c - 1:c, :]
        st_ref[...] = jnp.exp2(blast) * st + kv

    return chunk_step


def _hgrn_kernel(qf_ref, ff_ref, vf_ref, qb_ref, fb_ref, vb_ref, lb_ref, trif_ref, trib_ref, of_ref, ob_ref,
                 stf_ref, stb_ref, bf_s, kf_s, qef_s, kef_s, bb_s, kb_s, qeb_s, keb_s):
    @pl.when(pl.program_id(1) == 0)
    def _():
        stf_ref[...] = jnp.zeros_like(stf_ref)
        stb_ref[...] = jnp.zeros_like(stb_ref)

    step_f = _hgrn_direction(qf_ref, ff_ref, vf_ref, lb_ref[0:1, :], trif_ref, of_ref, stf_ref,
                             bf_s, kf_s, qef_s, kef_s, False)
    step_b = _hgrn_direction(qb_ref, fb_ref, vb_ref, lb_ref[1:2, :], trib_ref, ob_ref, stb_ref,
                             bb_s, kb_s, qeb_s, keb_s, True)

    def body(n, carry):
        step_f(n)
        step_b(n)
        return carry

    lax.fori_loop(0, TM // HG_CHUNK, body, 0, unroll=16)


def hgrn(z, lb, bsz, tpb):
    m = z.shape[0]
    w = BRANCH_W
    hd = w // HG_HEADS
    ri = jnp.arange(TM)[:, None]
    ci = jnp.arange(TM)[None, :]
    same = (ri // HG_CHUNK) == (ci // HG_CHUNK)
    tri = jnp.stack([same & (ci <= ri), same & (ci > ri), same & (ci >= ri), same & (ci < ri)])
    tri = tri.astype(BF16).reshape(2, 2, TM, TM)

    def spec(cb, d):
        return pl.BlockSpec((TM, w), lambda b, s: (_scan_tiles(b, s, tpb)[d], cb))

    out_spec = [pl.BlockSpec((TM, w), lambda b, s: (_scan_tiles(b, s, tpb)[0], 0)),
                pl.BlockSpec((TM, w), lambda b, s: (_scan_tiles(b, s, tpb)[1], 0))]
    return pl.pallas_call(
        _hgrn_kernel,
        grid=(bsz, tpb),
        in_specs=[spec(CB_HG_Q, 0), spec(CB_HG_FF, 0), spec(CB_HG_I, 0),
                  spec(CB_HG_Q, 1), spec(CB_HG_FB, 1), spec(CB_HG_I, 1),
                  pl.BlockSpec((2, w), lambda b, s: (0, 0)),
                  pl.BlockSpec((None, 2, TM, TM), lambda b, s: (0, 0, 0, 0)),
                  pl.BlockSpec((None, 2, TM, TM), lambda b, s: (1, 0, 0, 0))],
        out_specs=out_spec,
        out_shape=[jax.ShapeDtypeStruct((m, w), F32)] * 2,
        scratch_shapes=[pltpu.VMEM((hd, w), F32), pltpu.VMEM((hd, w), F32)]
                       + [pltpu.VMEM((TM, w), F32), pltpu.VMEM((TM, w), F32),
                          pltpu.VMEM((HG_HEADS * TM, w), BF16), pltpu.VMEM((HG_HEADS * TM, w), BF16)] * 2,
        compiler_params=_cparams(("arbitrary", "arbitrary")),
        name="hgrn2",
    )(z, z, z, z, z, z, lb, tri, tri)


def _ret_direction(q_ref, k_ref, qr_ref, kr_ref, v_ref, cos_ref, sin_ref, lgq, lg_ref, d, o_ref, st_ref,
                   reverse):
    nq = RET_HEADS * RET_DK
    nv = RET_HEADS * RET_DV
    cos = cos_ref[...]
    sin = sin_ref[...]
    qx = q_ref[...] * cos + qr_ref[...] * sin
    kx = (k_ref[...] * cos + kr_ref[...] * sin) * (RET_DK ** -0.5)
    v = v_ref[...]
    t = lax.broadcasted_iota(jnp.int32, (TM, 1), 0).astype(F32)
    pos = (TM - 1.0 - t) if reverse else t
    qd = qx * jnp.exp((pos + 1.0) * lgq)
    kd = kx * jnp.exp((TM - 1.0 - pos) * lgq)
    ti = lax.broadcasted_iota(jnp.int32, (TM, TM), 0)
    si = lax.broadcasted_iota(jnp.int32, (TM, TM), 1)
    rel = ((si - ti) if reverse else (ti - si)).astype(F32)
    live = rel >= 0.0
    relc = jnp.where(live, rel, 0.0)
    lane_q = lax.broadcasted_iota(jnp.int32, (1, nq), 1) // RET_DK
    lane_v = lax.broadcasted_iota(jnp.int32, (1, nv), 1) // RET_DV
    st = st_ref[...]
    o = _dot_nt(qd, st)
    vb = v.astype(BF16)
    for h in range(RET_HEADS):
        dmat = jnp.where(live, jnp.exp(relc * lg_ref[d, h]), 0.0)
        sc = _dot_nt(jnp.where(lane_q == h, qx, 0.0), kx) * dmat
        oh = jnp.dot(sc.astype(BF16), vb, preferred_element_type=F32)
        o = o + jnp.where(lane_v == h, oh, 0.0)
    o_ref[...] = o
    kv = _dot(v.T, kd)
    hr = lax.broadcasted_iota(jnp.int32, (nv, nq), 0) // RET_DV
    hc = lax.broadcasted_iota(jnp.int32, (nv, nq), 1) // RET_DK
    st_ref[...] = st * jnp.exp(float(TM) * lgq) + jnp.where(hr == hc, kv, 0.0)


def _ret_kernel(lg_ref, qf, kf, qrf, krf, vf, cosf, sinf, qb, kb, qrb, krb, vb, cosb, sinb, lgq_ref,
                of_ref, ob_ref, stf_ref, stb_ref):
    @pl.when(pl.program_id(1) == 0)
    def _():
        stf_ref[...] = jnp.zeros_like(stf_ref)
        stb_ref[...] = jnp.zeros_like(stb_ref)

    _ret_direction(qf, kf, qrf, krf, vf, cosf, sinf, lgq_ref[0:1, :], lg_ref, 0, of_ref, stf_ref, False)
    _ret_direction(qb, kb, qrb, krb, vb, cosb, sinb, lgq_ref[1:2, :], lg_ref, 1, ob_ref, stb_ref, True)


def retention(z, cos_tab, sin_tab, log_gamma, bsz, tpb):
    m = z.shape[0]
    nq = RET_HEADS * RET_DK
    nv = RET_HEADS * RET_DV
    lgq = jnp.repeat(log_gamma, RET_DK, axis=1)

    def zspec(cb, width, d):
        return pl.BlockSpec((TM, width), lambda b, s: (_scan_tiles(b, s, tpb)[d], cb))

    def tspec(d):
        return pl.BlockSpec((TM, nq), lambda b, s: (_scan_tiles(0, s, tpb)[d], 0))

    def side(d):
        return [zspec(CB128_RT_Q, nq, d), zspec(CB128_RT_K, nq, d), zspec(CB128_RT_QR, nq, d),
                zspec(CB128_RT_KR, nq, d), zspec(CB_RT_V, nv, d), tspec(d), tspec(d)]

    return pl.pallas_call(
        _ret_kernel,
        grid=(bsz, tpb),
        in_specs=[pl.BlockSpec(memory_space=pltpu.SMEM)] + side(0) + side(1)
                 + [pl.BlockSpec((2, nq), lambda b, s: (0, 0))],
        out_specs=[pl.BlockSpec((TM, nv), lambda b, s: (_scan_tiles(b, s, tpb)[0], 0)),
                   pl.BlockSpec((TM, nv), lambda b, s: (_scan_tiles(b, s, tpb)[1], 0))],
        out_shape=[jax.ShapeDtypeStruct((m, nv), F32)] * 2,
        scratch_shapes=[pltpu.VMEM((nv, nq), F32), pltpu.VMEM((nv, nq), F32)],
        compiler_params=_cparams(("arbitrary", "arbitrary")),
        name="retention",
    )(log_gamma, z, z, z, z, z, cos_tab, sin_tab, z, z, z, z, z, cos_tab, sin_tab, lgq)


S5_NP = S5_GROUPS * S5_STATE


S5_T = SUBLANES
S5_ROW_APPLY, S5_ROW_END, S5_ROW_GROUP, S5_ROW_STEP = 0, 8, 16, 24


def _s5g_direction(u_ref, bre_ref, bim_ref, cre_ref, cim_ref, kt_ref, tab_ref, o_ref, car_ref, reverse):
    ng = TM // S5_T
    nsg = ng // SUBLANES
    np_ = S5_NP
    w = BRANCH_W
    sub = lax.broadcasted_iota(jnp.int32, (1, SUBLANES, 1), 1)

    def neighbour(x3, k):
        shift = (SUBLANES - k) if reverse else k
        keep = (sub < SUBLANES - k) if reverse else (sub >= k)
        return jnp.where(keep, pltpu.roll(x3, shift, 1), 0.0)

    u3 = u_ref[...]
    u2 = u3.reshape(TM, w)
    ub = u2.astype(BF16)
    y = jnp.dot(ub, kt_ref[0, 0], preferred_element_type=F32)
    for tau in range(1, S5_T):
        y = y + jnp.dot(neighbour(u3, tau).reshape(TM, w).astype(BF16), kt_ref[0, tau],
                        preferred_element_type=F32)

    bur = jnp.dot(ub, bre_ref[0], preferred_element_type=F32).reshape(ng, S5_T, np_)
    bui = jnp.dot(ub, bim_ref[0], preferred_element_type=F32).reshape(ng, S5_T, np_)
    wr = tab_ref[0, 0, S5_ROW_END:S5_ROW_END + S5_T, :]
    wi = tab_ref[0, 1, S5_ROW_END:S5_ROW_END + S5_T, :]
    gsel = (lax.broadcasted_iota(jnp.int32, (ng, TM), 1) // S5_T
            == lax.broadcasted_iota(jnp.int32, (ng, TM), 0))
    gsel = jnp.where(gsel, 1.0, 0.0).astype(BF16)
    er = jnp.dot(gsel, (wr * bur - wi * bui).reshape(TM, np_).astype(BF16), preferred_element_type=F32)
    ei = jnp.dot(gsel, (wr * bui + wi * bur).reshape(TM, np_).astype(BF16), preferred_element_type=F32)

    sr = er.reshape(nsg, SUBLANES, np_)
    si = ei.reshape(nsg, SUBLANES, np_)
    for n, k in enumerate((1, 2, 4)):
        ar = tab_ref[0, 0, S5_ROW_STEP + n:S5_ROW_STEP + n + 1, :]
        ai = tab_ref[0, 1, S5_ROW_STEP + n:S5_ROW_STEP + n + 1, :]
        nr, ni = neighbour(sr, k), neighbour(si, k)
        sr, si = sr + ar * nr - ai * ni, si + ar * ni + ai * nr
    pr = tab_ref[0, 0, S5_ROW_GROUP:S5_ROW_GROUP + SUBLANES, :]
    pi = tab_ref[0, 1, S5_ROW_GROUP:S5_ROW_GROUP + SUBLANES, :]
    cin_r, cin_i = car_ref[0:1, :], car_ref[1:2, :]
    cr, ci = cin_r, cin_i
    blocks_r, blocks_i = [None] * nsg, [None] * nsg
    for sg in (reversed(range(nsg)) if reverse else range(nsg)):
        br = sr[sg] + pr * cr - pi * ci
        bi = si[sg] + pr * ci + pi * cr
        blocks_r[sg], blocks_i[sg] = br, bi
        cr, ci = (br[0:1, :], bi[0:1, :]) if reverse else (br[SUBLANES - 1:, :], bi[SUBLANES - 1:, :])
    car_ref[0:1, :] = cr
    car_ref[1:2, :] = ci
    end_r = jnp.concatenate(blocks_r, axis=0)
    end_i = jnp.concatenate(blocks_i, axis=0)
    grow = lax.broadcasted_iota(jnp.int32, (ng, 1), 0)
    edge = (ng - 1) if reverse else 0
    shift = (ng - 1) if reverse else 1
    pre_r = jnp.where(grow == edge, cin_r, pltpu.roll(end_r, shift, 0))
    pre_i = jnp.where(grow == edge, cin_i, pltpu.roll(end_i, shift, 0))

    ar = tab_ref[0, 0, S5_ROW_APPLY:S5_ROW_APPLY + S5_T, :][None]
    ai = tab_ref[0, 1, S5_ROW_APPLY:S5_ROW_APPLY + S5_T, :][None]
    pre_r3 = pre_r[:, None, :]
    pre_i3 = pre_i[:, None, :]
    xr = (ar * pre_r3 - ai * pre_i3).reshape(TM, np_).astype(BF16)
    xi = (ar * pre_i3 + ai * pre_r3).reshape(TM, np_).astype(BF16)
    o_ref[...] = (y + jnp.dot(xr, cre_ref[0], preferred_element_type=F32)
                  - jnp.dot(xi, cim_ref[0], preferred_element_type=F32))


def _s5g_kernel(uf, ub, bre, bim, cre, cim, ktf, tabf, bre2, bim2, cre2, cim2, ktb, tabb, of_ref, ob_ref,
                carf, carb):
    @pl.when(pl.program_id(1) == 0)
    def _():
        carf[...] = jnp.zeros_like(carf)
        carb[...] = jnp.zeros_like(carb)

    _s5g_direction(uf, bre, bim, cre, cim, ktf, tabf, of_ref, carf, False)
    _s5g_direction(ub, bre2, bim2, cre2, cim2, ktb, tabb, ob_ref, carb, True)


def s5g(z, bblk_re, bblk_im, cblk_re, cblk_im, ktoep, tab, bsz, tpb):
    m = z.shape[0]
    w = BRANCH_W
    ng = TM // S5_T
    z3 = z.reshape(m // S5_T, S5_T, z.shape[1])

    def pspec(shape, d):
        return pl.BlockSpec((1,) + shape, lambda b, s: (d,) + (0,) * len(shape))

    def uspec(d):
        return pl.BlockSpec((ng, S5_T, w), lambda b, s: (_scan_tiles(b, s, tpb)[d], 0, CB_S5))

    def side(d):
        return [pspec((w, S5_NP), d), pspec((w, S5_NP), d), pspec((S5_NP, w), d), pspec((S5_NP, w), d),
                pspec((S5_T, w, w), d), pspec((2, 32, S5_NP), d)]

    return pl.pallas_call(
        _s5g_kernel,
        grid=(bsz, tpb),
        in_specs=[uspec(0), uspec(1)] + side(0) + side(1),
        out_specs=[pl.BlockSpec((TM, w), lambda b, s: (_scan_tiles(b, s, tpb)[0], 0)),
                   pl.BlockSpec((TM, w), lambda b, s: (_scan_tiles(b, s, tpb)[1], 0))],
        out_shape=[jax.ShapeDtypeStruct((m, w), F32)] * 2,
        scratch_shapes=[pltpu.VMEM((2, S5_NP), F32), pltpu.VMEM((2, S5_NP), F32)],
        compiler_params=_cparams(("arbitrary", "arbitrary")),
        name="s5",
    )(z3, z3, bblk_re, bblk_im, cblk_re, cblk_im, ktoep, tab,
      bblk_re, bblk_im, cblk_re, cblk_im, ktoep, tab)


def s5g_params(lam_re, lam_im, log_dt, b_re, b_im, c_re, c_im):
    lr = jnp.minimum(lam_re.astype(F32), -1e-4)
    li = lam_im.astype(F32)
    dt = jnp.exp(log_dt.astype(F32))[..., None]

    def apow(k):
        mag = jnp.exp(k * lr * dt)
        return mag * jnp.cos(k * li * dt), mag * jnp.sin(k * li * dt)

    ar, ai = apow(1.0)
    den = lr * lr + li * li
    gr = ((ar - 1.0) * lr + ai * li) / den
    gi = (ai * lr - (ar - 1.0) * li) / den
    bbr = gr[..., None] * b_re - gi[..., None] * b_im
    bbi = gr[..., None] * b_im + gi[..., None] * b_re
    cr, ci = c_re.astype(F32), c_im.astype(F32)
    eye = jnp.eye(S5_GROUPS, dtype=F32)
    nch = S5_GROUPS * S5_GROUP_CH

    def blk_in(bb):
        return jnp.einsum('dgpc,gh->dgchp', bb, eye).reshape(2, nch, S5_NP).astype(BF16)

    def blk_out(cc):
        return jnp.einsum('dgcp,gh->dgphc', cc, eye).reshape(2, S5_NP, nch).astype(BF16)

    taus = jnp.arange(S5_T, dtype=F32)[:, None, None, None]
    pr, pi = apow(taus)
    car = cr * pr[:, :, :, None, :] - ci * pi[:, :, :, None, :]
    cai = cr * pi[:, :, :, None, :] + ci * pr[:, :, :, None, :]
    k = (jnp.einsum('tdgop,dgpi->dtgio', car, bbr, precision=HIGHEST)
         - jnp.einsum('tdgop,dgpi->dtgio', cai, bbi, precision=HIGHEST))
    ktoep = jnp.einsum('dtgio,gh->dtgiho', k, eye).reshape(2, S5_T, nch, nch).astype(BF16)

    t = S5_T
    j = np.arange(t)
    fwd = np.concatenate([j + 1, t - 1 - j, t * (j + 1), t * np.array([1, 2, 4])])
    bwd = np.concatenate([t - j, j, t * (t - j), t * np.array([1, 2, 4])])
    exps = np.zeros((2, 32), np.float32)
    exps[0, :fwd.size], exps[1, :bwd.size] = fwd, bwd
    used = (np.arange(32) < fwd.size).astype(np.float32)[None, :, None, None]
    e = jnp.asarray(exps)[:, :, None, None]
    mag = jnp.exp(e * (lr * dt)[:, None]) * used
    ang = e * (li * dt)[:, None]
    tab = jnp.stack([(mag * jnp.cos(ang)).reshape(2, 32, S5_NP),
                     (mag * jnp.sin(ang)).reshape(2, 32, S5_NP)], axis=1)
    return blk_in(bbr), blk_in(bbi), blk_out(cr), blk_out(ci), ktoep, tab


def _hy_prep_kernel(u_ref, prev_ref, next_ref, w_ref, b_ref, x0_ref, vx_ref, *, tpb):
    i = pl.program_id(0)
    r = i % tpb
    first = (r == 0) | (r == tpb - 1)
    last = r >= tpb - 2
    u = u_ref[...]
    w = w_ref[...]
    up_row = jnp.where(first, 0.0, prev_ref[SUBLANES - 1:SUBLANES, :])
    dn_row = jnp.where(last, 0.0, next_ref[0:1, :])
    row = lax.broadcasted_iota(jnp.int32, (TM, 1), 0)
    um = jnp.where(row == 0, up_row, pltpu.roll(u, 1, 0))
    un = jnp.where(row == TM - 1, dn_row, pltpu.roll(u, TM - 1, 0))
    y = um * w[0:1, :] + u * w[1:2, :] + un * w[2:3, :] + b_ref[...]
    x0_ref[...] = y[:, 0:HY_W]
    vx_ref[...] = y[:, 2 * HY_W:3 * HY_W] * y[:, HY_W:2 * HY_W]


def hy_prep(z, conv_w, conv_b, bsz, tpb):
    m = z.shape[0]
    w3 = 3 * HY_W
    rb = TM // SUBLANES
    nrb = m // SUBLANES
    return pl.pallas_call(
        functools.partial(_hy_prep_kernel, tpb=tpb),
        grid=(m // TM,),
        in_specs=[pl.BlockSpec((TM, w3), lambda i: (i, 0)),
                  pl.BlockSpec((SUBLANES, w3), lambda i: (jnp.maximum(i * rb - 1, 0), 0)),
                  pl.BlockSpec((SUBLANES, w3), lambda i: (jnp.minimum((i + 1) * rb, nrb - 1), 0)),
                  pl.BlockSpec((3, w3), lambda i: (0, 0)),
                  pl.BlockSpec((1, w3), lambda i: (0, 0))],
        out_specs=[pl.BlockSpec((TM, HY_W), lambda i: (i, 0)),
                   pl.BlockSpec((TM, HY_W), lambda i: (_split_tile(i, bsz, tpb), 0))],
        out_shape=[jax.ShapeDtypeStruct((m, HY_W), F32)] * 2,
        compiler_params=_cparams(("parallel",)),
        name="hy_prep",
    )(z, z, z, conv_w, conv_b.reshape(1, w3))


def _hy_filter_kernel(zf_ref, w1_ref, b1_ref, f1_ref, w2_ref, b2_ref, f2_ref, w3_ref, dl_ref,
                      hfb_ref, ss_ref):
    i = pl.program_id(0)
    zf = zf_ref[...]
    hid = jnp.sin(f1_ref[...] * (_dot_hi(zf, w1_ref[...]) + b1_ref[...]))
    hid = jnp.sin(f2_ref[...] * (_dot_hi(hid, w2_ref[...]) + b2_ref[...]))
    filt = _dot_hi(hid, w3_ref[...])
    window = jnp.exp(-zf[:, 0:1] * dl_ref[...])
    hf = filt[:, 0:HY_W] * window
    row = lax.broadcasted_iota(jnp.int32, (TM, 1), 0) + i * TM
    hb = jnp.where(row == 0, 0.0, filt[:, HY_W:2 * HY_W] * window)
    hfb_ref[0] = hf
    hfb_ref[1] = hb
    part = jnp.sum(hf * hf + hb * hb, axis=0, keepdims=True)

    @pl.when(i == 0)
    def _():
        ss_ref[...] = jnp.zeros_like(ss_ref)

    ss_ref[...] = ss_ref[...] + part


def hy_filter(length, w1, b1, fr1, w2, b2, fr2, w3):
    t = jnp.linspace(0.0, 1.0, length, dtype=F32)[:, None]
    wv = (2.0 * math.pi / length) * jnp.arange(length, dtype=F32)[:, None]
    f = jnp.linspace(1e-4, HY_BANDS - 1.0, HY_BANDS, dtype=F32)[None, :]
    zfeat = jnp.concatenate([t, jnp.cos(f * wv), -jnp.sin(f * wv)], axis=-1)
    zfeat = jnp.pad(zfeat, ((0, 0), (0, LANES - HY_EMB)))
    w1p = jnp.pad(w1, ((0, LANES - HY_EMB), (0, 0)))
    deltas = jnp.abs(jnp.linspace(math.log(HY_TARGET) / HY_SLOW_DECAY,
                                  math.log(HY_TARGET) / HY_FAST_DECAY, HY_W, dtype=F32))[None, :]
    full = lambda shape: pl.BlockSpec(shape, lambda i: (0,) * len(shape))
    return pl.pallas_call(
        _hy_filter_kernel,
        grid=(length // TM,),
        in_specs=[pl.BlockSpec((TM, LANES), lambda i: (i, 0)),
                  full((LANES, HY_ORDER)), full((1, HY_ORDER)), full((1, HY_ORDER)),
                  full((HY_ORDER, HY_ORDER)), full((1, HY_ORDER)), full((1, HY_ORDER)),
                  full((HY_ORDER, 2 * HY_W)), full((1, HY_W))],
        out_specs=[pl.BlockSpec((2, TM, HY_W), lambda i: (0, i, 0)),
                   pl.BlockSpec((1, HY_W), lambda i: (0, 0))],
        out_shape=[jax.ShapeDtypeStruct((2, length, HY_W), F32), jax.ShapeDtypeStruct((1, HY_W), F32)],
        compiler_params=_cparams(("arbitrary",)),
        name="hy_filter",
    )(zfeat, w1p, b1.reshape(1, -1), fr1.reshape(1, -1), w2, b2.reshape(1, -1), fr2.reshape(1, -1), w3, deltas)


def _dft_consts(length):
    n = 2 * length
    ns = n // FFT_FAST
    a = jnp.arange(ns, dtype=jnp.int32)
    ang = ((a[:, None] * a[None, :]) % ns).astype(F32) * (2.0 * math.pi / ns)
    f_fwd = jnp.concatenate([jnp.cos(ang), -jnp.sin(ang)], axis=0)
    f_inv = jnp.concatenate([jnp.cos(ang), -jnp.sin(ang)], axis=1)
    b = jnp.arange(FFT_FAST, dtype=jnp.int32)
    th1 = (a[:, None] * b[None, :]).astype(F32) * (2.0 * math.pi / n)
    th2 = ((b[:, None] * b[None, :]) % FFT_FAST).astype(F32) * (2.0 * math.pi / FFT_FAST)
    c1, s1 = jnp.cos(th1)[:, None, :], jnp.sin(th1)[:, None, :]
    c2, s2 = jnp.cos(th2)[None, :, :], jnp.sin(th2)[None, :, :]
    mr = c1 * c2 - s1 * s2
    mi = -(s1 * c2 + c1 * s2)
    m_fwd = jnp.concatenate([jnp.concatenate([mr, -mi], axis=2),
                             jnp.concatenate([mi, mr], axis=2)], axis=1)
    ir = jnp.transpose(mr, (0, 2, 1)) / n
    ii = jnp.transpose(-mi, (0, 2, 1)) / n
    m_inv = jnp.concatenate([jnp.concatenate([ir, -ii], axis=2),
                             jnp.concatenate([ii, ir], axis=2)], axis=1)
    return tuple(t.astype(BF16) for t in (f_fwd, f_inv, m_fwd, m_inv))


def _dft_slow_fwd_kernel(f_ref, x_ref, o_ref):
    ns = o_ref.shape[2]
    for bl in range(x_ref.shape[1]):
        r = jnp.dot(f_ref[...], x_ref[:, bl, :].astype(BF16), preferred_element_type=F32)
        o_ref[0, 0, :, bl, :] = r[0:ns]
        o_ref[0, 1, :, bl, :] = r[ns:]


def dft_slow_fwd(fmat, x3, bx):
    two_ns, na = fmat.shape
    ns = two_ns // 2
    cw = x3.shape[-1]
    kb = SUBLANES
    return pl.pallas_call(
        _dft_slow_fwd_kernel,
        grid=(bx, FFT_FAST // kb),
        in_specs=[pl.BlockSpec((two_ns, na), lambda b, j: (0, 0)),
                  pl.BlockSpec((na, kb, cw), lambda b, j: (b, j, 0))],
        out_specs=pl.BlockSpec((1, 2, ns, kb, cw), lambda b, j: (b, 0, 0, j, 0)),
        out_shape=jax.ShapeDtypeStruct((bx, 2, ns, FFT_FAST, cw), F32),
        compiler_params=_cparams(("parallel", "parallel")),
        name="dft_slow_fwd",
    )(fmat, x3)


def _dft_slow_inv_kernel(f_ref, b_ref, o_ref, *, bx):
    @pl.when(pl.program_id(0) < bx)
    def _():
        for bl in range(o_ref.shape[1]):
            bb = jnp.concatenate([b_ref[0, 0, :, bl, :], b_ref[0, 1, :, bl, :]], axis=0).astype(BF16)
            o_ref[:, bl, :] = jnp.dot(f_ref[...], bb, preferred_element_type=F32)

    @pl.when(pl.program_id(0) >= bx)
    def _():
        o_ref[...] = jnp.zeros_like(o_ref)


def dft_slow_inv(fmat, bv):
    bx, _, ns, nf, cw = bv.shape
    na = fmat.shape[0]
    kb = SUBLANES
    return pl.pallas_call(
        functools.partial(_dft_slow_inv_kernel, bx=bx),
        grid=(bx + 1, nf // kb),
        in_specs=[pl.BlockSpec((na, 2 * ns), lambda b, j: (0, 0)),
                  pl.BlockSpec((1, 2, ns, kb, cw), lambda b, j: (jnp.minimum(b, bx - 1), 0, 0, j, 0))],
        out_specs=pl.BlockSpec((na, kb, cw), lambda b, j: (b, j, 0)),
        out_shape=jax.ShapeDtypeStruct(((bx + 1) * na, nf, cw), F32),
        compiler_params=_cparams(("parallel", "parallel")),
        name="dft_slow_inv",
    )(fmat, bv)


FFT_GROUP = 8


def _dft_fast_filter_kernel(m_ref, a_ref, sc_ref, o_ref):
    sc = sc_ref[...]
    for j in range(FFT_GROUP):
        af = jnp.concatenate([a_ref[0, 0, j], a_ref[0, 1, j]], axis=0).astype(BF16)
        ab = jnp.concatenate([a_ref[1, 0, j], a_ref[1, 1, j]], axis=0).astype(BF16)
        xf = jnp.dot(m_ref[j], af, preferred_element_type=F32)
        xb = jnp.dot(m_ref[j], ab, preferred_element_type=F32)
        o_ref[0, j] = (xf[0:FFT_FAST] + xb[0:FFT_FAST]) * sc
        o_ref[1, j] = (xf[FFT_FAST:] - xb[FFT_FAST:]) * sc


def dft_fast_filter(m_fwd, a5, scale):
    _, _, ns, nf, cw = a5.shape
    g = FFT_GROUP
    return pl.pallas_call(
        _dft_fast_filter_kernel,
        grid=(ns // g,),
        in_specs=[pl.BlockSpec((g, 2 * nf, 2 * nf), lambda i: (i, 0, 0)),
                  pl.BlockSpec((2, 2, g, nf, cw), lambda i: (0, 0, i, 0, 0)),
                  pl.BlockSpec((1, cw), lambda i: (0, 0))],
        out_specs=pl.BlockSpec((2, g, nf, cw), lambda i: (0, i, 0, 0)),
        out_shape=jax.ShapeDtypeStruct((2, ns, nf, cw), F32),
        compiler_params=_cparams(("parallel",)),
        name="dft_fast_filter",
    )(m_fwd, a5, scale)


def _dft_fast_conv_kernel(mf_ref, mi_ref, a_ref, k_ref, o_ref):
    for j in range(FFT_GROUP):
        a = jnp.concatenate([a_ref[0, 0, j], a_ref[0, 1, j]], axis=0).astype(BF16)
        x = jnp.dot(mf_ref[j], a, preferred_element_type=F32)
        xr, xi = x[0:FFT_FAST], x[FFT_FAST:]
        kr, ki = k_ref[0, j], k_ref[1, j]
        y = jnp.concatenate([xr * kr - xi * ki, xr * ki + xi * kr], axis=0).astype(BF16)
        bv = jnp.dot(mi_ref[j], y, preferred_element_type=F32)
        o_ref[0, 0, j] = bv[0:FFT_FAST]
        o_ref[0, 1, j] = bv[FFT_FAST:]


def dft_fast_conv(m_fwd, m_inv, a5, kf):
    bx, _, ns, nf, cw = a5.shape
    g = FFT_GROUP
    return pl.pallas_call(
        _dft_fast_conv_kernel,
        grid=(bx, ns // g),
        in_specs=[pl.BlockSpec((g, 2 * nf, 2 * nf), lambda b, i: (i, 0, 0)),
                  pl.BlockSpec((g, 2 * nf, 2 * nf), lambda b, i: (i, 0, 0)),
                  pl.BlockSpec((1, 2, g, nf, cw), lambda b, i: (b, 0, i, 0, 0)),
                  pl.BlockSpec((2, g, nf, cw), lambda b, i: (0, i, 0, 0))],
        out_specs=pl.BlockSpec((1, 2, g, nf, cw), lambda b, i: (b, 0, i, 0, 0)),
        out_shape=jax.ShapeDtypeStruct((bx, 2, ns, nf, cw), F32),
        compiler_params=_cparams(("parallel", "parallel")),
        name="dft_fast_conv",
    )(m_fwd, m_inv, a5, kf)


def hyena_long_conv(vx_rows, bsz, length, hfb, scale):
    cw = vx_rows.shape[-1]
    ns = 2 * length // FFT_FAST
    f_fwd, f_inv, m_fwd, m_inv = _dft_consts(length)
    fa = dft_slow_fwd(f_fwd[:, :ns // 2], hfb.reshape(-1, FFT_FAST, cw), 2)
    kf = dft_fast_filter(m_fwd, fa, scale)
    xa = dft_slow_fwd(f_fwd[:, :ns // 2], vx_rows.reshape(-1, FFT_FAST, cw), bsz)
    bv = dft_fast_conv(m_fwd, m_inv, xa, kf)
    return dft_slow_inv(f_inv[:ns // 2], bv).reshape((bsz + 1) * length, cw)


def _hy_ctx_conv_kernel(v_ref, g_ref, y_hbm_ref, o_ref, *, lc):
    del y_hbm_ref
    nb = lc // SUBLANES

    def body(gi, acc):
        base = pl.multiple_of((nb - 1 - gi) * SUBLANES, SUBLANES)
        win = g_ref[pl.ds(base, lc + SUBLANES), :]
        for j in range(SUBLANES):
            vrow = v_ref[pl.ds(gi * SUBLANES + j, 1), :]
            acc = acc + win[SUBLANES - 1 - j:SUBLANES - 1 - j + lc, :] * vrow
        return acc

    o_ref[...] = lax.fori_loop(0, nb, body, jnp.zeros((lc, HY_W), F32))


def hy_ctx_conv(vx, gwin, yconv, bsz, tpb, lc):
    ctx_tile = lambda b: (bsz * (tpb - 1) + b, 0)
    return pl.pallas_call(
        functools.partial(_hy_ctx_conv_kernel, lc=lc),
        grid=(bsz,),
        in_specs=[pl.BlockSpec((lc, HY_W), ctx_tile),
                  pl.BlockSpec((2 * lc, HY_W), lambda b: (0, 0)),
                  pl.BlockSpec(memory_space=pl.ANY)],
        out_specs=pl.BlockSpec((lc, HY_W), ctx_tile),
        out_shape=jax.ShapeDtypeStruct(yconv.shape, F32),
        input_output_aliases={2: 0},
        compiler_params=_cparams(("parallel",)),
        name="hy_ctx_conv",
    )(vx, gwin, yconv)


ROUTE_E1, ROUTE_E2, ROUTE_W1, ROUTE_W2, ROUTE_R1, ROUTE_R2 = 0, 1, 2, 3, 4, 5
NEG_BIG = -1e30


def _gelu_tanh(x):
    return 0.5 * x * (1.0 + jnp.tanh(math.sqrt(2.0 / math.pi) * (x + 0.044715 * (x * x * x))))


def _lane_pick(slab, lane, idx):
    return jnp.sum(jnp.where(lane == idx, slab, 0.0), axis=-1, keepdims=True)


def _route_tile(logits, valid, base_ref, cnt_ref, route_ref, rt_ref):
    lane = lax.broadcasted_iota(jnp.int32, (TM, LANES), 1).astype(F32)
    gl = jnp.where(lane < MOE_GROUPS, logits, NEG_BIG)
    gmax = jnp.max(gl, axis=-1, keepdims=True)
    yield
    grp_p = 1.0 / jnp.sum(jnp.exp(gl - gmax), axis=-1, keepdims=True)
    gi = jnp.min(jnp.where(gl == gmax, lane, float(LANES)), axis=-1, keepdims=True)
    yield
    lo = MOE_GROUPS + MOE_PER_GROUP * gi
    el = jnp.where((lane >= lo) & (lane < lo + MOE_PER_GROUP), logits, NEG_BIG)
    m1 = jnp.max(el, axis=-1, keepdims=True)
    yield
    i1 = jnp.min(jnp.where(el == m1, lane, float(LANES)), axis=-1, keepdims=True)
    yield
    el2 = jnp.where(lane == i1, NEG_BIG, el)
    m2 = jnp.max(el2, axis=-1, keepdims=True)
    yield
    i2 = jnp.min(jnp.where(el2 == m2, lane, float(LANES)), axis=-1, keepdims=True)
    yield
    ratio = jnp.exp(m2 - m1)
    wt1 = grp_p / (1.0 + ratio)
    wt2 = grp_p * ratio / (1.0 + ratio)
    e1 = i1 - MOE_GROUPS
    e2 = i2 - MOE_GROUPS

    oh1 = jnp.where(lane == e1, valid, 0.0)
    oh2 = jnp.where(lane == e2, valid, 0.0)
    ri = lax.broadcasted_iota(jnp.int32, (TM, TM), 0)
    ci = lax.broadcasted_iota(jnp.int32, (TM, TM), 1)
    below = jnp.where(ci < ri, 1.0, 0.0).astype(BF16)
    c1 = jnp.dot(below, oh1.astype(BF16), preferred_element_type=F32)
    c2 = jnp.dot(below, oh2.astype(BF16), preferred_element_type=F32)
    tot1 = jnp.sum(oh1, axis=0, keepdims=True)
    tot2 = jnp.sum(oh2, axis=0, keepdims=True)
    yield
    base = base_ref[...]
    r1 = jnp.sum(oh1 * (base + c1), axis=-1, keepdims=True)
    r2 = jnp.sum(oh2 * (base + tot1 + c2), axis=-1, keepdims=True)
    base = base + tot1 + tot2
    base_ref[...] = base
    cnt_ref[...] = jnp.broadcast_to(base, cnt_ref.shape)
    route = jnp.zeros((TM, LANES), F32)
    for idx, val in ((ROUTE_E1, e1), (ROUTE_E2, e2), (ROUTE_W1, wt1), (ROUTE_W2, wt2),
                     (ROUTE_R1, r1), (ROUTE_R2, r2)):
        route = jnp.where(lane == idx, val, route)
    route_ref[...] = route
    rt_ref[...] = route.T[0:SUBLANES, :]


def _merge_kernel(x_ref, mod_ref, n1_ref, n2_ref, wg0_ref, wg1_ref, wg2_ref, wg3_ref, zhg_ref, zrt_ref, zs5_ref,
                  hgf_ref, hgb_ref, rtf_ref, rtb_ref, s5f_ref, s5b_ref, yc_ref, vx_ref, x0_ref,
                  vec_ref, glu1_ref, glu2_ref, wb_ref, wo_ref, wr_ref, br_ref,
                  xo_ref, h2_ref, route_ref, rt_ref, cnt_ref, base_ref, lg_s, *, d):
    i = pl.program_id(0)

    @pl.when(i == 0)
    def _():
        base_ref[...] = jnp.zeros_like(base_ref)
        lg_s[...] = jnp.zeros_like(lg_s)

    routing = _route_tile(lg_s[...], jnp.where(i > 0, 1.0, 0.0), base_ref, cnt_ref, route_ref, rt_ref)
    advance_routing = lambda: next(routing, None)

    w = BRANCH_W
    x = x_ref[...]
    mod = mod_ref[0]
    h = _modulated_norm(x, n1_ref[...], mod[:, 0:d], mod[:, d:2 * d]).astype(BF16)
    advance_routing()

    hd = w // HG_HEADS
    hr = lax.broadcasted_iota(jnp.int32, (w, w), 0) // hd
    hc = lax.broadcasted_iota(jnp.int32, (w, w), 1) // hd
    hm = jnp.where(hr == hc, 1.0, 0.0).astype(BF16)
    inv = 1.0 / hd

    hg = hgf_ref[...] + hgb_ref[...]
    ms = _split_dot(hg * hg, hm) * inv
    br_hg = hg * lax.rsqrt(ms + EPS) * vec_ref[0:1, :] * _silu(zhg_ref[...])
    advance_routing()

    rt = rtf_ref[...] + rtb_ref[...]
    cen = rt - _split_dot(rt, hm) * inv
    var = _split_dot(cen * cen, hm) * inv
    br_rt = cen * lax.rsqrt(var + EPS) * vec_ref[1:2, :] * _silu(zrt_ref[...])
    advance_routing()

    vx = vx_ref[...]
    br_hy = (yc_ref[...] + vx * vec_ref[3:4, :]) * x0_ref[...]

    y5 = _gelu_tanh(s5f_ref[...] + s5b_ref[...] + zs5_ref[...] * vec_ref[2:3, :]).astype(BF16)
    br_s5 = (jnp.dot(y5, glu1_ref[...], preferred_element_type=F32)
             * _sigmoid(jnp.dot(y5, glu2_ref[...], preferred_element_type=F32)))
    advance_routing()

    merged = jnp.zeros((TM, d), F32)
    for n, (br, wg_ref) in enumerate(zip((br_hg, br_hy, br_rt, br_s5), (wg0_ref, wg1_ref, wg2_ref, wg3_ref))):
        gate = _sigmoid(jnp.dot(h, wg_ref[...], preferred_element_type=F32))
        merged = merged + gate * jnp.dot(br.astype(BF16), wb_ref[n], preferred_element_type=F32)
        advance_routing()
    for _ in routing:
        pass
    mix = jnp.dot(merged.astype(BF16), wo_ref[...], preferred_element_type=F32)
    x_mid = x + mod[:, 2 * d:3 * d] * mix
    xo_ref[...] = x_mid
    h2 = _modulated_norm(x_mid, n2_ref[...], mod[:, 3 * d:4 * d], mod[:, 4 * d:5 * d])
    h2_ref[...] = h2

    lg_s[...] = _dot3(h2, wr_ref[...]) + br_ref[...]


def merge(x_all, mod3, n1, n2, w_in_bf, layer, z, hgf, hgb, rtf, rtb, s5f, s5b, yconv, vx, x0c,
          vecs, glu1, glu2, w_branch, w_out, w_router, b_router, bsz, tpb):
    m, d = x_all.shape
    w = BRANCH_W
    nt = m // TM
    cur = lambda i: jnp.minimum(i, nt - 1)
    prev = lambda i: jnp.maximum(i - 1, 0)
    row = lambda width: pl.BlockSpec((TM, width), lambda i: (cur(i), 0))
    srow = pl.BlockSpec((TM, w), lambda i: (_split_tile(cur(i), bsz, tpb), 0))
    zcol = lambda cb: pl.BlockSpec((TM, w), lambda i: (cur(i), cb))
    full = lambda shape: pl.BlockSpec(shape, lambda i: (0,) * len(shape))
    gate_w = lambda n: pl.BlockSpec((None, d, d), lambda i: (layer, 0, N_MIX // d + n))
    return pl.pallas_call(
        functools.partial(_merge_kernel, d=d),
        grid=(nt + 1,),
        in_specs=[row(d),
                  pl.BlockSpec((1, 1, 6 * d), lambda i: (_seg_of_tile(cur(i), tpb), 0, 0)),
                  full((1, d)), full((1, d)), gate_w(0), gate_w(1), gate_w(2), gate_w(3),
                  zcol(CB_HG_G), zcol(CB_RT_G), zcol(CB_S5)]
                 + [row(w)] * 6 + [srow, srow, row(w)]
                 + [full((SUBLANES, w)), full((w, w)), full((w, w)), full((N_BRANCH, w, d)), full((d, d)),
                    full((d, LANES)), full((1, LANES))],
        out_specs=[row(d), row(d),
                   pl.BlockSpec((TM, LANES), lambda i: (prev(i), 0)),
                   pl.BlockSpec((None, SUBLANES, TM), lambda i: (prev(i), 0, 0)),
                   full((SUBLANES, LANES))],
        out_shape=[jax.ShapeDtypeStruct((m, d), F32), jax.ShapeDtypeStruct((m, d), F32),
                   jax.ShapeDtypeStruct((m, LANES), F32), jax.ShapeDtypeStruct((nt, SUBLANES, TM), F32),
                   jax.ShapeDtypeStruct((SUBLANES, LANES), F32)],
        scratch_shapes=[pltpu.VMEM((1, LANES), F32), pltpu.VMEM((TM, LANES), F32)],
        compiler_params=_cparams(("arbitrary",)),
        name="merge",
    )(x_all, mod3, n1, n2, w_in_bf, w_in_bf, w_in_bf, w_in_bf, z, z, z,
      hgf, hgb, rtf, rtb, s5f, s5b, yconv, vx, x0c,
      vecs, glu1, glu2, w_branch, w_out, w_router, b_router)


DMA_UNROLL = 8


def _row_copy(src_ref, src_row, dst_ref, dst_row, sem):
    return pltpu.make_async_copy(src_ref.at[pl.ds(src_row, 1), :], dst_ref.at[pl.ds(dst_row, 1), :], sem)


DISPATCH_ROWS = 6 * TM


def _dispatch_kernel(dest_ref, h_ref, init_ref, buf_ref, sem, *, m):
    del init_ref
    rows = h_ref.shape[0]
    base = pl.program_id(0) * rows

    def issue(r, carry):
        for k in range(2):
            _row_copy(h_ref, r, buf_ref, dest_ref[k * m + base + r], sem).start(priority=k)
        return carry

    lax.fori_loop(0, rows, issue, 0, unroll=DMA_UNROLL)
    for k in range(2):
        pltpu.make_async_copy(h_ref, buf_ref.at[pl.ds(0, rows), :], sem).wait()


def moe_dispatch(dest, h2, init):
    m, d = h2.shape
    n_rows = init.shape[0]
    assert m % DISPATCH_ROWS == 0
    grid_spec = pltpu.PrefetchScalarGridSpec(
        num_scalar_prefetch=1,
        grid=(m // DISPATCH_ROWS,),
        in_specs=[pl.BlockSpec((DISPATCH_ROWS, d), lambda i, dest: (i, 0)),
                  pl.BlockSpec(memory_space=pl.ANY)],
        out_specs=pl.BlockSpec(memory_space=pl.ANY),
        scratch_shapes=[pltpu.SemaphoreType.DMA(())],
    )
    return pl.pallas_call(
        functools.partial(_dispatch_kernel, m=m),
        grid_spec=grid_spec,
        out_shape=jax.ShapeDtypeStruct((n_rows, d), F32),
        input_output_aliases={2: 0},
        compiler_params=_cparams(("arbitrary",)),
        name="moe_dispatch",
    )(dest, h2, init)


def _expert_kernel(plan_ref, x_ref, w1_ref, w3_ref, w2_ref, y_ref, *, n_blocks):
    used = pl.program_id(0) < plan_ref[n_blocks]

    @pl.when(used)
    def _():
        xb = x_ref[...].astype(BF16)
        a = jnp.dot(xb, w1_ref[...].astype(BF16), preferred_element_type=F32)
        g = jnp.dot(xb, w3_ref[...].astype(BF16), preferred_element_type=F32)
        y_ref[...] = jnp.dot((_silu(a) * g).astype(BF16), w2_ref[...].astype(BF16),
                             preferred_element_type=F32)

    @pl.when(jnp.logical_not(used))
    def _():
        y_ref[...] = jnp.zeros_like(y_ref)


def moe_experts(plan, buf, w1, w3, w2, layer):
    n_rows, d = buf.shape
    hid = w1.shape[-1]
    n_blocks = n_rows // MOE_BLOCK
    blk = lambda j, plan: jnp.minimum(j, plan[n_blocks] - 1)
    grid_spec = pltpu.PrefetchScalarGridSpec(
        num_scalar_prefetch=1,
        grid=(n_blocks,),
        in_specs=[pl.BlockSpec((MOE_BLOCK, d), lambda j, plan: (blk(j, plan), 0)),
                  pl.BlockSpec((None, None, d, hid), lambda j, plan: (layer, plan[blk(j, plan)], 0, 0)),
                  pl.BlockSpec((None, None, d, hid), lambda j, plan: (layer, plan[blk(j, plan)], 0, 0)),
                  pl.BlockSpec((None, None, hid, d), lambda j, plan: (layer, plan[blk(j, plan)], 0, 0))],
        out_specs=pl.BlockSpec((MOE_BLOCK, d), lambda j, plan: (j, 0)),
    )
    return pl.pallas_call(
        functools.partial(_expert_kernel, n_blocks=n_blocks),
        grid_spec=grid_spec,
        out_shape=jax.ShapeDtypeStruct((n_rows, d), F32),
        compiler_params=_cparams(("arbitrary",)),
        name="moe_experts",
    )(plan, buf, w1, w3, w2)


def _combine_kernel(dest_ref, x_ref, route_ref, mod_ref, gfin_ref, y_ref, o_ref, gath, sem, *,
                    d, m, bsz, tpb, nr, final):
    step = pl.program_id(0) * nr + pl.program_id(1)
    slot = step % 2

    def start_gather(s, sl):
        base = ((s // nr) * tpb + s % nr) * TM

        def issue(r, carry):
            for k in range(2):
                _row_copy(y_ref, dest_ref[k * m + base + r], gath.at[sl, k], r, sem.at[sl]).start(priority=k)
            return carry

        lax.fori_loop(0, TM, issue, 0, unroll=DMA_UNROLL)

    @pl.when(step == 0)
    def _():
        start_gather(0, 0)

    @pl.when(step + 1 < bsz * nr)
    def _():
        start_gather(step + 1, 1 - slot)

    for k in range(2):
        pltpu.make_async_copy(y_ref.at[pl.ds(0, TM), :], gath.at[slot, k], sem.at[slot]).wait()
    route = route_ref[...]
    lane = lax.broadcasted_iota(jnp.int32, (TM, LANES), 1)
    moe = (_lane_pick(route, lane, ROUTE_W1) * gath[slot, 0]
           + _lane_pick(route, lane, ROUTE_W2) * gath[slot, 1])
    x_new = x_ref[...] + mod_ref[0][:, 5 * d:6 * d] * moe
    if final:
        ms = jnp.mean(x_new * x_new, axis=-1, keepdims=True)
        x_new = x_new * lax.rsqrt(ms + EPS) * gfin_ref[...]
    o_ref[...] = x_new


def moe_combine(dest, x_mid, route, mod3, gain_final, ybuf, bsz, tpb, final):
    m, d = x_mid.shape
    nr = tpb - 1 if final else tpb
    tile = lambda b, r: b * tpb + r
    grid_spec = pltpu.PrefetchScalarGridSpec(
        num_scalar_prefetch=1,
        grid=(bsz, nr),
        in_specs=[pl.BlockSpec((TM, d), lambda b, r, dest: (tile(b, r), 0)),
                  pl.BlockSpec((TM, LANES), lambda b, r, dest: (tile(b, r), 0)),
                  pl.BlockSpec((1, 1, 6 * d), lambda b, r, dest: (_seg_of_tile(tile(b, r), tpb), 0, 0)),
                  pl.BlockSpec((1, d), lambda b, r, dest: (0, 0)),
                  pl.BlockSpec(memory_space=pl.ANY)],
        out_specs=pl.BlockSpec((TM, d), lambda b, r, dest: (b * nr + r, 0)),
        scratch_shapes=[pltpu.VMEM((2, 2, TM, d), F32), pltpu.SemaphoreType.DMA((2,))],
    )
    return pl.pallas_call(
        functools.partial(_combine_kernel, d=d, m=m, bsz=bsz, tpb=tpb, nr=nr, final=final),
        grid_spec=grid_spec,
        out_shape=jax.ShapeDtypeStruct((bsz * nr * TM, d), F32),
        compiler_params=_cparams(("arbitrary", "arbitrary")),
        name="moe_combine",
    )(dest, x_mid, route, mod3, gain_final, ybuf)


def moe_plan(rt, counts_row, n_blocks):
    counts = counts_row[:MOE_EXPERTS].astype(jnp.int32)
    padded = (counts + MOE_BLOCK - 1) // MOE_BLOCK * MOE_BLOCK
    pad_end = jnp.cumsum(padded)
    pad_start = pad_end - padded
    by_choice = lambda f: jnp.transpose(rt[:, f:f + 2, :], (1, 0, 2)).reshape(2, -1).astype(jnp.int32)
    expert = by_choice(ROUTE_E1)
    rank = by_choice(ROUTE_R1)
    is_e = expert[None] == jnp.arange(MOE_EXPERTS, dtype=jnp.int32)[:, None, None]
    dest = (jnp.sum(jnp.where(is_e, pad_start[:, None, None], 0), axis=0) + rank).reshape(-1)
    block_row0 = jnp.arange(n_blocks, dtype=jnp.int32) * MOE_BLOCK
    block_expert = jnp.minimum(jnp.sum((pad_end[None, :] <= block_row0[:, None]).astype(jnp.int32), axis=1),
                               MOE_EXPERTS - 1)
    n_used = pad_end[MOE_EXPERTS - 1:] // MOE_BLOCK
    return dest, jnp.concatenate([block_expert, n_used])


def _rope_tables(length, lc):
    half = RET_DK // 2
    freqs = ROPE_BASE ** (-jnp.arange(0, half, 2, dtype=F32) / half)
    t = jnp.arange(length)
    row_pos = (t // GRID_W).astype(F32)[:, None]
    col_pos = (t % GRID_W).astype(F32)[:, None]
    lane = jnp.arange(RET_HEADS * RET_DK)
    fr = freqs[lane % (half // 2)][None, :]
    ang = jnp.where(((lane % RET_DK) < half)[None, :], row_pos, col_pos) * fr
    cos = jnp.concatenate([jnp.cos(ang), jnp.ones((lc, lane.shape[0]), F32)], axis=0)
    sin = jnp.concatenate([jnp.sin(ang), jnp.zeros((lc, lane.shape[0]), F32)], axis=0)
    return cos, sin


def _rotate_half_cols(wq):
    quarter = RET_DK // 4
    lane = np.arange(wq.shape[1])
    first = (lane % (2 * quarter)) < quarter
    src = np.where(first, lane + quarter, lane - quarter)
    sign = jnp.asarray(np.where(first, -1.0, 1.0), F32)
    return wq[:, src] * sign


def kernel(x, c, ctx, c_ctx, w_mod, b_mod, norm1, norm2, w_in, hg_lb_logits, hg_norm, hy_conv_w, hy_conv_b, hy_filt_w1, hy_filt_b1, hy_filt_freq1, hy_filt_w2, hy_filt_b2, hy_filt_freq2, hy_filt_w3, hy_skip, ret_decay_logit, ret_norm, s5_lam_re, s5_lam_im, s5_log_dt, s5_b_re, s5_b_im, s5_c_re, s5_c_im, s5_d, s5_glu_w1, s5_glu_w2, w_branch, w_out, moe_w_grp, moe_b_grp, moe_w_exp, moe_b_exp, moe_w1, moe_w3, moe_w2, norm_final):
    bsz, length, d = x.shape
    lc = ctx.shape[1]
    depth = w_mod.shape[0]
    assert lc == TM and length % TM == 0 and (2 * length) % (FFT_FAST * FFT_GROUP) == 0
    tt = length + lc
    tpb = tt // TM
    m = bsz * tt
    n_blocks = -(-(m * 2) // MOE_BLOCK) + MOE_EXPERTS
    w = BRANCH_W

    x_all = jnp.concatenate([x, ctx], axis=1).reshape(m, d)
    cmat = jnp.zeros((SUBLANES, d), F32).at[0].set(c_ctx).at[1:1 + bsz].set(c)
    mods = modvec(cmat, w_mod, b_mod)
    lb_sm = jax.nn.softmax(hg_lb_logits.astype(F32), axis=0)
    lb_all = jnp.cumsum(lb_sm, axis=0) - lb_sm[0]
    cos_tab, sin_tab = _rope_tables(length, lc)
    log_gamma = jax.nn.log_sigmoid(ret_decay_logit.astype(F32))
    gfin = norm_final.reshape(1, d)

    w_in_bf = w_in.astype(BF16)
    rq0 = 5 * w + 3 * HY_W
    nq = RET_HEADS * RET_DK

    for l in range(depth):
        w_rot = jnp.concatenate([_rotate_half_cols(w_in[l, :, rq0:rq0 + nq]),
                                 _rotate_half_cols(w_in[l, :, rq0 + nq:rq0 + 2 * nq])], axis=1).astype(BF16)
        mod3 = mods[l].reshape(SUBLANES, 1, 6 * d)

        z = inproj(x_all, mod3, norm1[l].reshape(1, d), w_in_bf, l, w_rot, tpb)
        hgf, hgb = hgrn(z, lb_all[l], bsz, tpb)
        rtf, rtb = retention(z, cos_tab, sin_tab, log_gamma[l], bsz, tpb)
        s5f, s5b = s5g(z, *s5g_params(s5_lam_re[l], s5_lam_im[l], s5_log_dt[l], s5_b_re[l], s5_b_im[l],
                                      s5_c_re[l], s5_c_im[l]), bsz, tpb)

        x0c, vx = hy_prep(z, hy_conv_w[l], hy_conv_b[l], bsz, tpb)
        fargs = (hy_filt_w1[l], hy_filt_b1[l], hy_filt_freq1[l], hy_filt_w2[l], hy_filt_b2[l],
                 hy_filt_freq2[l], hy_filt_w3[l])
        hfb, ss = hy_filter(length, *fargs)
        y_lat = hyena_long_conv(vx, bsz, length, hfb, lax.rsqrt(ss + EPS))
        hfbc, ssc = hy_filter(lc, *fargs)
        gwin = jnp.concatenate([hfbc[1, 1:][::-1], hfbc[0], jnp.zeros((1, HY_W), F32)], axis=0) \
            * lax.rsqrt(ssc + EPS)
        yconv = hy_ctx_conv(vx, gwin, y_lat, bsz, tpb, lc)

        vecs = jnp.zeros((SUBLANES, w), F32).at[0].set(hg_norm[l]).at[1].set(ret_norm[l]) \
            .at[2].set(s5_d[l]).at[3].set(hy_skip[l])
        w_router = jnp.zeros((d, LANES), F32).at[:, :MOE_GROUPS].set(moe_w_grp[l]) \
            .at[:, MOE_GROUPS:MOE_GROUPS + MOE_EXPERTS].set(moe_w_exp[l])
        b_router = jnp.zeros((1, LANES), F32).at[0, :MOE_GROUPS].set(moe_b_grp[l]) \
            .at[0, MOE_GROUPS:MOE_GROUPS + MOE_EXPERTS].set(moe_b_exp[l])
        x_mid, h2, route, rt, counts = merge(
            x_all, mod3, norm1[l].reshape(1, d), norm2[l].reshape(1, d), w_in_bf, l, z,
            hgf, hgb, rtf, rtb, s5f, s5b, yconv, vx, x0c, vecs,
            s5_glu_w1[l].astype(BF16), s5_glu_w2[l].astype(BF16), w_branch[l].astype(BF16),
            w_out[l].astype(BF16), w_router, b_router, bsz, tpb)

        dest, block_expert = moe_plan(rt, counts[0], n_blocks)
        buf = moe_dispatch(dest, h2, jnp.zeros((n_blocks * MOE_BLOCK, d), F32) if l == 0 else buf)
        ybuf = moe_experts(block_expert, buf, moe_w1, moe_w3, moe_w2, l)
        x_all = moe_combine(dest, x_mid, route, mod3, gfin, ybuf, bsz, tpb, l == depth - 1)

    return x_all.reshape(bsz, length, d)
```

```python
import functools
import math

import numpy as np
import jax
import jax.numpy as jnp
from jax import lax
from jax.experimental import pallas as pl
from jax.experimental.pallas import tpu as pltpu

F32 = jnp.float32
BF16 = jnp.bfloat16
HIGHEST = lax.Precision.HIGHEST

GRID_W = 64
N_BRANCH = 4
BRANCH_W = 256
HG_HEADS = 4
HG_CHUNK = 16
HY_W = 256
HY_EMB = 33
HY_BANDS = 16
HY_ORDER = 64
HY_FAST_DECAY = 0.3
HY_SLOW_DECAY = 1.5
HY_TARGET = 1e-2
RET_HEADS = 4
RET_DK = 32
RET_DV = 64
ROPE_BASE = 10000.0
S5_GROUP_CH = 16
S5_GROUPS = 16
S5_STATE = 64
MOE_GROUPS = 4
MOE_PER_GROUP = 8
MOE_EXPERTS = 32
MOE_HIDDEN = 512
MOE_BLOCK = 256
EPS = 1e-6

LANES = 128
SUBLANES = 8
TM = 256
FFT_FAST = 128
VMEM_LIMIT = 56 * 1024 * 1024

N_MIX = 3072
N_ZCOLS = N_MIX + 2 * RET_HEADS * RET_DK
CB_HG_Q, CB_HG_FF, CB_HG_FB, CB_HG_I, CB_HG_G = 3, 4, 5, 6, 7
CB_RT_V, CB_RT_G, CB_S5 = 9, 10, 11
CB128_RT_Q, CB128_RT_K, CB128_RT_QR, CB128_RT_KR = 16, 17, 24, 25


def _cparams(sem):
    return pltpu.CompilerParams(dimension_semantics=sem, vmem_limit_bytes=VMEM_LIMIT)


def _sigmoid(x):
    return 1.0 / (1.0 + jnp.exp(-x))


def _silu(x):
    return x * _sigmoid(x)


def _dot(a, b):
    return jnp.dot(a.astype(BF16), b.astype(BF16), preferred_element_type=F32)


def _dot_nt(a, b):
    return lax.dot_general(a.astype(BF16), b.astype(BF16), (((1,), (1,)), ((), ())),
                           preferred_element_type=F32)


def _dot_hi(a, b):
    return jnp.dot(a, b, precision=HIGHEST, preferred_element_type=F32)


def _dot3(a, b):
    a_hi = a.astype(BF16)
    a_lo = (a - a_hi.astype(F32)).astype(BF16)
    b_hi = b.astype(BF16)
    b_lo = (b - b_hi.astype(F32)).astype(BF16)
    return (jnp.dot(a_hi, b_hi, preferred_element_type=F32)
            + jnp.dot(a_lo, b_hi, preferred_element_type=F32)
            + jnp.dot(a_hi, b_lo, preferred_element_type=F32))


def _split_dot(x, m_bf16):
    hi = x.astype(BF16)
    lo = (x - hi.astype(F32)).astype(BF16)
    return (jnp.dot(hi, m_bf16, preferred_element_type=F32)
            + jnp.dot(lo, m_bf16, preferred_element_type=F32))


def _split_dot_left(m_bf16, x):
    hi = x.astype(BF16)
    lo = (x - hi.astype(F32)).astype(BF16)
    return (jnp.dot(m_bf16, hi, preferred_element_type=F32)
            + jnp.dot(m_bf16, lo, preferred_element_type=F32))


def _seg_of_tile(i, tpb):
    return jnp.where(i % tpb == tpb - 1, 0, 1 + i // tpb)


def _split_tile(i, bsz, tpb):
    b, r = i // tpb, i % tpb
    return jnp.where(r == tpb - 1, bsz * (tpb - 1) + b, b * (tpb - 1) + r)


def _scan_tiles(b, s, tpb):
    last = tpb - 1
    fwd = jnp.where(s == 0, last, s - 1)
    bwd = jnp.where(s == 0, last, last - s)
    return b * tpb + fwd, b * tpb + bwd


def _modvec_kernel(c_ref, w_ref, b_ref, o_ref):
    c = c_ref[...]
    o_ref[0] = _dot_hi(_silu(c), w_ref[0]) + b_ref[0]


def modvec(cmat, w_mod, b_mod):
    depth, d, n6 = w_mod.shape
    nb = n6 // d
    return pl.pallas_call(
        _modvec_kernel,
        grid=(depth, nb),
        in_specs=[pl.BlockSpec((SUBLANES, d), lambda l, j: (0, 0)),
                  pl.BlockSpec((1, d, d), lambda l, j: (l, 0, j)),
                  pl.BlockSpec((1, 1, d), lambda l, j: (l, 0, j))],
        out_specs=pl.BlockSpec((1, SUBLANES, d), lambda l, j: (l, 0, j)),
        out_shape=jax.ShapeDtypeStruct((depth, SUBLANES, n6), F32),
        compiler_params=_cparams(("arbitrary", "arbitrary")),
        name="modvec",
    )(cmat, w_mod, b_mod.reshape(depth, 1, n6))


def _modulated_norm(x, gain, shift, scale):
    ms = jnp.mean(x * x, axis=-1, keepdims=True)
    return (x * lax.rsqrt(ms + EPS) * gain) * (1.0 + scale) + shift


def _inproj_kernel(x_ref, mod_ref, g_ref, w_ref, wrot_ref, z_ref, *, d):
    h = _modulated_norm(x_ref[...], g_ref[...], mod_ref[0, :, 0:d], mod_ref[0, :, d:2 * d]).astype(BF16)
    hy0 = 5 * BRANCH_W
    hy1 = hy0 + 3 * HY_W
    z_ref[:, 0:hy1 - hy0] = jnp.dot(h, w_ref[:, hy0:hy1], preferred_element_type=F32)
    z_ref[:, hy1 - hy0:hy1] = jnp.dot(h, w_ref[:, 0:hy0], preferred_element_type=F32)
    z_ref[:, hy1:N_MIX] = jnp.dot(h, w_ref[:, hy1:N_MIX], preferred_element_type=F32)
    z_ref[:, N_MIX:N_ZCOLS] = jnp.dot(h, wrot_ref[...], preferred_element_type=F32)


def inproj(x_all, mod3, gain, w_in_bf, layer, w_rot, tpb):
    m, d = x_all.shape
    return pl.pallas_call(
        functools.partial(_inproj_kernel, d=d),
        grid=(m // TM,),
        in_specs=[pl.BlockSpec((TM, d), lambda i: (i, 0)),
                  pl.BlockSpec((1, 1, 6 * d), lambda i: (_seg_of_tile(i, tpb), 0, 0)),
                  pl.BlockSpec((1, d), lambda i: (0, 0)),
                  pl.BlockSpec((None, d, N_MIX), lambda i: (layer, 0, 0)),
                  pl.BlockSpec((d, N_ZCOLS - N_MIX), lambda i: (0, 0))],
        out_specs=pl.BlockSpec((TM, N_ZCOLS), lambda i: (i, 0)),
        out_shape=jax.ShapeDtypeStruct((m, N_ZCOLS), F32),
        compiler_params=_cparams(("parallel",)),
        name="inproj",
    )(x_all, mod3, gain, w_in_bf, w_rot)


def _head_masked_rows(x, nchunk, c, hd):
    w = x.shape[-1]
    x3 = x.reshape(nchunk, c, w)
    lane_head = lax.broadcasted_iota(jnp.int32, (1, 1, w), 2) // hd
    parts = [jnp.where(lane_head == h, x3, 0.0) for h in range(w // hd)]
    return jnp.concatenate(parts, axis=1).reshape(nchunk * (w // hd) * c, w)


def _hgrn_direction(q_ref, f_ref, v_ref, lb, tri_ref, o_ref, st_ref, b_s, k_s, qe_s, ke_s, reverse):
    c = HG_CHUNK
    w = BRANCH_W
    nchunk = TM // c
    nh = HG_HEADS
    hd = w // nh
    half = c // 2
    z = f_ref[...]
    sg = _sigmoid(z)
    log2_f = jnp.log2(lb + (1.0 - lb) * sg)
    k = (1.0 - lb) * (1.0 - sg)
    b2 = _split_dot_left(tri_ref[0], log2_f)
    rem2 = _split_dot_left(tri_ref[1], log2_f)
    b_s[...] = b2
    k_s[...] = k
    qe_s[...] = _head_masked_rows(q_ref[...] * jnp.exp2(b2), nchunk, c, hd).astype(BF16)
    ke_s[...] = _head_masked_rows(k * jnp.exp2(rem2), nchunk, c, hd).astype(BF16)

    head_r = lax.broadcasted_iota(jnp.int32, (w, w), 0) // hd
    head_c = lax.broadcasted_iota(jnp.int32, (w, w), 1) // hd
    hm = jnp.where(head_r == head_c, 1.0, 0.0).astype(BF16)
    rows = lax.broadcasted_iota(jnp.int32, (half, 1), 0)

    pieces = []
    for s in range(c):
        for g in range(2):
            lo, hi = g * half, (g + 1) * half - 1
            if (hi < s) if not reverse else (lo > s):
                continue
            full = (lo >= s) if not reverse else (hi <= s)
            pieces.append((s, g, full))

    def chunk_step(n):
        cidx = (nchunk - 1 - n) if reverse else n
        off = pl.multiple_of(cidx * c, c)
        eoff = pl.multiple_of(cidx * (nh * c), nh * c)
        qc = q_ref[pl.ds(off, c), :]
        bc = b_s[pl.ds(off, c), :]
        kc = k_s[pl.ds(off, c), :]
        vc = v_ref[pl.ds(off, c), :]
        parts = []
        for s, g, full in pieces:
            sl = slice(g * half, (g + 1) * half)
            e = jnp.exp2(bc[sl] - bc[s:s + 1, :])
            if not full:
                t = rows + g * half
                e = jnp.where((t <= s) if reverse else (t >= s), e, 0.0)
            parts.append((qc[sl] * kc[s:s + 1, :]) * e)
        p = jnp.concatenate(parts, axis=0).astype(BF16)
        r = jnp.dot(p, hm, preferred_element_type=F32)
        og = [jnp.zeros((half, w), F32), jnp.zeros((half, w), F32)]
        for i, (s, g, _) in enumerate(pieces):
            og[g] = og[g] + r[i * half:(i + 1) * half, :] * vc[s:s + 1, :]
        st = st_ref[...]
        ri = _dot_nt(qe_s[pl.ds(eoff, nh * c), :], st)
        inter = jnp.concatenate([ri[h * c:(h + 1) * c, :] for h in range(nh)], axis=1)
        o_ref[pl.ds(off, c), :] = jnp.concatenate(og, axis=0) + inter
        vexp = jnp.concatenate([vc[:, h * hd:(h + 1) * hd] for h in range(nh)], axis=0)
        kv = lax.dot_general(vexp.astype(BF16), ke_s[pl.ds(eoff, nh * c), :], (((0,), (0,)), ((), ())),
                             preferred_element_type=F32)
        blast = bc[0:1, :] if reverse else bc[c - 1:c, :]
        st_ref[...] = jnp.exp2(blast) * st + kv

    return chunk_step


def _hgrn_kernel(qf_ref, ff_ref, vf_ref, qb_ref, fb_ref, vb_ref, lb_ref, trif_ref, trib_ref, of_ref, ob_ref,
                 stf_ref, stb_ref, bf_s, kf_s, qef_s, kef_s, bb_s, kb_s, qeb_s, keb_s):
    @pl.when(pl.program_id(1) == 0)
    def _():
        stf_ref[...] = jnp.zeros_like(stf_ref)
        stb_ref[...] = jnp.zeros_like(stb_ref)

    step_f = _hgrn_direction(qf_ref, ff_ref, vf_ref, lb_ref[0:1, :], trif_ref, of_ref, stf_ref,
                             bf_s, kf_s, qef_s, kef_s, False)
    step_b = _hgrn_direction(qb_ref, fb_ref, vb_ref, lb_ref[1:2, :], trib_ref, ob_ref, stb_ref,
                             bb_s, kb_s, qeb_s, keb_s, True)

    def body(n, carry):
        step_f(n)
        step_b(n)
        return carry

    lax.fori_loop(0, TM // HG_CHUNK, body, 0, unroll=16)


def hgrn(z, lb, bsz, tpb):
    m = z.shape[0]
    w = BRANCH_W
    hd = w // HG_HEADS
    ri = jnp.arange(TM)[:, None]
    ci = jnp.arange(TM)[None, :]
    same = (ri // HG_CHUNK) == (ci // HG_CHUNK)
    tri = jnp.stack([same & (ci <= ri), same & (ci > ri), same & (ci >= ri), same & (ci < ri)])
    tri = tri.astype(BF16).reshape(2, 2, TM, TM)

    def spec(cb, d):
        return pl.BlockSpec((TM, w), lambda b, s: (_scan_tiles(b, s, tpb)[d], cb))

    out_spec = [pl.BlockSpec((TM, w), lambda b, s: (_scan_tiles(b, s, tpb)[0], 0)),
                pl.BlockSpec((TM, w), lambda b, s: (_scan_tiles(b, s, tpb)[1], 0))]
    return pl.pallas_call(
        _hgrn_kernel,
        grid=(bsz, tpb),
        in_specs=[spec(CB_HG_Q, 0), spec(CB_HG_FF, 0), spec(CB_HG_I, 0),
                  spec(CB_HG_Q, 1), spec(CB_HG_FB, 1), spec(CB_HG_I, 1),
                  pl.BlockSpec((2, w), lambda b, s: (0, 0)),
                  pl.BlockSpec((None, 2, TM, TM), lambda b, s: (0, 0, 0, 0)),
                  pl.BlockSpec((None, 2, TM, TM), lambda b, s: (1, 0, 0, 0))],
        out_specs=out_spec,
        out_shape=[jax.ShapeDtypeStruct((m, w), F32)] * 2,
        scratch_shapes=[pltpu.VMEM((hd, w), F32), pltpu.VMEM((hd, w), F32)]
                       + [pltpu.VMEM((TM, w), F32), pltpu.VMEM((TM, w), F32),
                          pltpu.VMEM((HG_HEADS * TM, w), BF16), pltpu.VMEM((HG_HEADS * TM, w), BF16)] * 2,
        compiler_params=_cparams(("arbitrary", "arbitrary")),
        name="hgrn2",
    )(z, z, z, z, z, z, lb, tri, tri)


def _ret_direction(q_ref, k_ref, qr_ref, kr_ref, v_ref, cos_ref, sin_ref, lgq, lg_ref, d, o_ref, st_ref,
                   reverse):
    nq = RET_HEADS * RET_DK
    nv = RET_HEADS * RET_DV
    cos = cos_ref[...]
    sin = sin_ref[...]
    qx = q_ref[...] * cos + qr_ref[...] * sin
    kx = (k_ref[...] * cos + kr_ref[...] * sin) * (RET_DK ** -0.5)
    v = v_ref[...]
    t = lax.broadcasted_iota(jnp.int32, (TM, 1), 0).astype(F32)
    pos = (TM - 1.0 - t) if reverse else t
    qd = qx * jnp.exp((pos + 1.0) * lgq)
    kd = kx * jnp.exp((TM - 1.0 - pos) * lgq)
    ti = lax.broadcasted_iota(jnp.int32, (TM, TM), 0)
    si = lax.broadcasted_iota(jnp.int32, (TM, TM), 1)
    rel = ((si - ti) if reverse else (ti - si)).astype(F32)
    live = rel >= 0.0
    relc = jnp.where(live, rel, 0.0)
    lane_q = lax.broadcasted_iota(jnp.int32, (1, nq), 1) // RET_DK
    lane_v = lax.broadcasted_iota(jnp.int32, (1, nv), 1) // RET_DV
    st = st_ref[...]
    o = _dot_nt(qd, st)
    vb = v.astype(BF16)
    for h in range(RET_HEADS):
        dmat = jnp.where(live, jnp.exp(relc * lg_ref[d, h]), 0.0)
        sc = _dot_nt(jnp.where(lane_q == h, qx, 0.0), kx) * dmat
        oh = jnp.dot(sc.astype(BF16), vb, preferred_element_type=F32)
        o = o + jnp.where(lane_v == h, oh, 0.0)
    o_ref[...] = o
    kv = _dot(v.T, kd)
    hr = lax.broadcasted_iota(jnp.int32, (nv, nq), 0) // RET_DV
    hc = lax.broadcasted_iota(jnp.int32, (nv, nq), 1) // RET_DK
    st_ref[...] = st * jnp.exp(float(TM) * lgq) + jnp.where(hr == hc, kv, 0.0)


def _ret_kernel(lg_ref, qf, kf, qrf, krf, vf, cosf, sinf, qb, kb, qrb, krb, vb, cosb, sinb, lgq_ref,
                of_ref, ob_ref, stf_ref, stb_ref):
    @pl.when(pl.program_id(1) == 0)
    def _():
        stf_ref[...] = jnp.zeros_like(stf_ref)
        stb_ref[...] = jnp.zeros_like(stb_ref)

    _ret_direction(qf, kf, qrf, krf, vf, cosf, sinf, lgq_ref[0:1, :], lg_ref, 0, of_ref, stf_ref, False)
    _ret_direction(qb, kb, qrb, krb, vb, cosb, sinb, lgq_ref[1:2, :], lg_ref, 1, ob_ref, stb_ref, True)


def retention(z, cos_tab, sin_tab, log_gamma, bsz, tpb):
    m = z.shape[0]
    nq = RET_HEADS * RET_DK
    nv = RET_HEADS * RET_DV
    lgq = jnp.repeat(log_gamma, RET_DK, axis=1)

    def zspec(cb, width, d):
        return pl.BlockSpec((TM, width), lambda b, s: (_scan_tiles(b, s, tpb)[d], cb))

    def tspec(d):
        return pl.BlockSpec((TM, nq), lambda b, s: (_scan_tiles(0, s, tpb)[d], 0))

    def side(d):
        return [zspec(CB128_RT_Q, nq, d), zspec(CB128_RT_K, nq, d), zspec(CB128_RT_QR, nq, d),
                zspec(CB128_RT_KR, nq, d), zspec(CB_RT_V, nv, d), tspec(d), tspec(d)]

    return pl.pallas_call(
        _ret_kernel,
        grid=(bsz, tpb),
        in_specs=[pl.BlockSpec(memory_space=pltpu.SMEM)] + side(0) + side(1)
                 + [pl.BlockSpec((2, nq), lambda b, s: (0, 0))],
        out_specs=[pl.BlockSpec((TM, nv), lambda b, s: (_scan_tiles(b, s, tpb)[0], 0)),
                   pl.BlockSpec((TM, nv), lambda b, s: (_scan_tiles(b, s, tpb)[1], 0))],
        out_shape=[jax.ShapeDtypeStruct((m, nv), F32)] * 2,
        scratch_shapes=[pltpu.VMEM((nv, nq), F32), pltpu.VMEM((nv, nq), F32)],
        compiler_params=_cparams(("arbitrary", "arbitrary")),
        name="retention",
    )(log_gamma, z, z, z, z, z, cos_tab, sin_tab, z, z, z, z, z, cos_tab, sin_tab, lgq)


S5_NP = S5_GROUPS * S5_STATE


S5_T = SUBLANES
S5_ROW_APPLY, S5_ROW_END, S5_ROW_GROUP, S5_ROW_STEP = 0, 8, 16, 24


def _s5g_direction(u_ref, bre_ref, bim_ref, cre_ref, cim_ref, kt_ref, tab_ref, o_ref, car_ref, reverse):
    ng = TM // S5_T
    nsg = ng // SUBLANES
    np_ = S5_NP
    w = BRANCH_W
    sub = lax.broadcasted_iota(jnp.int32, (1, SUBLANES, 1), 1)

    def neighbour(x3, k):
        shift = (SUBLANES - k) if reverse else k
        keep = (sub < SUBLANES - k) if reverse else (sub >= k)
        return jnp.where(keep, pltpu.roll(x3, shift, 1), 0.0)

    u3 = u_ref[...]
    u2 = u3.reshape(TM, w)
    ub = u2.astype(BF16)
    y = jnp.dot(ub, kt_ref[0, 0], preferred_element_type=F32)
    for tau in range(1, S5_T):
        y = y + jnp.dot(neighbour(u3, tau).reshape(TM, w).astype(BF16), kt_ref[0, tau],
                        preferred_element_type=F32)

    bur = jnp.dot(ub, bre_ref[0], preferred_element_type=F32).reshape(ng, S5_T, np_)
    bui = jnp.dot(ub, bim_ref[0], preferred_element_type=F32).reshape(ng, S5_T, np_)
    wr = tab_ref[0, 0, S5_ROW_END:S5_ROW_END + S5_T, :]
    wi = tab_ref[0, 1, S5_ROW_END:S5_ROW_END + S5_T, :]
    gsel = (lax.broadcasted_iota(jnp.int32, (ng, TM), 1) // S5_T
            == lax.broadcasted_iota(jnp.int32, (ng, TM), 0))
    gsel = jnp.where(gsel, 1.0, 0.0).astype(BF16)
    er = jnp.dot(gsel, (wr * bur - wi * bui).reshape(TM, np_).astype(BF16), preferred_element_type=F32)
    ei = jnp.dot(gsel, (wr * bui + wi * bur).reshape(TM, np_).astype(BF16), preferred_element_type=F32)

    sr = er.reshape(nsg, SUBLANES, np_)
    si = ei.reshape(nsg, SUBLANES, np_)
    for n, k in enumerate((1, 2, 4)):
        ar = tab_ref[0, 0, S5_ROW_STEP + n:S5_ROW_STEP + n + 1, :]
        ai = tab_ref[0, 1, S5_ROW_STEP + n:S5_ROW_STEP + n + 1, :]
        nr, ni = neighbour(sr, k), neighbour(si, k)
        sr, si = sr + ar * nr - ai * ni, si + ar * ni + ai * nr
    pr = tab_ref[0, 0, S5_ROW_GROUP:S5_ROW_GROUP + SUBLANES, :]
    pi = tab_ref[0, 1, S5_ROW_GROUP:S5_ROW_GROUP + SUBLANES, :]
    cin_r, cin_i = car_ref[0:1, :], car_ref[1:2, :]
    cr, ci = cin_r, cin_i
    blocks_r, blocks_i = [None] * nsg, [None] * nsg
    for sg in (reversed(range(nsg)) if reverse else range(nsg)):
        br = sr[sg] + pr * cr - pi * ci
        bi = si[sg] + pr * ci + pi * cr
        blocks_r[sg], blocks_i[sg] = br, bi
        cr, ci = (br[0:1, :], bi[0:1, :]) if reverse else (br[SUBLANES - 1:, :], bi[SUBLANES - 1:, :])
    car_ref[0:1, :] = cr
    car_ref[1:2, :] = ci
    end_r = jnp.concatenate(blocks_r, axis=0)
    end_i = jnp.concatenate(blocks_i, axis=0)
    grow = lax.broadcasted_iota(jnp.int32, (ng, 1), 0)
    edge = (ng - 1) if reverse else 0
    shift = (ng - 1) if reverse else 1
    pre_r = jnp.where(grow == edge, cin_r, pltpu.roll(end_r, shift, 0))
    pre_i = jnp.where(grow == edge, cin_i, pltpu.roll(end_i, shift, 0))

    ar = tab_ref[0, 0, S5_ROW_APPLY:S5_ROW_APPLY + S5_T, :][None]
    ai = tab_ref[0, 1, S5_ROW_APPLY:S5_ROW_APPLY + S5_T, :][None]
    pre_r3 = pre_r[:, None, :]
    pre_i3 = pre_i[:, None, :]
    xr = (ar * pre_r3 - ai * pre_i3).reshape(TM, np_).astype(BF16)
    xi = (ar * pre_i3 + ai * pre_r3).reshape(TM, np_).astype(BF16)
    o_ref[...] = (y + jnp.dot(xr, cre_ref[0], preferred_element_type=F32)
                  - jnp.dot(xi, cim_ref[0], preferred_element_type=F32))


def _s5g_kernel(uf, ub, bre, bim, cre, cim, ktf, tabf, bre2, bim2, cre2, cim2, ktb, tabb, of_ref, ob_ref,
                carf, carb):
    @pl.when(pl.program_id(1) == 0)
    def _():
        carf[...] = jnp.zeros_like(carf)
        carb[...] = jnp.zeros_like(carb)

    _s5g_direction(uf, bre, bim, cre, cim, ktf, tabf, of_ref, carf, False)
    _s5g_direction(ub, bre2, bim2, cre2, cim2, ktb, tabb, ob_ref, carb, True)


def s5g(z, bblk_re, bblk_im, cblk_re, cblk_im, ktoep, tab, bsz, tpb):
    m = z.shape[0]
    w = BRANCH_W
    ng = TM // S5_T
    z3 = z.reshape(m // S5_T, S5_T, z.shape[1])

    def pspec(shape, d):
        return pl.BlockSpec((1,) + shape, lambda b, s: (d,) + (0,) * len(shape))

    def uspec(d):
        return pl.BlockSpec((ng, S5_T, w), lambda b, s: (_scan_tiles(b, s, tpb)[d], 0, CB_S5))

    def side(d):
        return [pspec((w, S5_NP), d), pspec((w, S5_NP), d), pspec((S5_NP, w), d), pspec((S5_NP, w), d),
                pspec((S5_T, w, w), d), pspec((2, 32, S5_NP), d)]

    return pl.pallas_call(
        _s5g_kernel,
        grid=(bsz, tpb),
        in_specs=[uspec(0), uspec(1)] + side(0) + side(1),
        out_specs=[pl.BlockSpec((TM, w), lambda b, s: (_scan_tiles(b, s, tpb)[0], 0)),
                   pl.BlockSpec((TM, w), lambda b, s: (_scan_tiles(b, s, tpb)[1], 0))],
        out_shape=[jax.ShapeDtypeStruct((m, w), F32)] * 2,
        scratch_shapes=[pltpu.VMEM((2, S5_NP), F32), pltpu.VMEM((2, S5_NP), F32)],
        compiler_params=_cparams(("arbitrary", "arbitrary")),
        name="s5",
    )(z3, z3, bblk_re, bblk_im, cblk_re, cblk_im, ktoep, tab,
      bblk_re, bblk_im, cblk_re, cblk_im, ktoep, tab)


def s5g_params(lam_re, lam_im, log_dt, b_re, b_im, c_re, c_im):
    lr = jnp.minimum(lam_re.astype(F32), -1e-4)
    li = lam_im.astype(F32)
    dt = jnp.exp(log_dt.astype(F32))[..., None]

    def apow(k):
        mag = jnp.exp(k * lr * dt)
        return mag * jnp.cos(k * li * dt), mag * jnp.sin(k * li * dt)

    ar, ai = apow(1.0)
    den = lr * lr + li * li
    gr = ((ar - 1.0) * lr + ai * li) / den
    gi = (ai * lr - (ar - 1.0) * li) / den
    bbr = gr[..., None] * b_re - gi[..., None] * b_im
    bbi = gr[..., None] * b_im + gi[..., None] * b_re
    cr, ci = c_re.astype(F32), c_im.astype(F32)
    eye = jnp.eye(S5_GROUPS, dtype=F32)
    nch = S5_GROUPS * S5_GROUP_CH

    def blk_in(bb):
        return jnp.einsum('dgpc,gh->dgchp', bb, eye).reshape(2, nch, S5_NP).astype(BF16)

    def blk_out(cc):
        return jnp.einsum('dgcp,gh->dgphc', cc, eye).reshape(2, S5_NP, nch).astype(BF16)

    taus = jnp.arange(S5_T, dtype=F32)[:, None, None, None]
    pr, pi = apow(taus)
    car = cr * pr[:, :, :, None, :] - ci * pi[:, :, :, None, :]
    cai = cr * pi[:, :, :, None, :] + ci * pr[:, :, :, None, :]
    k = (jnp.einsum('tdgop,dgpi->dtgio', car, bbr, precision=HIGHEST)
         - jnp.einsum('tdgop,dgpi->dtgio', cai, bbi, precision=HIGHEST))
    ktoep = jnp.einsum('dtgio,gh->dtgiho', k, eye).reshape(2, S5_T, nch, nch).astype(BF16)

    t = S5_T
    j = np.arange(t)
    fwd = np.concatenate([j + 1, t - 1 - j, t * (j + 1), t * np.array([1, 2, 4])])
    bwd = np.concatenate([t - j, j, t * (t - j), t * np.array([1, 2, 4])])
    exps = np.zeros((2, 32), np.float32)
    exps[0, :fwd.size], exps[1, :bwd.size] = fwd, bwd
    used = (np.arange(32) < fwd.size).astype(np.float32)[None, :, None, None]
    e = jnp.asarray(exps)[:, :, None, None]
    mag = jnp.exp(e * (lr * dt)[:, None]) * used
    ang = e * (li * dt)[:, None]
    tab = jnp.stack([(mag * jnp.cos(ang)).reshape(2, 32, S5_NP),
                     (mag * jnp.sin(ang)).reshape(2, 32, S5_NP)], axis=1)
    return blk_in(bbr), blk_in(bbi), blk_out(cr), blk_out(ci), ktoep, tab


def _hy_prep_kernel(u_ref, prev_ref, next_ref, w_ref, b_ref, x0_ref, vx_ref, *, tpb):
    i = pl.program_id(0)
    r = i % tpb
    first = (r == 0) | (r == tpb - 1)
    last = r >= tpb - 2
    u = u_ref[...]
    w = w_ref[...]
    up_row = jnp.where(first, 0.0, prev_ref[SUBLANES - 1:SUBLANES, :])
    dn_row = jnp.where(last, 0.0, next_ref[0:1, :])
    row = lax.broadcasted_iota(jnp.int32, (TM, 1), 0)
    um = jnp.where(row == 0, up_row, pltpu.roll(u, 1, 0))
    un = jnp.where(row == TM - 1, dn_row, pltpu.roll(u, TM - 1, 0))
    y = um * w[0:1, :] + u * w[1:2, :] + un * w[2:3, :] + b_ref[...]
    x0_ref[...] = y[:, 0:HY_W]
    vx_ref[...] = y[:, 2 * HY_W:3 * HY_W] * y[:, HY_W:2 * HY_W]


def hy_prep(z, conv_w, conv_b, bsz, tpb):
    m = z.shape[0]
    w3 = 3 * HY_W
    rb = TM // SUBLANES
    nrb = m // SUBLANES
    return pl.pallas_call(
        functools.partial(_hy_prep_kernel, tpb=tpb),
        grid=(m // TM,),
        in_specs=[pl.BlockSpec((TM, w3), lambda i: (i, 0)),
                  pl.BlockSpec((SUBLANES, w3), lambda i: (jnp.maximum(i * rb - 1, 0), 0)),
                  pl.BlockSpec((SUBLANES, w3), lambda i: (jnp.minimum((i + 1) * rb, nrb - 1), 0)),
                  pl.BlockSpec((3, w3), lambda i: (0, 0)),
                  pl.BlockSpec((1, w3), lambda i: (0, 0))],
        out_specs=[pl.BlockSpec((TM, HY_W), lambda i: (i, 0)),
                   pl.BlockSpec((TM, HY_W), lambda i: (_split_tile(i, bsz, tpb), 0))],
        out_shape=[jax.ShapeDtypeStruct((m, HY_W), F32)] * 2,
        compiler_params=_cparams(("parallel",)),
        name="hy_prep",
    )(z, z, z, conv_w, conv_b.reshape(1, w3))


def _hy_filter_kernel(zf_ref, w1_ref, b1_ref, f1_ref, w2_ref, b2_ref, f2_ref, w3_ref, dl_ref,
                      hfb_ref, ss_ref):
    i = pl.program_id(0)
    zf = zf_ref[...]
    hid = jnp.sin(f1_ref[...] * (_dot_hi(zf, w1_ref[...]) + b1_ref[...]))
    hid = jnp.sin(f2_ref[...] * (_dot_hi(hid, w2_ref[...]) + b2_ref[...]))
    filt = _dot_hi(hid, w3_ref[...])
    window = jnp.exp(-zf[:, 0:1] * dl_ref[...])
    hf = filt[:, 0:HY_W] * window
    row = lax.broadcasted_iota(jnp.int32, (TM, 1), 0) + i * TM
    hb = jnp.where(row == 0, 0.0, filt[:, HY_W:2 * HY_W] * window)
    hfb_ref[0] = hf
    hfb_ref[1] = hb
    part = jnp.sum(hf * hf + hb * hb, axis=0, keepdims=True)

    @pl.when(i == 0)
    def _():
        ss_ref[...] = jnp.zeros_like(ss_ref)

    ss_ref[...] = ss_ref[...] + part


def hy_filter(length, w1, b1, fr1, w2, b2, fr2, w3):
    t = jnp.linspace(0.0, 1.0, length, dtype=F32)[:, None]
    wv = (2.0 * math.pi / length) * jnp.arange(length, dtype=F32)[:, None]
    f = jnp.linspace(1e-4, HY_BANDS - 1.0, HY_BANDS, dtype=F32)[None, :]
    zfeat = jnp.concatenate([t, jnp.cos(f * wv), -jnp.sin(f * wv)], axis=-1)
    zfeat = jnp.pad(zfeat, ((0, 0), (0, LANES - HY_EMB)))
    w1p = jnp.pad(w1, ((0, LANES - HY_EMB), (0, 0)))
    deltas = jnp.abs(jnp.linspace(math.log(HY_TARGET) / HY_SLOW_DECAY,
                                  math.log(HY_TARGET) / HY_FAST_DECAY, HY_W, dtype=F32))[None, :]
    full = lambda shape: pl.BlockSpec(shape, lambda i: (0,) * len(shape))
    return pl.pallas_call(
        _hy_filter_kernel,
        grid=(length // TM,),
        in_specs=[pl.BlockSpec((TM, LANES), lambda i: (i, 0)),
                  full((LANES, HY_ORDER)), full((1, HY_ORDER)), full((1, HY_ORDER)),
                  full((HY_ORDER, HY_ORDER)), full((1, HY_ORDER)), full((1, HY_ORDER)),
                  full((HY_ORDER, 2 * HY_W)), full((1, HY_W))],
        out_specs=[pl.BlockSpec((2, TM, HY_W), lambda i: (0, i, 0)),
                   pl.BlockSpec((1, HY_W), lambda i: (0, 0))],
        out_shape=[jax.ShapeDtypeStruct((2, length, HY_W), F32), jax.ShapeDtypeStruct((1, HY_W), F32)],
        compiler_params=_cparams(("arbitrary",)),
        name="hy_filter",
    )(zfeat, w1p, b1.reshape(1, -1), fr1.reshape(1, -1), w2, b2.reshape(1, -1), fr2.reshape(1, -1), w3, deltas)


def _dft_consts(length):
    n = 2 * length
    ns = n // FFT_FAST
    a = jnp.arange(ns, dtype=jnp.int32)
    ang = ((a[:, None] * a[None, :]) % ns).astype(F32) * (2.0 * math.pi / ns)
    f_fwd = jnp.concatenate([jnp.cos(ang), -jnp.sin(ang)], axis=0)
    f_inv = jnp.concatenate([jnp.cos(ang), -jnp.sin(ang)], axis=1)
    b = jnp.arange(FFT_FAST, dtype=jnp.int32)
    th1 = (a[:, None] * b[None, :]).astype(F32) * (2.0 * math.pi / n)
    th2 = ((b[:, None] * b[None, :]) % FFT_FAST).astype(F32) * (2.0 * math.pi / FFT_FAST)
    c1, s1 = jnp.cos(th1)[:, None, :], jnp.sin(th1)[:, None, :]
    c2, s2 = jnp.cos(th2)[None, :, :], jnp.sin(th2)[None, :, :]
    mr = c1 * c2 - s1 * s2
    mi = -(s1 * c2 + c1 * s2)
    m_fwd = jnp.concatenate([jnp.concatenate([mr, -mi], axis=2),
                             jnp.concatenate([mi, mr], axis=2)], axis=1)
    ir = jnp.transpose(mr, (0, 2, 1)) / n
    ii = jnp.transpose(-mi, (0, 2, 1)) / n
    m_inv = jnp.concatenate([jnp.concatenate([ir, -ii], axis=2),
                             jnp.concatenate([ii, ir], axis=2)], axis=1)
    return tuple(t.astype(BF16) for t in (f_fwd, f_inv, m_fwd, m_inv))


def _dft_slow_fwd_kernel(f_ref, x_ref, o_ref):
    ns = o_ref.shape[2]
    for bl in range(x_ref.shape[1]):
        r = jnp.dot(f_ref[...], x_ref[:, bl, :].astype(BF16), preferred_element_type=F32)
        o_ref[0, 0, :, bl, :] = r[0:ns]
        o_ref[0, 1, :, bl, :] = r[ns:]


def dft_slow_fwd(fmat, x3, bx):
    two_ns, na = fmat.shape
    ns = two_ns // 2
    cw = x3.shape[-1]
    kb = SUBLANES
    return pl.pallas_call(
        _dft_slow_fwd_kernel,
        grid=(bx, FFT_FAST // kb),
        in_specs=[pl.BlockSpec((two_ns, na), lambda b, j: (0, 0)),
                  pl.BlockSpec((na, kb, cw), lambda b, j: (b, j, 0))],
        out_specs=pl.BlockSpec((1, 2, ns, kb, cw), lambda b, j: (b, 0, 0, j, 0)),
        out_shape=jax.ShapeDtypeStruct((bx, 2, ns, FFT_FAST, cw), F32),
        compiler_params=_cparams(("parallel", "parallel")),
        name="dft_slow_fwd",
    )(fmat, x3)


def _dft_slow_inv_kernel(f_ref, b_ref, o_ref, *, bx):
    @pl.when(pl.program_id(0) < bx)
    def _():
        for bl in range(o_ref.shape[1]):
            bb = jnp.concatenate([b_ref[0, 0, :, bl, :], b_ref[0, 1, :, bl, :]], axis=0).astype(BF16)
            o_ref[:, bl, :] = jnp.dot(f_ref[...], bb, preferred_element_type=F32)

    @pl.when(pl.program_id(0) >= bx)
    def _():
        o_ref[...] = jnp.zeros_like(o_ref)


def dft_slow_inv(fmat, bv):
    bx, _, ns, nf, cw = bv.shape
    na = fmat.shape[0]
    kb = SUBLANES
    return pl.pallas_call(
        functools.partial(_dft_slow_inv_kernel, bx=bx),
        grid=(bx + 1, nf // kb),
        in_specs=[pl.BlockSpec((na, 2 * ns), lambda b, j: (0, 0)),
                  pl.BlockSpec((1, 2, ns, kb, cw), lambda b, j: (jnp.minimum(b, bx - 1), 0, 0, j, 0))],
        out_specs=pl.BlockSpec((na, kb, cw), lambda b, j: (b, j, 0)),
        out_shape=jax.ShapeDtypeStruct(((bx + 1) * na, nf, cw), F32),
        compiler_params=_cparams(("parallel", "parallel")),
        name="dft_slow_inv",
    )(fmat, bv)


FFT_GROUP = 8


def _dft_fast_filter_kernel(m_ref, a_ref, sc_ref, o_ref):
    sc = sc_ref[...]
    for j in range(FFT_GROUP):
        af = jnp.concatenate([a_ref[0, 0, j], a_ref[0, 1, j]], axis=0).astype(BF16)
        ab = jnp.concatenate([a_ref[1, 0, j], a_ref[1, 1, j]], axis=0).astype(BF16)
        xf = jnp.dot(m_ref[j], af, preferred_element_type=F32)
        xb = jnp.dot(m_ref[j], ab, preferred_element_type=F32)
        o_ref[0, j] = (xf[0:FFT_FAST] + xb[0:FFT_FAST]) * sc
        o_ref[1, j] = (xf[FFT_FAST:] - xb[FFT_FAST:]) * sc


def dft_fast_filter(m_fwd, a5, scale):
    _, _, ns, nf, cw = a5.shape
    g = FFT_GROUP
    return pl.pallas_call(
        _dft_fast_filter_kernel,
        grid=(ns // g,),
        in_specs=[pl.BlockSpec((g, 2 * nf, 2 * nf), lambda i: (i, 0, 0)),
                  pl.BlockSpec((2, 2, g, nf, cw), lambda i: (0, 0, i, 0, 0)),
                  pl.BlockSpec((1, cw), lambda i: (0, 0))],
        out_specs=pl.BlockSpec((2, g, nf, cw), lambda i: (0, i, 0, 0)),
        out_shape=jax.ShapeDtypeStruct((2, ns, nf, cw), F32),
        compiler_params=_cparams(("parallel",)),
        name="dft_fast_filter",
    )(m_fwd, a5, scale)


def _dft_fast_conv_kernel(mf_ref, mi_ref, a_ref, k_ref, o_ref):
    for j in range(FFT_GROUP):
        a = jnp.concatenate([a_ref[0, 0, j], a_ref[0, 1, j]], axis=0).astype(BF16)
        x = jnp.dot(mf_ref[j], a, preferred_element_type=F32)
        xr, xi = x[0:FFT_FAST], x[FFT_FAST:]
        kr, ki = k_ref[0, j], k_ref[1, j]
        y = jnp.concatenate([xr * kr - xi * ki, xr * ki + xi * kr], axis=0).astype(BF16)
        bv = jnp.dot(mi_ref[j], y, preferred_element_type=F32)
        o_ref[0, 0, j] = bv[0:FFT_FAST]
        o_ref[0, 1, j] = bv[FFT_FAST:]


def dft_fast_conv(m_fwd, m_inv, a5, kf):
    bx, _, ns, nf, cw = a5.shape
    g = FFT_GROUP
    return pl.pallas_call(
        _dft_fast_conv_kernel,
        grid=(bx, ns // g),
        in_specs=[pl.BlockSpec((g, 2 * nf, 2 * nf), lambda b, i: (i, 0, 0)),
                  pl.BlockSpec((g, 2 * nf, 2 * nf), lambda b, i: (i, 0, 0)),
                  pl.BlockSpec((1, 2, g, nf, cw), lambda b, i: (b, 0, i, 0, 0)),
                  pl.BlockSpec((2, g, nf, cw), lambda b, i: (0, i, 0, 0))],
        out_specs=pl.BlockSpec((1, 2, g, nf, cw), lambda b, i: (b, 0, i, 0, 0)),
        out_shape=jax.ShapeDtypeStruct((bx, 2, ns, nf, cw), F32),
        compiler_params=_cparams(("parallel", "parallel")),
        name="dft_fast_conv",
    )(m_fwd, m_inv, a5, kf)


def hyena_long_conv(vx_rows, bsz, length, hfb, scale):
    cw = vx_rows.shape[-1]
    ns = 2 * length // FFT_FAST
    f_fwd, f_inv, m_fwd, m_inv = _dft_consts(length)
    fa = dft_slow_fwd(f_fwd[:, :ns // 2], hfb.reshape(-1, FFT_FAST, cw), 2)
    kf = dft_fast_filter(m_fwd, fa, scale)
    xa = dft_slow_fwd(f_fwd[:, :ns // 2], vx_rows.reshape(-1, FFT_FAST, cw), bsz)
    bv = dft_fast_conv(m_fwd, m_inv, xa, kf)
    return dft_slow_inv(f_inv[:ns // 2], bv).reshape((bsz + 1) * length, cw)


def _hy_ctx_conv_kernel(v_ref, g_ref, y_hbm_ref, o_ref, *, lc):
    del y_hbm_ref
    nb = lc // SUBLANES

    def body(gi, acc):
        base = pl.multiple_of((nb - 1 - gi) * SUBLANES, SUBLANES)
        win = g_ref[pl.ds(base, lc + SUBLANES), :]
        for j in range(SUBLANES):
            vrow = v_ref[pl.ds(gi * SUBLANES + j, 1), :]
            acc = acc + win[SUBLANES - 1 - j:SUBLANES - 1 - j + lc, :] * vrow
        return acc

    o_ref[...] = lax.fori_loop(0, nb, body, jnp.zeros((lc, HY_W), F32))


def hy_ctx_conv(vx, gwin, yconv, bsz, tpb, lc):
    ctx_tile = lambda b: (bsz * (tpb - 1) + b, 0)
    return pl.pallas_call(
        functools.partial(_hy_ctx_conv_kernel, lc=lc),
        grid=(bsz,),
        in_specs=[pl.BlockSpec((lc, HY_W), ctx_tile),
                  pl.BlockSpec((2 * lc, HY_W), lambda b: (0, 0)),
                  pl.BlockSpec(memory_space=pl.ANY)],
        out_specs=pl.BlockSpec((lc, HY_W), ctx_tile),
        out_shape=jax.ShapeDtypeStruct(yconv.shape, F32),
        input_output_aliases={2: 0},
        compiler_params=_cparams(("parallel",)),
        name="hy_ctx_conv",
    )(vx, gwin, yconv)


ROUTE_E1, ROUTE_E2, ROUTE_W1, ROUTE_W2, ROUTE_R1, ROUTE_R2 = 0, 1, 2, 3, 4, 5
NEG_BIG = -1e30


def _gelu_tanh(x):
    return 0.5 * x * (1.0 + jnp.tanh(math.sqrt(2.0 / math.pi) * (x + 0.044715 * (x * x * x))))


def _lane_pick(slab, lane, idx):
    return jnp.sum(jnp.where(lane == idx, slab, 0.0), axis=-1, keepdims=True)


def _route_tile(logits, valid, base_ref, cnt_ref, route_ref, rt_ref):
    lane = lax.broadcasted_iota(jnp.int32, (TM, LANES), 1).astype(F32)
    gl = jnp.where(lane < MOE_GROUPS, logits, NEG_BIG)
    gmax = jnp.max(gl, axis=-1, keepdims=True)
    yield
    grp_p = 1.0 / jnp.sum(jnp.exp(gl - gmax), axis=-1, keepdims=True)
    gi = jnp.min(jnp.where(gl == gmax, lane, float(LANES)), axis=-1, keepdims=True)
    yield
    lo = MOE_GROUPS + MOE_PER_GROUP * gi
    el = jnp.where((lane >= lo) & (lane < lo + MOE_PER_GROUP), logits, NEG_BIG)
    m1 = jnp.max(el, axis=-1, keepdims=True)
    yield
    i1 = jnp.min(jnp.where(el == m1, lane, float(LANES)), axis=-1, keepdims=True)
    yield
    el2 = jnp.where(lane == i1, NEG_BIG, el)
    m2 = jnp.max(el2, axis=-1, keepdims=True)
    yield
    i2 = jnp.min(jnp.where(el2 == m2, lane, float(LANES)), axis=-1, keepdims=True)
    yield
    ratio = jnp.exp(m2 - m1)
    wt1 = grp_p / (1.0 + ratio)
    wt2 = grp_p * ratio / (1.0 + ratio)
    e1 = i1 - MOE_GROUPS
    e2 = i2 - MOE_GROUPS

    oh1 = jnp.where(lane == e1, valid, 0.0)
    oh2 = jnp.where(lane == e2, valid, 0.0)
    ri = lax.broadcasted_iota(jnp.int32, (TM, TM), 0)
    ci = lax.broadcasted_iota(jnp.int32, (TM, TM), 1)
    below = jnp.where(ci < ri, 1.0, 0.0).astype(BF16)
    c1 = jnp.dot(below, oh1.astype(BF16), preferred_element_type=F32)
    c2 = jnp.dot(below, oh2.astype(BF16), preferred_element_type=F32)
    tot1 = jnp.sum(oh1, axis=0, keepdims=True)
    tot2 = jnp.sum(oh2, axis=0, keepdims=True)
    yield
    base = base_ref[...]
    r1 = jnp.sum(oh1 * (base + c1), axis=-1, keepdims=True)
    r2 = jnp.sum(oh2 * (base + tot1 + c2), axis=-1, keepdims=True)
    base = base + tot1 + tot2
    base_ref[...] = base
    cnt_ref[...] = jnp.broadcast_to(base, cnt_ref.shape)
    route = jnp.zeros((TM, LANES), F32)
    for idx, val in ((ROUTE_E1, e1), (ROUTE_E2, e2), (ROUTE_W1, wt1), (ROUTE_W2, wt2),
                     (ROUTE_R1, r1), (ROUTE_R2, r2)):
        route = jnp.where(lane == idx, val, route)
    route_ref[...] = route
    rt_ref[...] = route.T[0:SUBLANES, :]


def _merge_kernel(x_ref, mod_ref, n1_ref, n2_ref, wg0_ref, wg1_ref, wg2_ref, wg3_ref, zhg_ref, zrt_ref, zs5_ref,
                  hgf_ref, hgb_ref, rtf_ref, rtb_ref, s5f_ref, s5b_ref, yc_ref, vx_ref, x0_ref,
                  vec_ref, glu1_ref, glu2_ref, wb_ref, wo_ref, wr_ref, br_ref,
                  xo_ref, h2_ref, route_ref, rt_ref, cnt_ref, base_ref, lg_s, *, d):
    i = pl.program_id(0)

    @pl.when(i == 0)
    def _():
        base_ref[...] = jnp.zeros_like(base_ref)
        lg_s[...] = jnp.zeros_like(lg_s)

    routing = _route_tile(lg_s[...], jnp.where(i > 0, 1.0, 0.0), base_ref, cnt_ref, route_ref, rt_ref)
    advance_routing = lambda: next(routing, None)

    w = BRANCH_W
    x = x_ref[...]
    mod = mod_ref[0]
    h = _modulated_norm(x, n1_ref[...], mod[:, 0:d], mod[:, d:2 * d]).astype(BF16)
    advance_routing()

    hd = w // HG_HEADS
    hr = lax.broadcasted_iota(jnp.int32, (w, w), 0) // hd
    hc = lax.broadcasted_iota(jnp.int32, (w, w), 1) // hd
    hm = jnp.where(hr == hc, 1.0, 0.0).astype(BF16)
    inv = 1.0 / hd

    hg = hgf_ref[...] + hgb_ref[...]
    ms = _split_dot(hg * hg, hm) * inv
    br_hg = hg * lax.rsqrt(ms + EPS) * vec_ref[0:1, :] * _silu(zhg_ref[...])
    advance_routing()

    rt = rtf_ref[...] + rtb_ref[...]
    cen = rt - _split_dot(rt, hm) * inv
    var = _split_dot(cen * cen, hm) * inv
    br_rt = cen * lax.rsqrt(var + EPS) * vec_ref[1:2, :] * _silu(zrt_ref[...])
    advance_routing()

    vx = vx_ref[...]
    br_hy = (yc_ref[...] + vx * vec_ref[3:4, :]) * x0_ref[...]

    y5 = _gelu_tanh(s5f_ref[...] + s5b_ref[...] + zs5_ref[...] * vec_ref[2:3, :]).astype(BF16)
    br_s5 = (jnp.dot(y5, glu1_ref[...], preferred_element_type=F32)
             * _sigmoid(jnp.dot(y5, glu2_ref[...], preferred_element_type=F32)))
    advance_routing()

    merged = jnp.zeros((TM, d), F32)
    for n, (br, wg_ref) in enumerate(zip((br_hg, br_hy, br_rt, br_s5), (wg0_ref, wg1_ref, wg2_ref, wg3_ref))):
        gate = _sigmoid(jnp.dot(h, wg_ref[...], preferred_element_type=F32))
        merged = merged + gate * jnp.dot(br.astype(BF16), wb_ref[n], preferred_element_type=F32)
        advance_routing()
    for _ in routing:
        pass
    mix = jnp.dot(merged.astype(BF16), wo_ref[...], preferred_element_type=F32)
    x_mid = x + mod[:, 2 * d:3 * d] * mix
    xo_ref[...] = x_mid
    h2 = _modulated_norm(x_mid, n2_ref[...], mod[:, 3 * d:4 * d], mod[:, 4 * d:5 * d])
    h2_ref[...] = h2

    lg_s[...] = _dot3(h2, wr_ref[...]) + br_ref[...]


def merge(x_all, mod3, n1, n2, w_in_bf, layer, z, hgf, hgb, rtf, rtb, s5f, s5b, yconv, vx, x0c,
          vecs, glu1, glu2, w_branch, w_out, w_router, b_router, bsz, tpb):
    m, d = x_all.shape
    w = BRANCH_W
    nt = m // TM
    cur = lambda i: jnp.minimum(i, nt - 1)
    prev = lambda i: jnp.maximum(i - 1, 0)
    row = lambda width: pl.BlockSpec((TM, width), lambda i: (cur(i), 0))
    srow = pl.BlockSpec((TM, w), lambda i: (_split_tile(cur(i), bsz, tpb), 0))
    zcol = lambda cb: pl.BlockSpec((TM, w), lambda i: (cur(i), cb))
    full = lambda shape: pl.BlockSpec(shape, lambda i: (0,) * len(shape))
    gate_w = lambda n: pl.BlockSpec((None, d, d), lambda i: (layer, 0, N_MIX // d + n))
    return pl.pallas_call(
        functools.partial(_merge_kernel, d=d),
        grid=(nt + 1,),
        in_specs=[row(d),
                  pl.BlockSpec((1, 1, 6 * d), lambda i: (_seg_of_tile(cur(i), tpb), 0, 0)),
                  full((1, d)), full((1, d)), gate_w(0), gate_w(1), gate_w(2), gate_w(3),
                  zcol(CB_HG_G), zcol(CB_RT_G), zcol(CB_S5)]
                 + [row(w)] * 6 + [srow, srow, row(w)]
                 + [full((SUBLANES, w)), full((w, w)), full((w, w)), full((N_BRANCH, w, d)), full((d, d)),
                    full((d, LANES)), full((1, LANES))],
        out_specs=[row(d), row(d),
                   pl.BlockSpec((TM, LANES), lambda i: (prev(i), 0)),
                   pl.BlockSpec((None, SUBLANES, TM), lambda i: (prev(i), 0, 0)),
                   full((SUBLANES, LANES))],
        out_shape=[jax.ShapeDtypeStruct((m, d), F32), jax.ShapeDtypeStruct((m, d), F32),
                   jax.ShapeDtypeStruct((m, LANES), F32), jax.ShapeDtypeStruct((nt, SUBLANES, TM), F32),
                   jax.ShapeDtypeStruct((SUBLANES, LANES), F32)],
        scratch_shapes=[pltpu.VMEM((1, LANES), F32), pltpu.VMEM((TM, LANES), F32)],
        compiler_params=_cparams(("arbitrary",)),
        name="merge",
    )(x_all, mod3, n1, n2, w_in_bf, w_in_bf, w_in_bf, w_in_bf, z, z, z,
      hgf, hgb, rtf, rtb, s5f, s5b, yconv, vx, x0c,
      vecs, glu1, glu2, w_branch, w_out, w_router, b_router)


DMA_UNROLL = 8


def _row_copy(src_ref, src_row, dst_ref, dst_row, sem):
    return pltpu.make_async_copy(src_ref.at[pl.ds(src_row, 1), :], dst_ref.at[pl.ds(dst_row, 1), :], sem)


DISPATCH_ROWS = 6 * TM


def _dispatch_kernel(dest_ref, h_ref, init_ref, buf_ref, sem, *, m):
    del init_ref
    rows = h_ref.shape[0]
    base = pl.program_id(0) * rows

    def issue(r, carry):
        for k in range(2):
            _row_copy(h_ref, r, buf_ref, dest_ref[k * m + base + r], sem).start(priority=k)
        return carry

    lax.fori_loop(0, rows, issue, 0, unroll=DMA_UNROLL)
    for k in range(2):
        pltpu.make_async_copy(h_ref, buf_ref.at[pl.ds(0, rows), :], sem).wait()


def moe_dispatch(dest, h2, init):
    m, d = h2.shape
    n_rows = init.shape[0]
    assert m % DISPATCH_ROWS == 0
    grid_spec = pltpu.PrefetchScalarGridSpec(
        num_scalar_prefetch=1,
        grid=(m // DISPATCH_ROWS,),
        in_specs=[pl.BlockSpec((DISPATCH_ROWS, d), lambda i, dest: (i, 0)),
                  pl.BlockSpec(memory_space=pl.ANY)],
        out_specs=pl.BlockSpec(memory_space=pl.ANY),
        scratch_shapes=[pltpu.SemaphoreType.DMA(())],
    )
    return pl.pallas_call(
        functools.partial(_dispatch_kernel, m=m),
        grid_spec=grid_spec,
        out_shape=jax.ShapeDtypeStruct((n_rows, d), F32),
        input_output_aliases={2: 0},
        compiler_params=_cparams(("arbitrary",)),
        name="moe_dispatch",
    )(dest, h2, init)


EXPERT_SLOTS = 3


def _expert_kernel(plan_ref, x_hbm, w1_ref, w3_ref, w2_ref, y_ref, xbuf, sem, *, n_blocks):
    j = pl.program_id(0)
    n_used = plan_ref[n_blocks]
    used = j < n_used

    def block_copy(blk):
        row0 = pl.multiple_of(blk * MOE_BLOCK, MOE_BLOCK)
        slot = blk % EXPERT_SLOTS
        return pltpu.make_async_copy(x_hbm.at[pl.ds(row0, MOE_BLOCK), :], xbuf.at[slot], sem.at[slot])

    @pl.when(j == 0)
    def _():
        block_copy(0).start()

        @pl.when(n_used > 1)
        def _():
            block_copy(1).start()

    @pl.when(j + 2 < n_used)
    def _():
        block_copy(j + 2).start()

    @pl.when(used)
    def _():
        block_copy(j).wait()
        xb = xbuf[j % EXPERT_SLOTS].astype(BF16)
        a = jnp.dot(xb, w1_ref[...].astype(BF16), preferred_element_type=F32)
        g = jnp.dot(xb, w3_ref[...].astype(BF16), preferred_element_type=F32)
        y_ref[...] = jnp.dot((_silu(a) * g).astype(BF16), w2_ref[...].astype(BF16),
                             preferred_element_type=F32)

    @pl.when(jnp.logical_not(used))
    def _():
        y_ref[...] = jnp.zeros_like(y_ref)


def moe_experts(plan, buf, w1, w3, w2, layer):
    n_rows, d = buf.shape
    hid = w1.shape[-1]
    n_blocks = n_rows // MOE_BLOCK
    blk = lambda j, plan: jnp.minimum(j, plan[n_blocks] - 1)
    grid_spec = pltpu.PrefetchScalarGridSpec(
        num_scalar_prefetch=1,
        grid=(n_blocks,),
        in_specs=[pl.BlockSpec(memory_space=pl.ANY),
                  pl.BlockSpec((None, None, d, hid), lambda j, plan: (layer, plan[blk(j, plan)], 0, 0)),
                  pl.BlockSpec((None, None, d, hid), lambda j, plan: (layer, plan[blk(j, plan)], 0, 0)),
                  pl.BlockSpec((None, None, hid, d), lambda j, plan: (layer, plan[blk(j, plan)], 0, 0))],
        out_specs=pl.BlockSpec((MOE_BLOCK, d), lambda j, plan: (j, 0)),
        scratch_shapes=[pltpu.VMEM((EXPERT_SLOTS, MOE_BLOCK, d), F32),
                        pltpu.SemaphoreType.DMA((EXPERT_SLOTS,))],
    )
    return pl.pallas_call(
        functools.partial(_expert_kernel, n_blocks=n_blocks),
        grid_spec=grid_spec,
        out_shape=jax.ShapeDtypeStruct((n_rows, d), F32),
        compiler_params=_cparams(("arbitrary",)),
        name="moe_experts",
    )(plan, buf, w1, w3, w2)


def _combine_kernel(dest_ref, x_ref, route_ref, mod_ref, gfin_ref, y_ref, o_ref, gath, sem, *,
                    d, m, bsz, tpb, nr, final):
    step = pl.program_id(0) * nr + pl.program_id(1)
    slot = step % 2

    def start_gather(s, sl):
        base = ((s // nr) * tpb + s % nr) * TM

        def issue(r, carry):
            for k in range(2):
                _row_copy(y_ref, dest_ref[k * m + base + r], gath.at[sl, k], r, sem.at[sl]).start(priority=k)
            return carry

        lax.fori_loop(0, TM, issue, 0, unroll=DMA_UNROLL)

    @pl.when(step == 0)
    def _():
        start_gather(0, 0)

    @pl.when(step + 1 < bsz * nr)
    def _():
        start_gather(step + 1, 1 - slot)

    for k in range(2):
        pltpu.make_async_copy(y_ref.at[pl.ds(0, TM), :], gath.at[slot, k], sem.at[slot]).wait()
    route = route_ref[...]
    lane = lax.broadcasted_iota(jnp.int32, (TM, LANES), 1)
    moe = (_lane_pick(route, lane, ROUTE_W1) * gath[slot, 0]
           + _lane_pick(route, lane, ROUTE_W2) * gath[slot, 1])
    x_new = x_ref[...] + mod_ref[0][:, 5 * d:6 * d] * moe
    if final:
        ms = jnp.mean(x_new * x_new, axis=-1, keepdims=True)
        x_new = x_new * lax.rsqrt(ms + EPS) * gfin_ref[...]
    o_ref[...] = x_new


def moe_combine(dest, x_mid, route, mod3, gain_final, ybuf, bsz, tpb, final):
    m, d = x_mid.shape
    nr = tpb - 1 if final else tpb
    tile = lambda b, r: b * tpb + r
    grid_spec = pltpu.PrefetchScalarGridSpec(
        num_scalar_prefetch=1,
        grid=(bsz, nr),
        in_specs=[pl.BlockSpec((TM, d), lambda b, r, dest: (tile(b, r), 0)),
                  pl.BlockSpec((TM, LANES), lambda b, r, dest: (tile(b, r), 0)),
                  pl.BlockSpec((1, 1, 6 * d), lambda b, r, dest: (_seg_of_tile(tile(b, r), tpb), 0, 0)),
                  pl.BlockSpec((1, d), lambda b, r, dest: (0, 0)),
                  pl.BlockSpec(memory_space=pl.ANY)],
        out_specs=pl.BlockSpec((TM, d), lambda b, r, dest: (b * nr + r, 0)),
        scratch_shapes=[pltpu.VMEM((2, 2, TM, d), F32), pltpu.SemaphoreType.DMA((2,))],
    )
    return pl.pallas_call(
        functools.partial(_combine_kernel, d=d, m=m, bsz=bsz, tpb=tpb, nr=nr, final=final),
        grid_spec=grid_spec,
        out_shape=jax.ShapeDtypeStruct((bsz * nr * TM, d), F32),
        compiler_params=_cparams(("arbitrary", "arbitrary")),
        name="moe_combine",
    )(dest, x_mid, route, mod3, gain_final, ybuf)


def moe_plan(rt, counts_row, n_blocks):
    counts = counts_row[:MOE_EXPERTS].astype(jnp.int32)
    padded = (counts + MOE_BLOCK - 1) // MOE_BLOCK * MOE_BLOCK
    pad_end = jnp.cumsum(padded)
    pad_start = pad_end - padded
    by_choice = lambda f: jnp.transpose(rt[:, f:f + 2, :], (1, 0, 2)).reshape(2, -1).astype(jnp.int32)
    expert = by_choice(ROUTE_E1)
    rank = by_choice(ROUTE_R1)
    is_e = expert[None] == jnp.arange(MOE_EXPERTS, dtype=jnp.int32)[:, None, None]
    dest = (jnp.sum(jnp.where(is_e, pad_start[:, None, None], 0), axis=0) + rank).reshape(-1)
    block_row0 = jnp.arange(n_blocks, dtype=jnp.int32) * MOE_BLOCK
    block_expert = jnp.minimum(jnp.sum((pad_end[None, :] <= block_row0[:, None]).astype(jnp.int32), axis=1),
                               MOE_EXPERTS - 1)
    n_used = pad_end[MOE_EXPERTS - 1:] // MOE_BLOCK
    return dest, jnp.concatenate([block_expert, n_used])


def _rope_tables(length, lc):
    half = RET_DK // 2
    freqs = ROPE_BASE ** (-jnp.arange(0, half, 2, dtype=F32) / half)
    t = jnp.arange(length)
    row_pos = (t // GRID_W).astype(F32)[:, None]
    col_pos = (t % GRID_W).astype(F32)[:, None]
    lane = jnp.arange(RET_HEADS * RET_DK)
    fr = freqs[lane % (half // 2)][None, :]
    ang = jnp.where(((lane % RET_DK) < half)[None, :], row_pos, col_pos) * fr
    cos = jnp.concatenate([jnp.cos(ang), jnp.ones((lc, lane.shape[0]), F32)], axis=0)
    sin = jnp.concatenate([jnp.sin(ang), jnp.zeros((lc, lane.shape[0]), F32)], axis=0)
    return cos, sin


def _rotate_half_cols(wq):
    quarter = RET_DK // 4
    lane = np.arange(wq.shape[1])
    first = (lane % (2 * quarter)) < quarter
    src = np.where(first, lane + quarter, lane - quarter)
    sign = jnp.asarray(np.where(first, -1.0, 1.0), F32)
    return wq[:, src] * sign


def kernel(x, c, ctx, c_ctx, w_mod, b_mod, norm1, norm2, w_in, hg_lb_logits, hg_norm, hy_conv_w, hy_conv_b, hy_filt_w1, hy_filt_b1, hy_filt_freq1, hy_filt_w2, hy_filt_b2, hy_filt_freq2, hy_filt_w3, hy_skip, ret_decay_logit, ret_norm, s5_lam_re, s5_lam_im, s5_log_dt, s5_b_re, s5_b_im, s5_c_re, s5_c_im, s5_d, s5_glu_w1, s5_glu_w2, w_branch, w_out, moe_w_grp, moe_b_grp, moe_w_exp, moe_b_exp, moe_w1, moe_w3, moe_w2, norm_final):
    bsz, length, d = x.shape
    lc = ctx.shape[1]
    depth = w_mod.shape[0]
    assert lc == TM and length % TM == 0 and (2 * length) % (FFT_FAST * FFT_GROUP) == 0
    tt = length + lc
    tpb = tt // TM
    m = bsz * tt
    n_blocks = -(-(m * 2) // MOE_BLOCK) + MOE_EXPERTS
    w = BRANCH_W

    x_all = jnp.concatenate([x, ctx], axis=1).reshape(m, d)
    cmat = jnp.zeros((SUBLANES, d), F32).at[0].set(c_ctx).at[1:1 + bsz].set(c)
    mods = modvec(cmat, w_mod, b_mod)
    lb_sm = jax.nn.softmax(hg_lb_logits.astype(F32), axis=0)
    lb_all = jnp.cumsum(lb_sm, axis=0) - lb_sm[0]
    cos_tab, sin_tab = _rope_tables(length, lc)
    log_gamma = jax.nn.log_sigmoid(ret_decay_logit.astype(F32))
    gfin = norm_final.reshape(1, d)

    w_in_bf = w_in.astype(BF16)
    rq0 = 5 * w + 3 * HY_W
    nq = RET_HEADS * RET_DK

    for l in range(depth):
        w_rot = jnp.concatenate([_rotate_half_cols(w_in[l, :, rq0:rq0 + nq]),
                                 _rotate_half_cols(w_in[l, :, rq0 + nq:rq0 + 2 * nq])], axis=1).astype(BF16)
        mod3 = mods[l].reshape(SUBLANES, 1, 6 * d)

        z = inproj(x_all, mod3, norm1[l].reshape(1, d), w_in_bf, l, w_rot, tpb)
        hgf, hgb = hgrn(z, lb_all[l], bsz, tpb)
        rtf, rtb = retention(z, cos_tab, sin_tab, log_gamma[l], bsz, tpb)
        s5f, s5b = s5g(z, *s5g_params(s5_lam_re[l], s5_lam_im[l], s5_log_dt[l], s5_b_re[l], s5_b_im[l],
                                      s5_c_re[l], s5_c_im[l]), bsz, tpb)

        x0c, vx = hy_prep(z, hy_conv_w[l], hy_conv_b[l], bsz, tpb)
        fargs = (hy_filt_w1[l], hy_filt_b1[l], hy_filt_freq1[l], hy_filt_w2[l], hy_filt_b2[l],
                 hy_filt_freq2[l], hy_filt_w3[l])
        hfb, ss = hy_filter(length, *fargs)
        y_lat = hyena_long_conv(vx, bsz, length, hfb, lax.rsqrt(ss + EPS))
        hfbc, ssc = hy_filter(lc, *fargs)
        gwin = jnp.concatenate([hfbc[1, 1:][::-1], hfbc[0], jnp.zeros((1, HY_W), F32)], axis=0) \
            * lax.rsqrt(ssc + EPS)
        yconv = hy_ctx_conv(vx, gwin, y_lat, bsz, tpb, lc)

        vecs = jnp.zeros((SUBLANES, w), F32).at[0].set(hg_norm[l]).at[1].set(ret_norm[l]) \
            .at[2].set(s5_d[l]).at[3].set(hy_skip[l])
        w_router = jnp.zeros((d, LANES), F32).at[:, :MOE_GROUPS].set(moe_w_grp[l]) \
            .at[:, MOE_GROUPS:MOE_GROUPS + MOE_EXPERTS].set(moe_w_exp[l])
        b_router = jnp.zeros((1, LANES), F32).at[0, :MOE_GROUPS].set(moe_b_grp[l]) \
            .at[0, MOE_GROUPS:MOE_GROUPS + MOE_EXPERTS].set(moe_b_exp[l])
        x_mid, h2, route, rt, counts = merge(
            x_all, mod3, norm1[l].reshape(1, d), norm2[l].reshape(1, d), w_in_bf, l, z,
            hgf, hgb, rtf, rtb, s5f, s5b, yconv, vx, x0c, vecs,
            s5_glu_w1[l].astype(BF16), s5_glu_w2[l].astype(BF16), w_branch[l].astype(BF16),
            w_out[l].astype(BF16), w_router, b_router, bsz, tpb)

        dest, block_expert = moe_plan(rt, counts[0], n_blocks)
        buf = moe_dispatch(dest, h2, jnp.zeros((n_blocks * MOE_BLOCK, d), F32) if l == 0 else buf)
        ybuf = moe_experts(block_expert, buf, moe_w1, moe_w3, moe_w2, l)
        x_all = moe_combine(dest, x_mid, route, mod3, gfin, ybuf, bsz, tpb, l == depth - 1)

    return x_all.reshape(bsz, length, d)
```
